```python
import math
import jax, jax.numpy as jnp
from jax import lax
import numpy as np

D_MODEL = 2048
BATCH = 8
SEQ = 4096
DEPTH = 1

MIX_WIDTH = D_MODEL
POOL_WIDTH = MIX_WIDTH // 2
LRU_WIDTH = MIX_WIDTH - POOL_WIDTH
POOL_WINDOWS = (2, 4, 8, 16)
N_POOL_GROUPS = len(POOL_WINDOWS)
POOL_GROUP = POOL_WIDTH // N_POOL_GROUPS
LRU_HEADS = 8
LRU_HEAD_DIM = LRU_WIDTH // LRU_HEADS
CONV_WIDTH = 4
LRU_C = 8.0
N_MEM = 256
XATTN_HEADS = 4
XATTN_HEAD_DIM = D_MODEL // XATTN_HEADS
D_FF = 4 * D_MODEL
LN_EPS = 1e-5
DEEPNORM_ALPHA = (2.0 * DEPTH) ** 0.25
DEEPNORM_BETA = (8.0 * DEPTH) ** -0.25

kernel_name = "hymba_pool_rglru_deepnorm_layer"


def layer_norm(x, g, b):
    xf = x.astype(jnp.float32)
    mu = jnp.mean(xf, axis=-1, keepdims=True)
    var = jnp.mean(jnp.square(xf - mu), axis=-1, keepdims=True)
    y = (xf - mu) * lax.rsqrt(var + LN_EPS)
    return (y * g.astype(jnp.float32) + b.astype(jnp.float32)).astype(x.dtype)


def multiscale_pool(u, w_pool, b_pool, pool_scale):
    B, S, _ = u.shape
    uf = u.astype(jnp.float32).reshape(B, S, N_POOL_GROUPS, POOL_GROUP)
    csum = jnp.cumsum(uf, axis=1)
    t = jnp.arange(S)
    means = []
    for g, w in enumerate(POOL_WINDOWS):
        c = csum[:, :, g]
        c_prev = jnp.pad(c, ((0, 0), (w, 0), (0, 0)))[:, :S]
        cnt = jnp.minimum(t + 1, w).astype(jnp.float32)[None, :, None]
        means.append((c - c_prev) / cnt)
    mixed = (jnp.stack(means, axis=2) - uf).astype(u.dtype)
    y = jnp.einsum('bsgc,gcd->bsgd', mixed, w_pool) + b_pool
    return y.reshape(B, S, POOL_WIDTH) * pool_scale


def causal_depthwise_conv(x, w, b):
    S = x.shape[1]
    xp = jnp.pad(x, ((0, 0), (CONV_WIDTH - 1, 0), (0, 0)))
    y = xp[:, 0:S] * w[0]
    for k in range(1, CONV_WIDTH):
        y = y + xp[:, k:k + S] * w[k]
    return y + b


def rg_lru(x, w_a, b_a, w_x, b_x, lam):
    B, S, _ = x.shape
    xh = x.reshape(B, S, LRU_HEADS, LRU_HEAD_DIM)
    r = jax.nn.sigmoid(jnp.einsum('bshi,hij->bshj', xh, w_a) + b_a).reshape(B, S, LRU_WIDTH)
    i = jax.nn.sigmoid(jnp.einsum('bshi,hij->bshj', xh, w_x) + b_x).reshape(B, S, LRU_WIDTH)
    log_a = -LRU_C * r.astype(jnp.float32) * jax.nn.softplus(-lam.astype(jnp.float32))
    a = jnp.exp(log_a)
    mult = jnp.sqrt(-jnp.expm1(2.0 * log_a))
    mult = jnp.where((jnp.arange(S) == 0)[None, :, None], 1.0, mult)
    bterm = mult * (i * x).astype(jnp.float32)

    def combine(lhs, rhs):
        a1, b1 = lhs
        a2, b2 = rhs
        return a1 * a2, a2 * b1 + b2

    _, h = lax.associative_scan(combine, (a, bterm), axis=1)
    return h.astype(x.dtype)


def hybrid_mixer(x, w_in, conv_w, conv_b, w_a, b_a, w_x, b_x, lam,
                 w_pool, b_pool, pool_scale, w_out):
    proj = x @ w_in
    u_pool = proj[..., :POOL_WIDTH]
    u_lru = proj[..., POOL_WIDTH:POOL_WIDTH + LRU_WIDTH]
    u_gate = proj[..., POOL_WIDTH + LRU_WIDTH:]
    y_pool = multiscale_pool(u_pool, w_pool, b_pool, pool_scale)
    h = rg_lru(causal_depthwise_conv(u_lru, conv_w, conv_b), w_a, b_a, w_x, b_x, lam)
    y_lru = h * jax.nn.gelu(u_gate)
    return jnp.concatenate([y_pool, y_lru], axis=-1) @ w_out


def memory_cross_attention(x, mem, w_q, w_k, w_v, w_o):
    B, S, _ = x.shape
    M = mem.shape[1]
    q = (x @ w_q).reshape(B, S, XATTN_HEADS, XATTN_HEAD_DIM)
    k = (mem @ w_k).reshape(B, M, XATTN_HEADS, XATTN_HEAD_DIM)
    v = (mem @ w_v).reshape(B, M, XATTN_HEADS, XATTN_HEAD_DIM)
    s = jnp.einsum('bqhd,bmhd->bhqm', q, k).astype(jnp.float32) * (XATTN_HEAD_DIM ** -0.5)
    p = jax.nn.softmax(s, axis=-1).astype(v.dtype)
    o = jnp.einsum('bhqm,bmhd->bqhd', p, v).reshape(B, S, D_MODEL)
    return o @ w_o


def squared_relu_mlp(x, w1, w2):
    return jnp.square(jax.nn.relu(x @ w1)) @ w2


def _fwd_setup_inputs(seed: int = 0) -> dict:
    key = jax.random.key(seed)
    ks = jax.random.split(key, 32)
    f32 = jnp.float32

    def nrm(k, shape, scale):
        return jax.random.normal(k, shape, f32) * scale

    L = DEPTH
    u = jax.random.uniform(ks[10], (L, LRU_WIDTH), f32, 0.9, 0.999)
    s = u ** (1.0 / LRU_C)
    lam = jnp.log(s) - jnp.log1p(-s)
    return {
        "x": nrm(ks[0], (BATCH, SEQ, D_MODEL), 1.0),
        "mem": nrm(ks[1], (BATCH, N_MEM, D_MODEL), 1.0),
        "w_in": nrm(ks[2], (L, D_MODEL, POOL_WIDTH + 2 * LRU_WIDTH), D_MODEL ** -0.5),
        "conv_w": nrm(ks[3], (L, CONV_WIDTH, LRU_WIDTH), CONV_WIDTH ** -0.5),
        "conv_b": nrm(ks[4], (L, LRU_WIDTH), 0.01),
        "w_a": nrm(ks[5], (L, LRU_HEADS, LRU_HEAD_DIM, LRU_HEAD_DIM), LRU_HEAD_DIM ** -0.5),
        "b_a": nrm(ks[6], (L, LRU_HEADS, LRU_HEAD_DIM), 0.01),
        "w_x": nrm(ks[7], (L, LRU_HEADS, LRU_HEAD_DIM, LRU_HEAD_DIM), LRU_HEAD_DIM ** -0.5),
        "b_x": nrm(ks[8], (L, LRU_HEADS, LRU_HEAD_DIM), 0.01),
        "lru_lambda": lam,
        "w_pool": nrm(ks[11], (L, N_POOL_GROUPS, POOL_GROUP, POOL_GROUP), POOL_GROUP ** -0.5),
        "b_pool": nrm(ks[12], (L, N_POOL_GROUPS, POOL_GROUP), 0.01),
        "pool_scale": 1.0 + nrm(ks[13], (L, POOL_WIDTH), 0.1),
        "w_out": nrm(ks[14], (L, MIX_WIDTH, D_MODEL), MIX_WIDTH ** -0.5 * DEEPNORM_BETA),
        "ln1_g": 1.0 + nrm(ks[15], (L, D_MODEL), 0.05),
        "ln1_b": nrm(ks[16], (L, D_MODEL), 0.01),
        "w_q": nrm(ks[17], (L, D_MODEL, D_MODEL), D_MODEL ** -0.5),
        "w_k": nrm(ks[18], (L, D_MODEL, D_MODEL), D_MODEL ** -0.5),
        "w_v": nrm(ks[19], (L, D_MODEL, D_MODEL), D_MODEL ** -0.5 * DEEPNORM_BETA),
        "w_o": nrm(ks[20], (L, D_MODEL, D_MODEL), D_MODEL ** -0.5 * DEEPNORM_BETA),
        "ln2_g": 1.0 + nrm(ks[21], (L, D_MODEL), 0.05),
        "ln2_b": nrm(ks[22], (L, D_MODEL), 0.01),
        "w_ff1": nrm(ks[23], (L, D_MODEL, D_FF), D_MODEL ** -0.5 * DEEPNORM_BETA),
        "w_ff2": nrm(ks[24], (L, D_FF, D_MODEL), D_FF ** -0.5 * DEEPNORM_BETA),
        "ln3_g": 1.0 + nrm(ks[25], (L, D_MODEL), 0.05),
        "ln3_b": nrm(ks[26], (L, D_MODEL), 0.01),
    }


def _fwd_reference(x, mem, w_in, conv_w, conv_b, w_a, b_a, w_x, b_x, lru_lambda,
              w_pool, b_pool, pool_scale, w_out, ln1_g, ln1_b,
              w_q, w_k, w_v, w_o, ln2_g, ln2_b, w_ff1, w_ff2, ln3_g, ln3_b):
    for l in range(DEPTH):
        y = hybrid_mixer(x, w_in[l], conv_w[l], conv_b[l], w_a[l], b_a[l], w_x[l], b_x[l],
                         lru_lambda[l], w_pool[l], b_pool[l], pool_scale[l], w_out[l])
        x = layer_norm(DEEPNORM_ALPHA * x + y, ln1_g[l], ln1_b[l])
        y = memory_cross_attention(x, mem, w_q[l], w_k[l], w_v[l], w_o[l])
        x = layer_norm(DEEPNORM_ALPHA * x + y, ln2_g[l], ln2_b[l])
        y = squared_relu_mlp(x, w_ff1[l], w_ff2[l])
        x = layer_norm(DEEPNORM_ALPHA * x + y, ln3_g[l], ln3_b[l])
    return x


import jax as _jax
import jax.numpy as _jnp

TWIN_FORMAT = 'train_step'
FWD_PARAMS = ['x', 'mem', 'w_in', 'conv_w', 'conv_b', 'w_a', 'b_a', 'w_x', 'b_x', 'lru_lambda', 'w_pool', 'b_pool', 'pool_scale', 'w_out', 'ln1_g', 'ln1_b', 'w_q', 'w_k', 'w_v', 'w_o', 'ln2_g', 'ln2_b', 'w_ff1', 'w_ff2', 'ln3_g', 'ln3_b']
TWIN_WEIGHTS = ['w_in', 'conv_w', 'conv_b', 'w_a', 'b_a', 'w_x', 'b_x', 'lru_lambda', 'w_pool', 'b_pool', 'pool_scale', 'w_out', 'ln1_g', 'ln1_b', 'w_q', 'w_k', 'w_v', 'w_o', 'ln2_g', 'ln2_b', 'w_ff1', 'w_ff2', 'ln3_g', 'ln3_b']
TWIN_DIFF_INPUT = 'x'
TWIN_INPUTS = ['x', 'mem', 'w_in', 'conv_w', 'conv_b', 'w_a', 'b_a', 'w_x', 'b_x', 'lru_lambda', 'w_pool', 'b_pool', 'pool_scale', 'w_out', 'ln1_g', 'ln1_b', 'w_q', 'w_k', 'w_v', 'w_o', 'ln2_g', 'ln2_b', 'w_ff1', 'w_ff2', 'ln3_g', 'ln3_b', 'loss_target', 'm_w_in', 'm_conv_w', 'm_conv_b', 'm_w_a', 'm_b_a', 'm_w_x', 'm_b_x', 'm_lru_lambda', 'm_w_pool', 'm_b_pool', 'm_pool_scale', 'm_w_out', 'm_ln1_g', 'm_ln1_b', 'm_w_q', 'm_w_k', 'm_w_v', 'm_w_o', 'm_ln2_g', 'm_ln2_b', 'm_w_ff1', 'm_w_ff2', 'm_ln3_g', 'm_ln3_b', 'v_w_in', 'v_conv_w', 'v_conv_b', 'v_w_a', 'v_b_a', 'v_w_x', 'v_b_x', 'v_lru_lambda', 'v_w_pool', 'v_b_pool', 'v_pool_scale', 'v_w_out', 'v_ln1_g', 'v_ln1_b', 'v_w_q', 'v_w_k', 'v_w_v', 'v_w_o', 'v_ln2_g', 'v_ln2_b', 'v_w_ff1', 'v_w_ff2', 'v_ln3_g', 'v_ln3_b']
TWIN_OUTPUTS = ['loss', 'grad_x', 'grad_w_in', 'grad_conv_w', 'grad_conv_b', 'grad_w_a', 'grad_b_a', 'grad_w_x', 'grad_b_x', 'grad_lru_lambda', 'grad_w_pool', 'grad_b_pool', 'grad_pool_scale', 'grad_w_out', 'grad_ln1_g', 'grad_ln1_b', 'grad_w_q', 'grad_w_k', 'grad_w_v', 'grad_w_o', 'grad_ln2_g', 'grad_ln2_b', 'grad_w_ff1', 'grad_w_ff2', 'grad_ln3_g', 'grad_ln3_b', 'delta_w_in', 'delta_conv_w', 'delta_conv_b', 'delta_w_a', 'delta_b_a', 'delta_w_x', 'delta_b_x', 'delta_lru_lambda', 'delta_w_pool', 'delta_b_pool', 'delta_pool_scale', 'delta_w_out', 'delta_ln1_g', 'delta_ln1_b', 'delta_w_q', 'delta_w_k', 'delta_w_v', 'delta_w_o', 'delta_ln2_g', 'delta_ln2_b', 'delta_w_ff1', 'delta_w_ff2', 'delta_ln3_g', 'delta_ln3_b', 'new_m_w_in', 'new_m_conv_w', 'new_m_conv_b', 'new_m_w_a', 'new_m_b_a', 'new_m_w_x', 'new_m_b_x', 'new_m_lru_lambda', 'new_m_w_pool', 'new_m_b_pool', 'new_m_pool_scale', 'new_m_w_out', 'new_m_ln1_g', 'new_m_ln1_b', 'new_m_w_q', 'new_m_w_k', 'new_m_w_v', 'new_m_w_o', 'new_m_ln2_g', 'new_m_ln2_b', 'new_m_w_ff1', 'new_m_w_ff2', 'new_m_ln3_g', 'new_m_ln3_b', 'new_v_w_in', 'new_v_conv_w', 'new_v_conv_b', 'new_v_w_a', 'new_v_b_a', 'new_v_w_x', 'new_v_b_x', 'new_v_lru_lambda', 'new_v_w_pool', 'new_v_b_pool', 'new_v_pool_scale', 'new_v_w_out', 'new_v_ln1_g', 'new_v_ln1_b', 'new_v_w_q', 'new_v_w_k', 'new_v_w_v', 'new_v_w_o', 'new_v_ln2_g', 'new_v_ln2_b', 'new_v_w_ff1', 'new_v_w_ff2', 'new_v_ln3_g', 'new_v_ln3_b']
TWIN_LEAF_KINDS = {'loss': 'loss', 'grad_x': 'grad_x', 'grad_w_in': 'grad_w', 'grad_conv_w': 'grad_w', 'grad_conv_b': 'grad_w', 'grad_w_a': 'grad_w', 'grad_b_a': 'grad_w', 'grad_w_x': 'grad_w', 'grad_b_x': 'grad_w', 'grad_lru_lambda': 'grad_w', 'grad_w_pool': 'grad_w', 'grad_b_pool': 'grad_w', 'grad_pool_scale': 'grad_w', 'grad_w_out': 'grad_w', 'grad_ln1_g': 'grad_w', 'grad_ln1_b': 'grad_w', 'grad_w_q': 'grad_w', 'grad_w_k': 'grad_w', 'grad_w_v': 'grad_w', 'grad_w_o': 'grad_w', 'grad_ln2_g': 'grad_w', 'grad_ln2_b': 'grad_w', 'grad_w_ff1': 'grad_w', 'grad_w_ff2': 'grad_w', 'grad_ln3_g': 'grad_w', 'grad_ln3_b': 'grad_w', 'delta_w_in': 'delta_w', 'delta_conv_w': 'delta_w', 'delta_conv_b': 'delta_w', 'delta_w_a': 'delta_w', 'delta_b_a': 'delta_w', 'delta_w_x': 'delta_w', 'delta_b_x': 'delta_w', 'delta_lru_lambda': 'delta_w', 'delta_w_pool': 'delta_w', 'delta_b_pool': 'delta_w', 'delta_pool_scale': 'delta_w', 'delta_w_out': 'delta_w', 'delta_ln1_g': 'delta_w', 'delta_ln1_b': 'delta_w', 'delta_w_q': 'delta_w', 'delta_w_k': 'delta_w', 'delta_w_v': 'delta_w', 'delta_w_o': 'delta_w', 'delta_ln2_g': 'delta_w', 'delta_ln2_b': 'delta_w', 'delta_w_ff1': 'delta_w', 'delta_w_ff2': 'delta_w', 'delta_ln3_g': 'delta_w', 'delta_ln3_b': 'delta_w', 'new_m_w_in': 'new_m', 'new_m_conv_w': 'new_m', 'new_m_conv_b': 'new_m', 'new_m_w_a': 'new_m', 'new_m_b_a': 'new_m', 'new_m_w_x': 'new_m', 'new_m_b_x': 'new_m', 'new_m_lru_lambda': 'new_m', 'new_m_w_pool': 'new_m', 'new_m_b_pool': 'new_m', 'new_m_pool_scale': 'new_m', 'new_m_w_out': 'new_m', 'new_m_ln1_g': 'new_m', 'new_m_ln1_b': 'new_m', 'new_m_w_q': 'new_m', 'new_m_w_k': 'new_m', 'new_m_w_v': 'new_m', 'new_m_w_o': 'new_m', 'new_m_ln2_g': 'new_m', 'new_m_ln2_b': 'new_m', 'new_m_w_ff1': 'new_m', 'new_m_w_ff2': 'new_m', 'new_m_ln3_g': 'new_m', 'new_m_ln3_b': 'new_m', 'new_v_w_in': 'new_v', 'new_v_conv_w': 'new_v', 'new_v_conv_b': 'new_v', 'new_v_w_a': 'new_v', 'new_v_b_a': 'new_v', 'new_v_w_x': 'new_v', 'new_v_b_x': 'new_v', 'new_v_lru_lambda': 'new_v', 'new_v_w_pool': 'new_v', 'new_v_b_pool': 'new_v', 'new_v_pool_scale': 'new_v', 'new_v_w_out': 'new_v', 'new_v_ln1_g': 'new_v', 'new_v_ln1_b': 'new_v', 'new_v_w_q': 'new_v', 'new_v_w_k': 'new_v', 'new_v_w_v': 'new_v', 'new_v_w_o': 'new_v', 'new_v_ln2_g': 'new_v', 'new_v_ln2_b': 'new_v', 'new_v_w_ff1': 'new_v', 'new_v_w_ff2': 'new_v', 'new_v_ln3_g': 'new_v', 'new_v_ln3_b': 'new_v'}


def _forward(args):
    return _fwd_reference(*[args[k] for k in FWD_PARAMS])


def _output_shape():
    def fwd():
        inp = _fwd_setup_inputs(0)
        return _fwd_reference(*[inp[k] for k in FWD_PARAMS])
    out = _jax.eval_shape(fwd)
    return out.shape, out.dtype

N_MICROBATCH = 1
ADAM_LR = 0.001
ADAM_B1 = 0.9
ADAM_B2 = 0.999
ADAM_EPS = 1e-08
ADAM_WD = 0.01
ADAM_STEP = 10
PER_EXAMPLE_BATCH_AXIS = {'x': 0, 'mem': 0, 'loss_target': 0}
SHARED_INPUTS = []
_WEIGHT_DTYPES = {'w_in': _jnp.float32, 'conv_w': _jnp.float32, 'conv_b': _jnp.float32, 'w_a': _jnp.float32, 'b_a': _jnp.float32, 'w_x': _jnp.float32, 'b_x': _jnp.float32, 'lru_lambda': _jnp.float32, 'w_pool': _jnp.float32, 'b_pool': _jnp.float32, 'pool_scale': _jnp.float32, 'w_out': _jnp.float32, 'ln1_g': _jnp.float32, 'ln1_b': _jnp.float32, 'w_q': _jnp.float32, 'w_k': _jnp.float32, 'w_v': _jnp.float32, 'w_o': _jnp.float32, 'ln2_g': _jnp.float32, 'ln2_b': _jnp.float32, 'w_ff1': _jnp.float32, 'w_ff2': _jnp.float32, 'ln3_g': _jnp.float32, 'ln3_b': _jnp.float32}
MOMENT_SCALE = {'w_in': 2.615189e-02, 'conv_w': 1.969180e-02, 'conv_b': 2.517520e-01, 'w_a': 6.497371e-03, 'b_a': 4.633782e-03, 'w_x': 1.187344e-02, 'b_x': 6.112566e-03, 'lru_lambda': 9.442226e-03, 'w_pool': 3.836809e-02, 'b_pool': 9.724556e-02, 'pool_scale': 3.778026e-02, 'w_out': 5.074176e-02, 'ln1_g': 1.517507e+00, 'ln1_b': 1.950068e-01, 'w_q': 2.717344e-03, 'w_k': 2.724909e-03, 'w_v': 4.931353e-03, 'w_o': 4.957408e-03, 'ln2_g': 1.513703e+00, 'ln2_b': 1.949736e-01, 'w_ff1': 1.852147e-02, 'w_ff2': 3.948263e-02, 'ln3_g': 1.613677e+01, 'ln3_b': 1.467377e+00}


def _to_microbatches(a, axis):
    t = _jnp.moveaxis(a, axis, 0)
    t = t.reshape((N_MICROBATCH, t.shape[0] // N_MICROBATCH) + t.shape[1:])
    return _jnp.moveaxis(t, 1, axis + 1)


def setup_inputs(seed: int = 0) -> dict:
    inp = _fwd_setup_inputs(seed)
    key = _jax.random.fold_in(_jax.random.key(seed), 7919)
    shape, _ = _output_shape()
    out = dict(inp)
    out["loss_target"] = _jax.random.normal(_jax.random.fold_in(key, 0), shape, _jnp.float32)
    for i, name in enumerate(TWIN_WEIGHTS):
        w = inp[name].astype(_jnp.float32)
        if MOMENT_SCALE is None:
            s = _jnp.sqrt(_jnp.mean(_jnp.square(w)) + 1e-30)
        else:
            s = MOMENT_SCALE[name]
        km, kv = _jax.random.split(_jax.random.fold_in(key, i + 1))
        out[name] = w
        out["m_" + name] = s * _jax.random.normal(km, w.shape, _jnp.float32)
        out["v_" + name] = (s * s) * _jax.random.uniform(kv, w.shape, _jnp.float32, 0.5, 1.5)
    if N_MICROBATCH > 1:
        for name, axis in PER_EXAMPLE_BATCH_AXIS.items():
            out[name] = _to_microbatches(out[name], axis)
    return {'x': out['x'], 'mem': out['mem'], 'w_in': out['w_in'], 'conv_w': out['conv_w'], 'conv_b': out['conv_b'], 'w_a': out['w_a'], 'b_a': out['b_a'], 'w_x': out['w_x'], 'b_x': out['b_x'], 'lru_lambda': out['lru_lambda'], 'w_pool': out['w_pool'], 'b_pool': out['b_pool'], 'pool_scale': out['pool_scale'], 'w_out': out['w_out'], 'ln1_g': out['ln1_g'], 'ln1_b': out['ln1_b'], 'w_q': out['w_q'], 'w_k': out['w_k'], 'w_v': out['w_v'], 'w_o': out['w_o'], 'ln2_g': out['ln2_g'], 'ln2_b': out['ln2_b'], 'w_ff1': out['w_ff1'], 'w_ff2': out['w_ff2'], 'ln3_g': out['ln3_g'], 'ln3_b': out['ln3_b'], 'loss_target': out['loss_target'], 'm_w_in': out['m_w_in'], 'm_conv_w': out['m_conv_w'], 'm_conv_b': out['m_conv_b'], 'm_w_a': out['m_w_a'], 'm_b_a': out['m_b_a'], 'm_w_x': out['m_w_x'], 'm_b_x': out['m_b_x'], 'm_lru_lambda': out['m_lru_lambda'], 'm_w_pool': out['m_w_pool'], 'm_b_pool': out['m_b_pool'], 'm_pool_scale': out['m_pool_scale'], 'm_w_out': out['m_w_out'], 'm_ln1_g': out['m_ln1_g'], 'm_ln1_b': out['m_ln1_b'], 'm_w_q': out['m_w_q'], 'm_w_k': out['m_w_k'], 'm_w_v': out['m_w_v'], 'm_w_o': out['m_w_o'], 'm_ln2_g': out['m_ln2_g'], 'm_ln2_b': out['m_ln2_b'], 'm_w_ff1': out['m_w_ff1'], 'm_w_ff2': out['m_w_ff2'], 'm_ln3_g': out['m_ln3_g'], 'm_ln3_b': out['m_ln3_b'], 'v_w_in': out['v_w_in'], 'v_conv_w': out['v_conv_w'], 'v_conv_b': out['v_conv_b'], 'v_w_a': out['v_w_a'], 'v_b_a': out['v_b_a'], 'v_w_x': out['v_w_x'], 'v_b_x': out['v_b_x'], 'v_lru_lambda': out['v_lru_lambda'], 'v_w_pool': out['v_w_pool'], 'v_b_pool': out['v_b_pool'], 'v_pool_scale': out['v_pool_scale'], 'v_w_out': out['v_w_out'], 'v_ln1_g': out['v_ln1_g'], 'v_ln1_b': out['v_ln1_b'], 'v_w_q': out['v_w_q'], 'v_w_k': out['v_w_k'], 'v_w_v': out['v_w_v'], 'v_w_o': out['v_w_o'], 'v_ln2_g': out['v_ln2_g'], 'v_ln2_b': out['v_ln2_b'], 'v_w_ff1': out['v_w_ff1'], 'v_w_ff2': out['v_w_ff2'], 'v_ln3_g': out['v_ln3_g'], 'v_ln3_b': out['v_ln3_b']}


def _loss(weights, diff, rest, loss_target):
    with _jax.named_scope("forward"):
        args = {**rest, TWIN_DIFF_INPUT: diff, **{k: w.astype(_WEIGHT_DTYPES[k]) for k, w in weights.items()}}
        y = _forward(args)
    with _jax.named_scope("loss_head"):
        err = _jnp.square(y.astype(_jnp.float32) - loss_target)
        return 0.5 * _jnp.sum(_jnp.mean(err, axis=-1)) if err.ndim else 0.5 * err


def _adamw(w, g, m, v):
    m = ADAM_B1 * m + (1.0 - ADAM_B1) * g
    v = ADAM_B2 * v + (1.0 - ADAM_B2) * _jnp.square(g)
    m_hat = m / (1.0 - ADAM_B1 ** ADAM_STEP)
    v_hat = v / (1.0 - ADAM_B2 ** ADAM_STEP)
    delta = -ADAM_LR * (m_hat / (_jnp.sqrt(v_hat) + ADAM_EPS) + ADAM_WD * w)
    return delta, m, v


def reference(x, mem, w_in, conv_w, conv_b, w_a, b_a, w_x, b_x, lru_lambda, w_pool, b_pool, pool_scale, w_out, ln1_g, ln1_b, w_q, w_k, w_v, w_o, ln2_g, ln2_b, w_ff1, w_ff2, ln3_g, ln3_b, loss_target, m_w_in, m_conv_w, m_conv_b, m_w_a, m_b_a, m_w_x, m_b_x, m_lru_lambda, m_w_pool, m_b_pool, m_pool_scale, m_w_out, m_ln1_g, m_ln1_b, m_w_q, m_w_k, m_w_v, m_w_o, m_ln2_g, m_ln2_b, m_w_ff1, m_w_ff2, m_ln3_g, m_ln3_b, v_w_in, v_conv_w, v_conv_b, v_w_a, v_b_a, v_w_x, v_b_x, v_lru_lambda, v_w_pool, v_b_pool, v_pool_scale, v_w_out, v_ln1_g, v_ln1_b, v_w_q, v_w_k, v_w_v, v_w_o, v_ln2_g, v_ln2_b, v_w_ff1, v_w_ff2, v_ln3_g, v_ln3_b):
    given = dict(x=x, mem=mem, w_in=w_in, conv_w=conv_w, conv_b=conv_b, w_a=w_a, b_a=b_a, w_x=w_x, b_x=b_x, lru_lambda=lru_lambda, w_pool=w_pool, b_pool=b_pool, pool_scale=pool_scale, w_out=w_out, ln1_g=ln1_g, ln1_b=ln1_b, w_q=w_q, w_k=w_k, w_v=w_v, w_o=w_o, ln2_g=ln2_g, ln2_b=ln2_b, w_ff1=w_ff1, w_ff2=w_ff2, ln3_g=ln3_g, ln3_b=ln3_b, loss_target=loss_target, m_w_in=m_w_in, m_conv_w=m_conv_w, m_conv_b=m_conv_b, m_w_a=m_w_a, m_b_a=m_b_a, m_w_x=m_w_x, m_b_x=m_b_x, m_lru_lambda=m_lru_lambda, m_w_pool=m_w_pool, m_b_pool=m_b_pool, m_pool_scale=m_pool_scale, m_w_out=m_w_out, m_ln1_g=m_ln1_g, m_ln1_b=m_ln1_b, m_w_q=m_w_q, m_w_k=m_w_k, m_w_v=m_w_v, m_w_o=m_w_o, m_ln2_g=m_ln2_g, m_ln2_b=m_ln2_b, m_w_ff1=m_w_ff1, m_w_ff2=m_w_ff2, m_ln3_g=m_ln3_g, m_ln3_b=m_ln3_b, v_w_in=v_w_in, v_conv_w=v_conv_w, v_conv_b=v_conv_b, v_w_a=v_w_a, v_b_a=v_b_a, v_w_x=v_w_x, v_b_x=v_b_x, v_lru_lambda=v_lru_lambda, v_w_pool=v_w_pool, v_b_pool=v_b_pool, v_pool_scale=v_pool_scale, v_w_out=v_w_out, v_ln1_g=v_ln1_g, v_ln1_b=v_ln1_b, v_w_q=v_w_q, v_w_k=v_w_k, v_w_v=v_w_v, v_w_o=v_w_o, v_ln2_g=v_ln2_g, v_ln2_b=v_ln2_b, v_w_ff1=v_w_ff1, v_w_ff2=v_w_ff2, v_ln3_g=v_ln3_g, v_ln3_b=v_ln3_b)
    weights = {n: given[n] for n in TWIN_WEIGHTS}
    shared = {n: given[n] for n in SHARED_INPUTS}
    per_example = {n: given[n] for n in ['x', 'mem']}
    grad_fn = _jax.value_and_grad(_loss, argnums=(0, 1))

    def one_microbatch(ex, loss_target):
        ex = dict(ex)
        diff = ex.pop(TWIN_DIFF_INPUT)
        return grad_fn(weights, diff, {**shared, **ex}, loss_target)

    if N_MICROBATCH == 1:
        loss, (grad_w, grad_x) = one_microbatch(per_example, given["loss_target"])
    else:
        def body(carry, xs):
            loss_sum, grad_sum = carry
            l_k, (gw_k, gx_k) = one_microbatch(xs[0], xs[1])
            with _jax.named_scope("update"):
                return (loss_sum + l_k, _jax.tree.map(_jnp.add, grad_sum, gw_k)), gx_k

        init = (_jnp.zeros((), _jnp.float32), _jax.tree.map(_jnp.zeros_like, weights))
        (loss, grad_w), grad_x = _jax.lax.scan(body, init, (per_example, given["loss_target"]))
    with _jax.named_scope("update"):
        delta_w, new_m, new_v = {}, {}, {}
        for n in TWIN_WEIGHTS:
            delta_w[n], new_m[n], new_v[n] = _adamw(weights[n], grad_w[n], given["m_" + n], given["v_" + n])
    return (loss, grad_x, *[grad_w[n] for n in TWIN_WEIGHTS], *[delta_w[n] for n in TWIN_WEIGHTS],
            *[new_m[n] for n in TWIN_WEIGHTS], *[new_v[n] for n in TWIN_WEIGHTS])
```

```python
import functools

import jax
import jax.numpy as jnp
from jax import lax
from jax.experimental import pallas as pl
from jax.experimental.pallas import tpu as pltpu

F32 = jnp.float32
BF16 = jnp.bfloat16

N_DEV = 8
MESH_AXES = ("x", "y", "c")
POOL_WINDOWS = (2, 4, 8, 16)
N_POOL_GROUPS = len(POOL_WINDOWS)
POOL_HALO = 16
CONV_WIDTH = 4
CONV_HALO = 8
LRU_HEADS = 8
LRU_C = 8.0
XATTN_HEADS = 4
LN_EPS = 1e-5
ALPHA = 2.0 ** 0.25
ADAM_LR = 0.001
ADAM_B1 = 0.9
ADAM_B2 = 0.999
ADAM_EPS = 1e-08
ADAM_WD = 0.01
ADAM_STEP = 10
SUBLANES = 8
VMEM_LIMIT = 56 * 1024 * 1024

NT_DIMS = (((1,), (1,)), ((), ()))
TN_DIMS = (((0,), (0,)), ((), ()))


def _params(*sem):
    return pltpu.CompilerParams(dimension_semantics=sem, vmem_limit_bytes=VMEM_LIMIT)


TILES = dict(tm=1024, tn=1024, tk=512, row=256, attn=512, mixer=256, adam=128)


def _tile(pref, n):
    for t in range(min(pref, n), 0, -1):
        if n % t == 0 and (t % SUBLANES == 0 or t == n):
            return t
    return n


def _mm_nn(a, b3, out_dtypes, name, *, tm=None, tn=None, tk=None, epilogue=None, extras=()):
    m, k = a.shape
    g, k2, ns = b3.shape
    assert k == k2
    n = g * ns
    tm, tn, tk = _tile(tm or TILES["tm"], m), _tile(tn or TILES["tn"], ns), _tile(tk or TILES["tk"], k)
    nb, nk = ns // tn, k // tk
    n_ex, n_out = len(extras), len(out_dtypes)

    def body(*refs):
        a_ref, b_ref = refs[:2]
        ex = refs[2:2 + n_ex]
        outs = refs[2 + n_ex:2 + n_ex + n_out]
        acc = refs[-1]
        kk = pl.program_id(2)

        @pl.when(kk == 0)
        def _():
            acc[...] = jnp.zeros_like(acc)

        acc[...] += jnp.dot(a_ref[...], b_ref[...], preferred_element_type=F32)

        @pl.when(kk == nk - 1)
        def _():
            r = acc[...]
            res = epilogue(r, *[e[...] for e in ex]) if epilogue is not None else (r,)
            for o, v in zip(outs, res):
                o[...] = v.astype(o.dtype)

    tile_out = pl.BlockSpec((tm, tn), lambda i, j, kk: (i, j))
    res = pl.pallas_call(
        body, grid=(m // tm, n // tn, nk),
        in_specs=[pl.BlockSpec((tm, tk), lambda i, j, kk: (i, kk)),
                  pl.BlockSpec((None, tk, tn), lambda i, j, kk: (j // nb, kk, j % nb))] + [tile_out] * n_ex,
        out_specs=[tile_out] * n_out,
        out_shape=[jax.ShapeDtypeStruct((m, n), d) for d in out_dtypes],
        scratch_shapes=[pltpu.VMEM((tm, tn), F32)],
        compiler_params=_params("parallel", "parallel", "arbitrary"), name=name,
    )(a, b3, *extras)
    return res


def _mm_nt(a, b3, out_dtypes, name, *, tm=None, tn=None, tk=None, epilogue=None, extras=()):
    m, n = a.shape
    g, k, ns = b3.shape
    assert n == g * ns
    tm, tn, tk = _tile(tm or TILES["tm"], m), _tile(tn or TILES["tn"], k), _tile(tk or TILES["tk"], ns)
    nb, nc = ns // tk, n // tk
    n_ex, n_out = len(extras), len(out_dtypes)

    def body(*refs):
        a_ref, b_ref = refs[:2]
        ex = refs[2:2 + n_ex]
        outs = refs[2 + n_ex:2 + n_ex + n_out]
        acc = refs[-1]
        cc = pl.program_id(2)

        @pl.when(cc == 0)
        def _():
            acc[...] = jnp.zeros_like(acc)

        acc[...] += lax.dot_general(a_ref[...], b_ref[...], NT_DIMS, preferred_element_type=F32)

        @pl.when(cc == nc - 1)
        def _():
            r = acc[...]
            res = epilogue(r, *[e[...] for e in ex]) if epilogue is not None else (r,)
            for o, v in zip(outs, res):
                o[...] = v.astype(o.dtype)

    tile_out = pl.BlockSpec((tm, tn), lambda i, j, cc: (i, j))
    res = pl.pallas_call(
        body, grid=(m // tm, k // tn, nc),
        in_specs=[pl.BlockSpec((tm, tk), lambda i, j, cc: (i, cc)),
                  pl.BlockSpec((None, tn, tk), lambda i, j, cc: (cc // nb, j, cc % nb))] + [tile_out] * n_ex,
        out_specs=[tile_out] * n_out,
        out_shape=[jax.ShapeDtypeStruct((m, k), d) for d in out_dtypes],
        scratch_shapes=[pltpu.VMEM((tm, tn), F32)],
        compiler_params=_params("parallel", "parallel", "arbitrary"), name=name,
    )(a, b3, *extras)
    return res


def _mm_tn(a, b, g, out_dtype, name, *, tm=None, tn=None, tk=None):
    s, m = a.shape
    s2, n = b.shape
    assert s == s2 and n % g == 0
    ns = n // g
    tm, tn, tk = _tile(tm or TILES["tm"], m), _tile(tn or TILES["tn"], ns), _tile(tk or TILES["tk"], s)
    nb, nc = ns // tn, s // tk

    def body(a_ref, b_ref, o_ref, acc):
        cc = pl.program_id(2)

        @pl.when(cc == 0)
        def _():
            acc[...] = jnp.zeros_like(acc)

        acc[...] += lax.dot_general(a_ref[...], b_ref[...], TN_DIMS, preferred_element_type=F32)

        @pl.when(cc == nc - 1)
        def _():
            o_ref[...] = acc[...].astype(o_ref.dtype)

    return pl.pallas_call(
        body, grid=(m // tm, n // tn, nc),
        in_specs=[pl.BlockSpec((tk, tm), lambda i, j, cc: (cc, i)),
                  pl.BlockSpec((tk, tn), lambda i, j, cc: (cc, j))],
        out_specs=pl.BlockSpec((None, tm, tn), lambda i, j, cc: (j // nb, i, j % nb)),
        out_shape=jax.ShapeDtypeStruct((g, m, ns), out_dtype),
        scratch_shapes=[pltpu.VMEM((tm, tn), F32)],
        compiler_params=_params("parallel", "parallel", "arbitrary"), name=name,
    )(a, b)


def _ln_stats(z):
    mu = jnp.mean(z, axis=-1, keepdims=True)
    zc = z - mu
    var = jnp.mean(zc * zc, axis=-1, keepdims=True)
    rstd = lax.rsqrt(var + LN_EPS)
    return zc * rstd, rstd


def _ln_grad(dout, xhat, rstd, gain):
    dxhat = dout * gain
    m1 = jnp.mean(dxhat, axis=-1, keepdims=True)
    m2 = jnp.mean(dxhat * xhat, axis=-1, keepdims=True)
    return rstd * (dxhat - m1 - xhat * m2)


def _ln_fwd(xres, y, gain, bias, name):
    s, d = xres.shape
    tr = _tile(TILES["row"], s)

    def body(x_ref, y_ref, g_ref, b_ref, xn_ref, xnb_ref, xhat_ref, rstd_ref):
        xhat, rstd = _ln_stats(ALPHA * x_ref[...] + y_ref[...])
        out = xhat * g_ref[...] + b_ref[...]
        xn_ref[...] = out
        xnb_ref[...] = out.astype(BF16)
        xhat_ref[...] = xhat
        rstd_ref[...] = rstd

    row = pl.BlockSpec((tr, d), lambda i: (i, 0))
    vec = pl.BlockSpec((1, d), lambda i: (0, 0))
    return pl.pallas_call(
        body, grid=(s // tr,), in_specs=[row, row, vec, vec],
        out_specs=[row, row, row, pl.BlockSpec((tr, 1), lambda i: (i, 0))],
        out_shape=[jax.ShapeDtypeStruct((s, d), F32), jax.ShapeDtypeStruct((s, d), BF16),
                   jax.ShapeDtypeStruct((s, d), F32), jax.ShapeDtypeStruct((s, 1), F32)],
        compiler_params=_params("parallel"), name=name)(xres, y, gain, bias)


def _ln_bwd(dout, xhat, rstd, gain, name):
    s, d = dout.shape
    tr = _tile(TILES["row"], s)

    def body(d_ref, xhat_ref, rstd_ref, g_ref, dz_ref, dzb_ref, dg_ref, db_ref):
        @pl.when(pl.program_id(0) == 0)
        def _():
            dg_ref[...] = jnp.zeros_like(dg_ref)
            db_ref[...] = jnp.zeros_like(db_ref)

        dout_t, xhat_t = d_ref[...], xhat_ref[...]
        dz = _ln_grad(dout_t, xhat_t, rstd_ref[...], g_ref[...])
        dz_ref[...] = dz
        dzb_ref[...] = dz.astype(BF16)
        dg_ref[...] += jnp.sum(dout_t * xhat_t, axis=0, keepdims=True)
        db_ref[...] += jnp.sum(dout_t, axis=0, keepdims=True)

    row = pl.BlockSpec((tr, d), lambda i: (i, 0))
    vec = pl.BlockSpec((1, d), lambda i: (0, 0))
    return pl.pallas_call(
        body, grid=(s // tr,), in_specs=[row, row, pl.BlockSpec((tr, 1), lambda i: (i, 0)), vec],
        out_specs=[row, row, vec, vec],
        out_shape=[jax.ShapeDtypeStruct((s, d), F32), jax.ShapeDtypeStruct((s, d), BF16),
                   jax.ShapeDtypeStruct((1, d), F32), jax.ShapeDtypeStruct((1, d), F32)],
        compiler_params=_params("arbitrary"), name=name)(dout, xhat, rstd, gain)


def _ln_loss(xres, y, gain, bias, target, name):
    s, d = xres.shape
    tr = _tile(TILES["row"], s)

    def body(x_ref, y_ref, g_ref, b_ref, t_ref, loss_ref, dz_ref, dzb_ref, dg_ref, db_ref):
        @pl.when(pl.program_id(0) == 0)
        def _():
            loss_ref[...] = jnp.zeros_like(loss_ref)
            dg_ref[...] = jnp.zeros_like(dg_ref)
            db_ref[...] = jnp.zeros_like(db_ref)

        xhat, rstd = _ln_stats(ALPHA * x_ref[...] + y_ref[...])
        diff = xhat * g_ref[...] + b_ref[...] - t_ref[...]
        per_row = jnp.mean(diff * diff, axis=-1, keepdims=True)
        loss_ref[...] += 0.5 * jnp.sum(per_row, axis=0, keepdims=True)
        dout = diff * (1.0 / d)
        dz = _ln_grad(dout, xhat, rstd, g_ref[...])
        dz_ref[...] = dz
        dzb_ref[...] = dz.astype(BF16)
        dg_ref[...] += jnp.sum(dout * xhat, axis=0, keepdims=True)
        db_ref[...] += jnp.sum(dout, axis=0, keepdims=True)

    row = pl.BlockSpec((tr, d), lambda i: (i, 0))
    vec = pl.BlockSpec((1, d), lambda i: (0, 0))
    return pl.pallas_call(
        body, grid=(s // tr,), in_specs=[row, row, vec, vec, row],
        out_specs=[pl.BlockSpec((1, 128), lambda i: (0, 0)), row, row, vec, vec],
        out_shape=[jax.ShapeDtypeStruct((1, 128), F32), jax.ShapeDtypeStruct((s, d), F32),
                   jax.ShapeDtypeStruct((s, d), BF16), jax.ShapeDtypeStruct((1, d), F32),
                   jax.ShapeDtypeStruct((1, d), F32)],
        compiler_params=_params("arbitrary"), name=name)(xres, y, gain, bias, target)


def _softmax_rows(s):
    e = jnp.exp(s - jnp.max(s, axis=-1, keepdims=True))
    return e / jnp.sum(e, axis=-1, keepdims=True)


def _attn_fwd(q, k, v, name):
    s, d = q.shape
    m = k.shape[0]
    hd = d // XATTN_HEADS
    ts = _tile(TILES["attn"], s)
    scale = hd ** -0.5

    def body(q_ref, k_ref, v_ref, o_ref):
        for h in range(XATTN_HEADS):
            hs = slice(h * hd, (h + 1) * hd)
            sc = lax.dot_general(q_ref[:, hs], k_ref[:, hs], NT_DIMS, preferred_element_type=F32) * scale
            p = _softmax_rows(sc).astype(BF16)
            o_ref[:, hs] = jnp.dot(p, v_ref[:, hs], preferred_element_type=F32).astype(BF16)

    row = pl.BlockSpec((ts, d), lambda i: (i, 0))
    memb = pl.BlockSpec((m, d), lambda i: (0, 0))
    return pl.pallas_call(
        body, grid=(s // ts,), in_specs=[row, memb, memb], out_specs=row,
        out_shape=jax.ShapeDtypeStruct((s, d), BF16),
        compiler_params=_params("parallel"), name=name)(q, k, v)


def _attn_bwd(q, k, v, do, name):
    s, d = q.shape
    m = k.shape[0]
    hd = d // XATTN_HEADS
    ts = _tile(TILES["attn"], s)
    scale = hd ** -0.5

    def body(q_ref, k_ref, v_ref, do_ref, dq_ref, dk_ref, dv_ref):
        @pl.when(pl.program_id(0) == 0)
        def _():
            dk_ref[...] = jnp.zeros_like(dk_ref)
            dv_ref[...] = jnp.zeros_like(dv_ref)

        for h in range(XATTN_HEADS):
            hs = slice(h * hd, (h + 1) * hd)
            qh, kh, vh, doh = q_ref[:, hs], k_ref[:, hs], v_ref[:, hs], do_ref[:, hs]
            sc = lax.dot_general(qh, kh, NT_DIMS, preferred_element_type=F32) * scale
            p = _softmax_rows(sc)
            pb = p.astype(BF16)
            dp = lax.dot_general(doh, vh, NT_DIMS, preferred_element_type=F32)
            ds = (p * (dp - jnp.sum(dp * p, axis=-1, keepdims=True)) * scale).astype(BF16)
            dq_ref[:, hs] = jnp.dot(ds, kh, preferred_element_type=F32).astype(BF16)
            dk_ref[:, hs] += lax.dot_general(ds, qh, TN_DIMS, preferred_element_type=F32)
            dv_ref[:, hs] += lax.dot_general(pb, doh, TN_DIMS, preferred_element_type=F32)

    row = pl.BlockSpec((ts, d), lambda i: (i, 0))
    memb = pl.BlockSpec((m, d), lambda i: (0, 0))
    return pl.pallas_call(
        body, grid=(s // ts,), in_specs=[row, memb, memb, row], out_specs=[row, memb, memb],
        out_shape=[jax.ShapeDtypeStruct((s, d), BF16), jax.ShapeDtypeStruct((m, d), F32),
                   jax.ShapeDtypeStruct((m, d), F32)],
        compiler_params=_params("arbitrary"), name=name)(q, k, v, do)


def _sigmoid(x):
    return 1.0 / (1.0 + jnp.exp(-x))


def _log1p(x):
    u = 1.0 + x
    return jnp.where(u == 1.0, x, jnp.log(u) * (x / jnp.where(u == 1.0, 1.0, u - 1.0)))


def _softplus(x):
    return jnp.maximum(x, 0.0) + _log1p(jnp.exp(-jnp.abs(x)))


def _expm1(x):
    series = x * (1.0 + x * 0.5 * (1.0 + x * (1.0 / 3.0) * (1.0 + x * 0.25 * (1.0 + x * 0.2 * (1.0 + x * (1.0 / 6.0))))))
    return jnp.where(jnp.abs(x) < 0.1, series, jnp.exp(x) - 1.0)


GELU_K = 0.7978845608028654
GELU_C = 0.044715


def _gelu(x):
    return 0.5 * x * (1.0 + jnp.tanh(GELU_K * (x + GELU_C * (x * x * x))))


def _gelu_grad(x):
    th = jnp.tanh(GELU_K * (x + GELU_C * (x * x * x)))
    return 0.5 * (1.0 + th) + 0.5 * x * (1.0 - th * th) * GELU_K * (1.0 + 3.0 * GELU_C * x * x)


def _window_sum(ext_ref, first, rows, cols, w, step):
    acc = ext_ref[first:first + rows, cols]
    for kk in range(1, w):
        acc = acc + ext_ref[first + step * kk:first + step * kk + rows, cols]
    return acc


def _lru_gates(c_s, wa_ref, ba_ref, wx_ref, bx_ref, lam_ref, t_idx, hd, r_s, i_s, a_s, mult_s):
    sp = _softplus(-lam_ref[...])
    for h in range(LRU_HEADS):
        hs = slice(h * hd, (h + 1) * hd)
        chb = c_s[:, hs].astype(BF16)
        r = _sigmoid(jnp.dot(chb, wa_ref[h], preferred_element_type=F32) + ba_ref[:, hs])
        ig = _sigmoid(jnp.dot(chb, wx_ref[h], preferred_element_type=F32) + bx_ref[:, hs])
        log_a = -LRU_C * r * sp[:, hs]
        mult = jnp.sqrt(-_expm1(2.0 * log_a))
        r_s[:, hs] = r
        i_s[:, hs] = ig
        a_s[:, hs] = jnp.exp(log_a)
        mult_s[:, hs] = jnp.where(t_idx == 0, 1.0, mult)


def _conv(ext_ref, cw_ref, cb_ref, rows):
    acc = cb_ref[...] + cw_ref[0:1, :] * ext_ref[CONV_HALO - 3:CONV_HALO - 3 + rows, :]
    for kk in range(1, CONV_WIDTH):
        off = CONV_HALO - (CONV_WIDTH - 1) + kk
        acc = acc + cw_ref[kk:kk + 1, :] * ext_ref[off:off + rows, :]
    return acc


def _mixer_fwd(proj, wp, bp, ps, cw, cb, wa, ba, wx, bx, lam, name):
    s, p3 = proj.shape
    p = p3 // 3
    cg, hd = p // N_POOL_GROUPS, p // LRU_HEADS
    t = _tile(TILES["mixer"], s)

    def body(up_ref, ul_ref, ug_ref, wp_ref, bp_ref, ps_ref, cw_ref, cb_ref, wa_ref, ba_ref, wx_ref, bx_ref,
             lam_ref, ycat_ref, h_ref, extp, extl, hc, c_s, r_s, i_s, a_s, b_s):
        i = pl.program_id(0)

        @pl.when(i == 0)
        def _():
            extp[0:POOL_HALO, :] = jnp.zeros((POOL_HALO, p), F32)
            extl[0:CONV_HALO, :] = jnp.zeros((CONV_HALO, p), F32)
            hc[...] = jnp.zeros_like(hc)

        t_idx = i * t + lax.broadcasted_iota(jnp.int32, (t, 1), 0)

        extp[POOL_HALO:POOL_HALO + t, :] = up_ref[...]
        for g, w in enumerate(POOL_WINDOWS):
            cs = slice(g * cg, (g + 1) * cg)
            cnt = jnp.minimum(t_idx + 1, w).astype(F32)
            mixed = _window_sum(extp, POOL_HALO, t, cs, w, -1) / cnt - up_ref[:, cs]
            pre = jnp.dot(mixed.astype(BF16), wp_ref[g], preferred_element_type=F32) + bp_ref[:, cs]
            ycat_ref[:, cs] = (pre * ps_ref[:, cs]).astype(BF16)
        extp[0:POOL_HALO, :] = extp[t:t + POOL_HALO, :]

        extl[CONV_HALO:CONV_HALO + t, :] = ul_ref[...]
        c_s[...] = _conv(extl, cw_ref, cb_ref, t)
        extl[0:CONV_HALO, :] = extl[t:t + CONV_HALO, :]
        _lru_gates(c_s, wa_ref, ba_ref, wx_ref, bx_ref, lam_ref, t_idx, hd, r_s, i_s, a_s, b_s)
        b_s[...] = b_s[...] * (i_s[...] * c_s[...])

        rows = lax.broadcasted_iota(jnp.int32, (SUBLANES, p), 0)

        def block(bi, h):
            r0 = pl.multiple_of(bi * SUBLANES, SUBLANES)
            at = a_s[pl.ds(r0, SUBLANES), :]
            bt = b_s[pl.ds(r0, SUBLANES), :]
            out = jnp.zeros((SUBLANES, p), F32)
            for j in range(SUBLANES):
                h = at[j:j + 1, :] * h + bt[j:j + 1, :]
                out = jnp.where(rows == j, h, out)
            h_ref[pl.ds(r0, SUBLANES), :] = out
            return h

        hc[0:1, :] = lax.fori_loop(0, t // SUBLANES, block, hc[0:1, :])
        ycat_ref[:, p:2 * p] = (h_ref[...] * _gelu(ug_ref[...])).astype(BF16)

    def col(j):
        return pl.BlockSpec((t, p), lambda i: (i, j))

    def whole(a):
        nd = a.ndim
        return pl.BlockSpec(a.shape, lambda i: (0,) * nd)

    consts = (wp, bp, ps, cw, cb, wa, ba, wx, bx, lam)
    tile = pltpu.VMEM((t, p), F32)
    return pl.pallas_call(
        body, grid=(s // t,), in_specs=[col(0), col(1), col(2)] + [whole(a) for a in consts],
        out_specs=[pl.BlockSpec((t, 2 * p), lambda i: (i, 0)), pl.BlockSpec((t, p), lambda i: (i, 0))],
        out_shape=[jax.ShapeDtypeStruct((s, 2 * p), BF16), jax.ShapeDtypeStruct((s, p), F32)],
        scratch_shapes=[pltpu.VMEM((t + POOL_HALO, p), F32), pltpu.VMEM((t + CONV_HALO, p), F32),
                        pltpu.VMEM((SUBLANES, p), F32), tile, tile, tile, tile, tile],
        compiler_params=_params("arbitrary"), name=name)(proj, proj, proj, *consts)


def _mixer_bwd(dycat, proj, hsave, wp, bp, ps, cw, cb, wa, ba, wx, bx, lam, name):
    s, p3 = proj.shape
    p = p3 // 3
    cg, hd = p // N_POOL_GROUPS, p // LRU_HEADS
    t = _tile(TILES["mixer"], s)
    nt = s // t

    def body(dyp_ref, dyl_ref, up_ref, ul_ref, ug_ref, upp_ref, ulp_ref, h_ref, hp_ref,
             wp_ref, bp_ref, ps_ref, cw_ref, cb_ref, wa_ref, ba_ref, wx_ref, bx_ref, lam_ref,
             dproj_ref, dwp_ref, dbp_ref, dps_ref, dcw_ref, dcb_ref, dwa_ref, dba_ref, dwx_ref, dbx_ref, dlam_ref,
             extp, extg, extl, extdc, exth, ghc, c_s, r_s, i_s, a_s, mult_s, gh_s):
        i = pl.program_id(0)
        ib = nt - 1 - i

        @pl.when(i == 0)
        def _():
            for ref in (dwp_ref, dbp_ref, dps_ref, dcw_ref, dcb_ref, dwa_ref, dba_ref, dwx_ref, dbx_ref, dlam_ref):
                ref[...] = jnp.zeros_like(ref)
            extg[t:t + POOL_HALO, :] = jnp.zeros((POOL_HALO, p), F32)
            extdc[t:t + CONV_HALO, :] = jnp.zeros((CONV_HALO, p), F32)
            ghc[...] = jnp.zeros_like(ghc)

        t_idx = ib * t + lax.broadcasted_iota(jnp.int32, (t, 1), 0)
        seq_start = ib == 0

        extl[0:CONV_HALO, :] = jnp.where(seq_start, 0.0, ulp_ref[...])
        extl[CONV_HALO:CONV_HALO + t, :] = ul_ref[...]
        c_s[...] = _conv(extl, cw_ref, cb_ref, t)
        _lru_gates(c_s, wa_ref, ba_ref, wx_ref, bx_ref, lam_ref, t_idx, hd, r_s, i_s, a_s, mult_s)
        exth[0:SUBLANES, :] = jnp.where(seq_start, 0.0, hp_ref[...])
        exth[SUBLANES:SUBLANES + t, :] = h_ref[...]

        ug = ug_ref[...]
        dyl = dyl_ref[...]
        dproj_ref[:, 2 * p:3 * p] = (dyl * h_ref[...] * _gelu_grad(ug)).astype(BF16)
        gh_s[...] = dyl * _gelu(ug)

        rows = lax.broadcasted_iota(jnp.int32, (SUBLANES, p), 0)
        nblk = t // SUBLANES

        def block(bi, carry):
            r0 = pl.multiple_of((nblk - 1 - bi) * SUBLANES, SUBLANES)
            at = a_s[pl.ds(r0, SUBLANES), :]
            dt = gh_s[pl.ds(r0, SUBLANES), :]
            out = jnp.zeros((SUBLANES, p), F32)
            for j in range(SUBLANES - 1, -1, -1):
                gh = dt[j:j + 1, :] + carry
                out = jnp.where(rows == j, gh, out)
                carry = at[j:j + 1, :] * gh
            gh_s[pl.ds(r0, SUBLANES), :] = out
            return carry

        ghc[0:1, :] = lax.fori_loop(0, nblk, block, ghc[0:1, :])

        sp = _softplus(-lam_ref[...])
        dsp_dlam = -_sigmoid(-lam_ref[...])
        for h in range(LRU_HEADS):
            hs = slice(h * hd, (h + 1) * hd)
            gh, a, mult, r, ig, c = gh_s[:, hs], a_s[:, hs], mult_s[:, hs], r_s[:, hs], i_s[:, hs], c_s[:, hs]
            hprev = exth[SUBLANES - 1:SUBLANES - 1 + t, hs]
            dmult = gh * (ig * c)
            dlog_a = a * gh * hprev + jnp.where(t_idx == 0, 0.0, -dmult * a * a / mult)
            dlam_ref[:, hs] += jnp.sum(dlog_a * r, axis=0, keepdims=True) * (-LRU_C) * dsp_dlam[:, hs]
            dpa = dlog_a * (-LRU_C * sp[:, hs]) * r * (1.0 - r)
            dpx = gh * mult * c * ig * (1.0 - ig)
            dpab, dpxb, chb = dpa.astype(BF16), dpx.astype(BF16), c.astype(BF16)
            dwa_ref[h] += lax.dot_general(chb, dpab, TN_DIMS, preferred_element_type=F32)
            dwx_ref[h] += lax.dot_general(chb, dpxb, TN_DIMS, preferred_element_type=F32)
            dba_ref[:, hs] += jnp.sum(dpa, axis=0, keepdims=True)
            dbx_ref[:, hs] += jnp.sum(dpx, axis=0, keepdims=True)
            dc = (gh * mult * ig
                  + lax.dot_general(dpab, wa_ref[h], NT_DIMS, preferred_element_type=F32)
                  + lax.dot_general(dpxb, wx_ref[h], NT_DIMS, preferred_element_type=F32))
            extdc[0:t, hs] = dc
            dcb_ref[:, hs] += jnp.sum(dc, axis=0, keepdims=True)
            for kk in range(CONV_WIDTH):
                off = CONV_HALO - (CONV_WIDTH - 1) + kk
                dcw_ref[kk:kk + 1, hs] += jnp.sum(dc * extl[off:off + t, hs], axis=0, keepdims=True)
        du_lru = cw_ref[0:1, :] * extdc[CONV_WIDTH - 1:CONV_WIDTH - 1 + t, :]
        for kk in range(1, CONV_WIDTH):
            off = CONV_WIDTH - 1 - kk
            du_lru = du_lru + cw_ref[kk:kk + 1, :] * extdc[off:off + t, :]
        dproj_ref[:, p:2 * p] = du_lru.astype(BF16)
        extdc[t:t + CONV_HALO, :] = extdc[0:CONV_HALO, :]

        extp[0:POOL_HALO, :] = jnp.where(seq_start, 0.0, upp_ref[...])
        extp[POOL_HALO:POOL_HALO + t, :] = up_ref[...]
        for g, w in enumerate(POOL_WINDOWS):
            cs = slice(g * cg, (g + 1) * cg)
            cnt = jnp.minimum(t_idx + 1, w).astype(F32)
            mixed = (_window_sum(extp, POOL_HALO, t, cs, w, -1) / cnt - up_ref[:, cs]).astype(BF16)
            pre = jnp.dot(mixed, wp_ref[g], preferred_element_type=F32) + bp_ref[:, cs]
            dyp = dyp_ref[:, cs]
            dps_ref[:, cs] += jnp.sum(dyp * pre, axis=0, keepdims=True)
            dpre = dyp * ps_ref[:, cs]
            dpreb = dpre.astype(BF16)
            dbp_ref[:, cs] += jnp.sum(dpre, axis=0, keepdims=True)
            dwp_ref[g] += lax.dot_general(mixed, dpreb, TN_DIMS, preferred_element_type=F32)
            dmixed = lax.dot_general(dpreb, wp_ref[g], NT_DIMS, preferred_element_type=F32)
            extg[0:t, cs] = dmixed / cnt
            dproj_ref[:, cs] = (_window_sum(extg, 0, t, cs, w, 1) - dmixed).astype(BF16)
        extg[t:t + POOL_HALO, :] = extg[0:POOL_HALO, :]

    def col(j):
        return pl.BlockSpec((t, p), lambda i: (nt - 1 - i, j))

    def prev(rows, j):
        per = t // rows
        return pl.BlockSpec((rows, p), lambda i: (jnp.maximum((nt - 1 - i) * per - 1, 0), j))

    def whole(a):
        nd = a.ndim
        return pl.BlockSpec(a.shape, lambda i: (0,) * nd)

    consts = (wp, bp, ps, cw, cb, wa, ba, wx, bx, lam)
    grads = (wp, bp, ps, cw, cb, wa, ba, wx, bx, lam)
    tile = pltpu.VMEM((t, p), F32)
    return pl.pallas_call(
        body, grid=(nt,),
        in_specs=[col(0), col(1), col(0), col(1), col(2), prev(POOL_HALO, 0), prev(CONV_HALO, 1), col(0),
                  prev(SUBLANES, 0)] + [whole(a) for a in consts],
        out_specs=[pl.BlockSpec((t, 3 * p), lambda i: (nt - 1 - i, 0))] + [whole(a) for a in grads],
        out_shape=[jax.ShapeDtypeStruct((s, 3 * p), BF16)] + [jax.ShapeDtypeStruct(a.shape, F32) for a in grads],
        scratch_shapes=[pltpu.VMEM((t + POOL_HALO, p), F32), pltpu.VMEM((t + POOL_HALO, p), F32),
                        pltpu.VMEM((t + CONV_HALO, p), F32), pltpu.VMEM((t + CONV_HALO, p), F32),
                        pltpu.VMEM((t + SUBLANES, p), F32), pltpu.VMEM((SUBLANES, p), F32),
                        tile, tile, tile, tile, tile, tile],
        compiler_params=_params("arbitrary"), name=name,
    )(dycat, dycat, proj, proj, proj, proj, proj, hsave, hsave, *consts)


def _place():
    return lax.axis_index("x"), lax.axis_index("y"), lax.axis_index("c")


def _all_gather(shard, name):
    def body(x_ref, out_ref, send_sems, recv_sems, local_sem):
        x, y, c = _place()
        me, sibling = (x, y, c), (x, y, 1 - c)
        chips = [(1 - x, y), (x, 1 - y), (1 - x, 1 - y)]

        def slot(px, py, pc):
            return out_ref.at[4 * px + 2 * py + pc]

        def copy(k, block, to, src=None):
            return pltpu.make_async_remote_copy(
                src_ref=slot(*block) if src is None else src, dst_ref=slot(*block),
                send_sem=send_sems.at[k], recv_sem=recv_sems.at[k],
                device_id=to, device_id_type=pl.DeviceIdType.MESH)

        mine = pltpu.make_async_copy(x_ref, slot(*me), local_sem)
        mine.start()
        first = [copy(0, me, sibling, src=x_ref)]
        first += [copy(1 + j, me, (*chip, c), src=x_ref) for j, chip in enumerate(chips)]
        for cp in first:
            cp.start()
        passed = [copy(4 + j, (*chip, c), sibling) for j, chip in enumerate(chips)]
        for j, chip in enumerate(chips):
            copy(1 + j, (*chip, c), me).wait_recv()
            passed[j].start()
        copy(0, sibling, me).wait_recv()
        for j, chip in enumerate(chips):
            copy(4 + j, (*chip, 1 - c), me).wait_recv()
        for cp in first + passed:
            cp.wait_send()
        mine.wait()

    return pl.pallas_call(
        body, out_shape=jax.ShapeDtypeStruct((N_DEV,) + shard.shape, shard.dtype),
        in_specs=[pl.BlockSpec(memory_space=pl.ANY)], out_specs=pl.BlockSpec(memory_space=pl.ANY),
        scratch_shapes=[pltpu.SemaphoreType.DMA((7,)), pltpu.SemaphoreType.DMA((7,)), pltpu.SemaphoreType.DMA],
        name=name)(shard)


def _exchange(parts, name):
    def body(p_ref, out_ref, send_sems, recv_sems, local_sem):
        x, y, c = _place()
        me = 4 * x + 2 * y + c
        mine = pltpu.make_async_copy(p_ref.at[me], out_ref.at[me], local_sem)
        mine.start()
        sends, recvs = [], []
        for rel in range(1, N_DEV):
            px = 1 - x if rel & 4 else x
            py = 1 - y if rel & 2 else y
            pc = 1 - c if rel & 1 else c
            peer = 4 * px + 2 * py + pc

            def copy(src_block, dst_block, rel=rel, to=(px, py, pc)):
                return pltpu.make_async_remote_copy(
                    src_ref=p_ref.at[src_block], dst_ref=out_ref.at[dst_block],
                    send_sem=send_sems.at[rel - 1], recv_sem=recv_sems.at[rel - 1],
                    device_id=to, device_id_type=pl.DeviceIdType.MESH)

            sends.append(copy(peer, me))
            recvs.append(copy(me, peer))
        for cp in sends:
            cp.start()
        for cp in recvs:
            cp.wait_recv()
        for cp in sends:
            cp.wait_send()
        mine.wait()

    return pl.pallas_call(
        body, out_shape=jax.ShapeDtypeStruct(parts.shape, parts.dtype),
        in_specs=[pl.BlockSpec(memory_space=pl.ANY)], out_specs=pl.BlockSpec(memory_space=pl.ANY),
        scratch_shapes=[pltpu.SemaphoreType.DMA((7,)), pltpu.SemaphoreType.DMA((7,)), pltpu.SemaphoreType.DMA],
        name=name)(parts)


def _sum_parts(parts, name):
    n, r, c = parts.shape
    tr = _tile(TILES["adam"], r)

    def body(p_ref, o_ref):
        acc = p_ref[0].astype(F32)
        for d in range(1, n):
            acc = acc + p_ref[d].astype(F32)
        o_ref[...] = acc

    return pl.pallas_call(
        body, grid=(r // tr,), in_specs=[pl.BlockSpec((n, tr, c), lambda i: (0, i, 0))],
        out_specs=pl.BlockSpec((tr, c), lambda i: (i, 0)), out_shape=jax.ShapeDtypeStruct((r, c), F32),
        compiler_params=_params("parallel"), name=name)(parts)


def _adamw(w, m, v, parts, name):
    r, c = w.shape
    n = parts.shape[0]
    tr = _tile(TILES["adam"], r)

    def body(w_ref, m_ref, v_ref, p_ref, g_ref, d_ref, nm_ref, nv_ref):
        g = p_ref[0].astype(F32)
        for d in range(1, n):
            g = g + p_ref[d].astype(F32)
        nm = ADAM_B1 * m_ref[...] + (1.0 - ADAM_B1) * g
        nv = ADAM_B2 * v_ref[...] + (1.0 - ADAM_B2) * (g * g)
        m_hat = nm / (1.0 - ADAM_B1 ** ADAM_STEP)
        v_hat = nv / (1.0 - ADAM_B2 ** ADAM_STEP)
        g_ref[...] = g
        d_ref[...] = -ADAM_LR * (m_hat / (jnp.sqrt(v_hat) + ADAM_EPS) + ADAM_WD * w_ref[...])
        nm_ref[...] = nm
        nv_ref[...] = nv

    row = pl.BlockSpec((tr, c), lambda i: (i, 0))
    return pl.pallas_call(
        body, grid=(r // tr,), in_specs=[row, row, row, pl.BlockSpec((n, tr, c), lambda i: (0, i, 0))],
        out_specs=[row] * 4, out_shape=[jax.ShapeDtypeStruct((r, c), F32)] * 4,
        compiler_params=_params("parallel"), name=name)(w, m, v, parts)


def _local_step(x, mem, tgt, W):
    d = x.shape[1]
    xb = x.astype(BF16)
    memb = mem.astype(BF16)

    (proj,) = _mm_nn(xb, W["w_in"], [F32], "fwd_proj")
    ycat, hsave = _mixer_fwd(proj, W["w_pool"], W["b_pool"], W["pool_scale"], W["conv_w"], W["conv_b"],
                             W["w_a"], W["b_a"], W["w_x"], W["b_x"], W["lru_lambda"], "fwd_mixer")
    (y1,) = _mm_nn(ycat, W["w_out"], [F32], "fwd_out")
    x1, x1b, xhat1, rstd1 = _ln_fwd(x, y1, W["ln1_g"], W["ln1_b"], "fwd_ln1")
    (q,) = _mm_nn(x1b, W["w_q"], [BF16], "fwd_q")
    (k,) = _mm_nn(memb, W["w_k"], [BF16], "fwd_k")
    (v,) = _mm_nn(memb, W["w_v"], [BF16], "fwd_v")
    o = _attn_fwd(q, k, v, "fwd_attn")
    (y2,) = _mm_nn(o, W["w_o"], [F32], "fwd_o")
    x2, x2b, xhat2, rstd2 = _ln_fwd(x1, y2, W["ln2_g"], W["ln2_b"], "fwd_ln2")

    def relu_sq(acc):
        r = jnp.maximum(acc, 0.0)
        return r, r * r

    rb, act = _mm_nn(x2b, W["w_ff1"], [BF16, BF16], "fwd_ff1", epilogue=relu_sq)
    (y3,) = _mm_nn(act, W["w_ff2"], [F32], "fwd_ff2")
    loss, dz3, dz3b, dg3, db3 = _ln_loss(x2, y3, W["ln3_g"], W["ln3_b"], tgt, "ln3_loss")

    big, small = {}, {"ln3_g": dg3, "ln3_b": db3}
    big["w_ff2"] = _mm_tn(act, dz3b, 1, BF16, "bwd_dw_ff2")
    (dhid,) = _mm_nt(dz3b, W["w_ff2"], [BF16], "bwd_dact", epilogue=lambda acc, r: (acc * (2.0 * r.astype(F32)),),
                     extras=(rb,))
    big["w_ff1"] = _mm_tn(x2b, dhid, N_DEV, BF16, "bwd_dw_ff1")
    (dx2,) = _mm_nt(dhid, W["w_ff1"], [F32], "bwd_dx2", epilogue=lambda acc, e: (acc + ALPHA * e,), extras=(dz3,))
    dz2, dz2b, small["ln2_g"], small["ln2_b"] = _ln_bwd(dx2, xhat2, rstd2, W["ln2_g"], "bwd_ln2")

    big["w_o"] = _mm_tn(o, dz2b, 1, BF16, "bwd_dw_o")
    (do,) = _mm_nt(dz2b, W["w_o"], [BF16], "bwd_do")
    dq, dk, dv = _attn_bwd(q, k, v, do, "bwd_attn")
    big["w_q"] = _mm_tn(x1b, dq, 1, BF16, "bwd_dw_q")
    big["w_k"] = _mm_tn(memb, dk.astype(BF16), 1, BF16, "bwd_dw_k")
    big["w_v"] = _mm_tn(memb, dv.astype(BF16), 1, BF16, "bwd_dw_v")
    (dx1,) = _mm_nt(dq, W["w_q"], [F32], "bwd_dx1", epilogue=lambda acc, e: (acc + ALPHA * e,), extras=(dz2,))
    dz1, dz1b, small["ln1_g"], small["ln1_b"] = _ln_bwd(dx1, xhat1, rstd1, W["ln1_g"], "bwd_ln1")

    big["w_out"] = _mm_tn(ycat, dz1b, 1, BF16, "bwd_dw_out")
    (dycat,) = _mm_nt(dz1b, W["w_out"], [F32], "bwd_dycat")
    (dproj, dwp, small["b_pool"], small["pool_scale"], small["conv_w"], small["conv_b"], small["w_a"], small["b_a"],
     small["w_x"], small["b_x"], small["lru_lambda"]) = _mixer_bwd(
        dycat, proj, hsave, W["w_pool"], W["b_pool"], W["pool_scale"], W["conv_w"], W["conv_b"],
        W["w_a"], W["b_a"], W["w_x"], W["b_x"], W["lru_lambda"], "bwd_mixer")
    big["w_pool"] = dwp
    big["w_in"] = _mm_tn(xb, dproj, N_DEV, BF16, "bwd_dw_in")
    (grad_x,) = _mm_nt(dproj, W["w_in"], [F32], "bwd_dx", epilogue=lambda acc, e: (acc + ALPHA * e,), extras=(dz1,))
    return loss[0, 0], grad_x, big, small


SMALL_ORDER = ("w_a", "w_x", "conv_w", "b_pool", "conv_b", "b_a", "b_x", "lru_lambda", "pool_scale",
               "ln1_g", "ln1_b", "ln2_g", "ln2_b", "ln3_g", "ln3_b")


def _pack_rows(a, p):
    flat = a.reshape(-1, p)
    pad = (-flat.shape[0]) % SUBLANES
    return jnp.pad(flat, ((0, pad), (0, 0))) if pad else flat


def kernel(x, mem, w_in, conv_w, conv_b, w_a, b_a, w_x, b_x, lru_lambda, w_pool, b_pool, pool_scale, w_out, ln1_g, ln1_b, w_q, w_k, w_v, w_o, ln2_g, ln2_b, w_ff1, w_ff2, ln3_g, ln3_b, loss_target, m_w_in, m_conv_w, m_conv_b, m_w_a, m_b_a, m_w_x, m_b_x, m_lru_lambda, m_w_pool, m_b_pool, m_pool_scale, m_w_out, m_ln1_g, m_ln1_b, m_w_q, m_w_k, m_w_v, m_w_o, m_ln2_g, m_ln2_b, m_w_ff1, m_w_ff2, m_ln3_g, m_ln3_b, v_w_in, v_conv_w, v_conv_b, v_w_a, v_b_a, v_w_x, v_b_x, v_lru_lambda, v_w_pool, v_b_pool, v_pool_scale, v_w_out, v_ln1_g, v_ln1_b, v_w_q, v_w_k, v_w_v, v_w_o, v_ln2_g, v_ln2_b, v_w_ff1, v_w_ff2, v_ln3_g, v_ln3_b):
    names = ("w_in", "conv_w", "conv_b", "w_a", "b_a", "w_x", "b_x", "lru_lambda", "w_pool", "b_pool", "pool_scale",
             "w_out", "ln1_g", "ln1_b", "w_q", "w_k", "w_v", "w_o", "ln2_g", "ln2_b", "w_ff1", "w_ff2", "ln3_g", "ln3_b")
    w_loc = dict(zip(names, (w_in, conv_w, conv_b, w_a, b_a, w_x, b_x, lru_lambda, w_pool, b_pool, pool_scale,
                             w_out, ln1_g, ln1_b, w_q, w_k, w_v, w_o, ln2_g, ln2_b, w_ff1, w_ff2, ln3_g, ln3_b)))
    m_loc = dict(zip(names, (m_w_in, m_conv_w, m_conv_b, m_w_a, m_b_a, m_w_x, m_b_x, m_lru_lambda, m_w_pool, m_b_pool,
                             m_pool_scale, m_w_out, m_ln1_g, m_ln1_b, m_w_q, m_w_k, m_w_v, m_w_o, m_ln2_g, m_ln2_b,
                             m_w_ff1, m_w_ff2, m_ln3_g, m_ln3_b)))
    v_loc = dict(zip(names, (v_w_in, v_conv_w, v_conv_b, v_w_a, v_b_a, v_w_x, v_b_x, v_lru_lambda, v_w_pool, v_b_pool,
                             v_pool_scale, v_w_out, v_ln1_g, v_ln1_b, v_w_q, v_w_k, v_w_v, v_w_o, v_ln2_g, v_ln2_b,
                             v_w_ff1, v_w_ff2, v_ln3_g, v_ln3_b)))
    s, d = x.shape[1], x.shape[2]
    p = conv_b.shape[1]
    cg = p // N_POOL_GROUPS
    hd = p // LRU_HEADS
    me = 4 * lax.axis_index("x") + 2 * lax.axis_index("y") + lax.axis_index("c")

    W = {}
    for n in ("w_in", "w_out", "w_q", "w_k", "w_v", "w_o", "w_ff1", "w_ff2"):
        W[n] = _all_gather(w_loc[n][0].astype(BF16), "gather_" + n)
    for n in ("w_out", "w_q", "w_k", "w_v", "w_o", "w_ff2"):
        W[n] = W[n].reshape(1, -1, W[n].shape[-1])
    wp_g = _all_gather(w_pool[0].astype(BF16), "gather_w_pool")
    W["w_pool"] = jnp.transpose(wp_g, (1, 0, 2, 3)).reshape(N_POOL_GROUPS, cg, cg)
    tiny = jnp.concatenate([_pack_rows(conv_w[0], p // N_DEV),
                            _pack_rows(jnp.pad(b_pool[0], ((0, 0), (0, p // N_DEV - cg // N_DEV))), p // N_DEV)], axis=0)
    cwb = _all_gather(tiny, "gather_tiny")
    W["conv_w"] = jnp.transpose(cwb[:, :CONV_WIDTH, :], (1, 0, 2)).reshape(CONV_WIDTH, p)
    W["b_pool"] = jnp.transpose(cwb[:, SUBLANES:SUBLANES + N_POOL_GROUPS, :cg // N_DEV], (1, 0, 2)).reshape(1, p)
    W["conv_b"], W["b_a"], W["b_x"] = conv_b, b_a.reshape(1, p), b_x.reshape(1, p)
    W["lru_lambda"], W["pool_scale"] = lru_lambda, pool_scale
    W["w_a"], W["w_x"] = w_a[0].astype(BF16), w_x[0].astype(BF16)
    for n in ("ln1_g", "ln1_b", "ln2_g", "ln2_b", "ln3_g", "ln3_b"):
        W[n] = w_loc[n]

    loss_me, grad_x, big, small = _local_step(x[0], mem[0], loss_target[0], W)
    loss = lax.psum(loss_me, MESH_AXES)

    out_g, out_d, out_m, out_v = {}, {}, {}, {}

    def update(n, parts):
        shp = w_loc[n].shape
        rows = parts.shape[1]
        w2, m2, v2 = (a.reshape(rows, -1) for a in (w_loc[n], m_loc[n], v_loc[n]))
        res = _adamw(w2, m2, v2, parts.reshape(parts.shape[0], rows, -1), "adamw_" + n)
        out_g[n], out_d[n], out_m[n], out_v[n] = (r.reshape(shp) for r in res)

    big["w_pool"] = jnp.transpose(big["w_pool"].astype(BF16).reshape(N_POOL_GROUPS, N_DEV, cg // N_DEV, cg),
                                  (1, 0, 2, 3)).reshape(N_DEV, N_POOL_GROUPS * (cg // N_DEV), cg)
    for n in ("w_out", "w_q", "w_k", "w_v", "w_o", "w_ff2"):
        big[n] = big[n].reshape(N_DEV, -1, big[n].shape[-1])
    for n in ("w_ff2", "w_ff1", "w_o", "w_q", "w_k", "w_v", "w_out", "w_pool", "w_in"):
        update(n, _exchange(big[n], "exchange_" + n))

    pack = jnp.concatenate([_pack_rows(small[n], p) for n in SMALL_ORDER], axis=0)
    total = _sum_parts(_all_gather(pack, "gather_small"), "sum_small")
    row = 0
    for n in SMALL_ORDER:
        size = small[n].size
        nrows = size // p
        g_full = total[row:row + nrows].reshape(small[n].shape)
        row += nrows + (-nrows) % SUBLANES
        if n == "conv_w":
            g_loc = lax.dynamic_slice_in_dim(g_full, me * (p // N_DEV), p // N_DEV, axis=1)
        elif n == "b_pool":
            g_loc = lax.dynamic_slice_in_dim(g_full.reshape(N_POOL_GROUPS, cg), me * (cg // N_DEV), cg // N_DEV, axis=1)
        else:
            g_loc = g_full
        rows = g_loc.shape[0] if n not in ("w_a", "w_x") else LRU_HEADS * hd
        update(n, g_loc.reshape(1, rows, -1))

    order = names
    return (loss, grad_x[None], *[out_g[n] for n in order], *[out_d[n] for n in order],
            *[out_m[n] for n in order], *[out_v[n] for n in order])
```

```python
import functools

import jax
import jax.numpy as jnp
from jax import lax
from jax.experimental import pallas as pl
from jax.experimental.pallas import tpu as pltpu

F32 = jnp.float32
BF16 = jnp.bfloat16

N_DEV = 8
MESH_AXES = ("x", "y", "c")
POOL_WINDOWS = (2, 4, 8, 16)
N_POOL_GROUPS = len(POOL_WINDOWS)
POOL_HALO = 16
CONV_WIDTH = 4
CONV_HALO = 8
LRU_HEADS = 8
LRU_C = 8.0
XATTN_HEADS = 4
LN_EPS = 1e-5
ALPHA = 2.0 ** 0.25
ADAM_LR = 0.001
ADAM_B1 = 0.9
ADAM_B2 = 0.999
ADAM_EPS = 1e-08
ADAM_WD = 0.01
ADAM_STEP = 10
SUBLANES = 8
VMEM_LIMIT = 56 * 1024 * 1024

NT_DIMS = (((1,), (1,)), ((), ()))
TN_DIMS = (((0,), (0,)), ((), ()))


def _params(*sem):
    return pltpu.CompilerParams(dimension_semantics=sem, vmem_limit_bytes=VMEM_LIMIT)


def _place():
    return lax.axis_index("x"), lax.axis_index("y"), lax.axis_index("c")


def _remote(src, dst, send_sem, recv_sem, to):
    return pltpu.make_async_remote_copy(src_ref=src, dst_ref=dst, send_sem=send_sem, recv_sem=recv_sem,
                                        device_id=to, device_id_type=pl.DeviceIdType.MESH)


class _Job:
    N_SEMS = {"gather": (7, 7, 1), "pair": (4, 4, 4), "quad": (3, 3, 1)}

    def __init__(self, kind, src):
        self.kind, self.src, self.outs = kind, src, None

    def out_shapes(self):
        s = self.src.shape
        shapes = {"gather": [(N_DEV,) + s], "pair": [(4,) + s[1:]] * 2, "quad": [s]}[self.kind]
        return [jax.ShapeDtypeStruct(shp, self.src.dtype) for shp in shapes]

    def sem_shapes(self):
        return [pltpu.SemaphoreType.DMA((n,)) for n in self.N_SEMS[self.kind]]

    def ops(self, src, outs, send_sems, recv_sems, local_sems):
        return {"gather": _gather_ops, "pair": _pair_ops, "quad": _quad_ops}[self.kind](
            src, outs, send_sems, recv_sems, local_sems)


def _gather_ops(x_ref, outs, send_sems, recv_sems, local_sems):
    (out_ref,) = outs
    x, y, c = _place()
    me, sibling = (x, y, c), (x, y, 1 - c)
    chips = [(1 - x, y), (x, 1 - y), (1 - x, 1 - y)]

    def slot(px, py, pc):
        return out_ref.at[4 * px + 2 * py + pc]

    def copy(k, block, to, src=None):
        return _remote(slot(*block) if src is None else src, slot(*block), send_sems.at[k], recv_sems.at[k], to)

    mine = pltpu.make_async_copy(x_ref, slot(*me), local_sems.at[0])
    first = [copy(0, me, sibling, src=x_ref)] + [copy(1 + j, me, (*chip, c), src=x_ref) for j, chip in enumerate(chips)]
    passed = [copy(4 + j, (*chip, c), sibling) for j, chip in enumerate(chips)]

    def start():
        mine.start()
        for cp in first:
            cp.start()

    def finish():
        for j, chip in enumerate(chips):
            copy(1 + j, (*chip, c), me).wait_recv()
            passed[j].start()
        copy(0, sibling, me).wait_recv()
        for j, chip in enumerate(chips):
            copy(4 + j, (*chip, 1 - c), me).wait_recv()
        for cp in first + passed:
            cp.wait_send()
        mine.wait()

    return start, finish


def _pair_ops(p_ref, outs, send_sems, recv_sems, local_sems):
    own_ref, got_ref = outs
    x, y, c = _place()
    keep = [pltpu.make_async_copy(p_ref.at[2 * k + c], own_ref.at[k], local_sems.at[k]) for k in range(4)]
    give = [_remote(p_ref.at[2 * k + 1 - c], got_ref.at[k], send_sems.at[k], recv_sems.at[k], (x, y, 1 - c))
            for k in range(4)]

    def start():
        for cp in keep + give:
            cp.start()

    def finish():
        for cp in give:
            cp.wait_recv()
        for cp in give:
            cp.wait_send()
        for cp in keep:
            cp.wait()

    return start, finish


def _quad_ops(q_ref, outs, send_sems, recv_sems, local_sems):
    (out_ref,) = outs
    x, y, c = _place()
    chip = 2 * x + y
    mine = pltpu.make_async_copy(q_ref.at[chip], out_ref.at[chip], local_sems.at[0])
    sends, recvs = [], []
    for rel in range(1, 4):
        px = 1 - x if rel & 2 else x
        py = 1 - y if rel & 1 else y
        peer = 2 * px + py
        sends.append(_remote(q_ref.at[peer], out_ref.at[chip], send_sems.at[rel - 1], recv_sems.at[rel - 1], (px, py, c)))
        recvs.append(_remote(q_ref.at[chip], out_ref.at[peer], send_sems.at[rel - 1], recv_sems.at[rel - 1], (px, py, c)))

    def start():
        mine.start()
        for cp in sends:
            cp.start()

    def finish():
        for cp in recvs:
            cp.wait_recv()
        for cp in sends:
            cp.wait_send()
        mine.wait()

    return start, finish


def _call(body, *, grid, in_specs, out_specs, out_shape, scratch_shapes=(), semantics, name, args, jobs=()):
    in_specs, out_specs, out_shape = list(in_specs), list(out_specs), list(out_shape)
    scratch_shapes, jobs = list(scratch_shapes), list(jobs)
    if not jobs:
        return pl.pallas_call(body, grid=grid, in_specs=in_specs, out_specs=out_specs, out_shape=out_shape,
                              scratch_shapes=scratch_shapes, compiler_params=_params(*semantics), name=name)(*args)
    n_in, n_out, n_scr, n_job = len(in_specs), len(out_specs), len(scratch_shapes), len(jobs)
    job_outs = [j.out_shapes() for j in jobs]
    n_jout = sum(len(o) for o in job_outs)

    def hosted(*refs):
        ins, jin = refs[:n_in], refs[n_in:n_in + n_job]
        o0 = n_in + n_job
        outs, jout = refs[o0:o0 + n_out], refs[o0 + n_out:o0 + n_out + n_jout]
        s0 = o0 + n_out + n_jout
        scr, jsem = refs[s0:s0 + n_scr], refs[s0 + n_scr:]
        ops, at = [], 0
        for k, j in enumerate(jobs):
            ops.append(j.ops(jin[k], jout[at:at + len(job_outs[k])], *jsem[3 * k:3 * k + 3]))
            at += len(job_outs[k])
        first = functools.reduce(jnp.logical_and, [pl.program_id(a) == 0 for a in range(len(grid))])
        last = functools.reduce(jnp.logical_and, [pl.program_id(a) == g - 1 for a, g in enumerate(grid)])

        @pl.when(first)
        def _():
            for start, _ in ops:
                start()

        body(*ins, *outs, *scr)

        @pl.when(last)
        def _():
            for _, finish in ops:
                finish()

    hbm = pl.BlockSpec(memory_space=pl.ANY)
    res = pl.pallas_call(
        hosted, grid=grid, in_specs=in_specs + [hbm] * n_job, out_specs=out_specs + [hbm] * n_jout,
        out_shape=out_shape + [s for o in job_outs for s in o],
        scratch_shapes=scratch_shapes + [s for j in jobs for s in j.sem_shapes()],
        compiler_params=_params(*["arbitrary"] * len(grid)), name=name)(*args, *[j.src for j in jobs])
    at = n_out
    for j, o in zip(jobs, job_outs):
        j.outs = res[at:at + len(o)]
        at += len(o)
    return res[:n_out]


def _run_jobs(jobs, name):
    def body(tick_ref):
        tick_ref[...] = jnp.zeros_like(tick_ref)

    _call(body, grid=(1,), in_specs=[], out_specs=[pl.BlockSpec((SUBLANES, 128), lambda i: (0, 0))],
          out_shape=[jax.ShapeDtypeStruct((SUBLANES, 128), F32)], semantics=("arbitrary",), name=name, args=(), jobs=jobs)


TILES = dict(tm=1024, tn=1024, tk=512, row=256, attn=512, mixer=256, adam=128)


def _tile(pref, n):
    for t in range(min(pref, n), 0, -1):
        if n % t == 0 and (t % SUBLANES == 0 or t == n):
            return t
    return n


def _mm_nn(a, b3, out_dtypes, name, *, tm=None, tn=None, tk=None, epilogue=None, extras=(), jobs=()):
    m, k = a.shape
    g, k2, ns = b3.shape
    assert k == k2
    n = g * ns
    tm, tn, tk = _tile(tm or TILES["tm"], m), _tile(tn or TILES["tn"], ns), _tile(tk or TILES["tk"], k)
    nb, nk = ns // tn, k // tk
    n_ex, n_out = len(extras), len(out_dtypes)

    def body(*refs):
        a_ref, b_ref = refs[:2]
        ex = refs[2:2 + n_ex]
        outs = refs[2 + n_ex:2 + n_ex + n_out]
        acc = refs[-1]
        kk = pl.program_id(2)

        @pl.when(kk == 0)
        def _():
            acc[...] = jnp.zeros_like(acc)

        acc[...] += jnp.dot(a_ref[...], b_ref[...], preferred_element_type=F32)

        @pl.when(kk == nk - 1)
        def _():
            r = acc[...]
            res = epilogue(r, *[e[...] for e in ex]) if epilogue is not None else (r,)
            for o, v in zip(outs, res):
                o[...] = v.astype(o.dtype)

    tile_out = pl.BlockSpec((tm, tn), lambda i, j, kk: (i, j))
    return _call(
        body, grid=(m // tm, n // tn, nk),
        in_specs=[pl.BlockSpec((tm, tk), lambda i, j, kk: (i, kk)),
                  pl.BlockSpec((None, tk, tn), lambda i, j, kk: (j // nb, kk, j % nb))] + [tile_out] * n_ex,
        out_specs=[tile_out] * n_out,
        out_shape=[jax.ShapeDtypeStruct((m, n), d) for d in out_dtypes],
        scratch_shapes=[pltpu.VMEM((tm, tn), F32)],
        semantics=("parallel", "parallel", "arbitrary"), name=name, args=(a, b3, *extras), jobs=jobs)


def _mm_nt(a, b3, out_dtypes, name, *, tm=None, tn=None, tk=None, epilogue=None, extras=(), jobs=()):
    m, n = a.shape
    g, k, ns = b3.shape
    assert n == g * ns
    tm, tn, tk = _tile(tm or TILES["tm"], m), _tile(tn or TILES["tn"], k), _tile(tk or TILES["tk"], ns)
    nb, nc = ns // tk, n // tk
    n_ex, n_out = len(extras), len(out_dtypes)

    def body(*refs):
        a_ref, b_ref = refs[:2]
        ex = refs[2:2 + n_ex]
        outs = refs[2 + n_ex:2 + n_ex + n_out]
        acc = refs[-1]
        cc = pl.program_id(2)

        @pl.when(cc == 0)
        def _():
            acc[...] = jnp.zeros_like(acc)

        acc[...] += lax.dot_general(a_ref[...], b_ref[...], NT_DIMS, preferred_element_type=F32)

        @pl.when(cc == nc - 1)
        def _():
            r = acc[...]
            res = epilogue(r, *[e[...] for e in ex]) if epilogue is not None else (r,)
            for o, v in zip(outs, res):
                o[...] = v.astype(o.dtype)

    tile_out = pl.BlockSpec((tm, tn), lambda i, j, cc: (i, j))
    return _call(
        body, grid=(m // tm, k // tn, nc),
        in_specs=[pl.BlockSpec((tm, tk), lambda i, j, cc: (i, cc)),
                  pl.BlockSpec((None, tn, tk), lambda i, j, cc: (cc // nb, j, cc % nb))] + [tile_out] * n_ex,
        out_specs=[tile_out] * n_out,
        out_shape=[jax.ShapeDtypeStruct((m, k), d) for d in out_dtypes],
        scratch_shapes=[pltpu.VMEM((tm, tn), F32)],
        semantics=("parallel", "parallel", "arbitrary"), name=name, args=(a, b3, *extras), jobs=jobs)


def _mm_tn(a, b, g, out_dtype, name, *, tm=None, tn=None, tk=None, jobs=()):
    s, m = a.shape
    s2, n = b.shape
    assert s == s2 and n % g == 0
    ns = n // g
    tm, tn, tk = _tile(tm or TILES["tm"], m), _tile(tn or TILES["tn"], ns), _tile(tk or TILES["tk"], s)
    nb, nc = ns // tn, s // tk

    def body(a_ref, b_ref, o_ref, acc):
        cc = pl.program_id(2)

        @pl.when(cc == 0)
        def _():
            acc[...] = jnp.zeros_like(acc)

        acc[...] += lax.dot_general(a_ref[...], b_ref[...], TN_DIMS, preferred_element_type=F32)

        @pl.when(cc == nc - 1)
        def _():
            o_ref[...] = acc[...].astype(o_ref.dtype)

    return _call(
        body, grid=(m // tm, n // tn, nc),
        in_specs=[pl.BlockSpec((tk, tm), lambda i, j, cc: (cc, i)),
                  pl.BlockSpec((tk, tn), lambda i, j, cc: (cc, j))],
        out_specs=[pl.BlockSpec((None, tm, tn), lambda i, j, cc: (j // nb, i, j % nb))],
        out_shape=[jax.ShapeDtypeStruct((g, m, ns), out_dtype)],
        scratch_shapes=[pltpu.VMEM((tm, tn), F32)],
        semantics=("parallel", "parallel", "arbitrary"), name=name, args=(a, b), jobs=jobs)[0]


def _ln_stats(z):
    mu = jnp.mean(z, axis=-1, keepdims=True)
    zc = z - mu
    var = jnp.mean(zc * zc, axis=-1, keepdims=True)
    rstd = lax.rsqrt(var + LN_EPS)
    return zc * rstd, rstd


def _ln_grad(dout, xhat, rstd, gain):
    dxhat = dout * gain
    m1 = jnp.mean(dxhat, axis=-1, keepdims=True)
    m2 = jnp.mean(dxhat * xhat, axis=-1, keepdims=True)
    return rstd * (dxhat - m1 - xhat * m2)


def _ln_fwd(xres, y, gain, bias, name, jobs=()):
    s, d = xres.shape
    tr = _tile(TILES["row"], s)

    def body(x_ref, y_ref, g_ref, b_ref, xn_ref, xnb_ref, xhat_ref, rstd_ref):
        xhat, rstd = _ln_stats(ALPHA * x_ref[...] + y_ref[...])
        out = xhat * g_ref[...] + b_ref[...]
        xn_ref[...] = out
        xnb_ref[...] = out.astype(BF16)
        xhat_ref[...] = xhat
        rstd_ref[...] = rstd

    row = pl.BlockSpec((tr, d), lambda i: (i, 0))
    vec = pl.BlockSpec((1, d), lambda i: (0, 0))
    return _call(
        body, grid=(s // tr,), in_specs=[row, row, vec, vec],
        out_specs=[row, row, row, pl.BlockSpec((tr, 1), lambda i: (i, 0))],
        out_shape=[jax.ShapeDtypeStruct((s, d), F32), jax.ShapeDtypeStruct((s, d), BF16),
                   jax.ShapeDtypeStruct((s, d), F32), jax.ShapeDtypeStruct((s, 1), F32)],
        semantics=("parallel",), name=name, args=(xres, y, gain, bias), jobs=jobs)


def _ln_bwd(dout, xhat, rstd, gain, name, jobs=()):
    s, d = dout.shape
    tr = _tile(TILES["row"], s)

    def body(d_ref, xhat_ref, rstd_ref, g_ref, dz_ref, dzb_ref, dg_ref, db_ref):
        @pl.when(pl.program_id(0) == 0)
        def _():
            dg_ref[...] = jnp.zeros_like(dg_ref)
            db_ref[...] = jnp.zeros_like(db_ref)

        dout_t, xhat_t = d_ref[...], xhat_ref[...]
        dz = _ln_grad(dout_t, xhat_t, rstd_ref[...], g_ref[...])
        dz_ref[...] = dz
        dzb_ref[...] = dz.astype(BF16)
        dg_ref[...] += jnp.sum(dout_t * xhat_t, axis=0, keepdims=True)
        db_ref[...] += jnp.sum(dout_t, axis=0, keepdims=True)

    row = pl.BlockSpec((tr, d), lambda i: (i, 0))
    vec = pl.BlockSpec((1, d), lambda i: (0, 0))
    return _call(
        body, grid=(s // tr,), in_specs=[row, row, pl.BlockSpec((tr, 1), lambda i: (i, 0)), vec],
        out_specs=[row, row, vec, vec],
        out_shape=[jax.ShapeDtypeStruct((s, d), F32), jax.ShapeDtypeStruct((s, d), BF16),
                   jax.ShapeDtypeStruct((1, d), F32), jax.ShapeDtypeStruct((1, d), F32)],
        semantics=("arbitrary",), name=name, args=(dout, xhat, rstd, gain), jobs=jobs)


def _ln_loss(xres, y, gain, bias, target, name, jobs=()):
    s, d = xres.shape
    tr = _tile(TILES["row"], s)

    def body(x_ref, y_ref, g_ref, b_ref, t_ref, loss_ref, dz_ref, dzb_ref, dg_ref, db_ref):
        @pl.when(pl.program_id(0) == 0)
        def _():
            loss_ref[...] = jnp.zeros_like(loss_ref)
            dg_ref[...] = jnp.zeros_like(dg_ref)
            db_ref[...] = jnp.zeros_like(db_ref)

        xhat, rstd = _ln_stats(ALPHA * x_ref[...] + y_ref[...])
        diff = xhat * g_ref[...] + b_ref[...] - t_ref[...]
        per_row = jnp.mean(diff * diff, axis=-1, keepdims=True)
        loss_ref[...] += 0.5 * jnp.sum(per_row, axis=0, keepdims=True)
        dout = diff * (1.0 / d)
        dz = _ln_grad(dout, xhat, rstd, g_ref[...])
        dz_ref[...] = dz
        dzb_ref[...] = dz.astype(BF16)
        dg_ref[...] += jnp.sum(dout * xhat, axis=0, keepdims=True)
        db_ref[...] += jnp.sum(dout, axis=0, keepdims=True)

    row = pl.BlockSpec((tr, d), lambda i: (i, 0))
    vec = pl.BlockSpec((1, d), lambda i: (0, 0))
    return _call(
        body, grid=(s // tr,), in_specs=[row, row, vec, vec, row],
        out_specs=[pl.BlockSpec((1, 128), lambda i: (0, 0)), row, row, vec, vec],
        out_shape=[jax.ShapeDtypeStruct((1, 128), F32), jax.ShapeDtypeStruct((s, d), F32),
                   jax.ShapeDtypeStruct((s, d), BF16), jax.ShapeDtypeStruct((1, d), F32),
                   jax.ShapeDtypeStruct((1, d), F32)],
        semantics=("arbitrary",), name=name, args=(xres, y, gain, bias, target), jobs=jobs)


def _softmax_rows(s):
    e = jnp.exp(s - jnp.max(s, axis=-1, keepdims=True))
    return e / jnp.sum(e, axis=-1, keepdims=True)


def _attn_fwd(q, k, v, name, jobs=()):
    s, d = q.shape
    m = k.shape[0]
    hd = d // XATTN_HEADS
    ts = _tile(TILES["attn"], s)
    scale = hd ** -0.5

    def body(q_ref, k_ref, v_ref, o_ref):
        for h in range(XATTN_HEADS):
            hs = slice(h * hd, (h + 1) * hd)
            sc = lax.dot_general(q_ref[:, hs], k_ref[:, hs], NT_DIMS, preferred_element_type=F32) * scale
            p = _softmax_rows(sc).astype(BF16)
            o_ref[:, hs] = jnp.dot(p, v_ref[:, hs], preferred_element_type=F32).astype(BF16)

    row = pl.BlockSpec((ts, d), lambda i: (i, 0))
    memb = pl.BlockSpec((m, d), lambda i: (0, 0))
    return _call(
        body, grid=(s // ts,), in_specs=[row, memb, memb], out_specs=[row],
        out_shape=[jax.ShapeDtypeStruct((s, d), BF16)],
        semantics=("parallel",), name=name, args=(q, k, v), jobs=jobs)[0]


def _attn_bwd(q, k, v, do, name, jobs=()):
    s, d = q.shape
    m = k.shape[0]
    hd = d // XATTN_HEADS
    ts = _tile(TILES["attn"], s)
    scale = hd ** -0.5

    def body(q_ref, k_ref, v_ref, do_ref, dq_ref, dk_ref, dv_ref):
        @pl.when(pl.program_id(0) == 0)
        def _():
            dk_ref[...] = jnp.zeros_like(dk_ref)
            dv_ref[...] = jnp.zeros_like(dv_ref)

        for h in range(XATTN_HEADS):
            hs = slice(h * hd, (h + 1) * hd)
            qh, kh, vh, doh = q_ref[:, hs], k_ref[:, hs], v_ref[:, hs], do_ref[:, hs]
            sc = lax.dot_general(qh, kh, NT_DIMS, preferred_element_type=F32) * scale
            p = _softmax_rows(sc)
            pb = p.astype(BF16)
            dp = lax.dot_general(doh, vh, NT_DIMS, preferred_element_type=F32)
            ds = (p * (dp - jnp.sum(dp * p, axis=-1, keepdims=True)) * scale).astype(BF16)
            dq_ref[:, hs] = jnp.dot(ds, kh, preferred_element_type=F32).astype(BF16)
            dk_ref[:, hs] += lax.dot_general(ds, qh, TN_DIMS, preferred_element_type=F32)
            dv_ref[:, hs] += lax.dot_general(pb, doh, TN_DIMS, preferred_element_type=F32)

    row = pl.BlockSpec((ts, d), lambda i: (i, 0))
    memb = pl.BlockSpec((m, d), lambda i: (0, 0))
    return _call(
        body, grid=(s // ts,), in_specs=[row, memb, memb, row], out_specs=[row, memb, memb],
        out_shape=[jax.ShapeDtypeStruct((s, d), BF16), jax.ShapeDtypeStruct((m, d), F32),
                   jax.ShapeDtypeStruct((m, d), F32)],
        semantics=("arbitrary",), name=name, args=(q, k, v, do), jobs=jobs)


def _sigmoid(x):
    return 1.0 / (1.0 + jnp.exp(-x))


def _log1p(x):
    u = 1.0 + x
    return jnp.where(u == 1.0, x, jnp.log(u) * (x / jnp.where(u == 1.0, 1.0, u - 1.0)))


def _softplus(x):
    return jnp.maximum(x, 0.0) + _log1p(jnp.exp(-jnp.abs(x)))


def _expm1(x):
    series = x * (1.0 + x * 0.5 * (1.0 + x * (1.0 / 3.0) * (1.0 + x * 0.25 * (1.0 + x * 0.2 * (1.0 + x * (1.0 / 6.0))))))
    return jnp.where(jnp.abs(x) < 0.1, series, jnp.exp(x) - 1.0)


GELU_K = 0.7978845608028654
GELU_C = 0.044715


def _gelu(x):
    return 0.5 * x * (1.0 + jnp.tanh(GELU_K * (x + GELU_C * (x * x * x))))


def _gelu_grad(x):
    th = jnp.tanh(GELU_K * (x + GELU_C * (x * x * x)))
    return 0.5 * (1.0 + th) + 0.5 * x * (1.0 - th * th) * GELU_K * (1.0 + 3.0 * GELU_C * x * x)


def _window_sum(ext_ref, first, rows, cols, w, step):
    acc = ext_ref[first:first + rows, cols]
    for kk in range(1, w):
        acc = acc + ext_ref[first + step * kk:first + step * kk + rows, cols]
    return acc


def _lru_gates(c_s, wa_ref, ba_ref, wx_ref, bx_ref, lam_ref, t_idx, hd, r_s, i_s, a_s, mult_s):
    sp = _softplus(-lam_ref[...])
    for h in range(LRU_HEADS):
        hs = slice(h * hd, (h + 1) * hd)
        chb = c_s[:, hs].astype(BF16)
        r = _sigmoid(jnp.dot(chb, wa_ref[h], preferred_element_type=F32) + ba_ref[:, hs])
        ig = _sigmoid(jnp.dot(chb, wx_ref[h], preferred_element_type=F32) + bx_ref[:, hs])
        log_a = -LRU_C * r * sp[:, hs]
        mult = jnp.sqrt(-_expm1(2.0 * log_a))
        r_s[:, hs] = r
        i_s[:, hs] = ig
        a_s[:, hs] = jnp.exp(log_a)
        mult_s[:, hs] = jnp.where(t_idx == 0, 1.0, mult)


def _conv(ext_ref, cw_ref, cb_ref, rows):
    acc = cb_ref[...] + cw_ref[0:1, :] * ext_ref[CONV_HALO - 3:CONV_HALO - 3 + rows, :]
    for kk in range(1, CONV_WIDTH):
        off = CONV_HALO - (CONV_WIDTH - 1) + kk
        acc = acc + cw_ref[kk:kk + 1, :] * ext_ref[off:off + rows, :]
    return acc


def _mixer_fwd(proj, wp, bp, ps, cw, cb, wa, ba, wx, bx, lam, name, jobs=()):
    s, p3 = proj.shape
    p = p3 // 3
    cg, hd = p // N_POOL_GROUPS, p // LRU_HEADS
    t = _tile(TILES["mixer"], s)

    def body(up_ref, ul_ref, ug_ref, wp_ref, bp_ref, ps_ref, cw_ref, cb_ref, wa_ref, ba_ref, wx_ref, bx_ref,
             lam_ref, ycat_ref, h_ref, extp, extl, hc, c_s, r_s, i_s, a_s, b_s):
        i = pl.program_id(0)

        @pl.when(i == 0)
        def _():
            extp[0:POOL_HALO, :] = jnp.zeros((POOL_HALO, p), F32)
            extl[0:CONV_HALO, :] = jnp.zeros((CONV_HALO, p), F32)
            hc[...] = jnp.zeros_like(hc)

        t_idx = i * t + lax.broadcasted_iota(jnp.int32, (t, 1), 0)

        extp[POOL_HALO:POOL_HALO + t, :] = up_ref[...]
        for g, w in enumerate(POOL_WINDOWS):
            cs = slice(g * cg, (g + 1) * cg)
            cnt = jnp.minimum(t_idx + 1, w).astype(F32)
            mixed = _window_sum(extp, POOL_HALO, t, cs, w, -1) / cnt - up_ref[:, cs]
            pre = jnp.dot(mixed.astype(BF16), wp_ref[g], preferred_element_type=F32) + bp_ref[:, cs]
            ycat_ref[:, cs] = (pre * ps_ref[:, cs]).astype(BF16)
        extp[0:POOL_HALO, :] = extp[t:t + POOL_HALO, :]

        extl[CONV_HALO:CONV_HALO + t, :] = ul_ref[...]
        c_s[...] = _conv(extl, cw_ref, cb_ref, t)
        extl[0:CONV_HALO, :] = extl[t:t + CONV_HALO, :]
        _lru_gates(c_s, wa_ref, ba_ref, wx_ref, bx_ref, lam_ref, t_idx, hd, r_s, i_s, a_s, b_s)
        b_s[...] = b_s[...] * (i_s[...] * c_s[...])

        rows = lax.broadcasted_iota(jnp.int32, (SUBLANES, p), 0)

        def block(bi, h):
            r0 = pl.multiple_of(bi * SUBLANES, SUBLANES)
            at = a_s[pl.ds(r0, SUBLANES), :]
            bt = b_s[pl.ds(r0, SUBLANES), :]
            out = jnp.zeros((SUBLANES, p), F32)
            for j in range(SUBLANES):
                h = at[j:j + 1, :] * h + bt[j:j + 1, :]
                out = jnp.where(rows == j, h, out)
            h_ref[pl.ds(r0, SUBLANES), :] = out
            return h

        hc[0:1, :] = lax.fori_loop(0, t // SUBLANES, block, hc[0:1, :])
        ycat_ref[:, p:2 * p] = (h_ref[...] * _gelu(ug_ref[...])).astype(BF16)

    def col(j):
        return pl.BlockSpec((t, p), lambda i: (i, j))

    def whole(a):
        nd = a.ndim
        return pl.BlockSpec(a.shape, lambda i: (0,) * nd)

    consts = (wp, bp, ps, cw, cb, wa, ba, wx, bx, lam)
    tile = pltpu.VMEM((t, p), F32)
    return _call(
        body, grid=(s // t,), in_specs=[col(0), col(1), col(2)] + [whole(a) for a in consts],
        out_specs=[pl.BlockSpec((t, 2 * p), lambda i: (i, 0)), pl.BlockSpec((t, p), lambda i: (i, 0))],
        out_shape=[jax.ShapeDtypeStruct((s, 2 * p), BF16), jax.ShapeDtypeStruct((s, p), F32)],
        scratch_shapes=[pltpu.VMEM((t + POOL_HALO, p), F32), pltpu.VMEM((t + CONV_HALO, p), F32),
                        pltpu.VMEM((SUBLANES, p), F32), tile, tile, tile, tile, tile],
        semantics=("arbitrary",), name=name, args=(proj, proj, proj, *consts), jobs=jobs)


def _mixer_bwd(dycat, proj, hsave, wp, bp, ps, cw, cb, wa, ba, wx, bx, lam, name, jobs=()):
    s, p3 = proj.shape
    p = p3 // 3
    cg, hd = p // N_POOL_GROUPS, p // LRU_HEADS
    t = _tile(TILES["mixer"], s)
    nt = s // t

    def body(dyp_ref, dyl_ref, up_ref, ul_ref, ug_ref, upp_ref, ulp_ref, h_ref, hp_ref,
             wp_ref, bp_ref, ps_ref, cw_ref, cb_ref, wa_ref, ba_ref, wx_ref, bx_ref, lam_ref,
             dproj_ref, dwp_ref, dbp_ref, dps_ref, dcw_ref, dcb_ref, dwa_ref, dba_ref, dwx_ref, dbx_ref, dlam_ref,
             extp, extg, extl, extdc, exth, ghc, c_s, r_s, i_s, a_s, mult_s, gh_s):
        i = pl.program_id(0)
        ib = nt - 1 - i

        @pl.when(i == 0)
        def _():
            for ref in (dwp_ref, dbp_ref, dps_ref, dcw_ref, dcb_ref, dwa_ref, dba_ref, dwx_ref, dbx_ref, dlam_ref):
                ref[...] = jnp.zeros_like(ref)
            extg[t:t + POOL_HALO, :] = jnp.zeros((POOL_HALO, p), F32)
            extdc[t:t + CONV_HALO, :] = jnp.zeros((CONV_HALO, p), F32)
            ghc[...] = jnp.zeros_like(ghc)

        t_idx = ib * t + lax.broadcasted_iota(jnp.int32, (t, 1), 0)
        seq_start = ib == 0

        extl[0:CONV_HALO, :] = jnp.where(seq_start, 0.0, ulp_ref[...])
        extl[CONV_HALO:CONV_HALO + t, :] = ul_ref[...]
        c_s[...] = _conv(extl, cw_ref, cb_ref, t)
        _lru_gates(c_s, wa_ref, ba_ref, wx_ref, bx_ref, lam_ref, t_idx, hd, r_s, i_s, a_s, mult_s)
        exth[0:SUBLANES, :] = jnp.where(seq_start, 0.0, hp_ref[...])
        exth[SUBLANES:SUBLANES + t, :] = h_ref[...]

        ug = ug_ref[...]
        dyl = dyl_ref[...]
        dproj_ref[:, 2 * p:3 * p] = (dyl * h_ref[...] * _gelu_grad(ug)).astype(BF16)
        gh_s[...] = dyl * _gelu(ug)

        rows = lax.broadcasted_iota(jnp.int32, (SUBLANES, p), 0)
        nblk = t // SUBLANES

        def block(bi, carry):
            r0 = pl.multiple_of((nblk - 1 - bi) * SUBLANES, SUBLANES)
            at = a_s[pl.ds(r0, SUBLANES), :]
            dt = gh_s[pl.ds(r0, SUBLANES), :]
            out = jnp.zeros((SUBLANES, p), F32)
            for j in range(SUBLANES - 1, -1, -1):
                gh = dt[j:j + 1, :] + carry
                out = jnp.where(rows == j, gh, out)
                carry = at[j:j + 1, :] * gh
            gh_s[pl.ds(r0, SUBLANES), :] = out
            return carry

        ghc[0:1, :] = lax.fori_loop(0, nblk, block, ghc[0:1, :])

        sp = _softplus(-lam_ref[...])
        dsp_dlam = -_sigmoid(-lam_ref[...])
        for h in range(LRU_HEADS):
            hs = slice(h * hd, (h + 1) * hd)
            gh, a, mult, r, ig, c = gh_s[:, hs], a_s[:, hs], mult_s[:, hs], r_s[:, hs], i_s[:, hs], c_s[:, hs]
            hprev = exth[SUBLANES - 1:SUBLANES - 1 + t, hs]
            dmult = gh * (ig * c)
            dlog_a = a * gh * hprev + jnp.where(t_idx == 0, 0.0, -dmult * a * a / mult)
            dlam_ref[:, hs] += jnp.sum(dlog_a * r, axis=0, keepdims=True) * (-LRU_C) * dsp_dlam[:, hs]
            dpa = dlog_a * (-LRU_C * sp[:, hs]) * r * (1.0 - r)
            dpx = gh * mult * c * ig * (1.0 - ig)
            dpab, dpxb, chb = dpa.astype(BF16), dpx.astype(BF16), c.astype(BF16)
            dwa_ref[h] += lax.dot_general(chb, dpab, TN_DIMS, preferred_element_type=F32)
            dwx_ref[h] += lax.dot_general(chb, dpxb, TN_DIMS, preferred_element_type=F32)
            dba_ref[:, hs] += jnp.sum(dpa, axis=0, keepdims=True)
            dbx_ref[:, hs] += jnp.sum(dpx, axis=0, keepdims=True)
            dc = (gh * mult * ig
                  + lax.dot_general(dpab, wa_ref[h], NT_DIMS, preferred_element_type=F32)
                  + lax.dot_general(dpxb, wx_ref[h], NT_DIMS, preferred_element_type=F32))
            extdc[0:t, hs] = dc
            dcb_ref[:, hs] += jnp.sum(dc, axis=0, keepdims=True)
            for kk in range(CONV_WIDTH):
                off = CONV_HALO - (CONV_WIDTH - 1) + kk
                dcw_ref[kk:kk + 1, hs] += jnp.sum(dc * extl[off:off + t, hs], axis=0, keepdims=True)
        du_lru = cw_ref[0:1, :] * extdc[CONV_WIDTH - 1:CONV_WIDTH - 1 + t, :]
        for kk in range(1, CONV_WIDTH):
            off = CONV_WIDTH - 1 - kk
            du_lru = du_lru + cw_ref[kk:kk + 1, :] * extdc[off:off + t, :]
        dproj_ref[:, p:2 * p] = du_lru.astype(BF16)
        extdc[t:t + CONV_HALO, :] = extdc[0:CONV_HALO, :]

        extp[0:POOL_HALO, :] = jnp.where(seq_start, 0.0, upp_ref[...])
        extp[POOL_HALO:POOL_HALO + t, :] = up_ref[...]
        for g, w in enumerate(POOL_WINDOWS):
            cs = slice(g * cg, (g + 1) * cg)
            cnt = jnp.minimum(t_idx + 1, w).astype(F32)
            mixed = (_window_sum(extp, POOL_HALO, t, cs, w, -1) / cnt - up_ref[:, cs]).astype(BF16)
            pre = jnp.dot(mixed, wp_ref[g], preferred_element_type=F32) + bp_ref[:, cs]
            dyp = dyp_ref[:, cs]
            dps_ref[:, cs] += jnp.sum(dyp * pre, axis=0, keepdims=True)
            dpre = dyp * ps_ref[:, cs]
            dpreb = dpre.astype(BF16)
            dbp_ref[:, cs] += jnp.sum(dpre, axis=0, keepdims=True)
            dwp_ref[g] += lax.dot_general(mixed, dpreb, TN_DIMS, preferred_element_type=F32)
            dmixed = lax.dot_general(dpreb, wp_ref[g], NT_DIMS, preferred_element_type=F32)
            extg[0:t, cs] = dmixed / cnt
            dproj_ref[:, cs] = (_window_sum(extg, 0, t, cs, w, 1) - dmixed).astype(BF16)
        extg[t:t + POOL_HALO, :] = extg[0:POOL_HALO, :]

    def col(j):
        return pl.BlockSpec((t, p), lambda i: (nt - 1 - i, j))

    def prev(rows, j):
        per = t // rows
        return pl.BlockSpec((rows, p), lambda i: (jnp.maximum((nt - 1 - i) * per - 1, 0), j))

    def whole(a):
        nd = a.ndim
        return pl.BlockSpec(a.shape, lambda i: (0,) * nd)

    consts = (wp, bp, ps, cw, cb, wa, ba, wx, bx, lam)
    grads = (wp, bp, ps, cw, cb, wa, ba, wx, bx, lam)
    tile = pltpu.VMEM((t, p), F32)
    return _call(
        body, grid=(nt,),
        in_specs=[col(0), col(1), col(0), col(1), col(2), prev(POOL_HALO, 0), prev(CONV_HALO, 1), col(0),
                  prev(SUBLANES, 0)] + [whole(a) for a in consts],
        out_specs=[pl.BlockSpec((t, 3 * p), lambda i: (nt - 1 - i, 0))] + [whole(a) for a in grads],
        out_shape=[jax.ShapeDtypeStruct((s, 3 * p), BF16)] + [jax.ShapeDtypeStruct(a.shape, F32) for a in grads],
        scratch_shapes=[pltpu.VMEM((t + POOL_HALO, p), F32), pltpu.VMEM((t + POOL_HALO, p), F32),
                        pltpu.VMEM((t + CONV_HALO, p), F32), pltpu.VMEM((t + CONV_HALO, p), F32),
                        pltpu.VMEM((t + SUBLANES, p), F32), pltpu.VMEM((SUBLANES, p), F32),
                        tile, tile, tile, tile, tile, tile],
        semantics=("arbitrary",), name=name,
        args=(dycat, dycat, proj, proj, proj, proj, proj, hsave, hsave, *consts), jobs=jobs)


def _pair_add(own, got, name):
    n, r, c = own.shape
    tr = _tile(TILES["row"], r)

    def body(a_ref, b_ref, o_ref):
        o_ref[...] = (a_ref[...].astype(F32) + b_ref[...].astype(F32)).astype(o_ref.dtype)

    blk = pl.BlockSpec((None, tr, c), lambda k, i: (k, i, 0))
    return _call(body, grid=(n, r // tr), in_specs=[blk, blk], out_specs=[blk],
                 out_shape=[jax.ShapeDtypeStruct(own.shape, own.dtype)], semantics=("parallel", "parallel"),
                 name=name, args=(own, got))[0]


def _sum_parts(parts, name):
    n, r, c = parts.shape
    tr = _tile(TILES["adam"], r)

    def body(p_ref, o_ref):
        acc = p_ref[0].astype(F32)
        for d in range(1, n):
            acc = acc + p_ref[d].astype(F32)
        o_ref[...] = acc

    return _call(
        body, grid=(r // tr,), in_specs=[pl.BlockSpec((n, tr, c), lambda i: (0, i, 0))],
        out_specs=[pl.BlockSpec((tr, c), lambda i: (i, 0))], out_shape=[jax.ShapeDtypeStruct((r, c), F32)],
        semantics=("parallel",), name=name, args=(parts,))[0]


def _adamw(w, m, v, parts, name, jobs=()):
    r, c = w.shape
    n = parts.shape[0]
    tr = _tile(TILES["adam"], r)

    def body(w_ref, m_ref, v_ref, p_ref, g_ref, d_ref, nm_ref, nv_ref):
        g = p_ref[0].astype(F32)
        for d in range(1, n):
            g = g + p_ref[d].astype(F32)
        nm = ADAM_B1 * m_ref[...] + (1.0 - ADAM_B1) * g
        nv = ADAM_B2 * v_ref[...] + (1.0 - ADAM_B2) * (g * g)
        m_hat = nm / (1.0 - ADAM_B1 ** ADAM_STEP)
        v_hat = nv / (1.0 - ADAM_B2 ** ADAM_STEP)
        g_ref[...] = g
        d_ref[...] = -ADAM_LR * (m_hat / (jnp.sqrt(v_hat) + ADAM_EPS) + ADAM_WD * w_ref[...])
        nm_ref[...] = nm
        nv_ref[...] = nv

    row = pl.BlockSpec((tr, c), lambda i: (i, 0))
    return _call(
        body, grid=(r // tr,), in_specs=[row, row, row, pl.BlockSpec((n, tr, c), lambda i: (0, i, 0))],
        out_specs=[row] * 4, out_shape=[jax.ShapeDtypeStruct((r, c), F32)] * 4,
        semantics=("parallel",), name=name, args=(w, m, v, parts), jobs=jobs)


SMALL_ORDER = ("w_a", "w_x", "conv_w", "b_pool", "conv_b", "b_a", "b_x", "lru_lambda", "pool_scale",
               "ln1_g", "ln1_b", "ln2_g", "ln2_b", "ln3_g", "ln3_b")


def _pack_rows(a, p):
    flat = a.reshape(-1, p)
    pad = (-flat.shape[0]) % SUBLANES
    return jnp.pad(flat, ((0, pad), (0, 0))) if pad else flat


def kernel(x, mem, w_in, conv_w, conv_b, w_a, b_a, w_x, b_x, lru_lambda, w_pool, b_pool, pool_scale, w_out, ln1_g, ln1_b, w_q, w_k, w_v, w_o, ln2_g, ln2_b, w_ff1, w_ff2, ln3_g, ln3_b, loss_target, m_w_in, m_conv_w, m_conv_b, m_w_a, m_b_a, m_w_x, m_b_x, m_lru_lambda, m_w_pool, m_b_pool, m_pool_scale, m_w_out, m_ln1_g, m_ln1_b, m_w_q, m_w_k, m_w_v, m_w_o, m_ln2_g, m_ln2_b, m_w_ff1, m_w_ff2, m_ln3_g, m_ln3_b, v_w_in, v_conv_w, v_conv_b, v_w_a, v_b_a, v_w_x, v_b_x, v_lru_lambda, v_w_pool, v_b_pool, v_pool_scale, v_w_out, v_ln1_g, v_ln1_b, v_w_q, v_w_k, v_w_v, v_w_o, v_ln2_g, v_ln2_b, v_w_ff1, v_w_ff2, v_ln3_g, v_ln3_b):
    names = ("w_in", "conv_w", "conv_b", "w_a", "b_a", "w_x", "b_x", "lru_lambda", "w_pool", "b_pool", "pool_scale",
             "w_out", "ln1_g", "ln1_b", "w_q", "w_k", "w_v", "w_o", "ln2_g", "ln2_b", "w_ff1", "w_ff2", "ln3_g", "ln3_b")
    w_loc = dict(zip(names, (w_in, conv_w, conv_b, w_a, b_a, w_x, b_x, lru_lambda, w_pool, b_pool, pool_scale,
                             w_out, ln1_g, ln1_b, w_q, w_k, w_v, w_o, ln2_g, ln2_b, w_ff1, w_ff2, ln3_g, ln3_b)))
    m_loc = dict(zip(names, (m_w_in, m_conv_w, m_conv_b, m_w_a, m_b_a, m_w_x, m_b_x, m_lru_lambda, m_w_pool, m_b_pool,
                             m_pool_scale, m_w_out, m_ln1_g, m_ln1_b, m_w_q, m_w_k, m_w_v, m_w_o, m_ln2_g, m_ln2_b,
                             m_w_ff1, m_w_ff2, m_ln3_g, m_ln3_b)))
    v_loc = dict(zip(names, (v_w_in, v_conv_w, v_conv_b, v_w_a, v_b_a, v_w_x, v_b_x, v_lru_lambda, v_w_pool, v_b_pool,
                             v_pool_scale, v_w_out, v_ln1_g, v_ln1_b, v_w_q, v_w_k, v_w_v, v_w_o, v_ln2_g, v_ln2_b,
                             v_w_ff1, v_w_ff2, v_ln3_g, v_ln3_b)))
    s, d = x.shape[1], x.shape[2]
    p = conv_b.shape[1]
    cg = p // N_POOL_GROUPS
    hd = p // LRU_HEADS
    me = 4 * lax.axis_index("x") + 2 * lax.axis_index("y") + lax.axis_index("c")

    xs, mems, tgt = x[0], mem[0], loss_target[0]
    xb, memb = xs.astype(BF16), mems.astype(BF16)

    gathers = {n: _Job("gather", w_loc[n][0].astype(BF16))
               for n in ("w_in", "w_out", "w_q", "w_k", "w_v", "w_o", "w_ff1", "w_ff2", "w_pool")}
    tiny = jnp.concatenate([_pack_rows(conv_w[0], p // N_DEV),
                            _pack_rows(jnp.pad(b_pool[0], ((0, 0), (0, p // N_DEV - cg // N_DEV))), p // N_DEV)], axis=0)
    gathers["tiny"] = _Job("gather", tiny)

    def gathered(n):
        (full,) = gathers[n].outs
        if n in ("w_in", "w_ff1"):
            return full
        return full.reshape(1, -1, full.shape[-1])

    W = {"conv_b": conv_b, "b_a": b_a.reshape(1, p), "b_x": b_x.reshape(1, p), "lru_lambda": lru_lambda,
         "pool_scale": pool_scale, "w_a": w_a[0].astype(BF16), "w_x": w_x[0].astype(BF16)}
    for n in ("ln1_g", "ln1_b", "ln2_g", "ln2_b", "ln3_g", "ln3_b"):
        W[n] = w_loc[n]

    out_g, out_d, out_m, out_v = {}, {}, {}, {}
    pairs, quads = {}, {}

    def pair(n, partial):
        pairs[n] = _Job("pair", partial.reshape(N_DEV, -1, partial.shape[-1]))
        return pairs[n]

    def quad(n):
        quads[n] = _Job("quad", _pair_add(*pairs[n].outs, "add_" + n))
        return quads[n]

    def update(n, parts, jobs=()):
        shp = w_loc[n].shape
        rows = parts.shape[1]
        w2, m2, v2 = (a.reshape(rows, -1) for a in (w_loc[n], m_loc[n], v_loc[n]))
        res = _adamw(w2, m2, v2, parts.reshape(parts.shape[0], rows, -1), "adamw_" + n, jobs=jobs)
        out_g[n], out_d[n], out_m[n], out_v[n] = (r.reshape(shp) for r in res)

    _run_jobs([gathers["w_in"], gathers["tiny"], gathers["w_pool"]], "gather_first")
    W["w_pool"] = jnp.transpose(gathers["w_pool"].outs[0], (1, 0, 2, 3)).reshape(N_POOL_GROUPS, cg, cg)
    cwb = gathers["tiny"].outs[0]
    W["conv_w"] = jnp.transpose(cwb[:, :CONV_WIDTH, :], (1, 0, 2)).reshape(CONV_WIDTH, p)
    W["b_pool"] = jnp.transpose(cwb[:, SUBLANES:SUBLANES + N_POOL_GROUPS, :cg // N_DEV], (1, 0, 2)).reshape(1, p)
    mixer_consts = (W["w_pool"], W["b_pool"], W["pool_scale"], W["conv_w"], W["conv_b"], W["w_a"], W["b_a"],
                    W["w_x"], W["b_x"], W["lru_lambda"])

    (proj,) = _mm_nn(xb, gathered("w_in"), [F32], "fwd_proj", jobs=[gathers["w_ff1"]])
    ycat, hsave = _mixer_fwd(proj, *mixer_consts, "fwd_mixer", jobs=[gathers["w_out"], gathers["w_o"]])
    (y1,) = _mm_nn(ycat, gathered("w_out"), [F32], "fwd_out", jobs=[gathers["w_q"]])
    x1, x1b, xhat1, rstd1 = _ln_fwd(xs, y1, W["ln1_g"], W["ln1_b"], "fwd_ln1", jobs=[gathers["w_k"]])
    (q,) = _mm_nn(x1b, gathered("w_q"), [BF16], "fwd_q", jobs=[gathers["w_v"]])
    (k,) = _mm_nn(memb, gathered("w_k"), [BF16], "fwd_k")
    (v,) = _mm_nn(memb, gathered("w_v"), [BF16], "fwd_v")
    o = _attn_fwd(q, k, v, "fwd_attn")
    (y2,) = _mm_nn(o, gathered("w_o"), [F32], "fwd_o")
    x2, x2b, xhat2, rstd2 = _ln_fwd(x1, y2, W["ln2_g"], W["ln2_b"], "fwd_ln2")

    def relu_sq(acc):
        r = jnp.maximum(acc, 0.0)
        return r, r * r

    rb, act = _mm_nn(x2b, gathered("w_ff1"), [BF16, BF16], "fwd_ff1", epilogue=relu_sq, jobs=[gathers["w_ff2"]])
    (y3,) = _mm_nn(act, gathered("w_ff2"), [F32], "fwd_ff2")
    loss_rows, dz3, dz3b, dg3, db3 = _ln_loss(x2, y3, W["ln3_g"], W["ln3_b"], tgt, "ln3_loss")
    loss = lax.psum(loss_rows[0, 0], MESH_AXES)

    small = {"ln3_g": dg3, "ln3_b": db3}

    def add_residual(acc, e):
        return (acc + ALPHA * e,)

    dw_ff2 = _mm_tn(act, dz3b, 1, BF16, "bwd_dw_ff2")
    (dhid,) = _mm_nt(dz3b, gathered("w_ff2"), [BF16], "bwd_dact", extras=(rb,), jobs=[pair("w_ff2", dw_ff2)],
                     epilogue=lambda acc, r: (acc * (2.0 * r.astype(F32)),))
    dw_ff1 = _mm_tn(x2b, dhid, N_DEV, BF16, "bwd_dw_ff1", jobs=[quad("w_ff2")])
    (dx2,) = _mm_nt(dhid, gathered("w_ff1"), [F32], "bwd_dx2", epilogue=add_residual, extras=(dz3,),
                    jobs=[pair("w_ff1", dw_ff1)])
    dz2, dz2b, small["ln2_g"], small["ln2_b"] = _ln_bwd(dx2, xhat2, rstd2, W["ln2_g"], "bwd_ln2")

    dw_o = _mm_tn(o, dz2b, 1, BF16, "bwd_dw_o")
    (do,) = _mm_nt(dz2b, gathered("w_o"), [BF16], "bwd_do", jobs=[pair("w_o", dw_o)])
    dq, dk, dv = _attn_bwd(q, k, v, do, "bwd_attn", jobs=[quad("w_o")])
    dw_q = _mm_tn(x1b, dq, 1, BF16, "bwd_dw_q")
    dw_k = _mm_tn(memb, dk.astype(BF16), 1, BF16, "bwd_dw_k")
    dw_v = _mm_tn(memb, dv.astype(BF16), 1, BF16, "bwd_dw_v")
    (dx1,) = _mm_nt(dq, gathered("w_q"), [F32], "bwd_dx1", epilogue=add_residual, extras=(dz2,),
                    jobs=[pair("w_q", dw_q), pair("w_k", dw_k), pair("w_v", dw_v)])
    dz1, dz1b, small["ln1_g"], small["ln1_b"] = _ln_bwd(dx1, xhat1, rstd1, W["ln1_g"], "bwd_ln1", jobs=[quad("w_q")])

    dw_out = _mm_tn(ycat, dz1b, 1, BF16, "bwd_dw_out", jobs=[quad("w_k")])
    (dycat,) = _mm_nt(dz1b, gathered("w_out"), [F32], "bwd_dycat", jobs=[quad("w_v"), pair("w_out", dw_out)])
    (dproj, dwp, small["b_pool"], small["pool_scale"], small["conv_w"], small["conv_b"], small["w_a"], small["b_a"],
     small["w_x"], small["b_x"], small["lru_lambda"]) = _mixer_bwd(
        dycat, proj, hsave, *mixer_consts, "bwd_mixer", jobs=[quad("w_ff1")])
    dw_pool = jnp.transpose(dwp.astype(BF16).reshape(N_POOL_GROUPS, N_DEV, cg // N_DEV, cg), (1, 0, 2, 3))
    pack = jnp.concatenate([_pack_rows(small[n], p) for n in SMALL_ORDER], axis=0)
    small_gather = _Job("gather", pack)
    dw_in = _mm_tn(xb, dproj, N_DEV, BF16, "bwd_dw_in", jobs=[quad("w_out"), pair("w_pool", dw_pool), small_gather])
    (grad_x,) = _mm_nt(dproj, gathered("w_in"), [F32], "bwd_dx", epilogue=add_residual, extras=(dz1,),
                       jobs=[pair("w_in", dw_in), quad("w_pool")])

    update("w_ff2", quads["w_ff2"].outs[0], jobs=[quad("w_in")])
    for n in ("w_ff1", "w_o", "w_q", "w_k", "w_v", "w_out", "w_pool", "w_in"):
        update(n, quads[n].outs[0])

    total = _sum_parts(small_gather.outs[0], "sum_small")
    row = 0
    for n in SMALL_ORDER:
        size = small[n].size
        nrows = size // p
        g_full = total[row:row + nrows].reshape(small[n].shape)
        row += nrows + (-nrows) % SUBLANES
        if n == "conv_w":
            g_loc = lax.dynamic_slice_in_dim(g_full, me * (p // N_DEV), p // N_DEV, axis=1)
        elif n == "b_pool":
            g_loc = lax.dynamic_slice_in_dim(g_full.reshape(N_POOL_GROUPS, cg), me * (cg // N_DEV), cg // N_DEV, axis=1)
        else:
            g_loc = g_full
        rows = g_loc.shape[0] if n not in ("w_a", "w_x") else LRU_HEADS * hd
        update(n, g_loc.reshape(1, rows, -1))

    order = names
    return (loss, grad_x[None], *[out_g[n] for n in order], *[out_d[n] for n in order],
            *[out_m[n] for n in order], *[out_v[n] for n in order])
```

```python
import functools

import jax
import jax.numpy as jnp
from jax import lax
from jax.experimental import pallas as pl
from jax.experimental.pallas import tpu as pltpu

F32 = jnp.float32
BF16 = jnp.bfloat16

N_DEV = 8
MESH_AXES = ("x", "y", "c")
POOL_WINDOWS = (2, 4, 8, 16)
N_POOL_GROUPS = len(POOL_WINDOWS)
POOL_HALO = 16
CONV_WIDTH = 4
CONV_HALO = 8
LRU_HEADS = 8
LRU_C = 8.0
XATTN_HEADS = 4
LN_EPS = 1e-5
ALPHA = 2.0 ** 0.25
ADAM_LR = 0.001
ADAM_B1 = 0.9
ADAM_B2 = 0.999
ADAM_EPS = 1e-08
ADAM_WD = 0.01
ADAM_STEP = 10
SUBLANES = 8
VMEM_LIMIT = 56 * 1024 * 1024

NT_DIMS = (((1,), (1,)), ((), ()))
TN_DIMS = (((0,), (0,)), ((), ()))


def _params(*sem):
    return pltpu.CompilerParams(dimension_semantics=sem, vmem_limit_bytes=VMEM_LIMIT)


def _place():
    return lax.axis_index("x"), lax.axis_index("y"), lax.axis_index("c")


def _remote(src, dst, send_sem, recv_sem, to):
    return pltpu.make_async_remote_copy(src_ref=src, dst_ref=dst, send_sem=send_sem, recv_sem=recv_sem,
                                        device_id=to, device_id_type=pl.DeviceIdType.MESH)


class _Job:
    def __init__(self, kind, src):
        self.kind, self.src, self.out = kind, src, None

    def out_shape(self):
        s = self.src.shape
        shape = {"gather": (N_DEV,) + s, "pair": (4,) + s[1:], "quad": (3,) + s[1:]}[self.kind]
        return jax.ShapeDtypeStruct(shape, self.src.dtype)

    def scratch(self):
        n = {"gather": 7, "pair": 4, "quad": 3}[self.kind]
        sems = [pltpu.SemaphoreType.DMA((n,)), pltpu.SemaphoreType.DMA((n,))]
        if self.kind == "gather":
            sems += [pltpu.SemaphoreType.DMA((2,)), pltpu.VMEM(self.src.shape, self.src.dtype)]
        return sems

    def ops(self, src, out, *scratch):
        return {"gather": _gather_ops, "pair": _pair_ops, "quad": _quad_ops}[self.kind](src, out, *scratch)


def _gather_ops(x_ref, out_ref, send_sems, recv_sems, local_sems, bounce):
    x, y, c = _place()
    me, sibling = (x, y, c), (x, y, 1 - c)
    chips = [(1 - x, y), (x, 1 - y), (1 - x, 1 - y)]

    def slot(px, py, pc):
        return out_ref.at[4 * px + 2 * py + pc]

    def copy(k, block, to, src=None):
        return _remote(slot(*block) if src is None else src, slot(*block), send_sems.at[k], recv_sems.at[k], to)

    mine_in = pltpu.make_async_copy(x_ref, bounce, local_sems.at[0])
    mine_out = pltpu.make_async_copy(bounce, slot(*me), local_sems.at[1])
    first = [copy(0, me, sibling, src=x_ref)] + [copy(1 + j, me, (*chip, c), src=x_ref) for j, chip in enumerate(chips)]
    passed = [copy(4 + j, (*chip, c), sibling) for j, chip in enumerate(chips)]

    def start():
        mine_in.start()
        for cp in first:
            cp.start()

    def finish():
        mine_in.wait()
        mine_out.start()
        for j, chip in enumerate(chips):
            copy(1 + j, (*chip, c), me).wait_recv()
            passed[j].start()
        copy(0, sibling, me).wait_recv()
        for j, chip in enumerate(chips):
            copy(4 + j, (*chip, 1 - c), me).wait_recv()
        for cp in first + passed:
            cp.wait_send()
        mine_out.wait()

    return start, finish


def _pair_ops(p_ref, got_ref, send_sems, recv_sems):
    x, y, c = _place()
    give = [_remote(p_ref.at[2 * k + 1 - c], got_ref.at[k], send_sems.at[k], recv_sems.at[k], (x, y, 1 - c))
            for k in range(4)]

    def start():
        for cp in give:
            cp.start()

    def finish():
        for cp in give:
            cp.wait_recv()
        for cp in give:
            cp.wait_send()

    return start, finish


def _quad_ops(q_ref, out_ref, send_sems, recv_sems):
    x, y, c = _place()
    copies = []
    for rel in range(1, 4):
        px = 1 - x if rel & 2 else x
        py = 1 - y if rel & 1 else y
        copies.append(_remote(q_ref.at[2 * px + py], out_ref.at[rel - 1], send_sems.at[rel - 1], recv_sems.at[rel - 1],
                              (px, py, c)))

    def start():
        for cp in copies:
            cp.start()

    def finish():
        for cp in copies:
            cp.wait_recv()
        for cp in copies:
            cp.wait_send()

    return start, finish


def _call(body, *, grid, in_specs, out_specs, out_shape, scratch_shapes=(), semantics, name, args, jobs=(), index=None):
    in_specs, out_specs, out_shape = list(in_specs), list(out_specs), list(out_shape)
    scratch_shapes, jobs = list(scratch_shapes), list(jobs)
    n_in, n_out, n_scr, n_job = len(in_specs), len(out_specs), len(scratch_shapes), len(jobs)
    n_idx = 0 if index is None else 1
    job_scratch = [j.scratch() for j in jobs]

    def hosted(*refs):
        idx, refs = refs[:n_idx], refs[n_idx:]
        ins, jin = refs[:n_in], refs[n_in:n_in + n_job]
        o0 = n_in + n_job
        outs, jout = refs[o0:o0 + n_out], refs[o0 + n_out:o0 + n_out + n_job]
        s0 = o0 + n_out + n_job
        scr, jscr = refs[s0:s0 + n_scr], refs[s0 + n_scr:]
        ops, at = [], 0
        for k, j in enumerate(jobs):
            ops.append(j.ops(jin[k], jout[k], *jscr[at:at + len(job_scratch[k])]))
            at += len(job_scratch[k])
        first = functools.reduce(jnp.logical_and, [pl.program_id(a) == 0 for a in range(len(grid))])
        last = functools.reduce(jnp.logical_and, [pl.program_id(a) == g - 1 for a, g in enumerate(grid)])

        @pl.when(first)
        def _():
            for start, _ in ops:
                start()

        body(*idx, *ins, *outs, *scr)

        @pl.when(last)
        def _():
            for _, finish in ops:
                finish()

    hbm = pl.BlockSpec(memory_space=pl.ANY)
    spec = pltpu.PrefetchScalarGridSpec(
        num_scalar_prefetch=n_idx, grid=grid, in_specs=in_specs + [hbm] * n_job, out_specs=out_specs + [hbm] * n_job,
        scratch_shapes=scratch_shapes + [s for js in job_scratch for s in js])
    res = pl.pallas_call(
        hosted if jobs else body, grid_spec=spec, out_shape=out_shape + [j.out_shape() for j in jobs],
        compiler_params=_params(*(["arbitrary"] * len(grid) if jobs else semantics)), name=name,
    )(*([] if index is None else [index]), *args, *[j.src for j in jobs])
    for j, o in zip(jobs, res[n_out:]):
        j.out = o
    return res[:n_out]


def _run_jobs(jobs, name):
    def body(tick_ref):
        tick_ref[...] = jnp.zeros_like(tick_ref)

    _call(body, grid=(1,), in_specs=[], out_specs=[pl.BlockSpec((SUBLANES, 128), lambda i: (0, 0))],
          out_shape=[jax.ShapeDtypeStruct((SUBLANES, 128), F32)], semantics=("arbitrary",), name=name, args=(), jobs=jobs)


TILES = dict(tm=1024, tn=1024, tk=512, row=256, attn=512, mixer=256, adam=128)


def _tile(pref, n):
    for t in range(min(pref, n), 0, -1):
        if n % t == 0 and (t % SUBLANES == 0 or t == n):
            return t
    return n


def _mm_nn(a, b3, out_dtypes, name, *, tm=None, tn=None, tk=None, epilogue=None, extras=(), jobs=()):
    m, k = a.shape
    g, k2, ns = b3.shape
    assert k == k2
    n = g * ns
    tm, tn, tk = _tile(tm or TILES["tm"], m), _tile(tn or TILES["tn"], ns), _tile(tk or TILES["tk"], k)
    nb, nk = ns // tn, k // tk
    n_ex, n_out = len(extras), len(out_dtypes)

    def body(*refs):
        a_ref, b_ref = refs[:2]
        ex = refs[2:2 + n_ex]
        outs = refs[2 + n_ex:2 + n_ex + n_out]
        acc = refs[-1]
        kk = pl.program_id(2)

        @pl.when(kk == 0)
        def _():
            acc[...] = jnp.zeros_like(acc)

        acc[...] += jnp.dot(a_ref[...], b_ref[...], preferred_element_type=F32)

        @pl.when(kk == nk - 1)
        def _():
            r = acc[...]
            res = epilogue(r, *[e[...] for e in ex]) if epilogue is not None else (r,)
            for o, v in zip(outs, res):
                o[...] = v.astype(o.dtype)

    tile_out = pl.BlockSpec((tm, tn), lambda i, j, kk: (i, j))
    return _call(
        body, grid=(m // tm, n // tn, nk),
        in_specs=[pl.BlockSpec((tm, tk), lambda i, j, kk: (i, kk)),
                  pl.BlockSpec((None, tk, tn), lambda i, j, kk: (j // nb, kk, j % nb))] + [tile_out] * n_ex,
        out_specs=[tile_out] * n_out,
        out_shape=[jax.ShapeDtypeStruct((m, n), d) for d in out_dtypes],
        scratch_shapes=[pltpu.VMEM((tm, tn), F32)],
        semantics=("parallel", "parallel", "arbitrary"), name=name, args=(a, b3, *extras), jobs=jobs)


def _mm_nt(a, b3, out_dtypes, name, *, tm=None, tn=None, tk=None, epilogue=None, extras=(), jobs=()):
    m, n = a.shape
    g, k, ns = b3.shape
    assert n == g * ns
    tm, tn, tk = _tile(tm or TILES["tm"], m), _tile(tn or TILES["tn"], k), _tile(tk or TILES["tk"], ns)
    nb, nc = ns // tk, n // tk
    n_ex, n_out = len(extras), len(out_dtypes)

    def body(*refs):
        a_ref, b_ref = refs[:2]
        ex = refs[2:2 + n_ex]
        outs = refs[2 + n_ex:2 + n_ex + n_out]
        acc = refs[-1]
        cc = pl.program_id(2)

        @pl.when(cc == 0)
        def _():
            acc[...] = jnp.zeros_like(acc)

        acc[...] += lax.dot_general(a_ref[...], b_ref[...], NT_DIMS, preferred_element_type=F32)

        @pl.when(cc == nc - 1)
        def _():
            r = acc[...]
            res = epilogue(r, *[e[...] for e in ex]) if epilogue is not None else (r,)
            for o, v in zip(outs, res):
                o[...] = v.astype(o.dtype)

    tile_out = pl.BlockSpec((tm, tn), lambda i, j, cc: (i, j))
    return _call(
        body, grid=(m // tm, k // tn, nc),
        in_specs=[pl.BlockSpec((tm, tk), lambda i, j, cc: (i, cc)),
                  pl.BlockSpec((None, tn, tk), lambda i, j, cc: (cc // nb, j, cc % nb))] + [tile_out] * n_ex,
        out_specs=[tile_out] * n_out,
        out_shape=[jax.ShapeDtypeStruct((m, k), d) for d in out_dtypes],
        scratch_shapes=[pltpu.VMEM((tm, tn), F32)],
        semantics=("parallel", "parallel", "arbitrary"), name=name, args=(a, b3, *extras), jobs=jobs)


def _mm_tn(a, b, g, out_dtype, name, *, tm=None, tn=None, tk=None, jobs=()):
    s, m = a.shape
    s2, n = b.shape
    assert s == s2 and n % g == 0
    ns = n // g
    tm, tn, tk = _tile(tm or TILES["tm"], m), _tile(tn or TILES["tn"], ns), _tile(tk or TILES["tk"], s)
    nb, nc = ns // tn, s // tk

    def body(a_ref, b_ref, o_ref, acc):
        cc = pl.program_id(2)

        @pl.when(cc == 0)
        def _():
            acc[...] = jnp.zeros_like(acc)

        acc[...] += lax.dot_general(a_ref[...], b_ref[...], TN_DIMS, preferred_element_type=F32)

        @pl.when(cc == nc - 1)
        def _():
            o_ref[...] = acc[...].astype(o_ref.dtype)

    return _call(
        body, grid=(m // tm, n // tn, nc),
        in_specs=[pl.BlockSpec((tk, tm), lambda i, j, cc: (cc, i)),
                  pl.BlockSpec((tk, tn), lambda i, j, cc: (cc, j))],
        out_specs=[pl.BlockSpec((None, tm, tn), lambda i, j, cc: (j // nb, i, j % nb))],
        out_shape=[jax.ShapeDtypeStruct((g, m, ns), out_dtype)],
        scratch_shapes=[pltpu.VMEM((tm, tn), F32)],
        semantics=("parallel", "parallel", "arbitrary"), name=name, args=(a, b), jobs=jobs)[0]


def _ln_stats(z):
    mu = jnp.mean(z, axis=-1, keepdims=True)
    zc = z - mu
    var = jnp.mean(zc * zc, axis=-1, keepdims=True)
    rstd = lax.rsqrt(var + LN_EPS)
    return zc * rstd, rstd


def _ln_grad(dout, xhat, rstd, gain):
    dxhat = dout * gain
    m1 = jnp.mean(dxhat, axis=-1, keepdims=True)
    m2 = jnp.mean(dxhat * xhat, axis=-1, keepdims=True)
    return rstd * (dxhat - m1 - xhat * m2)


def _ln_fwd(xres, y, gain, bias, name, jobs=()):
    s, d = xres.shape
    tr = _tile(TILES["row"], s)

    def body(x_ref, y_ref, g_ref, b_ref, xn_ref, xnb_ref, xhat_ref, rstd_ref):
        xhat, rstd = _ln_stats(ALPHA * x_ref[...] + y_ref[...])
        out = xhat * g_ref[...] + b_ref[...]
        xn_ref[...] = out
        xnb_ref[...] = out.astype(BF16)
        xhat_ref[...] = xhat
        rstd_ref[...] = rstd

    row = pl.BlockSpec((tr, d), lambda i: (i, 0))
    vec = pl.BlockSpec((1, d), lambda i: (0, 0))
    return _call(
        body, grid=(s // tr,), in_specs=[row, row, vec, vec],
        out_specs=[row, row, row, pl.BlockSpec((tr, 1), lambda i: (i, 0))],
        out_shape=[jax.ShapeDtypeStruct((s, d), F32), jax.ShapeDtypeStruct((s, d), BF16),
                   jax.ShapeDtypeStruct((s, d), F32), jax.ShapeDtypeStruct((s, 1), F32)],
        semantics=("parallel",), name=name, args=(xres, y, gain, bias), jobs=jobs)


def _ln_bwd(dout, xhat, rstd, gain, name, jobs=()):
    s, d = dout.shape
    tr = _tile(TILES["row"], s)

    def body(d_ref, xhat_ref, rstd_ref, g_ref, dz_ref, dzb_ref, dg_ref, db_ref):
        @pl.when(pl.program_id(0) == 0)
        def _():
            dg_ref[...] = jnp.zeros_like(dg_ref)
            db_ref[...] = jnp.zeros_like(db_ref)

        dout_t, xhat_t = d_ref[...], xhat_ref[...]
        dz = _ln_grad(dout_t, xhat_t, rstd_ref[...], g_ref[...])
        dz_ref[...] = dz
        dzb_ref[...] = dz.astype(BF16)
        dg_ref[...] += jnp.sum(dout_t * xhat_t, axis=0, keepdims=True)
        db_ref[...] += jnp.sum(dout_t, axis=0, keepdims=True)

    row = pl.BlockSpec((tr, d), lambda i: (i, 0))
    vec = pl.BlockSpec((1, d), lambda i: (0, 0))
    return _call(
        body, grid=(s // tr,), in_specs=[row, row, pl.BlockSpec((tr, 1), lambda i: (i, 0)), vec],
        out_specs=[row, row, vec, vec],
        out_shape=[jax.ShapeDtypeStruct((s, d), F32), jax.ShapeDtypeStruct((s, d), BF16),
                   jax.ShapeDtypeStruct((1, d), F32), jax.ShapeDtypeStruct((1, d), F32)],
        semantics=("arbitrary",), name=name, args=(dout, xhat, rstd, gain), jobs=jobs)


def _ln_loss(xres, y, gain, bias, target, name, jobs=()):
    s, d = xres.shape
    tr = _tile(TILES["row"], s)

    def body(x_ref, y_ref, g_ref, b_ref, t_ref, loss_ref, dz_ref, dzb_ref, dg_ref, db_ref):
        @pl.when(pl.program_id(0) == 0)
        def _():
            loss_ref[...] = jnp.zeros_like(loss_ref)
            dg_ref[...] = jnp.zeros_like(dg_ref)
            db_ref[...] = jnp.zeros_like(db_ref)

        xhat, rstd = _ln_stats(ALPHA * x_ref[...] + y_ref[...])
        diff = xhat * g_ref[...] + b_ref[...] - t_ref[...]
        per_row = jnp.mean(diff * diff, axis=-1, keepdims=True)
        loss_ref[...] += 0.5 * jnp.sum(per_row, axis=0, keepdims=True)
        dout = diff * (1.0 / d)
        dz = _ln_grad(dout, xhat, rstd, g_ref[...])
        dz_ref[...] = dz
        dzb_ref[...] = dz.astype(BF16)
        dg_ref[...] += jnp.sum(dout * xhat, axis=0, keepdims=True)
        db_ref[...] += jnp.sum(dout, axis=0, keepdims=True)

    row = pl.BlockSpec((tr, d), lambda i: (i, 0))
    vec = pl.BlockSpec((1, d), lambda i: (0, 0))
    return _call(
        body, grid=(s // tr,), in_specs=[row, row, vec, vec, row],
        out_specs=[pl.BlockSpec((1, 128), lambda i: (0, 0)), row, row, vec, vec],
        out_shape=[jax.ShapeDtypeStruct((1, 128), F32), jax.ShapeDtypeStruct((s, d), F32),
                   jax.ShapeDtypeStruct((s, d), BF16), jax.ShapeDtypeStruct((1, d), F32),
                   jax.ShapeDtypeStruct((1, d), F32)],
        semantics=("arbitrary",), name=name, args=(xres, y, gain, bias, target), jobs=jobs)


def _softmax_rows(s):
    e = jnp.exp(s - jnp.max(s, axis=-1, keepdims=True))
    return e / jnp.sum(e, axis=-1, keepdims=True)


def _attn_fwd(q, k, v, name, jobs=()):
    s, d = q.shape
    m = k.shape[0]
    hd = d // XATTN_HEADS
    ts = _tile(TILES["attn"], s)
    scale = hd ** -0.5

    def body(q_ref, k_ref, v_ref, o_ref):
        for h in range(XATTN_HEADS):
            hs = slice(h * hd, (h + 1) * hd)
            sc = lax.dot_general(q_ref[:, hs], k_ref[:, hs], NT_DIMS, preferred_element_type=F32) * scale
            p = _softmax_rows(sc).astype(BF16)
            o_ref[:, hs] = jnp.dot(p, v_ref[:, hs], preferred_element_type=F32).astype(BF16)

    row = pl.BlockSpec((ts, d), lambda i: (i, 0))
    memb = pl.BlockSpec((m, d), lambda i: (0, 0))
    return _call(
        body, grid=(s // ts,), in_specs=[row, memb, memb], out_specs=[row],
        out_shape=[jax.ShapeDtypeStruct((s, d), BF16)],
        semantics=("parallel",), name=name, args=(q, k, v), jobs=jobs)[0]


def _attn_bwd(q, k, v, do, name, jobs=()):
    s, d = q.shape
    m = k.shape[0]
    hd = d // XATTN_HEADS
    ts = _tile(TILES["attn"], s)
    scale = hd ** -0.5

    def body(q_ref, k_ref, v_ref, do_ref, dq_ref, dk_ref, dv_ref):
        @pl.when(pl.program_id(0) == 0)
        def _():
            dk_ref[...] = jnp.zeros_like(dk_ref)
            dv_ref[...] = jnp.zeros_like(dv_ref)

        for h in range(XATTN_HEADS):
            hs = slice(h * hd, (h + 1) * hd)
            qh, kh, vh, doh = q_ref[:, hs], k_ref[:, hs], v_ref[:, hs], do_ref[:, hs]
            sc = lax.dot_general(qh, kh, NT_DIMS, preferred_element_type=F32) * scale
            p = _softmax_rows(sc)
            pb = p.astype(BF16)
            dp = lax.dot_general(doh, vh, NT_DIMS, preferred_element_type=F32)
            ds = (p * (dp - jnp.sum(dp * p, axis=-1, keepdims=True)) * scale).astype(BF16)
            dq_ref[:, hs] = jnp.dot(ds, kh, preferred_element_type=F32).astype(BF16)
            dk_ref[:, hs] += lax.dot_general(ds, qh, TN_DIMS, preferred_element_type=F32)
            dv_ref[:, hs] += lax.dot_general(pb, doh, TN_DIMS, preferred_element_type=F32)

    row = pl.BlockSpec((ts, d), lambda i: (i, 0))
    memb = pl.BlockSpec((m, d), lambda i: (0, 0))
    return _call(
        body, grid=(s // ts,), in_specs=[row, memb, memb, row], out_specs=[row, memb, memb],
        out_shape=[jax.ShapeDtypeStruct((s, d), BF16), jax.ShapeDtypeStruct((m, d), F32),
                   jax.ShapeDtypeStruct((m, d), F32)],
        semantics=("arbitrary",), name=name, args=(q, k, v, do), jobs=jobs)


def _sigmoid(x):
    return 1.0 / (1.0 + jnp.exp(-x))


def _log1p(x):
    u = 1.0 + x
    return jnp.where(u == 1.0, x, jnp.log(u) * (x / jnp.where(u == 1.0, 1.0, u - 1.0)))


def _softplus(x):
    return jnp.maximum(x, 0.0) + _log1p(jnp.exp(-jnp.abs(x)))


def _expm1(x):
    series = x * (1.0 + x * 0.5 * (1.0 + x * (1.0 / 3.0) * (1.0 + x * 0.25 * (1.0 + x * 0.2 * (1.0 + x * (1.0 / 6.0))))))
    return jnp.where(jnp.abs(x) < 0.1, series, jnp.exp(x) - 1.0)


GELU_K = 0.7978845608028654
GELU_C = 0.044715


def _gelu(x):
    return 0.5 * x * (1.0 + jnp.tanh(GELU_K * (x + GELU_C * (x * x * x))))


def _gelu_grad(x):
    th = jnp.tanh(GELU_K * (x + GELU_C * (x * x * x)))
    return 0.5 * (1.0 + th) + 0.5 * x * (1.0 - th * th) * GELU_K * (1.0 + 3.0 * GELU_C * x * x)


def _window_sum(ext_ref, first, rows, cols, w, step):
    acc = ext_ref[first:first + rows, cols]
    for kk in range(1, w):
        acc = acc + ext_ref[first + step * kk:first + step * kk + rows, cols]
    return acc


def _lru_gates(c_s, wa_ref, ba_ref, wx_ref, bx_ref, lam_ref, t_idx, hd, r_s, i_s, a_s, mult_s):
    sp = _softplus(-lam_ref[...])
    for h in range(LRU_HEADS):
        hs = slice(h * hd, (h + 1) * hd)
        chb = c_s[:, hs].astype(BF16)
        r = _sigmoid(jnp.dot(chb, wa_ref[h], preferred_element_type=F32) + ba_ref[:, hs])
        ig = _sigmoid(jnp.dot(chb, wx_ref[h], preferred_element_type=F32) + bx_ref[:, hs])
        log_a = -LRU_C * r * sp[:, hs]
        mult = jnp.sqrt(-_expm1(2.0 * log_a))
        r_s[:, hs] = r
        i_s[:, hs] = ig
        a_s[:, hs] = jnp.exp(log_a)
        mult_s[:, hs] = jnp.where(t_idx == 0, 1.0, mult)


def _conv(ext_ref, cw_ref, cb_ref, rows):
    acc = cb_ref[...] + cw_ref[0:1, :] * ext_ref[CONV_HALO - 3:CONV_HALO - 3 + rows, :]
    for kk in range(1, CONV_WIDTH):
        off = CONV_HALO - (CONV_WIDTH - 1) + kk
        acc = acc + cw_ref[kk:kk + 1, :] * ext_ref[off:off + rows, :]
    return acc


def _mixer_fwd(proj, wp, bp, ps, cw, cb, wa, ba, wx, bx, lam, name, jobs=()):
    s, p3 = proj.shape
    p = p3 // 3
    cg, hd = p // N_POOL_GROUPS, p // LRU_HEADS
    t = _tile(TILES["mixer"], s)

    def body(up_ref, ul_ref, ug_ref, wp_ref, bp_ref, ps_ref, cw_ref, cb_ref, wa_ref, ba_ref, wx_ref, bx_ref,
             lam_ref, ycat_ref, h_ref, extp, extl, hc, c_s, r_s, i_s, a_s, b_s):
        i = pl.program_id(0)

        @pl.when(i == 0)
        def _():
            extp[0:POOL_HALO, :] = jnp.zeros((POOL_HALO, p), F32)
            extl[0:CONV_HALO, :] = jnp.zeros((CONV_HALO, p), F32)
            hc[...] = jnp.zeros_like(hc)

        t_idx = i * t + lax.broadcasted_iota(jnp.int32, (t, 1), 0)

        extp[POOL_HALO:POOL_HALO + t, :] = up_ref[...]
        for g, w in enumerate(POOL_WINDOWS):
            cs = slice(g * cg, (g + 1) * cg)
            cnt = jnp.minimum(t_idx + 1, w).astype(F32)
            mixed = _window_sum(extp, POOL_HALO, t, cs, w, -1) / cnt - up_ref[:, cs]
            pre = jnp.dot(mixed.astype(BF16), wp_ref[g], preferred_element_type=F32) + bp_ref[:, cs]
            ycat_ref[:, cs] = (pre * ps_ref[:, cs]).astype(BF16)
        extp[0:POOL_HALO, :] = extp[t:t + POOL_HALO, :]

        extl[CONV_HALO:CONV_HALO + t, :] = ul_ref[...]
        c_s[...] = _conv(extl, cw_ref, cb_ref, t)
        extl[0:CONV_HALO, :] = extl[t:t + CONV_HALO, :]
        _lru_gates(c_s, wa_ref, ba_ref, wx_ref, bx_ref, lam_ref, t_idx, hd, r_s, i_s, a_s, b_s)
        b_s[...] = b_s[...] * (i_s[...] * c_s[...])

        rows = lax.broadcasted_iota(jnp.int32, (SUBLANES, p), 0)

        def block(bi, h):
            r0 = pl.multiple_of(bi * SUBLANES, SUBLANES)
            at = a_s[pl.ds(r0, SUBLANES), :]
            bt = b_s[pl.ds(r0, SUBLANES), :]
            out = jnp.zeros((SUBLANES, p), F32)
            for j in range(SUBLANES):
                h = at[j:j + 1, :] * h + bt[j:j + 1, :]
                out = jnp.where(rows == j, h, out)
            h_ref[pl.ds(r0, SUBLANES), :] = out
            return h

        hc[0:1, :] = lax.fori_loop(0, t // SUBLANES, block, hc[0:1, :])
        ycat_ref[:, p:2 * p] = (h_ref[...] * _gelu(ug_ref[...])).astype(BF16)

    def col(j):
        return pl.BlockSpec((t, p), lambda i: (i, j))

    def whole(a):
        nd = a.ndim
        return pl.BlockSpec(a.shape, lambda i: (0,) * nd)

    consts = (wp, bp, ps, cw, cb, wa, ba, wx, bx, lam)
    tile = pltpu.VMEM((t, p), F32)
    return _call(
        body, grid=(s // t,), in_specs=[col(0), col(1), col(2)] + [whole(a) for a in consts],
        out_specs=[pl.BlockSpec((t, 2 * p), lambda i: (i, 0)), pl.BlockSpec((t, p), lambda i: (i, 0))],
        out_shape=[jax.ShapeDtypeStruct((s, 2 * p), BF16), jax.ShapeDtypeStruct((s, p), F32)],
        scratch_shapes=[pltpu.VMEM((t + POOL_HALO, p), F32), pltpu.VMEM((t + CONV_HALO, p), F32),
                        pltpu.VMEM((SUBLANES, p), F32), tile, tile, tile, tile, tile],
        semantics=("arbitrary",), name=name, args=(proj, proj, proj, *consts), jobs=jobs)


def _mixer_bwd(dycat, proj, hsave, wp, bp, ps, cw, cb, wa, ba, wx, bx, lam, name, jobs=()):
    s, p3 = proj.shape
    p = p3 // 3
    cg, hd = p // N_POOL_GROUPS, p // LRU_HEADS
    t = _tile(TILES["mixer"], s)
    nt = s // t

    def body(dyp_ref, dyl_ref, up_ref, ul_ref, ug_ref, upp_ref, ulp_ref, h_ref, hp_ref,
             wp_ref, bp_ref, ps_ref, cw_ref, cb_ref, wa_ref, ba_ref, wx_ref, bx_ref, lam_ref,
             dproj_ref, dwp_ref, dbp_ref, dps_ref, dcw_ref, dcb_ref, dwa_ref, dba_ref, dwx_ref, dbx_ref, dlam_ref,
             extp, extg, extl, extdc, exth, ghc, c_s, r_s, i_s, a_s, mult_s, gh_s):
        i = pl.program_id(0)
        ib = nt - 1 - i

        @pl.when(i == 0)
        def _():
            for ref in (dwp_ref, dbp_ref, dps_ref, dcw_ref, dcb_ref, dwa_ref, dba_ref, dwx_ref, dbx_ref, dlam_ref):
                ref[...] = jnp.zeros_like(ref)
            extg[t:t + POOL_HALO, :] = jnp.zeros((POOL_HALO, p), F32)
            extdc[t:t + CONV_HALO, :] = jnp.zeros((CONV_HALO, p), F32)
            ghc[...] = jnp.zeros_like(ghc)

        t_idx = ib * t + lax.broadcasted_iota(jnp.int32, (t, 1), 0)
        seq_start = ib == 0

        extl[0:CONV_HALO, :] = jnp.where(seq_start, 0.0, ulp_ref[...])
        extl[CONV_HALO:CONV_HALO + t, :] = ul_ref[...]
        c_s[...] = _conv(extl, cw_ref, cb_ref, t)
        _lru_gates(c_s, wa_ref, ba_ref, wx_ref, bx_ref, lam_ref, t_idx, hd, r_s, i_s, a_s, mult_s)
        exth[0:SUBLANES, :] = jnp.where(seq_start, 0.0, hp_ref[...])
        exth[SUBLANES:SUBLANES + t, :] = h_ref[...]

        ug = ug_ref[...]
        dyl = dyl_ref[...]
        dproj_ref[:, 2 * p:3 * p] = (dyl * h_ref[...] * _gelu_grad(ug)).astype(BF16)
        gh_s[...] = dyl * _gelu(ug)

        rows = lax.broadcasted_iota(jnp.int32, (SUBLANES, p), 0)
        nblk = t // SUBLANES

        def block(bi, carry):
            r0 = pl.multiple_of((nblk - 1 - bi) * SUBLANES, SUBLANES)
            at = a_s[pl.ds(r0, SUBLANES), :]
            dt = gh_s[pl.ds(r0, SUBLANES), :]
            out = jnp.zeros((SUBLANES, p), F32)
            for j in range(SUBLANES - 1, -1, -1):
                gh = dt[j:j + 1, :] + carry
                out = jnp.where(rows == j, gh, out)
                carry = at[j:j + 1, :] * gh
            gh_s[pl.ds(r0, SUBLANES), :] = out
            return carry

        ghc[0:1, :] = lax.fori_loop(0, nblk, block, ghc[0:1, :])

        sp = _softplus(-lam_ref[...])
        dsp_dlam = -_sigmoid(-lam_ref[...])
        for h in range(LRU_HEADS):
            hs = slice(h * hd, (h + 1) * hd)
            gh, a, mult, r, ig, c = gh_s[:, hs], a_s[:, hs], mult_s[:, hs], r_s[:, hs], i_s[:, hs], c_s[:, hs]
            hprev = exth[SUBLANES - 1:SUBLANES - 1 + t, hs]
            dmult = gh * (ig * c)
            dlog_a = a * gh * hprev + jnp.where(t_idx == 0, 0.0, -dmult * a * a / mult)
            dlam_ref[:, hs] += jnp.sum(dlog_a * r, axis=0, keepdims=True) * (-LRU_C) * dsp_dlam[:, hs]
            dpa = dlog_a * (-LRU_C * sp[:, hs]) * r * (1.0 - r)
            dpx = gh * mult * c * ig * (1.0 - ig)
            dpab, dpxb, chb = dpa.astype(BF16), dpx.astype(BF16), c.astype(BF16)
            dwa_ref[h] += lax.dot_general(chb, dpab, TN_DIMS, preferred_element_type=F32)
            dwx_ref[h] += lax.dot_general(chb, dpxb, TN_DIMS, preferred_element_type=F32)
            dba_ref[:, hs] += jnp.sum(dpa, axis=0, keepdims=True)
            dbx_ref[:, hs] += jnp.sum(dpx, axis=0, keepdims=True)
            dc = (gh * mult * ig
                  + lax.dot_general(dpab, wa_ref[h], NT_DIMS, preferred_element_type=F32)
                  + lax.dot_general(dpxb, wx_ref[h], NT_DIMS, preferred_element_type=F32))
            extdc[0:t, hs] = dc
            dcb_ref[:, hs] += jnp.sum(dc, axis=0, keepdims=True)
            for kk in range(CONV_WIDTH):
                off = CONV_HALO - (CONV_WIDTH - 1) + kk
                dcw_ref[kk:kk + 1, hs] += jnp.sum(dc * extl[off:off + t, hs], axis=0, keepdims=True)
        du_lru = cw_ref[0:1, :] * extdc[CONV_WIDTH - 1:CONV_WIDTH - 1 + t, :]
        for kk in range(1, CONV_WIDTH):
            off = CONV_WIDTH - 1 - kk
            du_lru = du_lru + cw_ref[kk:kk + 1, :] * extdc[off:off + t, :]
        dproj_ref[:, p:2 * p] = du_lru.astype(BF16)
        extdc[t:t + CONV_HALO, :] = extdc[0:CONV_HALO, :]

        extp[0:POOL_HALO, :] = jnp.where(seq_start, 0.0, upp_ref[...])
        extp[POOL_HALO:POOL_HALO + t, :] = up_ref[...]
        for g, w in enumerate(POOL_WINDOWS):
            cs = slice(g * cg, (g + 1) * cg)
            cnt = jnp.minimum(t_idx + 1, w).astype(F32)
            mixed = (_window_sum(extp, POOL_HALO, t, cs, w, -1) / cnt - up_ref[:, cs]).astype(BF16)
            pre = jnp.dot(mixed, wp_ref[g], preferred_element_type=F32) + bp_ref[:, cs]
            dyp = dyp_ref[:, cs]
            dps_ref[:, cs] += jnp.sum(dyp * pre, axis=0, keepdims=True)
            dpre = dyp * ps_ref[:, cs]
            dpreb = dpre.astype(BF16)
            dbp_ref[:, cs] += jnp.sum(dpre, axis=0, keepdims=True)
            dwp_ref[g] += lax.dot_general(mixed, dpreb, TN_DIMS, preferred_element_type=F32)
            dmixed = lax.dot_general(dpreb, wp_ref[g], NT_DIMS, preferred_element_type=F32)
            extg[0:t, cs] = dmixed / cnt
            dproj_ref[:, cs] = (_window_sum(extg, 0, t, cs, w, 1) - dmixed).astype(BF16)
        extg[t:t + POOL_HALO, :] = extg[0:POOL_HALO, :]

    def col(j):
        return pl.BlockSpec((t, p), lambda i: (nt - 1 - i, j))

    def prev(rows, j):
        per = t // rows
        return pl.BlockSpec((rows, p), lambda i: (jnp.maximum((nt - 1 - i) * per - 1, 0), j))

    def whole(a):
        nd = a.ndim
        return pl.BlockSpec(a.shape, lambda i: (0,) * nd)

    consts = (wp, bp, ps, cw, cb, wa, ba, wx, bx, lam)
    grads = (wp, bp, ps, cw, cb, wa, ba, wx, bx, lam)
    tile = pltpu.VMEM((t, p), F32)
    return _call(
        body, grid=(nt,),
        in_specs=[col(0), col(1), col(0), col(1), col(2), prev(POOL_HALO, 0), prev(CONV_HALO, 1), col(0),
                  prev(SUBLANES, 0)] + [whole(a) for a in consts],
        out_specs=[pl.BlockSpec((t, 3 * p), lambda i: (nt - 1 - i, 0))] + [whole(a) for a in grads],
        out_shape=[jax.ShapeDtypeStruct((s, 3 * p), BF16)] + [jax.ShapeDtypeStruct(a.shape, F32) for a in grads],
        scratch_shapes=[pltpu.VMEM((t + POOL_HALO, p), F32), pltpu.VMEM((t + POOL_HALO, p), F32),
                        pltpu.VMEM((t + CONV_HALO, p), F32), pltpu.VMEM((t + CONV_HALO, p), F32),
                        pltpu.VMEM((t + SUBLANES, p), F32), pltpu.VMEM((SUBLANES, p), F32),
                        tile, tile, tile, tile, tile, tile],
        semantics=("arbitrary",), name=name,
        args=(dycat, dycat, proj, proj, proj, proj, proj, hsave, hsave, *consts), jobs=jobs)


def _pair_add(parts, got, core, name):
    n, r, c = got.shape
    tr = _tile(TILES["row"], r)

    def body(core_ref, a_ref, b_ref, o_ref):
        del core_ref
        o_ref[...] = (a_ref[...].astype(F32) + b_ref[...].astype(F32)).astype(o_ref.dtype)

    blk = pl.BlockSpec((None, tr, c), lambda k, i, core_ref: (k, i, 0))
    mine = pl.BlockSpec((None, tr, c), lambda k, i, core_ref: (2 * k + core_ref[0], i, 0))
    return _call(body, grid=(n, r // tr), in_specs=[mine, blk], out_specs=[blk],
                 out_shape=[jax.ShapeDtypeStruct(got.shape, got.dtype)], semantics=("parallel", "parallel"),
                 name=name, args=(parts, got), index=core)[0]


def _sum_parts(parts, name):
    n, r, c = parts.shape
    tr = _tile(TILES["adam"], r)

    def body(p_ref, o_ref):
        acc = p_ref[0].astype(F32)
        for d in range(1, n):
            acc = acc + p_ref[d].astype(F32)
        o_ref[...] = acc

    return _call(
        body, grid=(r // tr,), in_specs=[pl.BlockSpec((n, tr, c), lambda i: (0, i, 0))],
        out_specs=[pl.BlockSpec((tr, c), lambda i: (i, 0))], out_shape=[jax.ShapeDtypeStruct((r, c), F32)],
        semantics=("parallel",), name=name, args=(parts,))[0]


def _adamw(w, m, v, parts, name, jobs=(), own=None, chip=None):
    r, c = w.shape
    n = parts.shape[0]
    tr = _tile(TILES["adam"], r)

    def body(*refs):
        if own is not None:
            refs = refs[1:]
            own_ref, refs = refs[3], refs[:3] + refs[4:]
        w_ref, m_ref, v_ref, p_ref, g_ref, d_ref, nm_ref, nv_ref = refs
        g = p_ref[0].astype(F32)
        if own is not None:
            g = own_ref[...].astype(F32) + g
        for d in range(1, n):
            g = g + p_ref[d].astype(F32)
        nm = ADAM_B1 * m_ref[...] + (1.0 - ADAM_B1) * g
        nv = ADAM_B2 * v_ref[...] + (1.0 - ADAM_B2) * (g * g)
        m_hat = nm / (1.0 - ADAM_B1 ** ADAM_STEP)
        v_hat = nv / (1.0 - ADAM_B2 ** ADAM_STEP)
        g_ref[...] = g
        d_ref[...] = -ADAM_LR * (m_hat / (jnp.sqrt(v_hat) + ADAM_EPS) + ADAM_WD * w_ref[...])
        nm_ref[...] = nm
        nv_ref[...] = nv

    row = pl.BlockSpec((tr, c), lambda i, *_: (i, 0))
    in_specs, args = [row, row, row], [w, m, v]
    if own is not None:
        in_specs.append(pl.BlockSpec((None, tr, c), lambda i, chip_ref: (chip_ref[0], i, 0)))
        args.append(own)
    in_specs.append(pl.BlockSpec((n, tr, c), lambda i, *_: (0, i, 0)))
    args.append(parts)
    return _call(
        body, grid=(r // tr,), in_specs=in_specs, out_specs=[row] * 4, out_shape=[jax.ShapeDtypeStruct((r, c), F32)] * 4,
        semantics=("parallel",), name=name, args=args, jobs=jobs, index=chip if own is not None else None)


SMALL_ORDER = ("w_a", "w_x", "conv_w", "b_pool", "conv_b", "b_a", "b_x", "lru_lambda", "pool_scale",
               "ln1_g", "ln1_b", "ln2_g", "ln2_b", "ln3_g", "ln3_b")


def _pack_rows(a, p):
    flat = a.reshape(-1, p)
    pad = (-flat.shape[0]) % SUBLANES
    return jnp.pad(flat, ((0, pad), (0, 0))) if pad else flat


def kernel(x, mem, w_in, conv_w, conv_b, w_a, b_a, w_x, b_x, lru_lambda, w_pool, b_pool, pool_scale, w_out, ln1_g, ln1_b, w_q, w_k, w_v, w_o, ln2_g, ln2_b, w_ff1, w_ff2, ln3_g, ln3_b, loss_target, m_w_in, m_conv_w, m_conv_b, m_w_a, m_b_a, m_w_x, m_b_x, m_lru_lambda, m_w_pool, m_b_pool, m_pool_scale, m_w_out, m_ln1_g, m_ln1_b, m_w_q, m_w_k, m_w_v, m_w_o, m_ln2_g, m_ln2_b, m_w_ff1, m_w_ff2, m_ln3_g, m_ln3_b, v_w_in, v_conv_w, v_conv_b, v_w_a, v_b_a, v_w_x, v_b_x, v_lru_lambda, v_w_pool, v_b_pool, v_pool_scale, v_w_out, v_ln1_g, v_ln1_b, v_w_q, v_w_k, v_w_v, v_w_o, v_ln2_g, v_ln2_b, v_w_ff1, v_w_ff2, v_ln3_g, v_ln3_b):
    names = ("w_in", "conv_w", "conv_b", "w_a", "b_a", "w_x", "b_x", "lru_lambda", "w_pool", "b_pool", "pool_scale",
             "w_out", "ln1_g", "ln1_b", "w_q", "w_k", "w_v", "w_o", "ln2_g", "ln2_b", "w_ff1", "w_ff2", "ln3_g", "ln3_b")
    w_loc = dict(zip(names, (w_in, conv_w, conv_b, w_a, b_a, w_x, b_x, lru_lambda, w_pool, b_pool, pool_scale,
                             w_out, ln1_g, ln1_b, w_q, w_k, w_v, w_o, ln2_g, ln2_b, w_ff1, w_ff2, ln3_g, ln3_b)))
    m_loc = dict(zip(names, (m_w_in, m_conv_w, m_conv_b, m_w_a, m_b_a, m_w_x, m_b_x, m_lru_lambda, m_w_pool, m_b_pool,
                             m_pool_scale, m_w_out, m_ln1_g, m_ln1_b, m_w_q, m_w_k, m_w_v, m_w_o, m_ln2_g, m_ln2_b,
                             m_w_ff1, m_w_ff2, m_ln3_g, m_ln3_b)))
    v_loc = dict(zip(names, (v_w_in, v_conv_w, v_conv_b, v_w_a, v_b_a, v_w_x, v_b_x, v_lru_lambda, v_w_pool, v_b_pool,
                             v_pool_scale, v_w_out, v_ln1_g, v_ln1_b, v_w_q, v_w_k, v_w_v, v_w_o, v_ln2_g, v_ln2_b,
                             v_w_ff1, v_w_ff2, v_ln3_g, v_ln3_b)))
    s, d = x.shape[1], x.shape[2]
    p = conv_b.shape[1]
    cg = p // N_POOL_GROUPS
    hd = p // LRU_HEADS
    me = 4 * lax.axis_index("x") + 2 * lax.axis_index("y") + lax.axis_index("c")

    xs, mems, tgt = x[0], mem[0], loss_target[0]
    xb, memb = xs.astype(BF16), mems.astype(BF16)

    gathers = {n: _Job("gather", w_loc[n][0].astype(BF16))
               for n in ("w_in", "w_out", "w_q", "w_k", "w_v", "w_o", "w_ff1", "w_ff2", "w_pool")}
    tiny = jnp.concatenate([_pack_rows(conv_w[0], p // N_DEV),
                            _pack_rows(jnp.pad(b_pool[0], ((0, 0), (0, p // N_DEV - cg // N_DEV))), p // N_DEV)], axis=0)
    gathers["tiny"] = _Job("gather", tiny)

    def gathered(n):
        full = gathers[n].out
        if n in ("w_in", "w_ff1"):
            return full
        return full.reshape(1, -1, full.shape[-1])

    W = {"conv_b": conv_b, "b_a": b_a.reshape(1, p), "b_x": b_x.reshape(1, p), "lru_lambda": lru_lambda,
         "pool_scale": pool_scale, "w_a": w_a[0].astype(BF16), "w_x": w_x[0].astype(BF16)}
    for n in ("ln1_g", "ln1_b", "ln2_g", "ln2_b", "ln3_g", "ln3_b"):
        W[n] = w_loc[n]

    out_g, out_d, out_m, out_v = {}, {}, {}, {}
    pairs, quads = {}, {}
    core = lax.axis_index("c").astype(jnp.int32).reshape(1)
    chip = (2 * lax.axis_index("x") + lax.axis_index("y")).astype(jnp.int32).reshape(1)

    def pair(n, partial):
        pairs[n] = _Job("pair", partial.reshape(N_DEV, -1, partial.shape[-1]))
        return pairs[n]

    def quad(n):
        quads[n] = _Job("quad", _pair_add(pairs[n].src, pairs[n].out, core, "add_" + n))
        return quads[n]

    def update(n, parts, jobs=(), own=None):
        shp = w_loc[n].shape
        rows = parts.shape[1]
        w2, m2, v2 = (a.reshape(rows, -1) for a in (w_loc[n], m_loc[n], v_loc[n]))
        res = _adamw(w2, m2, v2, parts.reshape(parts.shape[0], rows, -1), "adamw_" + n, jobs=jobs, own=own, chip=chip)
        out_g[n], out_d[n], out_m[n], out_v[n] = (r.reshape(shp) for r in res)

    _run_jobs([gathers["w_in"], gathers["tiny"], gathers["w_pool"]], "gather_first")
    W["w_pool"] = jnp.transpose(gathers["w_pool"].out, (1, 0, 2, 3)).reshape(N_POOL_GROUPS, cg, cg)
    cwb = gathers["tiny"].out
    W["conv_w"] = jnp.transpose(cwb[:, :CONV_WIDTH, :], (1, 0, 2)).reshape(CONV_WIDTH, p)
    W["b_pool"] = jnp.transpose(cwb[:, SUBLANES:SUBLANES + N_POOL_GROUPS, :cg // N_DEV], (1, 0, 2)).reshape(1, p)
    mixer_consts = (W["w_pool"], W["b_pool"], W["pool_scale"], W["conv_w"], W["conv_b"], W["w_a"], W["b_a"],
                    W["w_x"], W["b_x"], W["lru_lambda"])

    (proj,) = _mm_nn(xb, gathered("w_in"), [F32], "fwd_proj", jobs=[gathers["w_ff1"]])
    ycat, hsave = _mixer_fwd(proj, *mixer_consts, "fwd_mixer", jobs=[gathers["w_out"], gathers["w_o"]])
    (y1,) = _mm_nn(ycat, gathered("w_out"), [F32], "fwd_out", jobs=[gathers["w_q"]])
    x1, x1b, xhat1, rstd1 = _ln_fwd(xs, y1, W["ln1_g"], W["ln1_b"], "fwd_ln1", jobs=[gathers["w_k"]])
    (q,) = _mm_nn(x1b, gathered("w_q"), [BF16], "fwd_q", jobs=[gathers["w_v"]])
    (k,) = _mm_nn(memb, gathered("w_k"), [BF16], "fwd_k")
    (v,) = _mm_nn(memb, gathered("w_v"), [BF16], "fwd_v")
    o = _attn_fwd(q, k, v, "fwd_attn")
    (y2,) = _mm_nn(o, gathered("w_o"), [F32], "fwd_o")
    x2, x2b, xhat2, rstd2 = _ln_fwd(x1, y2, W["ln2_g"], W["ln2_b"], "fwd_ln2")

    def relu_sq(acc):
        r = jnp.maximum(acc, 0.0)
        return r, r * r

    rb, act = _mm_nn(x2b, gathered("w_ff1"), [BF16, BF16], "fwd_ff1", epilogue=relu_sq, jobs=[gathers["w_ff2"]])
    (y3,) = _mm_nn(act, gathered("w_ff2"), [F32], "fwd_ff2")
    loss_rows, dz3, dz3b, dg3, db3 = _ln_loss(x2, y3, W["ln3_g"], W["ln3_b"], tgt, "ln3_loss")
    loss = lax.psum(loss_rows[0, 0], MESH_AXES)

    small = {"ln3_g": dg3, "ln3_b": db3}

    def add_residual(acc, e):
        return (acc + ALPHA * e,)

    dw_ff2 = _mm_tn(act, dz3b, 1, BF16, "bwd_dw_ff2")
    (dhid,) = _mm_nt(dz3b, gathered("w_ff2"), [BF16], "bwd_dact", extras=(rb,), jobs=[pair("w_ff2", dw_ff2)],
                     epilogue=lambda acc, r: (acc * (2.0 * r.astype(F32)),))
    dw_ff1 = _mm_tn(x2b, dhid, N_DEV, BF16, "bwd_dw_ff1", jobs=[quad("w_ff2")])
    (dx2,) = _mm_nt(dhid, gathered("w_ff1"), [F32], "bwd_dx2", epilogue=add_residual, extras=(dz3,),
                    jobs=[pair("w_ff1", dw_ff1)])
    dz2, dz2b, small["ln2_g"], small["ln2_b"] = _ln_bwd(dx2, xhat2, rstd2, W["ln2_g"], "bwd_ln2")

    dw_o = _mm_tn(o, dz2b, 1, BF16, "bwd_dw_o")
    (do,) = _mm_nt(dz2b, gathered("w_o"), [BF16], "bwd_do", jobs=[pair("w_o", dw_o)])
    dq, dk, dv = _attn_bwd(q, k, v, do, "bwd_attn", jobs=[quad("w_o")])
    dw_q = _mm_tn(x1b, dq, 1, BF16, "bwd_dw_q")
    dw_k = _mm_tn(memb, dk.astype(BF16), 1, BF16, "bwd_dw_k")
    dw_v = _mm_tn(memb, dv.astype(BF16), 1, BF16, "bwd_dw_v")
    (dx1,) = _mm_nt(dq, gathered("w_q"), [F32], "bwd_dx1", epilogue=add_residual, extras=(dz2,),
                    jobs=[pair("w_q", dw_q), pair("w_k", dw_k), pair("w_v", dw_v)])
    dz1, dz1b, small["ln1_g"], small["ln1_b"] = _ln_bwd(dx1, xhat1, rstd1, W["ln1_g"], "bwd_ln1", jobs=[quad("w_q")])

    dw_out = _mm_tn(ycat, dz1b, 1, BF16, "bwd_dw_out", jobs=[quad("w_k")])
    (dycat,) = _mm_nt(dz1b, gathered("w_out"), [F32], "bwd_dycat", jobs=[quad("w_v"), pair("w_out", dw_out)])
    (dproj, dwp, small["b_pool"], small["pool_scale"], small["conv_w"], small["conv_b"], small["w_a"], small["b_a"],
     small["w_x"], small["b_x"], small["lru_lambda"]) = _mixer_bwd(
        dycat, proj, hsave, *mixer_consts, "bwd_mixer", jobs=[quad("w_ff1")])
    dw_pool = jnp.transpose(dwp.astype(BF16).reshape(N_POOL_GROUPS, N_DEV, cg // N_DEV, cg), (1, 0, 2, 3))
    pack = jnp.concatenate([_pack_rows(small[n], p) for n in SMALL_ORDER], axis=0)
    small_gather = _Job("gather", pack)
    dw_in = _mm_tn(xb, dproj, N_DEV, BF16, "bwd_dw_in", jobs=[quad("w_out"), pair("w_pool", dw_pool), small_gather])
    (grad_x,) = _mm_nt(dproj, gathered("w_in"), [F32], "bwd_dx", epilogue=add_residual, extras=(dz1,),
                       jobs=[pair("w_in", dw_in), quad("w_pool")])

    update("w_ff2", quads["w_ff2"].out, jobs=[quad("w_in")], own=quads["w_ff2"].src)
    for n in ("w_ff1", "w_o", "w_q", "w_k", "w_v", "w_out", "w_pool", "w_in"):
        update(n, quads[n].out, own=quads[n].src)

    total = _sum_parts(small_gather.out, "sum_small")
    row = 0
    for n in SMALL_ORDER:
        size = small[n].size
        nrows = size // p
        g_full = total[row:row + nrows].reshape(small[n].shape)
        row += nrows + (-nrows) % SUBLANES
        if n == "conv_w":
            g_loc = lax.dynamic_slice_in_dim(g_full, me * (p // N_DEV), p // N_DEV, axis=1)
        elif n == "b_pool":
            g_loc = lax.dynamic_slice_in_dim(g_full.reshape(N_POOL_GROUPS, cg), me * (cg // N_DEV), cg // N_DEV, axis=1)
        else:
            g_loc = g_full
        rows = g_loc.shape[0] if n not in ("w_a", "w_x") else LRU_HEADS * hd
        update(n, g_loc.reshape(1, rows, -1))

    order = names
    return (loss, grad_x[None], *[out_g[n] for n in order], *[out_d[n] for n in order],
            *[out_m[n] for n in order], *[out_v[n] for n in order])
```

```python
import functools

import jax
import jax.numpy as jnp
from jax import lax
from jax.experimental import pallas as pl
from jax.experimental.pallas import tpu as pltpu

F32 = jnp.float32
BF16 = jnp.bfloat16

N_DEV = 8
MESH_AXES = ("x", "y", "c")
POOL_WINDOWS = (2, 4, 8, 16)
N_POOL_GROUPS = len(POOL_WINDOWS)
POOL_HALO = 16
CONV_WIDTH = 4
CONV_HALO = 8
LRU_HEADS = 8
LRU_C = 8.0
XATTN_HEADS = 4
LN_EPS = 1e-5
ALPHA = 2.0 ** 0.25
ADAM_LR = 0.001
ADAM_B1 = 0.9
ADAM_B2 = 0.999
ADAM_EPS = 1e-08
ADAM_WD = 0.01
ADAM_STEP = 10
SUBLANES = 8
VMEM_LIMIT = 56 * 1024 * 1024

NT_DIMS = (((1,), (1,)), ((), ()))
TN_DIMS = (((0,), (0,)), ((), ()))


def _params(*sem):
    return pltpu.CompilerParams(dimension_semantics=sem, vmem_limit_bytes=VMEM_LIMIT)


def _place():
    return lax.axis_index("x"), lax.axis_index("y"), lax.axis_index("c")


def _remote(src, dst, send_sem, recv_sem, to):
    return pltpu.make_async_remote_copy(src_ref=src, dst_ref=dst, send_sem=send_sem, recv_sem=recv_sem,
                                        device_id=to, device_id_type=pl.DeviceIdType.MESH)


class _Job:
    def __init__(self, kind, src):
        self.kind, self.src, self.out = kind, src, None

    def out_shape(self):
        s = self.src.shape
        shape = {"gather": (N_DEV,) + s, "pair": (4,) + s[1:], "quad": (3,) + s[1:]}[self.kind]
        return jax.ShapeDtypeStruct(shape, self.src.dtype)

    def scratch(self):
        n = {"gather": 7, "pair": 4, "quad": 3}[self.kind]
        sems = [pltpu.SemaphoreType.DMA((n,)), pltpu.SemaphoreType.DMA((n,))]
        if self.kind == "gather":
            sems += [pltpu.SemaphoreType.DMA((2,)), pltpu.VMEM(self.src.shape, self.src.dtype)]
        return sems

    def ops(self, src, out, *scratch):
        return {"gather": _gather_ops, "pair": _pair_ops, "quad": _quad_ops}[self.kind](src, out, *scratch)


def _gather_ops(x_ref, out_ref, send_sems, recv_sems, local_sems, bounce):
    x, y, c = _place()
    me, sibling = (x, y, c), (x, y, 1 - c)
    chips = [(1 - x, y), (x, 1 - y), (1 - x, 1 - y)]

    def slot(px, py, pc):
        return out_ref.at[4 * px + 2 * py + pc]

    def copy(k, block, to, src=None):
        return _remote(slot(*block) if src is None else src, slot(*block), send_sems.at[k], recv_sems.at[k], to)

    mine_in = pltpu.make_async_copy(x_ref, bounce, local_sems.at[0])
    mine_out = pltpu.make_async_copy(bounce, slot(*me), local_sems.at[1])
    first = [copy(0, me, sibling, src=x_ref)] + [copy(1 + j, me, (*chip, c), src=x_ref) for j, chip in enumerate(chips)]
    passed = [copy(4 + j, (*chip, c), sibling) for j, chip in enumerate(chips)]

    def start():
        mine_in.start()
        for cp in first:
            cp.start()

    def finish():
        mine_in.wait()
        mine_out.start()
        for j, chip in enumerate(chips):
            copy(1 + j, (*chip, c), me).wait_recv()
            passed[j].start()
        copy(0, sibling, me).wait_recv()
        for j, chip in enumerate(chips):
            copy(4 + j, (*chip, 1 - c), me).wait_recv()
        for cp in first + passed:
            cp.wait_send()
        mine_out.wait()

    return start, finish


def _pair_ops(p_ref, got_ref, send_sems, recv_sems):
    x, y, c = _place()
    give = [_remote(p_ref.at[2 * k + 1 - c], got_ref.at[k], send_sems.at[k], recv_sems.at[k], (x, y, 1 - c))
            for k in range(4)]

    def start():
        for cp in give:
            cp.start()

    def finish():
        for cp in give:
            cp.wait_recv()
        for cp in give:
            cp.wait_send()

    return start, finish


def _quad_ops(q_ref, out_ref, send_sems, recv_sems):
    x, y, c = _place()
    copies = []
    for rel in range(1, 4):
        px = 1 - x if rel & 2 else x
        py = 1 - y if rel & 1 else y
        copies.append(_remote(q_ref.at[2 * px + py], out_ref.at[rel - 1], send_sems.at[rel - 1], recv_sems.at[rel - 1],
                              (px, py, c)))

    def start():
        for cp in copies:
            cp.start()

    def finish():
        for cp in copies:
            cp.wait_recv()
        for cp in copies:
            cp.wait_send()

    return start, finish


def _call(body, *, grid, in_specs, out_specs, out_shape, scratch_shapes=(), semantics, name, args, jobs=(), index=None):
    in_specs, out_specs, out_shape = list(in_specs), list(out_specs), list(out_shape)
    scratch_shapes, jobs = list(scratch_shapes), list(jobs)
    n_in, n_out, n_scr, n_job = len(in_specs), len(out_specs), len(scratch_shapes), len(jobs)
    n_idx = 0 if index is None else 1
    job_scratch = [j.scratch() for j in jobs]

    def hosted(*refs):
        idx, refs = refs[:n_idx], refs[n_idx:]
        ins, jin = refs[:n_in], refs[n_in:n_in + n_job]
        o0 = n_in + n_job
        outs, jout = refs[o0:o0 + n_out], refs[o0 + n_out:o0 + n_out + n_job]
        s0 = o0 + n_out + n_job
        scr, jscr = refs[s0:s0 + n_scr], refs[s0 + n_scr:]
        ops, at = [], 0
        for k, j in enumerate(jobs):
            ops.append(j.ops(jin[k], jout[k], *jscr[at:at + len(job_scratch[k])]))
            at += len(job_scratch[k])
        first = functools.reduce(jnp.logical_and, [pl.program_id(a) == 0 for a in range(len(grid))])
        last = functools.reduce(jnp.logical_and, [pl.program_id(a) == g - 1 for a, g in enumerate(grid)])

        @pl.when(first)
        def _():
            for start, _ in ops:
                start()

        body(*idx, *ins, *outs, *scr)

        @pl.when(last)
        def _():
            for _, finish in ops:
                finish()

    hbm = pl.BlockSpec(memory_space=pl.ANY)
    spec = pltpu.PrefetchScalarGridSpec(
        num_scalar_prefetch=n_idx, grid=grid, in_specs=in_specs + [hbm] * n_job, out_specs=out_specs + [hbm] * n_job,
        scratch_shapes=scratch_shapes + [s for js in job_scratch for s in js])
    res = pl.pallas_call(
        hosted if jobs else body, grid_spec=spec, out_shape=out_shape + [j.out_shape() for j in jobs],
        compiler_params=_params(*(["arbitrary"] * len(grid) if jobs else semantics)), name=name,
    )(*([] if index is None else [index]), *args, *[j.src for j in jobs])
    for j, o in zip(jobs, res[n_out:]):
        j.out = o
    return res[:n_out]


def _run_jobs(jobs, name):
    def body(tick_ref):
        tick_ref[...] = jnp.zeros_like(tick_ref)

    _call(body, grid=(1,), in_specs=[], out_specs=[pl.BlockSpec((SUBLANES, 128), lambda i: (0, 0))],
          out_shape=[jax.ShapeDtypeStruct((SUBLANES, 128), F32)], semantics=("arbitrary",), name=name, args=(), jobs=jobs)


TILES = dict(tm=1024, tn=1024, tk=1024, row=256, attn=512, mixer=256, adam=128, add=1024)


def _tile(pref, n):
    for t in range(min(pref, n), 0, -1):
        if n % t == 0 and (t % SUBLANES == 0 or t == n):
            return t
    return n


def _mm_nn(a, b3, out_dtypes, name, *, tm=None, tn=None, tk=None, epilogue=None, extras=(), jobs=()):
    m, k = a.shape
    g, k2, ns = b3.shape
    assert k == k2
    n = g * ns
    tm, tn, tk = _tile(tm or TILES["tm"], m), _tile(tn or TILES["tn"], ns), _tile(tk or TILES["tk"], k)
    nb, nk = ns // tn, k // tk
    n_ex, n_out = len(extras), len(out_dtypes)

    def body(*refs):
        a_ref, b_ref = refs[:2]
        ex = refs[2:2 + n_ex]
        outs = refs[2 + n_ex:2 + n_ex + n_out]
        acc = refs[-1]
        kk = pl.program_id(2)

        @pl.when(kk == 0)
        def _():
            acc[...] = jnp.zeros_like(acc)

        acc[...] += jnp.dot(a_ref[...], b_ref[...], preferred_element_type=F32)

        @pl.when(kk == nk - 1)
        def _():
            r = acc[...]
            res = epilogue(r, *[e[...] for e in ex]) if epilogue is not None else (r,)
            for o, v in zip(outs, res):
                o[...] = v.astype(o.dtype)

    tile_out = pl.BlockSpec((tm, tn), lambda i, j, kk: (i, j))
    return _call(
        body, grid=(m // tm, n // tn, nk),
        in_specs=[pl.BlockSpec((tm, tk), lambda i, j, kk: (i, kk)),
                  pl.BlockSpec((None, tk, tn), lambda i, j, kk: (j // nb, kk, j % nb))] + [tile_out] * n_ex,
        out_specs=[tile_out] * n_out,
        out_shape=[jax.ShapeDtypeStruct((m, n), d) for d in out_dtypes],
        scratch_shapes=[pltpu.VMEM((tm, tn), F32)],
        semantics=("parallel", "parallel", "arbitrary"), name=name, args=(a, b3, *extras), jobs=jobs)


def _mm_nt(a, b3, out_dtypes, name, *, tm=None, tn=None, tk=None, epilogue=None, extras=(), jobs=()):
    m, n = a.shape
    g, k, ns = b3.shape
    assert n == g * ns
    tm, tn, tk = _tile(tm or TILES["tm"], m), _tile(tn or TILES["tn"], k), _tile(tk or TILES["tk"], ns)
    nb, nc = ns // tk, n // tk
    n_ex, n_out = len(extras), len(out_dtypes)

    def body(*refs):
        a_ref, b_ref = refs[:2]
        ex = refs[2:2 + n_ex]
        outs = refs[2 + n_ex:2 + n_ex + n_out]
        acc = refs[-1]
        cc = pl.program_id(2)

        @pl.when(cc == 0)
        def _():
            acc[...] = jnp.zeros_like(acc)

        acc[...] += lax.dot_general(a_ref[...], b_ref[...], NT_DIMS, preferred_element_type=F32)

        @pl.when(cc == nc - 1)
        def _():
            r = acc[...]
            res = epilogue(r, *[e[...] for e in ex]) if epilogue is not None else (r,)
            for o, v in zip(outs, res):
                o[...] = v.astype(o.dtype)

    tile_out = pl.BlockSpec((tm, tn), lambda i, j, cc: (i, j))
    return _call(
        body, grid=(m // tm, k // tn, nc),
        in_specs=[pl.BlockSpec((tm, tk), lambda i, j, cc: (i, cc)),
                  pl.BlockSpec((None, tn, tk), lambda i, j, cc: (cc // nb, j, cc % nb))] + [tile_out] * n_ex,
        out_specs=[tile_out] * n_out,
        out_shape=[jax.ShapeDtypeStruct((m, k), d) for d in out_dtypes],
        scratch_shapes=[pltpu.VMEM((tm, tn), F32)],
        semantics=("parallel", "parallel", "arbitrary"), name=name, args=(a, b3, *extras), jobs=jobs)


def _mm_tn(a, b, g, out_dtype, name, *, tm=None, tn=None, tk=None, jobs=()):
    s, m = a.shape
    s2, n = b.shape
    assert s == s2 and n % g == 0
    ns = n // g
    tm, tn, tk = _tile(tm or TILES["tm"], m), _tile(tn or TILES["tn"], ns), _tile(tk or TILES["tk"], s)
    nb, nc = ns // tn, s // tk

    def body(a_ref, b_ref, o_ref, acc):
        cc = pl.program_id(2)

        @pl.when(cc == 0)
        def _():
            acc[...] = jnp.zeros_like(acc)

        acc[...] += lax.dot_general(a_ref[...], b_ref[...], TN_DIMS, preferred_element_type=F32)

        @pl.when(cc == nc - 1)
        def _():
            o_ref[...] = acc[...].astype(o_ref.dtype)

    return _call(
        body, grid=(m // tm, n // tn, nc),
        in_specs=[pl.BlockSpec((tk, tm), lambda i, j, cc: (cc, i)),
                  pl.BlockSpec((tk, tn), lambda i, j, cc: (cc, j))],
        out_specs=[pl.BlockSpec((None, tm, tn), lambda i, j, cc: (j // nb, i, j % nb))],
        out_shape=[jax.ShapeDtypeStruct((g, m, ns), out_dtype)],
        scratch_shapes=[pltpu.VMEM((tm, tn), F32)],
        semantics=("parallel", "parallel", "arbitrary"), name=name, args=(a, b), jobs=jobs)[0]


def _ln_stats(z):
    mu = jnp.mean(z, axis=-1, keepdims=True)
    zc = z - mu
    var = jnp.mean(zc * zc, axis=-1, keepdims=True)
    rstd = lax.rsqrt(var + LN_EPS)
    return zc * rstd, rstd


def _ln_grad(dout, xhat, rstd, gain):
    dxhat = dout * gain
    m1 = jnp.mean(dxhat, axis=-1, keepdims=True)
    m2 = jnp.mean(dxhat * xhat, axis=-1, keepdims=True)
    return rstd * (dxhat - m1 - xhat * m2)


def _ln_fwd(xres, y, gain, bias, name, jobs=()):
    s, d = xres.shape
    tr = _tile(TILES["row"], s)

    def body(x_ref, y_ref, g_ref, b_ref, xn_ref, xnb_ref, xhat_ref, rstd_ref):
        xhat, rstd = _ln_stats(ALPHA * x_ref[...] + y_ref[...])
        out = xhat * g_ref[...] + b_ref[...]
        xn_ref[...] = out
        xnb_ref[...] = out.astype(BF16)
        xhat_ref[...] = xhat
        rstd_ref[...] = rstd

    row = pl.BlockSpec((tr, d), lambda i: (i, 0))
    vec = pl.BlockSpec((1, d), lambda i: (0, 0))
    return _call(
        body, grid=(s // tr,), in_specs=[row, row, vec, vec],
        out_specs=[row, row, row, pl.BlockSpec((tr, 1), lambda i: (i, 0))],
        out_shape=[jax.ShapeDtypeStruct((s, d), F32), jax.ShapeDtypeStruct((s, d), BF16),
                   jax.ShapeDtypeStruct((s, d), F32), jax.ShapeDtypeStruct((s, 1), F32)],
        semantics=("parallel",), name=name, args=(xres, y, gain, bias), jobs=jobs)


def _ln_bwd(dout, xhat, rstd, gain, name, jobs=()):
    s, d = dout.shape
    tr = _tile(TILES["row"], s)

    def body(d_ref, xhat_ref, rstd_ref, g_ref, dz_ref, dzb_ref, dg_ref, db_ref):
        @pl.when(pl.program_id(0) == 0)
        def _():
            dg_ref[...] = jnp.zeros_like(dg_ref)
            db_ref[...] = jnp.zeros_like(db_ref)

        dout_t, xhat_t = d_ref[...], xhat_ref[...]
        dz = _ln_grad(dout_t, xhat_t, rstd_ref[...], g_ref[...])
        dz_ref[...] = dz
        dzb_ref[...] = dz.astype(BF16)
        dg_ref[...] += jnp.sum(dout_t * xhat_t, axis=0, keepdims=True)
        db_ref[...] += jnp.sum(dout_t, axis=0, keepdims=True)

    row = pl.BlockSpec((tr, d), lambda i: (i, 0))
    vec = pl.BlockSpec((1, d), lambda i: (0, 0))
    return _call(
        body, grid=(s // tr,), in_specs=[row, row, pl.BlockSpec((tr, 1), lambda i: (i, 0)), vec],
        out_specs=[row, row, vec, vec],
        out_shape=[jax.ShapeDtypeStruct((s, d), F32), jax.ShapeDtypeStruct((s, d), BF16),
                   jax.ShapeDtypeStruct((1, d), F32), jax.ShapeDtypeStruct((1, d), F32)],
        semantics=("arbitrary",), name=name, args=(dout, xhat, rstd, gain), jobs=jobs)


def _ln_loss(xres, y, gain, bias, target, name, jobs=()):
    s, d = xres.shape
    tr = _tile(TILES["row"], s)

    def body(x_ref, y_ref, g_ref, b_ref, t_ref, loss_ref, dz_ref, dzb_ref, dg_ref, db_ref):
        @pl.when(pl.program_id(0) == 0)
        def _():
            loss_ref[...] = jnp.zeros_like(loss_ref)
            dg_ref[...] = jnp.zeros_like(dg_ref)
            db_ref[...] = jnp.zeros_like(db_ref)

        xhat, rstd = _ln_stats(ALPHA * x_ref[...] + y_ref[...])
        diff = xhat * g_ref[...] + b_ref[...] - t_ref[...]
        per_row = jnp.mean(diff * diff, axis=-1, keepdims=True)
        loss_ref[...] += 0.5 * jnp.sum(per_row, axis=0, keepdims=True)
        dout = diff * (1.0 / d)
        dz = _ln_grad(dout, xhat, rstd, g_ref[...])
        dz_ref[...] = dz
        dzb_ref[...] = dz.astype(BF16)
        dg_ref[...] += jnp.sum(dout * xhat, axis=0, keepdims=True)
        db_ref[...] += jnp.sum(dout, axis=0, keepdims=True)

    row = pl.BlockSpec((tr, d), lambda i: (i, 0))
    vec = pl.BlockSpec((1, d), lambda i: (0, 0))
    return _call(
        body, grid=(s // tr,), in_specs=[row, row, vec, vec, row],
        out_specs=[pl.BlockSpec((1, 128), lambda i: (0, 0)), row, row, vec, vec],
        out_shape=[jax.ShapeDtypeStruct((1, 128), F32), jax.ShapeDtypeStruct((s, d), F32),
                   jax.ShapeDtypeStruct((s, d), BF16), jax.ShapeDtypeStruct((1, d), F32),
                   jax.ShapeDtypeStruct((1, d), F32)],
        semantics=("arbitrary",), name=name, args=(xres, y, gain, bias, target), jobs=jobs)


def _softmax_rows(s):
    e = jnp.exp(s - jnp.max(s, axis=-1, keepdims=True))
    return e / jnp.sum(e, axis=-1, keepdims=True)


def _attn_fwd(q, k, v, name, jobs=()):
    s, d = q.shape
    m = k.shape[0]
    hd = d // XATTN_HEADS
    ts = _tile(TILES["attn"], s)
    scale = hd ** -0.5

    def body(q_ref, k_ref, v_ref, o_ref):
        for h in range(XATTN_HEADS):
            hs = slice(h * hd, (h + 1) * hd)
            sc = lax.dot_general(q_ref[:, hs], k_ref[:, hs], NT_DIMS, preferred_element_type=F32) * scale
            p = _softmax_rows(sc).astype(BF16)
            o_ref[:, hs] = jnp.dot(p, v_ref[:, hs], preferred_element_type=F32).astype(BF16)

    row = pl.BlockSpec((ts, d), lambda i: (i, 0))
    memb = pl.BlockSpec((m, d), lambda i: (0, 0))
    return _call(
        body, grid=(s // ts,), in_specs=[row, memb, memb], out_specs=[row],
        out_shape=[jax.ShapeDtypeStruct((s, d), BF16)],
        semantics=("parallel",), name=name, args=(q, k, v), jobs=jobs)[0]


def _attn_bwd(q, k, v, do, name, jobs=()):
    s, d = q.shape
    m = k.shape[0]
    hd = d // XATTN_HEADS
    ts = _tile(TILES["attn"], s)
    scale = hd ** -0.5

    def body(q_ref, k_ref, v_ref, do_ref, dq_ref, dk_ref, dv_ref):
        @pl.when(pl.program_id(0) == 0)
        def _():
            dk_ref[...] = jnp.zeros_like(dk_ref)
            dv_ref[...] = jnp.zeros_like(dv_ref)

        for h in range(XATTN_HEADS):
            hs = slice(h * hd, (h + 1) * hd)
            qh, kh, vh, doh = q_ref[:, hs], k_ref[:, hs], v_ref[:, hs], do_ref[:, hs]
            sc = lax.dot_general(qh, kh, NT_DIMS, preferred_element_type=F32) * scale
            p = _softmax_rows(sc)
            pb = p.astype(BF16)
            dp = lax.dot_general(doh, vh, NT_DIMS, preferred_element_type=F32)
            ds = (p * (dp - jnp.sum(dp * p, axis=-1, keepdims=True)) * scale).astype(BF16)
            dq_ref[:, hs] = jnp.dot(ds, kh, preferred_element_type=F32).astype(BF16)
            dk_ref[:, hs] += lax.dot_general(ds, qh, TN_DIMS, preferred_element_type=F32)
            dv_ref[:, hs] += lax.dot_general(pb, doh, TN_DIMS, preferred_element_type=F32)

    row = pl.BlockSpec((ts, d), lambda i: (i, 0))
    memb = pl.BlockSpec((m, d), lambda i: (0, 0))
    return _call(
        body, grid=(s // ts,), in_specs=[row, memb, memb, row], out_specs=[row, memb, memb],
        out_shape=[jax.ShapeDtypeStruct((s, d), BF16), jax.ShapeDtypeStruct((m, d), F32),
                   jax.ShapeDtypeStruct((m, d), F32)],
        semantics=("arbitrary",), name=name, args=(q, k, v, do), jobs=jobs)


def _sigmoid(x):
    return 1.0 / (1.0 + jnp.exp(-x))


def _log1p(x):
    u = 1.0 + x
    return jnp.where(u == 1.0, x, jnp.log(u) * (x / jnp.where(u == 1.0, 1.0, u - 1.0)))


def _softplus(x):
    return jnp.maximum(x, 0.0) + _log1p(jnp.exp(-jnp.abs(x)))


def _expm1(x):
    series = x * (1.0 + x * 0.5 * (1.0 + x * (1.0 / 3.0) * (1.0 + x * 0.25 * (1.0 + x * 0.2 * (1.0 + x * (1.0 / 6.0))))))
    return jnp.where(jnp.abs(x) < 0.1, series, jnp.exp(x) - 1.0)


GELU_K = 0.7978845608028654
GELU_C = 0.044715


def _gelu(x):
    return 0.5 * x * (1.0 + jnp.tanh(GELU_K * (x + GELU_C * (x * x * x))))


def _gelu_grad(x):
    th = jnp.tanh(GELU_K * (x + GELU_C * (x * x * x)))
    return 0.5 * (1.0 + th) + 0.5 * x * (1.0 - th * th) * GELU_K * (1.0 + 3.0 * GELU_C * x * x)


def _window_sum(ext_ref, first, rows, cols, w, step):
    acc = ext_ref[first:first + rows, cols]
    for kk in range(1, w):
        acc = acc + ext_ref[first + step * kk:first + step * kk + rows, cols]
    return acc


def _lru_gates(c_s, wa_ref, ba_ref, wx_ref, bx_ref, lam_ref, t_idx, hd, r_s, i_s, a_s, mult_s):
    sp = _softplus(-lam_ref[...])
    for h in range(LRU_HEADS):
        hs = slice(h * hd, (h + 1) * hd)
        chb = c_s[:, hs].astype(BF16)
        r = _sigmoid(jnp.dot(chb, wa_ref[h], preferred_element_type=F32) + ba_ref[:, hs])
        ig = _sigmoid(jnp.dot(chb, wx_ref[h], preferred_element_type=F32) + bx_ref[:, hs])
        log_a = -LRU_C * r * sp[:, hs]
        mult = jnp.sqrt(-_expm1(2.0 * log_a))
        r_s[:, hs] = r
        i_s[:, hs] = ig
        a_s[:, hs] = jnp.exp(log_a)
        mult_s[:, hs] = jnp.where(t_idx == 0, 1.0, mult)


def _conv(ext_ref, cw_ref, cb_ref, rows):
    acc = cb_ref[...] + cw_ref[0:1, :] * ext_ref[CONV_HALO - 3:CONV_HALO - 3 + rows, :]
    for kk in range(1, CONV_WIDTH):
        off = CONV_HALO - (CONV_WIDTH - 1) + kk
        acc = acc + cw_ref[kk:kk + 1, :] * ext_ref[off:off + rows, :]
    return acc


def _mixer_fwd(proj, wp, bp, ps, cw, cb, wa, ba, wx, bx, lam, name, jobs=()):
    s, p3 = proj.shape
    p = p3 // 3
    cg, hd = p // N_POOL_GROUPS, p // LRU_HEADS
    t = _tile(TILES["mixer"], s)

    def body(up_ref, ul_ref, ug_ref, wp_ref, bp_ref, ps_ref, cw_ref, cb_ref, wa_ref, ba_ref, wx_ref, bx_ref,
             lam_ref, ycat_ref, h_ref, extp, extl, hc, c_s, r_s, i_s, a_s, b_s):
        i = pl.program_id(0)

        @pl.when(i == 0)
        def _():
            extp[0:POOL_HALO, :] = jnp.zeros((POOL_HALO, p), F32)
            extl[0:CONV_HALO, :] = jnp.zeros((CONV_HALO, p), F32)
            hc[...] = jnp.zeros_like(hc)

        t_idx = i * t + lax.broadcasted_iota(jnp.int32, (t, 1), 0)

        extp[POOL_HALO:POOL_HALO + t, :] = up_ref[...]
        for g, w in enumerate(POOL_WINDOWS):
            cs = slice(g * cg, (g + 1) * cg)
            cnt = jnp.minimum(t_idx + 1, w).astype(F32)
            mixed = _window_sum(extp, POOL_HALO, t, cs, w, -1) / cnt - up_ref[:, cs]
            pre = jnp.dot(mixed.astype(BF16), wp_ref[g], preferred_element_type=F32) + bp_ref[:, cs]
            ycat_ref[:, cs] = (pre * ps_ref[:, cs]).astype(BF16)
        extp[0:POOL_HALO, :] = extp[t:t + POOL_HALO, :]

        extl[CONV_HALO:CONV_HALO + t, :] = ul_ref[...]
        c_s[...] = _conv(extl, cw_ref, cb_ref, t)
        extl[0:CONV_HALO, :] = extl[t:t + CONV_HALO, :]
        _lru_gates(c_s, wa_ref, ba_ref, wx_ref, bx_ref, lam_ref, t_idx, hd, r_s, i_s, a_s, b_s)
        b_s[...] = b_s[...] * (i_s[...] * c_s[...])

        rows = lax.broadcasted_iota(jnp.int32, (SUBLANES, p), 0)

        def block(bi, h):
            r0 = pl.multiple_of(bi * SUBLANES, SUBLANES)
            at = a_s[pl.ds(r0, SUBLANES), :]
            bt = b_s[pl.ds(r0, SUBLANES), :]
            out = jnp.zeros((SUBLANES, p), F32)
            for j in range(SUBLANES):
                h = at[j:j + 1, :] * h + bt[j:j + 1, :]
                out = jnp.where(rows == j, h, out)
            h_ref[pl.ds(r0, SUBLANES), :] = out
            return h

        hc[0:1, :] = lax.fori_loop(0, t // SUBLANES, block, hc[0:1, :])
        ycat_ref[:, p:2 * p] = (h_ref[...] * _gelu(ug_ref[...])).astype(BF16)

    def col(j):
        return pl.BlockSpec((t, p), lambda i: (i, j))

    def whole(a):
        nd = a.ndim
        return pl.BlockSpec(a.shape, lambda i: (0,) * nd)

    consts = (wp, bp, ps, cw, cb, wa, ba, wx, bx, lam)
    tile = pltpu.VMEM((t, p), F32)
    return _call(
        body, grid=(s // t,), in_specs=[col(0), col(1), col(2)] + [whole(a) for a in consts],
        out_specs=[pl.BlockSpec((t, 2 * p), lambda i: (i, 0)), pl.BlockSpec((t, p), lambda i: (i, 0))],
        out_shape=[jax.ShapeDtypeStruct((s, 2 * p), BF16), jax.ShapeDtypeStruct((s, p), F32)],
        scratch_shapes=[pltpu.VMEM((t + POOL_HALO, p), F32), pltpu.VMEM((t + CONV_HALO, p), F32),
                        pltpu.VMEM((SUBLANES, p), F32), tile, tile, tile, tile, tile],
        semantics=("arbitrary",), name=name, args=(proj, proj, proj, *consts), jobs=jobs)


def _mixer_bwd(dycat, proj, hsave, wp, bp, ps, cw, cb, wa, ba, wx, bx, lam, name, jobs=()):
    s, p3 = proj.shape
    p = p3 // 3
    cg, hd = p // N_POOL_GROUPS, p // LRU_HEADS
    t = _tile(TILES["mixer"], s)
    nt = s // t

    def body(dyp_ref, dyl_ref, up_ref, ul_ref, ug_ref, upp_ref, ulp_ref, h_ref, hp_ref,
             wp_ref, bp_ref, ps_ref, cw_ref, cb_ref, wa_ref, ba_ref, wx_ref, bx_ref, lam_ref,
             dproj_ref, dwp_ref, dbp_ref, dps_ref, dcw_ref, dcb_ref, dwa_ref, dba_ref, dwx_ref, dbx_ref, dlam_ref,
             extp, extg, extl, extdc, exth, ghc, c_s, r_s, i_s, a_s, mult_s, gh_s):
        i = pl.program_id(0)
        ib = nt - 1 - i

        @pl.when(i == 0)
        def _():
            for ref in (dwp_ref, dbp_ref, dps_ref, dcw_ref, dcb_ref, dwa_ref, dba_ref, dwx_ref, dbx_ref, dlam_ref):
                ref[...] = jnp.zeros_like(ref)
            extg[t:t + POOL_HALO, :] = jnp.zeros((POOL_HALO, p), F32)
            extdc[t:t + CONV_HALO, :] = jnp.zeros((CONV_HALO, p), F32)
            ghc[...] = jnp.zeros_like(ghc)

        t_idx = ib * t + lax.broadcasted_iota(jnp.int32, (t, 1), 0)
        seq_start = ib == 0

        extl[0:CONV_HALO, :] = jnp.where(seq_start, 0.0, ulp_ref[...])
        extl[CONV_HALO:CONV_HALO + t, :] = ul_ref[...]
        c_s[...] = _conv(extl, cw_ref, cb_ref, t)
        _lru_gates(c_s, wa_ref, ba_ref, wx_ref, bx_ref, lam_ref, t_idx, hd, r_s, i_s, a_s, mult_s)
        exth[0:SUBLANES, :] = jnp.where(seq_start, 0.0, hp_ref[...])
        exth[SUBLANES:SUBLANES + t, :] = h_ref[...]

        ug = ug_ref[...]
        dyl = dyl_ref[...]
        dproj_ref[:, 2 * p:3 * p] = (dyl * h_ref[...] * _gelu_grad(ug)).astype(BF16)
        gh_s[...] = dyl * _gelu(ug)

        rows = lax.broadcasted_iota(jnp.int32, (SUBLANES, p), 0)
        nblk = t // SUBLANES

        def block(bi, carry):
            r0 = pl.multiple_of((nblk - 1 - bi) * SUBLANES, SUBLANES)
            at = a_s[pl.ds(r0, SUBLANES), :]
            dt = gh_s[pl.ds(r0, SUBLANES), :]
            out = jnp.zeros((SUBLANES, p), F32)
            for j in range(SUBLANES - 1, -1, -1):
                gh = dt[j:j + 1, :] + carry
                out = jnp.where(rows == j, gh, out)
                carry = at[j:j + 1, :] * gh
            gh_s[pl.ds(r0, SUBLANES), :] = out
            return carry

        ghc[0:1, :] = lax.fori_loop(0, nblk, block, ghc[0:1, :])

        sp = _softplus(-lam_ref[...])
        dsp_dlam = -_sigmoid(-lam_ref[...])
        for h in range(LRU_HEADS):
            hs = slice(h * hd, (h + 1) * hd)
            gh, a, mult, r, ig, c = gh_s[:, hs], a_s[:, hs], mult_s[:, hs], r_s[:, hs], i_s[:, hs], c_s[:, hs]
            hprev = exth[SUBLANES - 1:SUBLANES - 1 + t, hs]
            dmult = gh * (ig * c)
            dlog_a = a * gh * hprev + jnp.where(t_idx == 0, 0.0, -dmult * a * a / mult)
            dlam_ref[:, hs] += jnp.sum(dlog_a * r, axis=0, keepdims=True) * (-LRU_C) * dsp_dlam[:, hs]
            dpa = dlog_a * (-LRU_C * sp[:, hs]) * r * (1.0 - r)
            dpx = gh * mult * c * ig * (1.0 - ig)
            dpab, dpxb, chb = dpa.astype(BF16), dpx.astype(BF16), c.astype(BF16)
            dwa_ref[h] += lax.dot_general(chb, dpab, TN_DIMS, preferred_element_type=F32)
            dwx_ref[h] += lax.dot_general(chb, dpxb, TN_DIMS, preferred_element_type=F32)
            dba_ref[:, hs] += jnp.sum(dpa, axis=0, keepdims=True)
            dbx_ref[:, hs] += jnp.sum(dpx, axis=0, keepdims=True)
            dc = (gh * mult * ig
                  + lax.dot_general(dpab, wa_ref[h], NT_DIMS, preferred_element_type=F32)
                  + lax.dot_general(dpxb, wx_ref[h], NT_DIMS, preferred_element_type=F32))
            extdc[0:t, hs] = dc
            dcb_ref[:, hs] += jnp.sum(dc, axis=0, keepdims=True)
            for kk in range(CONV_WIDTH):
                off = CONV_HALO - (CONV_WIDTH - 1) + kk
                dcw_ref[kk:kk + 1, hs] += jnp.sum(dc * extl[off:off + t, hs], axis=0, keepdims=True)
        du_lru = cw_ref[0:1, :] * extdc[CONV_WIDTH - 1:CONV_WIDTH - 1 + t, :]
        for kk in range(1, CONV_WIDTH):
            off = CONV_WIDTH - 1 - kk
            du_lru = du_lru + cw_ref[kk:kk + 1, :] * extdc[off:off + t, :]
        dproj_ref[:, p:2 * p] = du_lru.astype(BF16)
        extdc[t:t + CONV_HALO, :] = extdc[0:CONV_HALO, :]

        extp[0:POOL_HALO, :] = jnp.where(seq_start, 0.0, upp_ref[...])
        extp[POOL_HALO:POOL_HALO + t, :] = up_ref[...]
        for g, w in enumerate(POOL_WINDOWS):
            cs = slice(g * cg, (g + 1) * cg)
            cnt = jnp.minimum(t_idx + 1, w).astype(F32)
            mixed = (_window_sum(extp, POOL_HALO, t, cs, w, -1) / cnt - up_ref[:, cs]).astype(BF16)
            pre = jnp.dot(mixed, wp_ref[g], preferred_element_type=F32) + bp_ref[:, cs]
            dyp = dyp_ref[:, cs]
            dps_ref[:, cs] += jnp.sum(dyp * pre, axis=0, keepdims=True)
            dpre = dyp * ps_ref[:, cs]
            dpreb = dpre.astype(BF16)
            dbp_ref[:, cs] += jnp.sum(dpre, axis=0, keepdims=True)
            dwp_ref[g] += lax.dot_general(mixed, dpreb, TN_DIMS, preferred_element_type=F32)
            dmixed = lax.dot_general(dpreb, wp_ref[g], NT_DIMS, preferred_element_type=F32)
            extg[0:t, cs] = dmixed / cnt
            dproj_ref[:, cs] = (_window_sum(extg, 0, t, cs, w, 1) - dmixed).astype(BF16)
        extg[t:t + POOL_HALO, :] = extg[0:POOL_HALO, :]

    def col(j):
        return pl.BlockSpec((t, p), lambda i: (nt - 1 - i, j))

    def prev(rows, j):
        per = t // rows
        return pl.BlockSpec((rows, p), lambda i: (jnp.maximum((nt - 1 - i) * per - 1, 0), j))

    def whole(a):
        nd = a.ndim
        return pl.BlockSpec(a.shape, lambda i: (0,) * nd)

    consts = (wp, bp, ps, cw, cb, wa, ba, wx, bx, lam)
    grads = (wp, bp, ps, cw, cb, wa, ba, wx, bx, lam)
    tile = pltpu.VMEM((t, p), F32)
    return _call(
        body, grid=(nt,),
        in_specs=[col(0), col(1), col(0), col(1), col(2), prev(POOL_HALO, 0), prev(CONV_HALO, 1), col(0),
                  prev(SUBLANES, 0)] + [whole(a) for a in consts],
        out_specs=[pl.BlockSpec((t, 3 * p), lambda i: (nt - 1 - i, 0))] + [whole(a) for a in grads],
        out_shape=[jax.ShapeDtypeStruct((s, 3 * p), BF16)] + [jax.ShapeDtypeStruct(a.shape, F32) for a in grads],
        scratch_shapes=[pltpu.VMEM((t + POOL_HALO, p), F32), pltpu.VMEM((t + POOL_HALO, p), F32),
                        pltpu.VMEM((t + CONV_HALO, p), F32), pltpu.VMEM((t + CONV_HALO, p), F32),
                        pltpu.VMEM((t + SUBLANES, p), F32), pltpu.VMEM((SUBLANES, p), F32),
                        tile, tile, tile, tile, tile, tile],
        semantics=("arbitrary",), name=name,
        args=(dycat, dycat, proj, proj, proj, proj, proj, hsave, hsave, *consts), jobs=jobs)


def _pair_add(parts, got, core, name):
    n, r, c = got.shape
    tr = _tile(TILES["add"], r)

    def body(core_ref, a_ref, b_ref, o_ref):
        del core_ref
        o_ref[...] = (a_ref[...].astype(F32) + b_ref[...].astype(F32)).astype(o_ref.dtype)

    blk = pl.BlockSpec((None, tr, c), lambda k, i, core_ref: (k, i, 0))
    mine = pl.BlockSpec((None, tr, c), lambda k, i, core_ref: (2 * k + core_ref[0], i, 0))
    return _call(body, grid=(n, r // tr), in_specs=[mine, blk], out_specs=[blk],
                 out_shape=[jax.ShapeDtypeStruct(got.shape, got.dtype)], semantics=("parallel", "parallel"),
                 name=name, args=(parts, got), index=core)[0]


def _sum_parts(parts, name):
    n, r, c = parts.shape
    tr = _tile(TILES["adam"], r)

    def body(p_ref, o_ref):
        acc = p_ref[0].astype(F32)
        for d in range(1, n):
            acc = acc + p_ref[d].astype(F32)
        o_ref[...] = acc

    return _call(
        body, grid=(r // tr,), in_specs=[pl.BlockSpec((n, tr, c), lambda i: (0, i, 0))],
        out_specs=[pl.BlockSpec((tr, c), lambda i: (i, 0))], out_shape=[jax.ShapeDtypeStruct((r, c), F32)],
        semantics=("parallel",), name=name, args=(parts,))[0]


def _adamw(w, m, v, parts, name, jobs=(), own=None, chip=None):
    r, c = w.shape
    n = parts.shape[0]
    tr = _tile(TILES["adam"], r)

    def body(*refs):
        if own is not None:
            refs = refs[1:]
            own_ref, refs = refs[3], refs[:3] + refs[4:]
        w_ref, m_ref, v_ref, p_ref, g_ref, d_ref, nm_ref, nv_ref = refs
        g = p_ref[0].astype(F32)
        if own is not None:
            g = own_ref[...].astype(F32) + g
        for d in range(1, n):
            g = g + p_ref[d].astype(F32)
        nm = ADAM_B1 * m_ref[...] + (1.0 - ADAM_B1) * g
        nv = ADAM_B2 * v_ref[...] + (1.0 - ADAM_B2) * (g * g)
        m_hat = nm / (1.0 - ADAM_B1 ** ADAM_STEP)
        v_hat = nv / (1.0 - ADAM_B2 ** ADAM_STEP)
        g_ref[...] = g
        d_ref[...] = -ADAM_LR * (m_hat / (jnp.sqrt(v_hat) + ADAM_EPS) + ADAM_WD * w_ref[...])
        nm_ref[...] = nm
        nv_ref[...] = nv

    row = pl.BlockSpec((tr, c), lambda i, *_: (i, 0))
    in_specs, args = [row, row, row], [w, m, v]
    if own is not None:
        in_specs.append(pl.BlockSpec((None, tr, c), lambda i, chip_ref: (chip_ref[0], i, 0)))
        args.append(own)
    in_specs.append(pl.BlockSpec((n, tr, c), lambda i, *_: (0, i, 0)))
    args.append(parts)
    return _call(
        body, grid=(r // tr,), in_specs=in_specs, out_specs=[row] * 4, out_shape=[jax.ShapeDtypeStruct((r, c), F32)] * 4,
        semantics=("parallel",), name=name, args=args, jobs=jobs, index=chip if own is not None else None)


SMALL_ORDER = ("w_a", "w_x", "conv_w", "b_pool", "conv_b", "b_a", "b_x", "lru_lambda", "pool_scale",
               "ln1_g", "ln1_b", "ln2_g", "ln2_b", "ln3_g", "ln3_b")


def _pack_rows(a, p):
    flat = a.reshape(-1, p)
    pad = (-flat.shape[0]) % SUBLANES
    return jnp.pad(flat, ((0, pad), (0, 0))) if pad else flat


def kernel(x, mem, w_in, conv_w, conv_b, w_a, b_a, w_x, b_x, lru_lambda, w_pool, b_pool, pool_scale, w_out, ln1_g, ln1_b, w_q, w_k, w_v, w_o, ln2_g, ln2_b, w_ff1, w_ff2, ln3_g, ln3_b, loss_target, m_w_in, m_conv_w, m_conv_b, m_w_a, m_b_a, m_w_x, m_b_x, m_lru_lambda, m_w_pool, m_b_pool, m_pool_scale, m_w_out, m_ln1_g, m_ln1_b, m_w_q, m_w_k, m_w_v, m_w_o, m_ln2_g, m_ln2_b, m_w_ff1, m_w_ff2, m_ln3_g, m_ln3_b, v_w_in, v_conv_w, v_conv_b, v_w_a, v_b_a, v_w_x, v_b_x, v_lru_lambda, v_w_pool, v_b_pool, v_pool_scale, v_w_out, v_ln1_g, v_ln1_b, v_w_q, v_w_k, v_w_v, v_w_o, v_ln2_g, v_ln2_b, v_w_ff1, v_w_ff2, v_ln3_g, v_ln3_b):
    names = ("w_in", "conv_w", "conv_b", "w_a", "b_a", "w_x", "b_x", "lru_lambda", "w_pool", "b_pool", "pool_scale",
             "w_out", "ln1_g", "ln1_b", "w_q", "w_k", "w_v", "w_o", "ln2_g", "ln2_b", "w_ff1", "w_ff2", "ln3_g", "ln3_b")
    w_loc = dict(zip(names, (w_in, conv_w, conv_b, w_a, b_a, w_x, b_x, lru_lambda, w_pool, b_pool, pool_scale,
                             w_out, ln1_g, ln1_b, w_q, w_k, w_v, w_o, ln2_g, ln2_b, w_ff1, w_ff2, ln3_g, ln3_b)))
    m_loc = dict(zip(names, (m_w_in, m_conv_w, m_conv_b, m_w_a, m_b_a, m_w_x, m_b_x, m_lru_lambda, m_w_pool, m_b_pool,
                             m_pool_scale, m_w_out, m_ln1_g, m_ln1_b, m_w_q, m_w_k, m_w_v, m_w_o, m_ln2_g, m_ln2_b,
                             m_w_ff1, m_w_ff2, m_ln3_g, m_ln3_b)))
    v_loc = dict(zip(names, (v_w_in, v_conv_w, v_conv_b, v_w_a, v_b_a, v_w_x, v_b_x, v_lru_lambda, v_w_pool, v_b_pool,
                             v_pool_scale, v_w_out, v_ln1_g, v_ln1_b, v_w_q, v_w_k, v_w_v, v_w_o, v_ln2_g, v_ln2_b,
                             v_w_ff1, v_w_ff2, v_ln3_g, v_ln3_b)))
    s, d = x.shape[1], x.shape[2]
    p = conv_b.shape[1]
    cg = p // N_POOL_GROUPS
    hd = p // LRU_HEADS
    me = 4 * lax.axis_index("x") + 2 * lax.axis_index("y") + lax.axis_index("c")

    xs, mems, tgt = x[0], mem[0], loss_target[0]
    xb, memb = xs.astype(BF16), mems.astype(BF16)

    gathers = {n: _Job("gather", w_loc[n][0].astype(BF16))
               for n in ("w_in", "w_out", "w_q", "w_k", "w_v", "w_o", "w_ff1", "w_ff2", "w_pool")}
    tiny = jnp.concatenate([_pack_rows(conv_w[0], p // N_DEV),
                            _pack_rows(jnp.pad(b_pool[0], ((0, 0), (0, p // N_DEV - cg // N_DEV))), p // N_DEV)], axis=0)
    gathers["tiny"] = _Job("gather", tiny)

    def gathered(n):
        full = gathers[n].out
        if n == "w_ff1":
            return full
        if n == "w_in":
            return jnp.transpose(full, (1, 0, 2)).reshape(1, full.shape[1], -1)
        return full.reshape(1, -1, full.shape[-1])

    W = {"conv_b": conv_b, "b_a": b_a.reshape(1, p), "b_x": b_x.reshape(1, p), "lru_lambda": lru_lambda,
         "pool_scale": pool_scale, "w_a": w_a[0].astype(BF16), "w_x": w_x[0].astype(BF16)}
    for n in ("ln1_g", "ln1_b", "ln2_g", "ln2_b", "ln3_g", "ln3_b"):
        W[n] = w_loc[n]

    out_g, out_d, out_m, out_v = {}, {}, {}, {}
    pairs, quads = {}, {}
    core = lax.axis_index("c").astype(jnp.int32).reshape(1)
    chip = (2 * lax.axis_index("x") + lax.axis_index("y")).astype(jnp.int32).reshape(1)

    def pair(n, partial):
        pairs[n] = _Job("pair", partial.reshape(N_DEV, -1, partial.shape[-1]))
        return pairs[n]

    def quad(n):
        quads[n] = _Job("quad", _pair_add(pairs[n].src, pairs[n].out, core, "add_" + n))
        return quads[n]

    def update(n, parts, jobs=(), own=None):
        shp = w_loc[n].shape
        rows = parts.shape[1]
        w2, m2, v2 = (a.reshape(rows, -1) for a in (w_loc[n], m_loc[n], v_loc[n]))
        res = _adamw(w2, m2, v2, parts.reshape(parts.shape[0], rows, -1), "adamw_" + n, jobs=jobs, own=own, chip=chip)
        out_g[n], out_d[n], out_m[n], out_v[n] = (r.reshape(shp) for r in res)

    _run_jobs([gathers["w_in"], gathers["tiny"], gathers["w_pool"]], "gather_first")
    W["w_pool"] = jnp.transpose(gathers["w_pool"].out, (1, 0, 2, 3)).reshape(N_POOL_GROUPS, cg, cg)
    cwb = gathers["tiny"].out
    W["conv_w"] = jnp.transpose(cwb[:, :CONV_WIDTH, :], (1, 0, 2)).reshape(CONV_WIDTH, p)
    W["b_pool"] = jnp.transpose(cwb[:, SUBLANES:SUBLANES + N_POOL_GROUPS, :cg // N_DEV], (1, 0, 2)).reshape(1, p)
    mixer_consts = (W["w_pool"], W["b_pool"], W["pool_scale"], W["conv_w"], W["conv_b"], W["w_a"], W["b_a"],
                    W["w_x"], W["b_x"], W["lru_lambda"])

    w_in_full = gathered("w_in")
    (proj,) = _mm_nn(xb, w_in_full, [F32], "fwd_proj", jobs=[gathers["w_ff1"]])
    ycat, hsave = _mixer_fwd(proj, *mixer_consts, "fwd_mixer", jobs=[gathers["w_out"], gathers["w_o"]])
    (y1,) = _mm_nn(ycat, gathered("w_out"), [F32], "fwd_out", jobs=[gathers["w_q"]])
    x1, x1b, xhat1, rstd1 = _ln_fwd(xs, y1, W["ln1_g"], W["ln1_b"], "fwd_ln1", jobs=[gathers["w_k"]])
    (q,) = _mm_nn(x1b, gathered("w_q"), [BF16], "fwd_q", jobs=[gathers["w_v"]])
    (k,) = _mm_nn(memb, gathered("w_k"), [BF16], "fwd_k")
    (v,) = _mm_nn(memb, gathered("w_v"), [BF16], "fwd_v")
    o = _attn_fwd(q, k, v, "fwd_attn")
    (y2,) = _mm_nn(o, gathered("w_o"), [F32], "fwd_o")
    x2, x2b, xhat2, rstd2 = _ln_fwd(x1, y2, W["ln2_g"], W["ln2_b"], "fwd_ln2")

    def relu_sq(acc):
        r = jnp.maximum(acc, 0.0)
        return r, r * r

    rb, act = _mm_nn(x2b, gathered("w_ff1"), [BF16, BF16], "fwd_ff1", epilogue=relu_sq, jobs=[gathers["w_ff2"]])
    (y3,) = _mm_nn(act, gathered("w_ff2"), [F32], "fwd_ff2")
    loss_rows, dz3, dz3b, dg3, db3 = _ln_loss(x2, y3, W["ln3_g"], W["ln3_b"], tgt, "ln3_loss")
    loss = lax.psum(loss_rows[0, 0], MESH_AXES)

    small = {"ln3_g": dg3, "ln3_b": db3}

    def add_residual(acc, e):
        return (acc + ALPHA * e,)

    dw_ff2 = _mm_tn(act, dz3b, 1, BF16, "bwd_dw_ff2")
    (dhid,) = _mm_nt(dz3b, gathered("w_ff2"), [BF16], "bwd_dact", extras=(rb,), jobs=[pair("w_ff2", dw_ff2)],
                     epilogue=lambda acc, r: (acc * (2.0 * r.astype(F32)),))
    dw_ff1 = _mm_tn(x2b, dhid, N_DEV, BF16, "bwd_dw_ff1", jobs=[quad("w_ff2")])
    (dx2,) = _mm_nt(dhid, gathered("w_ff1"), [F32], "bwd_dx2", epilogue=add_residual, extras=(dz3,),
                    jobs=[pair("w_ff1", dw_ff1)])
    dz2, dz2b, small["ln2_g"], small["ln2_b"] = _ln_bwd(dx2, xhat2, rstd2, W["ln2_g"], "bwd_ln2")

    dw_o = _mm_tn(o, dz2b, 1, BF16, "bwd_dw_o")
    (do,) = _mm_nt(dz2b, gathered("w_o"), [BF16], "bwd_do", jobs=[pair("w_o", dw_o)])
    dq, dk, dv = _attn_bwd(q, k, v, do, "bwd_attn", jobs=[quad("w_o")])
    dw_q = _mm_tn(x1b, dq, 1, BF16, "bwd_dw_q")
    dw_k = _mm_tn(memb, dk.astype(BF16), 1, BF16, "bwd_dw_k")
    dw_v = _mm_tn(memb, dv.astype(BF16), 1, BF16, "bwd_dw_v")
    (dx1,) = _mm_nt(dq, gathered("w_q"), [F32], "bwd_dx1", epilogue=add_residual, extras=(dz2,),
                    jobs=[pair("w_q", dw_q), pair("w_k", dw_k), pair("w_v", dw_v)])
    dz1, dz1b, small["ln1_g"], small["ln1_b"] = _ln_bwd(dx1, xhat1, rstd1, W["ln1_g"], "bwd_ln1", jobs=[quad("w_q")])

    dw_out = _mm_tn(ycat, dz1b, 1, BF16, "bwd_dw_out", jobs=[quad("w_k")])
    (dycat,) = _mm_nt(dz1b, gathered("w_out"), [F32], "bwd_dycat", jobs=[quad("w_v"), pair("w_out", dw_out)])
    (dproj, dwp, small["b_pool"], small["pool_scale"], small["conv_w"], small["conv_b"], small["w_a"], small["b_a"],
     small["w_x"], small["b_x"], small["lru_lambda"]) = _mixer_bwd(
        dycat, proj, hsave, *mixer_consts, "bwd_mixer", jobs=[quad("w_ff1")])
    dw_pool = jnp.transpose(dwp.astype(BF16).reshape(N_POOL_GROUPS, N_DEV, cg // N_DEV, cg), (1, 0, 2, 3))
    pack = jnp.concatenate([_pack_rows(small[n], p) for n in SMALL_ORDER], axis=0)
    small_gather = _Job("gather", pack)
    dw_in = _mm_tn(xb, dproj, 1, BF16, "bwd_dw_in", jobs=[quad("w_out"), pair("w_pool", dw_pool), small_gather])
    dw_in = jnp.transpose(dw_in.reshape(dw_in.shape[1], N_DEV, -1), (1, 0, 2))
    (grad_x,) = _mm_nt(dproj, w_in_full, [F32], "bwd_dx", epilogue=add_residual, extras=(dz1,),
                       jobs=[pair("w_in", dw_in), quad("w_pool")])

    update("w_ff2", quads["w_ff2"].out, jobs=[quad("w_in")], own=quads["w_ff2"].src)
    for n in ("w_ff1", "w_o", "w_q", "w_k", "w_v", "w_out", "w_pool", "w_in"):
        update(n, quads[n].out, own=quads[n].src)

    total = _sum_parts(small_gather.out, "sum_small")
    row = 0
    for n in SMALL_ORDER:
        size = small[n].size
        nrows = size // p
        g_full = total[row:row + nrows].reshape(small[n].shape)
        row += nrows + (-nrows) % SUBLANES
        if n == "conv_w":
            g_loc = lax.dynamic_slice_in_dim(g_full, me * (p // N_DEV), p // N_DEV, axis=1)
        elif n == "b_pool":
            g_loc = lax.dynamic_slice_in_dim(g_full.reshape(N_POOL_GROUPS, cg), me * (cg // N_DEV), cg // N_DEV, axis=1)
        else:
            g_loc = g_full
        rows = g_loc.shape[0] if n not in ("w_a", "w_x") else LRU_HEADS * hd
        update(n, g_loc.reshape(1, rows, -1))

    order = names
    return (loss, grad_x[None], *[out_g[n] for n in order], *[out_d[n] for n in order],
            *[out_m[n] for n in order], *[out_v[n] for n in order])
```

```python
import functools

import jax
import jax.numpy as jnp
from jax import lax
from jax.experimental import pallas as pl
from jax.experimental.pallas import tpu as pltpu

F32 = jnp.float32
BF16 = jnp.bfloat16

N_DEV = 8
MESH_AXES = ("x", "y", "c")
POOL_WINDOWS = (2, 4, 8, 16)
N_POOL_GROUPS = len(POOL_WINDOWS)
POOL_HALO = 16
CONV_WIDTH = 4
CONV_HALO = 8
FF1_PIECES = 4
LRU_HEADS = 8
LRU_C = 8.0
XATTN_HEADS = 4
LN_EPS = 1e-5
ALPHA = 2.0 ** 0.25
ADAM_LR = 0.001
ADAM_B1 = 0.9
ADAM_B2 = 0.999
ADAM_EPS = 1e-08
ADAM_WD = 0.01
ADAM_STEP = 10
SUBLANES = 8
VMEM_LIMIT = 56 * 1024 * 1024

NT_DIMS = (((1,), (1,)), ((), ()))
TN_DIMS = (((0,), (0,)), ((), ()))


def _params(*sem):
    return pltpu.CompilerParams(dimension_semantics=sem, vmem_limit_bytes=VMEM_LIMIT)


def _place():
    return lax.axis_index("x"), lax.axis_index("y"), lax.axis_index("c")


def _remote(src, dst, send_sem, recv_sem, to):
    return pltpu.make_async_remote_copy(src_ref=src, dst_ref=dst, send_sem=send_sem, recv_sem=recv_sem,
                                        device_id=to, device_id_type=pl.DeviceIdType.MESH)


class _Job:
    def __init__(self, kind, src):
        self.kind, self.src, self.out = kind, src, None

    def out_shape(self):
        s = self.src.shape
        shape = {"gather": (N_DEV,) + s, "pair": (4,) + s[1:], "quad": (3,) + s[1:]}[self.kind]
        return jax.ShapeDtypeStruct(shape, self.src.dtype)

    def scratch(self):
        n = {"gather": 7, "pair": 4, "quad": 3}[self.kind]
        sems = [pltpu.SemaphoreType.DMA((n,)), pltpu.SemaphoreType.DMA((n,))]
        if self.kind == "gather":
            sems += [pltpu.SemaphoreType.DMA((2,)), pltpu.VMEM(self.src.shape, self.src.dtype)]
        return sems

    def ops(self, src, out, *scratch):
        return {"gather": _gather_ops, "pair": _pair_ops, "quad": _quad_ops}[self.kind](src, out, *scratch)


def _gather_ops(x_ref, out_ref, send_sems, recv_sems, local_sems, bounce):
    x, y, c = _place()
    me, sibling = (x, y, c), (x, y, 1 - c)
    chips = [(1 - x, y), (x, 1 - y), (1 - x, 1 - y)]

    def slot(px, py, pc):
        return out_ref.at[4 * px + 2 * py + pc]

    def copy(k, block, to, src=None):
        return _remote(slot(*block) if src is None else src, slot(*block), send_sems.at[k], recv_sems.at[k], to)

    mine_in = pltpu.make_async_copy(x_ref, bounce, local_sems.at[0])
    mine_out = pltpu.make_async_copy(bounce, slot(*me), local_sems.at[1])
    first = [copy(0, me, sibling, src=x_ref)] + [copy(1 + j, me, (*chip, c), src=x_ref) for j, chip in enumerate(chips)]
    passed = [copy(4 + j, (*chip, c), sibling) for j, chip in enumerate(chips)]

    def start():
        mine_in.start()
        for cp in first:
            cp.start()

    def mid():
        mine_in.wait()
        mine_out.start()
        for j, chip in enumerate(chips):
            copy(1 + j, (*chip, c), me).wait_recv()
            passed[j].start()

    def finish():
        copy(0, sibling, me).wait_recv()
        for j, chip in enumerate(chips):
            copy(4 + j, (*chip, 1 - c), me).wait_recv()
        for cp in first + passed:
            cp.wait_send()
        mine_out.wait()

    return start, mid, finish


def _pair_ops(p_ref, got_ref, send_sems, recv_sems):
    x, y, c = _place()
    give = [_remote(p_ref.at[2 * k + 1 - c], got_ref.at[k], send_sems.at[k], recv_sems.at[k], (x, y, 1 - c))
            for k in range(4)]

    def start():
        for cp in give:
            cp.start()

    def finish():
        for cp in give:
            cp.wait_recv()
        for cp in give:
            cp.wait_send()

    return start, None, finish


def _quad_ops(q_ref, out_ref, send_sems, recv_sems):
    x, y, c = _place()
    copies = []
    for rel in range(1, 4):
        px = 1 - x if rel & 2 else x
        py = 1 - y if rel & 1 else y
        copies.append(_remote(q_ref.at[2 * px + py], out_ref.at[rel - 1], send_sems.at[rel - 1], recv_sems.at[rel - 1],
                              (px, py, c)))

    def start():
        for cp in copies:
            cp.start()

    def finish():
        for cp in copies:
            cp.wait_recv()
        for cp in copies:
            cp.wait_send()

    return start, None, finish


def _call(body, *, grid, in_specs, out_specs, out_shape, scratch_shapes=(), semantics, name, args, jobs=(), index=None):
    in_specs, out_specs, out_shape = list(in_specs), list(out_specs), list(out_shape)
    scratch_shapes, jobs = list(scratch_shapes), list(jobs)
    n_in, n_out, n_scr, n_job = len(in_specs), len(out_specs), len(scratch_shapes), len(jobs)
    n_idx = 0 if index is None else 1
    job_scratch = [j.scratch() for j in jobs]
    n_steps = functools.reduce(lambda a, b: a * b, grid, 1)
    early = n_steps - 1 - max(1, n_steps // 8) if n_steps >= 4 else None

    def hosted(*refs):
        idx, refs = refs[:n_idx], refs[n_idx:]
        ins, jin = refs[:n_in], refs[n_in:n_in + n_job]
        o0 = n_in + n_job
        outs, jout = refs[o0:o0 + n_out], refs[o0 + n_out:o0 + n_out + n_job]
        s0 = o0 + n_out + n_job
        scr, jscr = refs[s0:s0 + n_scr], refs[s0 + n_scr:]
        ops, at = [], 0
        for k, j in enumerate(jobs):
            ops.append(j.ops(jin[k], jout[k], *jscr[at:at + len(job_scratch[k])]))
            at += len(job_scratch[k])
        step = functools.reduce(lambda acc, a: acc * grid[a] + pl.program_id(a), range(len(grid)), 0)
        mids = [mid for _, mid, _ in ops if mid is not None]

        @pl.when(step == 0)
        def _():
            for start, _, _ in ops:
                start()

        if mids and early is not None:
            @pl.when(step == early)
            def _():
                for mid in mids:
                    mid()

        body(*idx, *ins, *outs, *scr)

        @pl.when(step == n_steps - 1)
        def _():
            if early is None:
                for mid in mids:
                    mid()
            for _, _, finish in ops:
                finish()

    hbm = pl.BlockSpec(memory_space=pl.ANY)
    spec = pltpu.PrefetchScalarGridSpec(
        num_scalar_prefetch=n_idx, grid=grid, in_specs=in_specs + [hbm] * n_job, out_specs=out_specs + [hbm] * n_job,
        scratch_shapes=scratch_shapes + [s for js in job_scratch for s in js])
    res = pl.pallas_call(
        hosted if jobs else body, grid_spec=spec, out_shape=out_shape + [j.out_shape() for j in jobs],
        compiler_params=_params(*(["arbitrary"] * len(grid) if jobs else semantics)), name=name,
    )(*([] if index is None else [index]), *args, *[j.src for j in jobs])
    for j, o in zip(jobs, res[n_out:]):
        j.out = o
    return res[:n_out]


def _run_jobs(jobs, name):
    def body(tick_ref):
        tick_ref[...] = jnp.zeros_like(tick_ref)

    _call(body, grid=(1,), in_specs=[], out_specs=[pl.BlockSpec((SUBLANES, 128), lambda i: (0, 0))],
          out_shape=[jax.ShapeDtypeStruct((SUBLANES, 128), F32)], semantics=("arbitrary",), name=name, args=(), jobs=jobs)


TILES = dict(tm=1024, tn=1024, tk=1024, row=256, attn=512, mixer=256, adam=128, add=1024)


def _tile(pref, n):
    for t in range(min(pref, n), 0, -1):
        if n % t == 0 and (t % SUBLANES == 0 or t == n):
            return t
    return n


def _mm_nn(a, b3, out_dtypes, name, *, tm=None, tn=None, tk=None, epilogue=None, extras=(), jobs=()):
    m, k = a.shape
    g, k2, ns = b3.shape
    assert k == k2
    n = g * ns
    tm, tn, tk = _tile(tm or TILES["tm"], m), _tile(tn or TILES["tn"], ns), _tile(tk or TILES["tk"], k)
    nb, nk = ns // tn, k // tk
    n_ex, n_out = len(extras), len(out_dtypes)

    def body(*refs):
        a_ref, b_ref = refs[:2]
        ex = refs[2:2 + n_ex]
        outs = refs[2 + n_ex:2 + n_ex + n_out]
        acc = refs[-1]
        kk = pl.program_id(2)

        @pl.when(kk == 0)
        def _():
            acc[...] = jnp.zeros_like(acc)

        acc[...] += jnp.dot(a_ref[...], b_ref[...], preferred_element_type=F32)

        @pl.when(kk == nk - 1)
        def _():
            r = acc[...]
            res = epilogue(r, *[e[...] for e in ex]) if epilogue is not None else (r,)
            for o, v in zip(outs, res):
                o[...] = v.astype(o.dtype)

    tile_out = pl.BlockSpec((tm, tn), lambda i, j, kk: (i, j))
    return _call(
        body, grid=(m // tm, n // tn, nk),
        in_specs=[pl.BlockSpec((tm, tk), lambda i, j, kk: (i, kk)),
                  pl.BlockSpec((None, tk, tn), lambda i, j, kk: (j // nb, kk, j % nb))] + [tile_out] * n_ex,
        out_specs=[tile_out] * n_out,
        out_shape=[jax.ShapeDtypeStruct((m, n), d) for d in out_dtypes],
        scratch_shapes=[pltpu.VMEM((tm, tn), F32)],
        semantics=("parallel", "parallel", "arbitrary"), name=name, args=(a, b3, *extras), jobs=jobs)


def _mm_nt(a, b3, out_dtypes, name, *, tm=None, tn=None, tk=None, epilogue=None, extras=(), jobs=()):
    m, n = a.shape
    g, k, ns = b3.shape
    assert n == g * ns
    tm, tn, tk = _tile(tm or TILES["tm"], m), _tile(tn or TILES["tn"], k), _tile(tk or TILES["tk"], ns)
    nb, nc = ns // tk, n // tk
    n_ex, n_out = len(extras), len(out_dtypes)

    def body(*refs):
        a_ref, b_ref = refs[:2]
        ex = refs[2:2 + n_ex]
        outs = refs[2 + n_ex:2 + n_ex + n_out]
        acc = refs[-1]
        cc = pl.program_id(2)

        @pl.when(cc == 0)
        def _():
            acc[...] = jnp.zeros_like(acc)

        acc[...] += lax.dot_general(a_ref[...], b_ref[...], NT_DIMS, preferred_element_type=F32)

        @pl.when(cc == nc - 1)
        def _():
            r = acc[...]
            res = epilogue(r, *[e[...] for e in ex]) if epilogue is not None else (r,)
            for o, v in zip(outs, res):
                o[...] = v.astype(o.dtype)

    tile_out = pl.BlockSpec((tm, tn), lambda i, j, cc: (i, j))
    return _call(
        body, grid=(m // tm, k // tn, nc),
        in_specs=[pl.BlockSpec((tm, tk), lambda i, j, cc: (i, cc)),
                  pl.BlockSpec((None, tn, tk), lambda i, j, cc: (cc // nb, j, cc % nb))] + [tile_out] * n_ex,
        out_specs=[tile_out] * n_out,
        out_shape=[jax.ShapeDtypeStruct((m, k), d) for d in out_dtypes],
        scratch_shapes=[pltpu.VMEM((tm, tn), F32)],
        semantics=("parallel", "parallel", "arbitrary"), name=name, args=(a, b3, *extras), jobs=jobs)


def _mm_tn(a, b, g, out_dtype, name, *, tm=None, tn=None, tk=None, jobs=()):
    s, m = a.shape
    s2, n = b.shape
    assert s == s2 and n % g == 0
    ns = n // g
    tm, tn, tk = _tile(tm or TILES["tm"], m), _tile(tn or TILES["tn"], ns), _tile(tk or TILES["tk"], s)
    nb, nc = ns // tn, s // tk

    def body(a_ref, b_ref, o_ref, acc):
        cc = pl.program_id(2)

        @pl.when(cc == 0)
        def _():
            acc[...] = jnp.zeros_like(acc)

        acc[...] += lax.dot_general(a_ref[...], b_ref[...], TN_DIMS, preferred_element_type=F32)

        @pl.when(cc == nc - 1)
        def _():
            o_ref[...] = acc[...].astype(o_ref.dtype)

    return _call(
        body, grid=(m // tm, n // tn, nc),
        in_specs=[pl.BlockSpec((tk, tm), lambda i, j, cc: (cc, i)),
                  pl.BlockSpec((tk, tn), lambda i, j, cc: (cc, j))],
        out_specs=[pl.BlockSpec((None, tm, tn), lambda i, j, cc: (j // nb, i, j % nb))],
        out_shape=[jax.ShapeDtypeStruct((g, m, ns), out_dtype)],
        scratch_shapes=[pltpu.VMEM((tm, tn), F32)],
        semantics=("parallel", "parallel", "arbitrary"), name=name, args=(a, b), jobs=jobs)[0]


def _ln_stats(z):
    mu = jnp.mean(z, axis=-1, keepdims=True)
    zc = z - mu
    var = jnp.mean(zc * zc, axis=-1, keepdims=True)
    rstd = lax.rsqrt(var + LN_EPS)
    return zc * rstd, rstd


def _ln_grad(dout, xhat, rstd, gain):
    dxhat = dout * gain
    m1 = jnp.mean(dxhat, axis=-1, keepdims=True)
    m2 = jnp.mean(dxhat * xhat, axis=-1, keepdims=True)
    return rstd * (dxhat - m1 - xhat * m2)


def _ln_fwd(xres, y, gain, bias, name, jobs=()):
    s, d = xres.shape
    tr = _tile(TILES["row"], s)

    def body(x_ref, y_ref, g_ref, b_ref, xn_ref, xnb_ref, xhat_ref, rstd_ref):
        xhat, rstd = _ln_stats(ALPHA * x_ref[...] + y_ref[...])
        out = xhat * g_ref[...] + b_ref[...]
        xn_ref[...] = out
        xnb_ref[...] = out.astype(BF16)
        xhat_ref[...] = xhat
        rstd_ref[...] = rstd

    row = pl.BlockSpec((tr, d), lambda i: (i, 0))
    vec = pl.BlockSpec((1, d), lambda i: (0, 0))
    return _call(
        body, grid=(s // tr,), in_specs=[row, row, vec, vec],
        out_specs=[row, row, row, pl.BlockSpec((tr, 1), lambda i: (i, 0))],
        out_shape=[jax.ShapeDtypeStruct((s, d), F32), jax.ShapeDtypeStruct((s, d), BF16),
                   jax.ShapeDtypeStruct((s, d), F32), jax.ShapeDtypeStruct((s, 1), F32)],
        semantics=("parallel",), name=name, args=(xres, y, gain, bias), jobs=jobs)


def _ln_bwd(dout, xhat, rstd, gain, name, jobs=()):
    s, d = dout.shape
    tr = _tile(TILES["row"], s)

    def body(d_ref, xhat_ref, rstd_ref, g_ref, dz_ref, dzb_ref, dg_ref, db_ref):
        @pl.when(pl.program_id(0) == 0)
        def _():
            dg_ref[...] = jnp.zeros_like(dg_ref)
            db_ref[...] = jnp.zeros_like(db_ref)

        dout_t, xhat_t = d_ref[...], xhat_ref[...]
        dz = _ln_grad(dout_t, xhat_t, rstd_ref[...], g_ref[...])
        dz_ref[...] = dz
        dzb_ref[...] = dz.astype(BF16)
        dg_ref[...] += jnp.sum(dout_t * xhat_t, axis=0, keepdims=True)
        db_ref[...] += jnp.sum(dout_t, axis=0, keepdims=True)

    row = pl.BlockSpec((tr, d), lambda i: (i, 0))
    vec = pl.BlockSpec((1, d), lambda i: (0, 0))
    return _call(
        body, grid=(s // tr,), in_specs=[row, row, pl.BlockSpec((tr, 1), lambda i: (i, 0)), vec],
        out_specs=[row, row, vec, vec],
        out_shape=[jax.ShapeDtypeStruct((s, d), F32), jax.ShapeDtypeStruct((s, d), BF16),
                   jax.ShapeDtypeStruct((1, d), F32), jax.ShapeDtypeStruct((1, d), F32)],
        semantics=("arbitrary",), name=name, args=(dout, xhat, rstd, gain), jobs=jobs)


def _ln_loss(xres, y, gain, bias, target, name, jobs=()):
    s, d = xres.shape
    tr = _tile(TILES["row"], s)

    def body(x_ref, y_ref, g_ref, b_ref, t_ref, loss_ref, dz_ref, dzb_ref, dg_ref, db_ref):
        @pl.when(pl.program_id(0) == 0)
        def _():
            loss_ref[...] = jnp.zeros_like(loss_ref)
            dg_ref[...] = jnp.zeros_like(dg_ref)
            db_ref[...] = jnp.zeros_like(db_ref)

        xhat, rstd = _ln_stats(ALPHA * x_ref[...] + y_ref[...])
        diff = xhat * g_ref[...] + b_ref[...] - t_ref[...]
        per_row = jnp.mean(diff * diff, axis=-1, keepdims=True)
        loss_ref[...] += 0.5 * jnp.sum(per_row, axis=0, keepdims=True)
        dout = diff * (1.0 / d)
        dz = _ln_grad(dout, xhat, rstd, g_ref[...])
        dz_ref[...] = dz
        dzb_ref[...] = dz.astype(BF16)
        dg_ref[...] += jnp.sum(dout * xhat, axis=0, keepdims=True)
        db_ref[...] += jnp.sum(dout, axis=0, keepdims=True)

    row = pl.BlockSpec((tr, d), lambda i: (i, 0))
    vec = pl.BlockSpec((1, d), lambda i: (0, 0))
    return _call(
        body, grid=(s // tr,), in_specs=[row, row, vec, vec, row],
        out_specs=[pl.BlockSpec((1, 128), lambda i: (0, 0)), row, row, vec, vec],
        out_shape=[jax.ShapeDtypeStruct((1, 128), F32), jax.ShapeDtypeStruct((s, d), F32),
                   jax.ShapeDtypeStruct((s, d), BF16), jax.ShapeDtypeStruct((1, d), F32),
                   jax.ShapeDtypeStruct((1, d), F32)],
        semantics=("arbitrary",), name=name, args=(xres, y, gain, bias, target), jobs=jobs)


def _softmax_rows(s):
    e = jnp.exp(s - jnp.max(s, axis=-1, keepdims=True))
    return e / jnp.sum(e, axis=-1, keepdims=True)


def _attn_fwd(q, k, v, name, jobs=()):
    s, d = q.shape
    m = k.shape[0]
    hd = d // XATTN_HEADS
    ts = _tile(TILES["attn"], s)
    scale = hd ** -0.5

    def body(q_ref, k_ref, v_ref, o_ref):
        for h in range(XATTN_HEADS):
            hs = slice(h * hd, (h + 1) * hd)
            sc = lax.dot_general(q_ref[:, hs], k_ref[:, hs], NT_DIMS, preferred_element_type=F32) * scale
            p = _softmax_rows(sc).astype(BF16)
            o_ref[:, hs] = jnp.dot(p, v_ref[:, hs], preferred_element_type=F32).astype(BF16)

    row = pl.BlockSpec((ts, d), lambda i: (i, 0))
    memb = pl.BlockSpec((m, d), lambda i: (0, 0))
    return _call(
        body, grid=(s // ts,), in_specs=[row, memb, memb], out_specs=[row],
        out_shape=[jax.ShapeDtypeStruct((s, d), BF16)],
        semantics=("parallel",), name=name, args=(q, k, v), jobs=jobs)[0]


def _attn_bwd(q, k, v, do, name, jobs=()):
    s, d = q.shape
    m = k.shape[0]
    hd = d // XATTN_HEADS
    ts = _tile(TILES["attn"], s)
    scale = hd ** -0.5

    def body(q_ref, k_ref, v_ref, do_ref, dq_ref, dk_ref, dv_ref):
        @pl.when(pl.program_id(0) == 0)
        def _():
            dk_ref[...] = jnp.zeros_like(dk_ref)
            dv_ref[...] = jnp.zeros_like(dv_ref)

        for h in range(XATTN_HEADS):
            hs = slice(h * hd, (h + 1) * hd)
            qh, kh, vh, doh = q_ref[:, hs], k_ref[:, hs], v_ref[:, hs], do_ref[:, hs]
            sc = lax.dot_general(qh, kh, NT_DIMS, preferred_element_type=F32) * scale
            p = _softmax_rows(sc)
            pb = p.astype(BF16)
            dp = lax.dot_general(doh, vh, NT_DIMS, preferred_element_type=F32)
            ds = (p * (dp - jnp.sum(dp * p, axis=-1, keepdims=True)) * scale).astype(BF16)
            dq_ref[:, hs] = jnp.dot(ds, kh, preferred_element_type=F32).astype(BF16)
            dk_ref[:, hs] += lax.dot_general(ds, qh, TN_DIMS, preferred_element_type=F32)
            dv_ref[:, hs] += lax.dot_general(pb, doh, TN_DIMS, preferred_element_type=F32)

    row = pl.BlockSpec((ts, d), lambda i: (i, 0))
    memb = pl.BlockSpec((m, d), lambda i: (0, 0))
    return _call(
        body, grid=(s // ts,), in_specs=[row, memb, memb, row], out_specs=[row, memb, memb],
        out_shape=[jax.ShapeDtypeStruct((s, d), BF16), jax.ShapeDtypeStruct((m, d), F32),
                   jax.ShapeDtypeStruct((m, d), F32)],
        semantics=("arbitrary",), name=name, args=(q, k, v, do), jobs=jobs)


def _sigmoid(x):
    return 1.0 / (1.0 + jnp.exp(-x))


def _log1p(x):
    u = 1.0 + x
    return jnp.where(u == 1.0, x, jnp.log(u) * (x / jnp.where(u == 1.0, 1.0, u - 1.0)))


def _softplus(x):
    return jnp.maximum(x, 0.0) + _log1p(jnp.exp(-jnp.abs(x)))


def _expm1(x):
    series = x * (1.0 + x * 0.5 * (1.0 + x * (1.0 / 3.0) * (1.0 + x * 0.25 * (1.0 + x * 0.2 * (1.0 + x * (1.0 / 6.0))))))
    return jnp.where(jnp.abs(x) < 0.1, series, jnp.exp(x) - 1.0)


GELU_K = 0.7978845608028654
GELU_C = 0.044715


def _gelu(x):
    return 0.5 * x * (1.0 + jnp.tanh(GELU_K * (x + GELU_C * (x * x * x))))


def _gelu_grad(x):
    th = jnp.tanh(GELU_K * (x + GELU_C * (x * x * x)))
    return 0.5 * (1.0 + th) + 0.5 * x * (1.0 - th * th) * GELU_K * (1.0 + 3.0 * GELU_C * x * x)


def _window_sum(ext_ref, first, rows, cols, w, step):
    acc = ext_ref[first:first + rows, cols]
    for kk in range(1, w):
        acc = acc + ext_ref[first + step * kk:first + step * kk + rows, cols]
    return acc


def _lru_gates(c_s, wa_ref, ba_ref, wx_ref, bx_ref, lam_ref, t_idx, hd, r_s, i_s, a_s, mult_s):
    sp = _softplus(-lam_ref[...])
    for h in range(LRU_HEADS):
        hs = slice(h * hd, (h + 1) * hd)
        chb = c_s[:, hs].astype(BF16)
        r = _sigmoid(jnp.dot(chb, wa_ref[h], preferred_element_type=F32) + ba_ref[:, hs])
        ig = _sigmoid(jnp.dot(chb, wx_ref[h], preferred_element_type=F32) + bx_ref[:, hs])
        log_a = -LRU_C * r * sp[:, hs]
        mult = jnp.sqrt(-_expm1(2.0 * log_a))
        r_s[:, hs] = r
        i_s[:, hs] = ig
        a_s[:, hs] = jnp.exp(log_a)
        mult_s[:, hs] = jnp.where(t_idx == 0, 1.0, mult)


def _conv(ext_ref, cw_ref, cb_ref, rows):
    acc = cb_ref[...] + cw_ref[0:1, :] * ext_ref[CONV_HALO - 3:CONV_HALO - 3 + rows, :]
    for kk in range(1, CONV_WIDTH):
        off = CONV_HALO - (CONV_WIDTH - 1) + kk
        acc = acc + cw_ref[kk:kk + 1, :] * ext_ref[off:off + rows, :]
    return acc


def _mixer_fwd(proj, wp, bp, ps, cw, cb, wa, ba, wx, bx, lam, name, jobs=()):
    s, p3 = proj.shape
    p = p3 // 3
    cg, hd = p // N_POOL_GROUPS, p // LRU_HEADS
    t = _tile(TILES["mixer"], s)

    def body(up_ref, ul_ref, ug_ref, wp_ref, bp_ref, ps_ref, cw_ref, cb_ref, wa_ref, ba_ref, wx_ref, bx_ref,
             lam_ref, ycat_ref, h_ref, extp, extl, hc, c_s, r_s, i_s, a_s, b_s):
        i = pl.program_id(0)

        @pl.when(i == 0)
        def _():
            extp[0:POOL_HALO, :] = jnp.zeros((POOL_HALO, p), F32)
            extl[0:CONV_HALO, :] = jnp.zeros((CONV_HALO, p), F32)
            hc[...] = jnp.zeros_like(hc)

        t_idx = i * t + lax.broadcasted_iota(jnp.int32, (t, 1), 0)

        extp[POOL_HALO:POOL_HALO + t, :] = up_ref[...]
        for g, w in enumerate(POOL_WINDOWS):
            cs = slice(g * cg, (g + 1) * cg)
            cnt = jnp.minimum(t_idx + 1, w).astype(F32)
            mixed = _window_sum(extp, POOL_HALO, t, cs, w, -1) / cnt - up_ref[:, cs]
            pre = jnp.dot(mixed.astype(BF16), wp_ref[g], preferred_element_type=F32) + bp_ref[:, cs]
            ycat_ref[:, cs] = (pre * ps_ref[:, cs]).astype(BF16)
        extp[0:POOL_HALO, :] = extp[t:t + POOL_HALO, :]

        extl[CONV_HALO:CONV_HALO + t, :] = ul_ref[...]
        c_s[...] = _conv(extl, cw_ref, cb_ref, t)
        extl[0:CONV_HALO, :] = extl[t:t + CONV_HALO, :]
        _lru_gates(c_s, wa_ref, ba_ref, wx_ref, bx_ref, lam_ref, t_idx, hd, r_s, i_s, a_s, b_s)
        b_s[...] = b_s[...] * (i_s[...] * c_s[...])

        rows = lax.broadcasted_iota(jnp.int32, (SUBLANES, p), 0)

        def block(bi, h):
            r0 = pl.multiple_of(bi * SUBLANES, SUBLANES)
            at = a_s[pl.ds(r0, SUBLANES), :]
            bt = b_s[pl.ds(r0, SUBLANES), :]
            out = jnp.zeros((SUBLANES, p), F32)
            for j in range(SUBLANES):
                h = at[j:j + 1, :] * h + bt[j:j + 1, :]
                out = jnp.where(rows == j, h, out)
            h_ref[pl.ds(r0, SUBLANES), :] = out
            return h

        hc[0:1, :] = lax.fori_loop(0, t // SUBLANES, block, hc[0:1, :])
        ycat_ref[:, p:2 * p] = (h_ref[...] * _gelu(ug_ref[...])).astype(BF16)

    def col(j):
        return pl.BlockSpec((t, p), lambda i: (i, j))

    def whole(a):
        nd = a.ndim
        return pl.BlockSpec(a.shape, lambda i: (0,) * nd)

    consts = (wp, bp, ps, cw, cb, wa, ba, wx, bx, lam)
    tile = pltpu.VMEM((t, p), F32)
    return _call(
        body, grid=(s // t,), in_specs=[col(0), col(1), col(2)] + [whole(a) for a in consts],
        out_specs=[pl.BlockSpec((t, 2 * p), lambda i: (i, 0)), pl.BlockSpec((t, p), lambda i: (i, 0))],
        out_shape=[jax.ShapeDtypeStruct((s, 2 * p), BF16), jax.ShapeDtypeStruct((s, p), F32)],
        scratch_shapes=[pltpu.VMEM((t + POOL_HALO, p), F32), pltpu.VMEM((t + CONV_HALO, p), F32),
                        pltpu.VMEM((SUBLANES, p), F32), tile, tile, tile, tile, tile],
        semantics=("arbitrary",), name=name, args=(proj, proj, proj, *consts), jobs=jobs)


def _mixer_bwd(dycat, proj, hsave, wp, bp, ps, cw, cb, wa, ba, wx, bx, lam, name, jobs=()):
    s, p3 = proj.shape
    p = p3 // 3
    cg, hd = p // N_POOL_GROUPS, p // LRU_HEADS
    t = _tile(TILES["mixer"], s)
    nt = s // t

    def body(dyp_ref, dyl_ref, up_ref, ul_ref, ug_ref, upp_ref, ulp_ref, h_ref, hp_ref,
             wp_ref, bp_ref, ps_ref, cw_ref, cb_ref, wa_ref, ba_ref, wx_ref, bx_ref, lam_ref,
             dproj_ref, dwp_ref, dbp_ref, dps_ref, dcw_ref, dcb_ref, dwa_ref, dba_ref, dwx_ref, dbx_ref, dlam_ref,
             extp, extg, extl, extdc, exth, ghc, c_s, r_s, i_s, a_s, mult_s, gh_s):
        i = pl.program_id(0)
        ib = nt - 1 - i

        @pl.when(i == 0)
        def _():
            for ref in (dwp_ref, dbp_ref, dps_ref, dcw_ref, dcb_ref, dwa_ref, dba_ref, dwx_ref, dbx_ref, dlam_ref):
                ref[...] = jnp.zeros_like(ref)
            extg[t:t + POOL_HALO, :] = jnp.zeros((POOL_HALO, p), F32)
            extdc[t:t + CONV_HALO, :] = jnp.zeros((CONV_HALO, p), F32)
            ghc[...] = jnp.zeros_like(ghc)

        t_idx = ib * t + lax.broadcasted_iota(jnp.int32, (t, 1), 0)
        seq_start = ib == 0

        extl[0:CONV_HALO, :] = jnp.where(seq_start, 0.0, ulp_ref[...])
        extl[CONV_HALO:CONV_HALO + t, :] = ul_ref[...]
        c_s[...] = _conv(extl, cw_ref, cb_ref, t)
        _lru_gates(c_s, wa_ref, ba_ref, wx_ref, bx_ref, lam_ref, t_idx, hd, r_s, i_s, a_s, mult_s)
        exth[0:SUBLANES, :] = jnp.where(seq_start, 0.0, hp_ref[...])
        exth[SUBLANES:SUBLANES + t, :] = h_ref[...]

        ug = ug_ref[...]
        dyl = dyl_ref[...]
        dproj_ref[:, 2 * p:3 * p] = (dyl * h_ref[...] * _gelu_grad(ug)).astype(BF16)
        gh_s[...] = dyl * _gelu(ug)

        rows = lax.broadcasted_iota(jnp.int32, (SUBLANES, p), 0)
        nblk = t // SUBLANES

        def block(bi, carry):
            r0 = pl.multiple_of((nblk - 1 - bi) * SUBLANES, SUBLANES)
            at = a_s[pl.ds(r0, SUBLANES), :]
            dt = gh_s[pl.ds(r0, SUBLANES), :]
            out = jnp.zeros((SUBLANES, p), F32)
            for j in range(SUBLANES - 1, -1, -1):
                gh = dt[j:j + 1, :] + carry
                out = jnp.where(rows == j, gh, out)
                carry = at[j:j + 1, :] * gh
            gh_s[pl.ds(r0, SUBLANES), :] = out
            return carry

        ghc[0:1, :] = lax.fori_loop(0, nblk, block, ghc[0:1, :])

        sp = _softplus(-lam_ref[...])
        dsp_dlam = -_sigmoid(-lam_ref[...])
        for h in range(LRU_HEADS):
            hs = slice(h * hd, (h + 1) * hd)
            gh, a, mult, r, ig, c = gh_s[:, hs], a_s[:, hs], mult_s[:, hs], r_s[:, hs], i_s[:, hs], c_s[:, hs]
            hprev = exth[SUBLANES - 1:SUBLANES - 1 + t, hs]
            dmult = gh * (ig * c)
            dlog_a = a * gh * hprev + jnp.where(t_idx == 0, 0.0, -dmult * a * a / mult)
            dlam_ref[:, hs] += jnp.sum(dlog_a * r, axis=0, keepdims=True) * (-LRU_C) * dsp_dlam[:, hs]
            dpa = dlog_a * (-LRU_C * sp[:, hs]) * r * (1.0 - r)
            dpx = gh * mult * c * ig * (1.0 - ig)
            dpab, dpxb, chb = dpa.astype(BF16), dpx.astype(BF16), c.astype(BF16)
            dwa_ref[h] += lax.dot_general(chb, dpab, TN_DIMS, preferred_element_type=F32)
            dwx_ref[h] += lax.dot_general(chb, dpxb, TN_DIMS, preferred_element_type=F32)
            dba_ref[:, hs] += jnp.sum(dpa, axis=0, keepdims=True)
            dbx_ref[:, hs] += jnp.sum(dpx, axis=0, keepdims=True)
            dc = (gh * mult * ig
                  + lax.dot_general(dpab, wa_ref[h], NT_DIMS, preferred_element_type=F32)
                  + lax.dot_general(dpxb, wx_ref[h], NT_DIMS, preferred_element_type=F32))
            extdc[0:t, hs] = dc
            dcb_ref[:, hs] += jnp.sum(dc, axis=0, keepdims=True)
            for kk in range(CONV_WIDTH):
                off = CONV_HALO - (CONV_WIDTH - 1) + kk
                dcw_ref[kk:kk + 1, hs] += jnp.sum(dc * extl[off:off + t, hs], axis=0, keepdims=True)
        du_lru = cw_ref[0:1, :] * extdc[CONV_WIDTH - 1:CONV_WIDTH - 1 + t, :]
        for kk in range(1, CONV_WIDTH):
            off = CONV_WIDTH - 1 - kk
            du_lru = du_lru + cw_ref[kk:kk + 1, :] * extdc[off:off + t, :]
        dproj_ref[:, p:2 * p] = du_lru.astype(BF16)
        extdc[t:t + CONV_HALO, :] = extdc[0:CONV_HALO, :]

        extp[0:POOL_HALO, :] = jnp.where(seq_start, 0.0, upp_ref[...])
        extp[POOL_HALO:POOL_HALO + t, :] = up_ref[...]
        for g, w in enumerate(POOL_WINDOWS):
            cs = slice(g * cg, (g + 1) * cg)
            cnt = jnp.minimum(t_idx + 1, w).astype(F32)
            mixed = (_window_sum(extp, POOL_HALO, t, cs, w, -1) / cnt - up_ref[:, cs]).astype(BF16)
            pre = jnp.dot(mixed, wp_ref[g], preferred_element_type=F32) + bp_ref[:, cs]
            dyp = dyp_ref[:, cs]
            dps_ref[:, cs] += jnp.sum(dyp * pre, axis=0, keepdims=True)
            dpre = dyp * ps_ref[:, cs]
            dpreb = dpre.astype(BF16)
            dbp_ref[:, cs] += jnp.sum(dpre, axis=0, keepdims=True)
            dwp_ref[g] += lax.dot_general(mixed, dpreb, TN_DIMS, preferred_element_type=F32)
            dmixed = lax.dot_general(dpreb, wp_ref[g], NT_DIMS, preferred_element_type=F32)
            extg[0:t, cs] = dmixed / cnt
            dproj_ref[:, cs] = (_window_sum(extg, 0, t, cs, w, 1) - dmixed).astype(BF16)
        extg[t:t + POOL_HALO, :] = extg[0:POOL_HALO, :]

    def col(j):
        return pl.BlockSpec((t, p), lambda i: (nt - 1 - i, j))

    def prev(rows, j):
        per = t // rows
        return pl.BlockSpec((rows, p), lambda i: (jnp.maximum((nt - 1 - i) * per - 1, 0), j))

    def whole(a):
        nd = a.ndim
        return pl.BlockSpec(a.shape, lambda i: (0,) * nd)

    consts = (wp, bp, ps, cw, cb, wa, ba, wx, bx, lam)
    grads = (wp, bp, ps, cw, cb, wa, ba, wx, bx, lam)
    tile = pltpu.VMEM((t, p), F32)
    return _call(
        body, grid=(nt,),
        in_specs=[col(0), col(1), col(0), col(1), col(2), prev(POOL_HALO, 0), prev(CONV_HALO, 1), col(0),
                  prev(SUBLANES, 0)] + [whole(a) for a in consts],
        out_specs=[pl.BlockSpec((t, 3 * p), lambda i: (nt - 1 - i, 0))] + [whole(a) for a in grads],
        out_shape=[jax.ShapeDtypeStruct((s, 3 * p), BF16)] + [jax.ShapeDtypeStruct(a.shape, F32) for a in grads],
        scratch_shapes=[pltpu.VMEM((t + POOL_HALO, p), F32), pltpu.VMEM((t + POOL_HALO, p), F32),
                        pltpu.VMEM((t + CONV_HALO, p), F32), pltpu.VMEM((t + CONV_HALO, p), F32),
                        pltpu.VMEM((t + SUBLANES, p), F32), pltpu.VMEM((SUBLANES, p), F32),
                        tile, tile, tile, tile, tile, tile],
        semantics=("arbitrary",), name=name,
        args=(dycat, dycat, proj, proj, proj, proj, proj, hsave, hsave, *consts), jobs=jobs)


def _pair_add(parts, got, core, name):
    n, r, c = got.shape
    tr = _tile(TILES["add"], r)

    def body(core_ref, a_ref, b_ref, o_ref):
        del core_ref
        o_ref[...] = (a_ref[...].astype(F32) + b_ref[...].astype(F32)).astype(o_ref.dtype)

    blk = pl.BlockSpec((None, tr, c), lambda k, i, core_ref: (k, i, 0))
    mine = pl.BlockSpec((None, tr, c), lambda k, i, core_ref: (2 * k + core_ref[0], i, 0))
    return _call(body, grid=(n, r // tr), in_specs=[mine, blk], out_specs=[blk],
                 out_shape=[jax.ShapeDtypeStruct(got.shape, got.dtype)], semantics=("parallel", "parallel"),
                 name=name, args=(parts, got), index=core)[0]


def _sum_parts(parts, name):
    n, r, c = parts.shape
    tr = _tile(TILES["adam"], r)

    def body(p_ref, o_ref):
        acc = p_ref[0].astype(F32)
        for d in range(1, n):
            acc = acc + p_ref[d].astype(F32)
        o_ref[...] = acc

    return _call(
        body, grid=(r // tr,), in_specs=[pl.BlockSpec((n, tr, c), lambda i: (0, i, 0))],
        out_specs=[pl.BlockSpec((tr, c), lambda i: (i, 0))], out_shape=[jax.ShapeDtypeStruct((r, c), F32)],
        semantics=("parallel",), name=name, args=(parts,))[0]


def _adamw(w, m, v, parts, name, jobs=(), own=None, chip=None):
    r, c = w.shape
    n = parts.shape[0]
    tr = _tile(TILES["adam"], r)

    def body(*refs):
        if own is not None:
            refs = refs[1:]
            own_ref, refs = refs[3], refs[:3] + refs[4:]
        w_ref, m_ref, v_ref, p_ref, g_ref, d_ref, nm_ref, nv_ref = refs
        g = p_ref[0].astype(F32)
        if own is not None:
            g = own_ref[...].astype(F32) + g
        for d in range(1, n):
            g = g + p_ref[d].astype(F32)
        nm = ADAM_B1 * m_ref[...] + (1.0 - ADAM_B1) * g
        nv = ADAM_B2 * v_ref[...] + (1.0 - ADAM_B2) * (g * g)
        m_hat = nm / (1.0 - ADAM_B1 ** ADAM_STEP)
        v_hat = nv / (1.0 - ADAM_B2 ** ADAM_STEP)
        g_ref[...] = g
        d_ref[...] = -ADAM_LR * (m_hat / (jnp.sqrt(v_hat) + ADAM_EPS) + ADAM_WD * w_ref[...])
        nm_ref[...] = nm
        nv_ref[...] = nv

    row = pl.BlockSpec((tr, c), lambda i, *_: (i, 0))
    in_specs, args = [row, row, row], [w, m, v]
    if own is not None:
        in_specs.append(pl.BlockSpec((None, tr, c), lambda i, chip_ref: (chip_ref[0], i, 0)))
        args.append(own)
    in_specs.append(pl.BlockSpec((n, tr, c), lambda i, *_: (0, i, 0)))
    args.append(parts)
    return _call(
        body, grid=(r // tr,), in_specs=in_specs, out_specs=[row] * 4, out_shape=[jax.ShapeDtypeStruct((r, c), F32)] * 4,
        semantics=("parallel",), name=name, args=args, jobs=jobs, index=chip if own is not None else None)


SMALL_ORDER = ("w_a", "w_x", "conv_w", "b_pool", "conv_b", "b_a", "b_x", "lru_lambda", "pool_scale",
               "ln1_g", "ln1_b", "ln2_g", "ln2_b", "ln3_g", "ln3_b")


def _pack_rows(a, p):
    flat = a.reshape(-1, p)
    pad = (-flat.shape[0]) % SUBLANES
    return jnp.pad(flat, ((0, pad), (0, 0))) if pad else flat


def kernel(x, mem, w_in, conv_w, conv_b, w_a, b_a, w_x, b_x, lru_lambda, w_pool, b_pool, pool_scale, w_out, ln1_g, ln1_b, w_q, w_k, w_v, w_o, ln2_g, ln2_b, w_ff1, w_ff2, ln3_g, ln3_b, loss_target, m_w_in, m_conv_w, m_conv_b, m_w_a, m_b_a, m_w_x, m_b_x, m_lru_lambda, m_w_pool, m_b_pool, m_pool_scale, m_w_out, m_ln1_g, m_ln1_b, m_w_q, m_w_k, m_w_v, m_w_o, m_ln2_g, m_ln2_b, m_w_ff1, m_w_ff2, m_ln3_g, m_ln3_b, v_w_in, v_conv_w, v_conv_b, v_w_a, v_b_a, v_w_x, v_b_x, v_lru_lambda, v_w_pool, v_b_pool, v_pool_scale, v_w_out, v_ln1_g, v_ln1_b, v_w_q, v_w_k, v_w_v, v_w_o, v_ln2_g, v_ln2_b, v_w_ff1, v_w_ff2, v_ln3_g, v_ln3_b):
    names = ("w_in", "conv_w", "conv_b", "w_a", "b_a", "w_x", "b_x", "lru_lambda", "w_pool", "b_pool", "pool_scale",
             "w_out", "ln1_g", "ln1_b", "w_q", "w_k", "w_v", "w_o", "ln2_g", "ln2_b", "w_ff1", "w_ff2", "ln3_g", "ln3_b")
    w_loc = dict(zip(names, (w_in, conv_w, conv_b, w_a, b_a, w_x, b_x, lru_lambda, w_pool, b_pool, pool_scale,
                             w_out, ln1_g, ln1_b, w_q, w_k, w_v, w_o, ln2_g, ln2_b, w_ff1, w_ff2, ln3_g, ln3_b)))
    m_loc = dict(zip(names, (m_w_in, m_conv_w, m_conv_b, m_w_a, m_b_a, m_w_x, m_b_x, m_lru_lambda, m_w_pool, m_b_pool,
                             m_pool_scale, m_w_out, m_ln1_g, m_ln1_b, m_w_q, m_w_k, m_w_v, m_w_o, m_ln2_g, m_ln2_b,
                             m_w_ff1, m_w_ff2, m_ln3_g, m_ln3_b)))
    v_loc = dict(zip(names, (v_w_in, v_conv_w, v_conv_b, v_w_a, v_b_a, v_w_x, v_b_x, v_lru_lambda, v_w_pool, v_b_pool,
                             v_pool_scale, v_w_out, v_ln1_g, v_ln1_b, v_w_q, v_w_k, v_w_v, v_w_o, v_ln2_g, v_ln2_b,
                             v_w_ff1, v_w_ff2, v_ln3_g, v_ln3_b)))
    s, d = x.shape[1], x.shape[2]
    p = conv_b.shape[1]
    cg = p // N_POOL_GROUPS
    hd = p // LRU_HEADS
    me = 4 * lax.axis_index("x") + 2 * lax.axis_index("y") + lax.axis_index("c")

    xs, mems, tgt = x[0], mem[0], loss_target[0]
    xb, memb = xs.astype(BF16), mems.astype(BF16)

    gathers = {n: _Job("gather", w_loc[n][0].astype(BF16))
               for n in ("w_in", "w_out", "w_q", "w_k", "w_v", "w_o", "w_ff2", "w_pool")}
    ff1_shard = w_ff1[0].astype(BF16)
    ff1_rows = ff1_shard.shape[0] // FF1_PIECES
    ff1_pieces = [_Job("gather", ff1_shard[i * ff1_rows:(i + 1) * ff1_rows]) for i in range(FF1_PIECES)]
    tiny = jnp.concatenate([_pack_rows(conv_w[0], p // N_DEV),
                            _pack_rows(jnp.pad(b_pool[0], ((0, 0), (0, p // N_DEV - cg // N_DEV))), p // N_DEV)], axis=0)
    gathers["tiny"] = _Job("gather", tiny)

    def gathered(n):
        full = gathers[n].out
        if n == "w_in":
            return jnp.transpose(full, (1, 0, 2)).reshape(1, full.shape[1], -1)
        return full.reshape(1, -1, full.shape[-1])

    W = {"conv_b": conv_b, "b_a": b_a.reshape(1, p), "b_x": b_x.reshape(1, p), "lru_lambda": lru_lambda,
         "pool_scale": pool_scale, "w_a": w_a[0].astype(BF16), "w_x": w_x[0].astype(BF16)}
    for n in ("ln1_g", "ln1_b", "ln2_g", "ln2_b", "ln3_g", "ln3_b"):
        W[n] = w_loc[n]

    out_g, out_d, out_m, out_v = {}, {}, {}, {}
    pairs, quads = {}, {}
    core = lax.axis_index("c").astype(jnp.int32).reshape(1)
    chip = (2 * lax.axis_index("x") + lax.axis_index("y")).astype(jnp.int32).reshape(1)

    def pair(n, partial):
        pairs[n] = _Job("pair", partial.reshape(N_DEV, -1, partial.shape[-1]))
        return pairs[n]

    def quad(n):
        quads[n] = _Job("quad", _pair_add(pairs[n].src, pairs[n].out, core, "add_" + n))
        return quads[n]

    def update(n, parts, jobs=(), own=None):
        shp = w_loc[n].shape
        rows = parts.shape[1]
        w2, m2, v2 = (a.reshape(rows, -1) for a in (w_loc[n], m_loc[n], v_loc[n]))
        res = _adamw(w2, m2, v2, parts.reshape(parts.shape[0], rows, -1), "adamw_" + n, jobs=jobs, own=own, chip=chip)
        out_g[n], out_d[n], out_m[n], out_v[n] = (r.reshape(shp) for r in res)

    _run_jobs([gathers["w_in"], gathers["tiny"], gathers["w_pool"]], "gather_first")
    W["w_pool"] = jnp.transpose(gathers["w_pool"].out, (1, 0, 2, 3)).reshape(N_POOL_GROUPS, cg, cg)
    cwb = gathers["tiny"].out
    W["conv_w"] = jnp.transpose(cwb[:, :CONV_WIDTH, :], (1, 0, 2)).reshape(CONV_WIDTH, p)
    W["b_pool"] = jnp.transpose(cwb[:, SUBLANES:SUBLANES + N_POOL_GROUPS, :cg // N_DEV], (1, 0, 2)).reshape(1, p)
    mixer_consts = (W["w_pool"], W["b_pool"], W["pool_scale"], W["conv_w"], W["conv_b"], W["w_a"], W["b_a"],
                    W["w_x"], W["b_x"], W["lru_lambda"])

    w_in_full = gathered("w_in")
    (proj,) = _mm_nn(xb, w_in_full, [F32], "fwd_proj", jobs=[gathers["w_out"], ff1_pieces[3]])
    ycat, hsave = _mixer_fwd(proj, *mixer_consts, "fwd_mixer", jobs=[gathers["w_q"], gathers["w_k"]])
    (y1,) = _mm_nn(ycat, gathered("w_out"), [F32], "fwd_out", jobs=[gathers["w_v"]])
    x1, x1b, xhat1, rstd1 = _ln_fwd(xs, y1, W["ln1_g"], W["ln1_b"], "fwd_ln1", jobs=[gathers["w_o"]])
    (q,) = _mm_nn(x1b, gathered("w_q"), [BF16], "fwd_q", jobs=[ff1_pieces[0]])
    (k,) = _mm_nn(memb, gathered("w_k"), [BF16], "fwd_k")
    (v,) = _mm_nn(memb, gathered("w_v"), [BF16], "fwd_v")
    o = _attn_fwd(q, k, v, "fwd_attn")
    (y2,) = _mm_nn(o, gathered("w_o"), [F32], "fwd_o", jobs=[ff1_pieces[1]])
    x2, x2b, xhat2, rstd2 = _ln_fwd(x1, y2, W["ln2_g"], W["ln2_b"], "fwd_ln2", jobs=[ff1_pieces[2]])
    w_ff1_full = jnp.concatenate([piece.out for piece in ff1_pieces], axis=1)

    def relu_sq(acc):
        r = jnp.maximum(acc, 0.0)
        return r, r * r

    rb, act = _mm_nn(x2b, w_ff1_full, [BF16, BF16], "fwd_ff1", epilogue=relu_sq, jobs=[gathers["w_ff2"]])
    (y3,) = _mm_nn(act, gathered("w_ff2"), [F32], "fwd_ff2")
    loss_rows, dz3, dz3b, dg3, db3 = _ln_loss(x2, y3, W["ln3_g"], W["ln3_b"], tgt, "ln3_loss")
    loss = lax.psum(loss_rows[0, 0], MESH_AXES)

    small = {"ln3_g": dg3, "ln3_b": db3}

    def add_residual(acc, e):
        return (acc + ALPHA * e,)

    dw_ff2 = _mm_tn(act, dz3b, 1, BF16, "bwd_dw_ff2")
    (dhid,) = _mm_nt(dz3b, gathered("w_ff2"), [BF16], "bwd_dact", extras=(rb,), jobs=[pair("w_ff2", dw_ff2)],
                     epilogue=lambda acc, r: (acc * (2.0 * r.astype(F32)),))
    dw_ff1 = _mm_tn(x2b, dhid, N_DEV, BF16, "bwd_dw_ff1", jobs=[quad("w_ff2")])
    (dx2,) = _mm_nt(dhid, w_ff1_full, [F32], "bwd_dx2", epilogue=add_residual, extras=(dz3,),
                    jobs=[pair("w_ff1", dw_ff1)])
    dz2, dz2b, small["ln2_g"], small["ln2_b"] = _ln_bwd(dx2, xhat2, rstd2, W["ln2_g"], "bwd_ln2")

    dw_o = _mm_tn(o, dz2b, 1, BF16, "bwd_dw_o")
    (do,) = _mm_nt(dz2b, gathered("w_o"), [BF16], "bwd_do", jobs=[pair("w_o", dw_o)])
    dq, dk, dv = _attn_bwd(q, k, v, do, "bwd_attn", jobs=[quad("w_o")])
    dw_q = _mm_tn(x1b, dq, 1, BF16, "bwd_dw_q")
    dw_k = _mm_tn(memb, dk.astype(BF16), 1, BF16, "bwd_dw_k")
    dw_v = _mm_tn(memb, dv.astype(BF16), 1, BF16, "bwd_dw_v")
    (dx1,) = _mm_nt(dq, gathered("w_q"), [F32], "bwd_dx1", epilogue=add_residual, extras=(dz2,),
                    jobs=[pair("w_q", dw_q), pair("w_k", dw_k), pair("w_v", dw_v)])
    dz1, dz1b, small["ln1_g"], small["ln1_b"] = _ln_bwd(dx1, xhat1, rstd1, W["ln1_g"], "bwd_ln1", jobs=[quad("w_q")])

    dw_out = _mm_tn(ycat, dz1b, 1, BF16, "bwd_dw_out", jobs=[quad("w_k")])
    (dycat,) = _mm_nt(dz1b, gathered("w_out"), [F32], "bwd_dycat", jobs=[quad("w_v"), pair("w_out", dw_out)])
    (dproj, dwp, small["b_pool"], small["pool_scale"], small["conv_w"], small["conv_b"], small["w_a"], small["b_a"],
     small["w_x"], small["b_x"], small["lru_lambda"]) = _mixer_bwd(
        dycat, proj, hsave, *mixer_consts, "bwd_mixer", jobs=[quad("w_ff1")])
    dw_pool = jnp.transpose(dwp.astype(BF16).reshape(N_POOL_GROUPS, N_DEV, cg // N_DEV, cg), (1, 0, 2, 3))
    pack = jnp.concatenate([_pack_rows(small[n], p) for n in SMALL_ORDER], axis=0)
    small_gather = _Job("gather", pack)
    dw_in = _mm_tn(xb, dproj, 1, BF16, "bwd_dw_in", jobs=[quad("w_out"), pair("w_pool", dw_pool), small_gather])
    dw_in = jnp.transpose(dw_in.reshape(dw_in.shape[1], N_DEV, -1), (1, 0, 2))
    (grad_x,) = _mm_nt(dproj, w_in_full, [F32], "bwd_dx", epilogue=add_residual, extras=(dz1,),
                       jobs=[pair("w_in", dw_in), quad("w_pool")])

    update("w_ff2", quads["w_ff2"].out, jobs=[quad("w_in")], own=quads["w_ff2"].src)
    for n in ("w_ff1", "w_o", "w_q", "w_k", "w_v", "w_out", "w_pool", "w_in"):
        update(n, quads[n].out, own=quads[n].src)

    total = _sum_parts(small_gather.out, "sum_small")
    row = 0
    for n in SMALL_ORDER:
        size = small[n].size
        nrows = size // p
        g_full = total[row:row + nrows].reshape(small[n].shape)
        row += nrows + (-nrows) % SUBLANES
        if n == "conv_w":
            g_loc = lax.dynamic_slice_in_dim(g_full, me * (p // N_DEV), p // N_DEV, axis=1)
        elif n == "b_pool":
            g_loc = lax.dynamic_slice_in_dim(g_full.reshape(N_POOL_GROUPS, cg), me * (cg // N_DEV), cg // N_DEV, axis=1)
        else:
            g_loc = g_full
        rows = g_loc.shape[0] if n not in ("w_a", "w_x") else LRU_HEADS * hd
        update(n, g_loc.reshape(1, rows, -1))

    order = names
    return (loss, grad_x[None], *[out_g[n] for n in order], *[out_d[n] for n in order],
            *[out_m[n] for n in order], *[out_v[n] for n in order])
```

```python
import functools

import jax
import jax.numpy as jnp
from jax import lax
from jax.experimental import pallas as pl
from jax.experimental.pallas import tpu as pltpu

F32 = jnp.float32
BF16 = jnp.bfloat16

N_DEV = 8
MESH_AXES = ("x", "y", "c")
POOL_WINDOWS = (2, 4, 8, 16)
N_POOL_GROUPS = len(POOL_WINDOWS)
POOL_HALO = 16
CONV_WIDTH = 4
CONV_HALO = 8
FF1_PIECES = 4
LRU_HEADS = 8
LRU_C = 8.0
XATTN_HEADS = 4
LN_EPS = 1e-5
ALPHA = 2.0 ** 0.25
ADAM_LR = 0.001
ADAM_B1 = 0.9
ADAM_B2 = 0.999
ADAM_EPS = 1e-08
ADAM_WD = 0.01
ADAM_STEP = 10
SUBLANES = 8
VMEM_LIMIT = 56 * 1024 * 1024

NT_DIMS = (((1,), (1,)), ((), ()))
TN_DIMS = (((0,), (0,)), ((), ()))


def _params(*sem):
    return pltpu.CompilerParams(dimension_semantics=sem, vmem_limit_bytes=VMEM_LIMIT)


def _place():
    return lax.axis_index("x"), lax.axis_index("y"), lax.axis_index("c")


def _remote(src, dst, send_sem, recv_sem, to):
    return pltpu.make_async_remote_copy(src_ref=src, dst_ref=dst, send_sem=send_sem, recv_sem=recv_sem,
                                        device_id=to, device_id_type=pl.DeviceIdType.MESH)


class _Job:
    def __init__(self, kind, src):
        self.kind, self.src, self.out = kind, src, None

    def out_shape(self):
        s = self.src.shape
        shape = {"gather": (N_DEV,) + s, "pair": (4,) + s[1:], "quad": (3,) + s[1:]}[self.kind]
        return jax.ShapeDtypeStruct(shape, self.src.dtype)

    def scratch(self):
        n = {"gather": 7, "pair": 4, "quad": 3}[self.kind]
        sems = [pltpu.SemaphoreType.DMA((n,)), pltpu.SemaphoreType.DMA((n,))]
        if self.kind == "gather":
            sems += [pltpu.SemaphoreType.DMA((2,)), pltpu.VMEM(self.src.shape, self.src.dtype)]
        return sems

    def ops(self, src, out, *scratch):
        return {"gather": _gather_ops, "pair": _pair_ops, "quad": _quad_ops}[self.kind](src, out, *scratch)


def _gather_ops(x_ref, out_ref, send_sems, recv_sems, local_sems, bounce):
    x, y, c = _place()
    me, sibling = (x, y, c), (x, y, 1 - c)
    chips = [(1 - x, y), (x, 1 - y), (1 - x, 1 - y)]

    def slot(px, py, pc):
        return out_ref.at[4 * px + 2 * py + pc]

    def copy(k, block, to, src=None):
        return _remote(slot(*block) if src is None else src, slot(*block), send_sems.at[k], recv_sems.at[k], to)

    mine_in = pltpu.make_async_copy(x_ref, bounce, local_sems.at[0])
    mine_out = pltpu.make_async_copy(bounce, slot(*me), local_sems.at[1])
    first = [copy(0, me, sibling, src=x_ref)] + [copy(1 + j, me, (*chip, c), src=x_ref) for j, chip in enumerate(chips)]
    passed = [copy(4 + j, (*chip, c), sibling) for j, chip in enumerate(chips)]

    def start():
        mine_in.start()
        for cp in first:
            cp.start()

    def mid():
        mine_in.wait()
        mine_out.start()
        for j, chip in enumerate(chips):
            copy(1 + j, (*chip, c), me).wait_recv()
            passed[j].start()

    def finish():
        copy(0, sibling, me).wait_recv()
        for j, chip in enumerate(chips):
            copy(4 + j, (*chip, 1 - c), me).wait_recv()
        for cp in first + passed:
            cp.wait_send()
        mine_out.wait()

    return start, mid, finish


def _pair_ops(p_ref, got_ref, send_sems, recv_sems):
    x, y, c = _place()
    give = [_remote(p_ref.at[2 * k + 1 - c], got_ref.at[k], send_sems.at[k], recv_sems.at[k], (x, y, 1 - c))
            for k in range(4)]

    def start():
        for cp in give:
            cp.start()

    def finish():
        for cp in give:
            cp.wait_recv()
        for cp in give:
            cp.wait_send()

    return start, None, finish


def _quad_ops(q_ref, out_ref, send_sems, recv_sems):
    x, y, c = _place()
    copies = []
    for rel in range(1, 4):
        px = 1 - x if rel & 2 else x
        py = 1 - y if rel & 1 else y
        copies.append(_remote(q_ref.at[2 * px + py], out_ref.at[rel - 1], send_sems.at[rel - 1], recv_sems.at[rel - 1],
                              (px, py, c)))

    def start():
        for cp in copies:
            cp.start()

    def finish():
        for cp in copies:
            cp.wait_recv()
        for cp in copies:
            cp.wait_send()

    return start, None, finish


def _call(body, *, grid, in_specs, out_specs, out_shape, scratch_shapes=(), semantics, name, args, jobs=(), index=None):
    in_specs, out_specs, out_shape = list(in_specs), list(out_specs), list(out_shape)
    scratch_shapes, jobs = list(scratch_shapes), list(jobs)
    n_in, n_out, n_scr, n_job = len(in_specs), len(out_specs), len(scratch_shapes), len(jobs)
    n_idx = 0 if index is None else 1
    job_scratch = [j.scratch() for j in jobs]
    n_steps = functools.reduce(lambda a, b: a * b, grid, 1)
    early = n_steps - 1 - max(1, n_steps // 8) if n_steps >= 4 else None

    def hosted(*refs):
        idx, refs = refs[:n_idx], refs[n_idx:]
        ins, jin = refs[:n_in], refs[n_in:n_in + n_job]
        o0 = n_in + n_job
        outs, jout = refs[o0:o0 + n_out], refs[o0 + n_out:o0 + n_out + n_job]
        s0 = o0 + n_out + n_job
        scr, jscr = refs[s0:s0 + n_scr], refs[s0 + n_scr:]
        ops, at = [], 0
        for k, j in enumerate(jobs):
            ops.append(j.ops(jin[k], jout[k], *jscr[at:at + len(job_scratch[k])]))
            at += len(job_scratch[k])
        step = functools.reduce(lambda acc, a: acc * grid[a] + pl.program_id(a), range(len(grid)), 0)
        mids = [mid for _, mid, _ in ops if mid is not None]

        @pl.when(step == 0)
        def _():
            for start, _, _ in ops:
                start()

        if mids and early is not None:
            @pl.when(step == early)
            def _():
                for mid in mids:
                    mid()

        body(*idx, *ins, *outs, *scr)

        @pl.when(step == n_steps - 1)
        def _():
            if early is None:
                for mid in mids:
                    mid()
            for _, _, finish in ops:
                finish()

    hbm = pl.BlockSpec(memory_space=pl.ANY)
    spec = pltpu.PrefetchScalarGridSpec(
        num_scalar_prefetch=n_idx, grid=grid, in_specs=in_specs + [hbm] * n_job, out_specs=out_specs + [hbm] * n_job,
        scratch_shapes=scratch_shapes + [s for js in job_scratch for s in js])
    res = pl.pallas_call(
        hosted if jobs else body, grid_spec=spec, out_shape=out_shape + [j.out_shape() for j in jobs],
        compiler_params=_params(*(["arbitrary"] * len(grid) if jobs else semantics)), name=name,
    )(*([] if index is None else [index]), *args, *[j.src for j in jobs])
    for j, o in zip(jobs, res[n_out:]):
        j.out = o
    return res[:n_out]


def _run_jobs(jobs, name):
    def body(tick_ref):
        tick_ref[...] = jnp.zeros_like(tick_ref)

    _call(body, grid=(1,), in_specs=[], out_specs=[pl.BlockSpec((SUBLANES, 128), lambda i: (0, 0))],
          out_shape=[jax.ShapeDtypeStruct((SUBLANES, 128), F32)], semantics=("arbitrary",), name=name, args=(), jobs=jobs)


TILES = dict(tm=1024, tn=1024, tk=2048, row=256, attn=512, mixer=256, adam=128, add=1024)


def _tile(pref, n):
    for t in range(min(pref, n), 0, -1):
        if n % t == 0 and (t % SUBLANES == 0 or t == n):
            return t
    return n


def _accumulate(acc, step, n_steps, product, write):
    if n_steps == 1:
        write(product)
        return

    @pl.when(step == 0)
    def _():
        acc[...] = product

    @pl.when(jnp.logical_and(step > 0, step < n_steps - 1))
    def _():
        acc[...] += product

    @pl.when(step == n_steps - 1)
    def _():
        write(acc[...] + product)


def _acc_scratch(n_steps, tm, tn):
    return [] if n_steps == 1 else [pltpu.VMEM((tm, tn), F32)]


def _mm_nn(a, b3, out_dtypes, name, *, tm=None, tn=None, tk=None, epilogue=None, extras=(), jobs=()):
    m, k = a.shape
    g, k2, ns = b3.shape
    assert k == k2
    n = g * ns
    tm, tn, tk = _tile(tm or TILES["tm"], m), _tile(tn or TILES["tn"], ns), _tile(tk or TILES["tk"], k)
    nb, nk = ns // tn, k // tk
    n_ex, n_out = len(extras), len(out_dtypes)

    def body(*refs):
        a_ref, b_ref = refs[:2]
        ex = refs[2:2 + n_ex]
        outs = refs[2 + n_ex:2 + n_ex + n_out]
        acc = refs[-1] if nk > 1 else None

        def write(r):
            res = epilogue(r, *[e[...] for e in ex]) if epilogue is not None else (r,)
            for o, v in zip(outs, res):
                o[...] = v.astype(o.dtype)

        _accumulate(acc, pl.program_id(2), nk, jnp.dot(a_ref[...], b_ref[...], preferred_element_type=F32), write)

    tile_out = pl.BlockSpec((tm, tn), lambda i, j, kk: (i, j))
    return _call(
        body, grid=(m // tm, n // tn, nk),
        in_specs=[pl.BlockSpec((tm, tk), lambda i, j, kk: (i, kk)),
                  pl.BlockSpec((None, tk, tn), lambda i, j, kk: (j // nb, kk, j % nb))] + [tile_out] * n_ex,
        out_specs=[tile_out] * n_out,
        out_shape=[jax.ShapeDtypeStruct((m, n), d) for d in out_dtypes],
        scratch_shapes=_acc_scratch(nk, tm, tn),
        semantics=("parallel", "parallel", "arbitrary"), name=name, args=(a, b3, *extras), jobs=jobs)


def _mm_nt(a, b3, out_dtypes, name, *, tm=None, tn=None, tk=None, epilogue=None, extras=(), jobs=()):
    m, n = a.shape
    g, k, ns = b3.shape
    assert n == g * ns
    tm, tn, tk = _tile(tm or TILES["tm"], m), _tile(tn or TILES["tn"], k), _tile(tk or TILES["tk"], ns)
    nb, nc = ns // tk, n // tk
    n_ex, n_out = len(extras), len(out_dtypes)

    def body(*refs):
        a_ref, b_ref = refs[:2]
        ex = refs[2:2 + n_ex]
        outs = refs[2 + n_ex:2 + n_ex + n_out]
        acc = refs[-1] if nc > 1 else None

        def write(r):
            res = epilogue(r, *[e[...] for e in ex]) if epilogue is not None else (r,)
            for o, v in zip(outs, res):
                o[...] = v.astype(o.dtype)

        _accumulate(acc, pl.program_id(2), nc,
                    lax.dot_general(a_ref[...], b_ref[...], NT_DIMS, preferred_element_type=F32), write)

    tile_out = pl.BlockSpec((tm, tn), lambda i, j, cc: (i, j))
    return _call(
        body, grid=(m // tm, k // tn, nc),
        in_specs=[pl.BlockSpec((tm, tk), lambda i, j, cc: (i, cc)),
                  pl.BlockSpec((None, tn, tk), lambda i, j, cc: (cc // nb, j, cc % nb))] + [tile_out] * n_ex,
        out_specs=[tile_out] * n_out,
        out_shape=[jax.ShapeDtypeStruct((m, k), d) for d in out_dtypes],
        scratch_shapes=_acc_scratch(nc, tm, tn),
        semantics=("parallel", "parallel", "arbitrary"), name=name, args=(a, b3, *extras), jobs=jobs)


def _mm_tn(a, b, g, out_dtype, name, *, tm=None, tn=None, tk=None, jobs=()):
    s, m = a.shape
    s2, n = b.shape
    assert s == s2 and n % g == 0
    ns = n // g
    tm, tn, tk = _tile(tm or TILES["tm"], m), _tile(tn or TILES["tn"], ns), _tile(tk or TILES["tk"], s)
    nb, nc = ns // tn, s // tk

    def body(a_ref, b_ref, o_ref, *scratch):
        def write(r):
            o_ref[...] = r.astype(o_ref.dtype)

        _accumulate(scratch[0] if nc > 1 else None, pl.program_id(2), nc,
                    lax.dot_general(a_ref[...], b_ref[...], TN_DIMS, preferred_element_type=F32), write)

    return _call(
        body, grid=(m // tm, n // tn, nc),
        in_specs=[pl.BlockSpec((tk, tm), lambda i, j, cc: (cc, i)),
                  pl.BlockSpec((tk, tn), lambda i, j, cc: (cc, j))],
        out_specs=[pl.BlockSpec((None, tm, tn), lambda i, j, cc: (j // nb, i, j % nb))],
        out_shape=[jax.ShapeDtypeStruct((g, m, ns), out_dtype)],
        scratch_shapes=_acc_scratch(nc, tm, tn),
        semantics=("parallel", "parallel", "arbitrary"), name=name, args=(a, b), jobs=jobs)[0]


def _ln_stats(z):
    mu = jnp.mean(z, axis=-1, keepdims=True)
    zc = z - mu
    var = jnp.mean(zc * zc, axis=-1, keepdims=True)
    rstd = lax.rsqrt(var + LN_EPS)
    return zc * rstd, rstd


def _ln_grad(dout, xhat, rstd, gain):
    dxhat = dout * gain
    m1 = jnp.mean(dxhat, axis=-1, keepdims=True)
    m2 = jnp.mean(dxhat * xhat, axis=-1, keepdims=True)
    return rstd * (dxhat - m1 - xhat * m2)


def _ln_fwd(xres, y, gain, bias, name, jobs=()):
    s, d = xres.shape
    tr = _tile(TILES["row"], s)

    def body(x_ref, y_ref, g_ref, b_ref, xn_ref, xnb_ref, xhat_ref, rstd_ref):
        xhat, rstd = _ln_stats(ALPHA * x_ref[...] + y_ref[...])
        out = xhat * g_ref[...] + b_ref[...]
        xn_ref[...] = out
        xnb_ref[...] = out.astype(BF16)
        xhat_ref[...] = xhat
        rstd_ref[...] = rstd

    row = pl.BlockSpec((tr, d), lambda i: (i, 0))
    vec = pl.BlockSpec((1, d), lambda i: (0, 0))
    return _call(
        body, grid=(s // tr,), in_specs=[row, row, vec, vec],
        out_specs=[row, row, row, pl.BlockSpec((tr, 1), lambda i: (i, 0))],
        out_shape=[jax.ShapeDtypeStruct((s, d), F32), jax.ShapeDtypeStruct((s, d), BF16),
                   jax.ShapeDtypeStruct((s, d), F32), jax.ShapeDtypeStruct((s, 1), F32)],
        semantics=("parallel",), name=name, args=(xres, y, gain, bias), jobs=jobs)


def _ln_bwd(dout, xhat, rstd, gain, name, jobs=()):
    s, d = dout.shape
    tr = _tile(TILES["row"], s)

    def body(d_ref, xhat_ref, rstd_ref, g_ref, dz_ref, dzb_ref, dg_ref, db_ref):
        @pl.when(pl.program_id(0) == 0)
        def _():
            dg_ref[...] = jnp.zeros_like(dg_ref)
            db_ref[...] = jnp.zeros_like(db_ref)

        dout_t, xhat_t = d_ref[...], xhat_ref[...]
        dz = _ln_grad(dout_t, xhat_t, rstd_ref[...], g_ref[...])
        dz_ref[...] = dz
        dzb_ref[...] = dz.astype(BF16)
        dg_ref[...] += jnp.sum(dout_t * xhat_t, axis=0, keepdims=True)
        db_ref[...] += jnp.sum(dout_t, axis=0, keepdims=True)

    row = pl.BlockSpec((tr, d), lambda i: (i, 0))
    vec = pl.BlockSpec((1, d), lambda i: (0, 0))
    return _call(
        body, grid=(s // tr,), in_specs=[row, row, pl.BlockSpec((tr, 1), lambda i: (i, 0)), vec],
        out_specs=[row, row, vec, vec],
        out_shape=[jax.ShapeDtypeStruct((s, d), F32), jax.ShapeDtypeStruct((s, d), BF16),
                   jax.ShapeDtypeStruct((1, d), F32), jax.ShapeDtypeStruct((1, d), F32)],
        semantics=("arbitrary",), name=name, args=(dout, xhat, rstd, gain), jobs=jobs)


def _ln_loss(xres, y, gain, bias, target, name, jobs=()):
    s, d = xres.shape
    tr = _tile(TILES["row"], s)

    def body(x_ref, y_ref, g_ref, b_ref, t_ref, loss_ref, dz_ref, dzb_ref, dg_ref, db_ref):
        @pl.when(pl.program_id(0) == 0)
        def _():
            loss_ref[...] = jnp.zeros_like(loss_ref)
            dg_ref[...] = jnp.zeros_like(dg_ref)
            db_ref[...] = jnp.zeros_like(db_ref)

        xhat, rstd = _ln_stats(ALPHA * x_ref[...] + y_ref[...])
        diff = xhat * g_ref[...] + b_ref[...] - t_ref[...]
        per_row = jnp.mean(diff * diff, axis=-1, keepdims=True)
        loss_ref[...] += 0.5 * jnp.sum(per_row, axis=0, keepdims=True)
        dout = diff * (1.0 / d)
        dz = _ln_grad(dout, xhat, rstd, g_ref[...])
        dz_ref[...] = dz
        dzb_ref[...] = dz.astype(BF16)
        dg_ref[...] += jnp.sum(dout * xhat, axis=0, keepdims=True)
        db_ref[...] += jnp.sum(dout, axis=0, keepdims=True)

    row = pl.BlockSpec((tr, d), lambda i: (i, 0))
    vec = pl.BlockSpec((1, d), lambda i: (0, 0))
    return _call(
        body, grid=(s // tr,), in_specs=[row, row, vec, vec, row],
        out_specs=[pl.BlockSpec((1, 128), lambda i: (0, 0)), row, row, vec, vec],
        out_shape=[jax.ShapeDtypeStruct((1, 128), F32), jax.ShapeDtypeStruct((s, d), F32),
                   jax.ShapeDtypeStruct((s, d), BF16), jax.ShapeDtypeStruct((1, d), F32),
                   jax.ShapeDtypeStruct((1, d), F32)],
        semantics=("arbitrary",), name=name, args=(xres, y, gain, bias, target), jobs=jobs)


def _softmax_rows(s):
    e = jnp.exp(s - jnp.max(s, axis=-1, keepdims=True))
    return e / jnp.sum(e, axis=-1, keepdims=True)


def _attn_fwd(q, k, v, name, jobs=()):
    s, d = q.shape
    m = k.shape[0]
    hd = d // XATTN_HEADS
    ts = _tile(TILES["attn"], s)
    scale = hd ** -0.5

    def body(q_ref, k_ref, v_ref, o_ref):
        for h in range(XATTN_HEADS):
            hs = slice(h * hd, (h + 1) * hd)
            sc = lax.dot_general(q_ref[:, hs], k_ref[:, hs], NT_DIMS, preferred_element_type=F32) * scale
            p = _softmax_rows(sc).astype(BF16)
            o_ref[:, hs] = jnp.dot(p, v_ref[:, hs], preferred_element_type=F32).astype(BF16)

    row = pl.BlockSpec((ts, d), lambda i: (i, 0))
    memb = pl.BlockSpec((m, d), lambda i: (0, 0))
    return _call(
        body, grid=(s // ts,), in_specs=[row, memb, memb], out_specs=[row],
        out_shape=[jax.ShapeDtypeStruct((s, d), BF16)],
        semantics=("parallel",), name=name, args=(q, k, v), jobs=jobs)[0]


def _attn_bwd(q, k, v, do, name, jobs=()):
    s, d = q.shape
    m = k.shape[0]
    hd = d // XATTN_HEADS
    ts = _tile(TILES["attn"], s)
    scale = hd ** -0.5

    def body(q_ref, k_ref, v_ref, do_ref, dq_ref, dk_ref, dv_ref):
        @pl.when(pl.program_id(0) == 0)
        def _():
            dk_ref[...] = jnp.zeros_like(dk_ref)
            dv_ref[...] = jnp.zeros_like(dv_ref)

        for h in range(XATTN_HEADS):
            hs = slice(h * hd, (h + 1) * hd)
            qh, kh, vh, doh = q_ref[:, hs], k_ref[:, hs], v_ref[:, hs], do_ref[:, hs]
            sc = lax.dot_general(qh, kh, NT_DIMS, preferred_element_type=F32) * scale
            p = _softmax_rows(sc)
            pb = p.astype(BF16)
            dp = lax.dot_general(doh, vh, NT_DIMS, preferred_element_type=F32)
            ds = (p * (dp - jnp.sum(dp * p, axis=-1, keepdims=True)) * scale).astype(BF16)
            dq_ref[:, hs] = jnp.dot(ds, kh, preferred_element_type=F32).astype(BF16)
            dk_ref[:, hs] += lax.dot_general(ds, qh, TN_DIMS, preferred_element_type=F32)
            dv_ref[:, hs] += lax.dot_general(pb, doh, TN_DIMS, preferred_element_type=F32)

    row = pl.BlockSpec((ts, d), lambda i: (i, 0))
    memb = pl.BlockSpec((m, d), lambda i: (0, 0))
    return _call(
        body, grid=(s // ts,), in_specs=[row, memb, memb, row], out_specs=[row, memb, memb],
        out_shape=[jax.ShapeDtypeStruct((s, d), BF16), jax.ShapeDtypeStruct((m, d), F32),
                   jax.ShapeDtypeStruct((m, d), F32)],
        semantics=("arbitrary",), name=name, args=(q, k, v, do), jobs=jobs)


def _sigmoid(x):
    return 1.0 / (1.0 + jnp.exp(-x))


def _log1p(x):
    u = 1.0 + x
    return jnp.where(u == 1.0, x, jnp.log(u) * (x / jnp.where(u == 1.0, 1.0, u - 1.0)))


def _softplus(x):
    return jnp.maximum(x, 0.0) + _log1p(jnp.exp(-jnp.abs(x)))


def _expm1(x):
    series = x * (1.0 + x * 0.5 * (1.0 + x * (1.0 / 3.0) * (1.0 + x * 0.25 * (1.0 + x * 0.2 * (1.0 + x * (1.0 / 6.0))))))
    return jnp.where(jnp.abs(x) < 0.1, series, jnp.exp(x) - 1.0)


GELU_K = 0.7978845608028654
GELU_C = 0.044715


def _gelu(x):
    return 0.5 * x * (1.0 + jnp.tanh(GELU_K * (x + GELU_C * (x * x * x))))


def _gelu_grad(x):
    th = jnp.tanh(GELU_K * (x + GELU_C * (x * x * x)))
    return 0.5 * (1.0 + th) + 0.5 * x * (1.0 - th * th) * GELU_K * (1.0 + 3.0 * GELU_C * x * x)


def _window_sum(ext_ref, first, rows, cols, w, step):
    acc = ext_ref[first:first + rows, cols]
    for kk in range(1, w):
        acc = acc + ext_ref[first + step * kk:first + step * kk + rows, cols]
    return acc


def _lru_gates(c_s, wa_ref, ba_ref, wx_ref, bx_ref, lam_ref, t_idx, hd, r_s, i_s, a_s, mult_s):
    sp = _softplus(-lam_ref[...])
    for h in range(LRU_HEADS):
        hs = slice(h * hd, (h + 1) * hd)
        chb = c_s[:, hs].astype(BF16)
        r = _sigmoid(jnp.dot(chb, wa_ref[h], preferred_element_type=F32) + ba_ref[:, hs])
        ig = _sigmoid(jnp.dot(chb, wx_ref[h], preferred_element_type=F32) + bx_ref[:, hs])
        log_a = -LRU_C * r * sp[:, hs]
        mult = jnp.sqrt(-_expm1(2.0 * log_a))
        r_s[:, hs] = r
        i_s[:, hs] = ig
        a_s[:, hs] = jnp.exp(log_a)
        mult_s[:, hs] = jnp.where(t_idx == 0, 1.0, mult)


def _conv(ext_ref, cw_ref, cb_ref, rows):
    acc = cb_ref[...] + cw_ref[0:1, :] * ext_ref[CONV_HALO - 3:CONV_HALO - 3 + rows, :]
    for kk in range(1, CONV_WIDTH):
        off = CONV_HALO - (CONV_WIDTH - 1) + kk
        acc = acc + cw_ref[kk:kk + 1, :] * ext_ref[off:off + rows, :]
    return acc


def _mixer_fwd(proj, wp, bp, ps, cw, cb, wa, ba, wx, bx, lam, name, jobs=()):
    s, p3 = proj.shape
    p = p3 // 3
    cg, hd = p // N_POOL_GROUPS, p // LRU_HEADS
    t = _tile(TILES["mixer"], s)

    def body(up_ref, ul_ref, ug_ref, wp_ref, bp_ref, ps_ref, cw_ref, cb_ref, wa_ref, ba_ref, wx_ref, bx_ref,
             lam_ref, ycat_ref, h_ref, extp, extl, hc, c_s, r_s, i_s, a_s, b_s):
        i = pl.program_id(0)

        @pl.when(i == 0)
        def _():
            extp[0:POOL_HALO, :] = jnp.zeros((POOL_HALO, p), F32)
            extl[0:CONV_HALO, :] = jnp.zeros((CONV_HALO, p), F32)
            hc[...] = jnp.zeros_like(hc)

        t_idx = i * t + lax.broadcasted_iota(jnp.int32, (t, 1), 0)

        extp[POOL_HALO:POOL_HALO + t, :] = up_ref[...]
        for g, w in enumerate(POOL_WINDOWS):
            cs = slice(g * cg, (g + 1) * cg)
            cnt = jnp.minimum(t_idx + 1, w).astype(F32)
            mixed = _window_sum(extp, POOL_HALO, t, cs, w, -1) / cnt - up_ref[:, cs]
            pre = jnp.dot(mixed.astype(BF16), wp_ref[g], preferred_element_type=F32) + bp_ref[:, cs]
            ycat_ref[:, cs] = (pre * ps_ref[:, cs]).astype(BF16)
        extp[0:POOL_HALO, :] = extp[t:t + POOL_HALO, :]

        extl[CONV_HALO:CONV_HALO + t, :] = ul_ref[...]
        c_s[...] = _conv(extl, cw_ref, cb_ref, t)
        extl[0:CONV_HALO, :] = extl[t:t + CONV_HALO, :]
        _lru_gates(c_s, wa_ref, ba_ref, wx_ref, bx_ref, lam_ref, t_idx, hd, r_s, i_s, a_s, b_s)
        b_s[...] = b_s[...] * (i_s[...] * c_s[...])

        rows = lax.broadcasted_iota(jnp.int32, (SUBLANES, p), 0)

        def block(bi, h):
            r0 = pl.multiple_of(bi * SUBLANES, SUBLANES)
            at = a_s[pl.ds(r0, SUBLANES), :]
            bt = b_s[pl.ds(r0, SUBLANES), :]
            out = jnp.zeros((SUBLANES, p), F32)
            for j in range(SUBLANES):
                h = at[j:j + 1, :] * h + bt[j:j + 1, :]
                out = jnp.where(rows == j, h, out)
            h_ref[pl.ds(r0, SUBLANES), :] = out
            return h

        hc[0:1, :] = lax.fori_loop(0, t // SUBLANES, block, hc[0:1, :])
        ycat_ref[:, p:2 * p] = (h_ref[...] * _gelu(ug_ref[...])).astype(BF16)

    def col(j):
        return pl.BlockSpec((t, p), lambda i: (i, j))

    def whole(a):
        nd = a.ndim
        return pl.BlockSpec(a.shape, lambda i: (0,) * nd)

    consts = (wp, bp, ps, cw, cb, wa, ba, wx, bx, lam)
    tile = pltpu.VMEM((t, p), F32)
    return _call(
        body, grid=(s // t,), in_specs=[col(0), col(1), col(2)] + [whole(a) for a in consts],
        out_specs=[pl.BlockSpec((t, 2 * p), lambda i: (i, 0)), pl.BlockSpec((t, p), lambda i: (i, 0))],
        out_shape=[jax.ShapeDtypeStruct((s, 2 * p), BF16), jax.ShapeDtypeStruct((s, p), F32)],
        scratch_shapes=[pltpu.VMEM((t + POOL_HALO, p), F32), pltpu.VMEM((t + CONV_HALO, p), F32),
                        pltpu.VMEM((SUBLANES, p), F32), tile, tile, tile, tile, tile],
        semantics=("arbitrary",), name=name, args=(proj, proj, proj, *consts), jobs=jobs)


def _mixer_bwd(dycat, proj, hsave, wp, bp, ps, cw, cb, wa, ba, wx, bx, lam, name, jobs=()):
    s, p3 = proj.shape
    p = p3 // 3
    cg, hd = p // N_POOL_GROUPS, p // LRU_HEADS
    t = _tile(TILES["mixer"], s)
    nt = s // t

    def body(dyp_ref, dyl_ref, up_ref, ul_ref, ug_ref, upp_ref, ulp_ref, h_ref, hp_ref,
             wp_ref, bp_ref, ps_ref, cw_ref, cb_ref, wa_ref, ba_ref, wx_ref, bx_ref, lam_ref,
             dproj_ref, dwp_ref, dbp_ref, dps_ref, dcw_ref, dcb_ref, dwa_ref, dba_ref, dwx_ref, dbx_ref, dlam_ref,
             extp, extg, extl, extdc, exth, ghc, c_s, r_s, i_s, a_s, mult_s, gh_s):
        i = pl.program_id(0)
        ib = nt - 1 - i

        @pl.when(i == 0)
        def _():
            for ref in (dwp_ref, dbp_ref, dps_ref, dcw_ref, dcb_ref, dwa_ref, dba_ref, dwx_ref, dbx_ref, dlam_ref):
                ref[...] = jnp.zeros_like(ref)
            extg[t:t + POOL_HALO, :] = jnp.zeros((POOL_HALO, p), F32)
            extdc[t:t + CONV_HALO, :] = jnp.zeros((CONV_HALO, p), F32)
            ghc[...] = jnp.zeros_like(ghc)

        t_idx = ib * t + lax.broadcasted_iota(jnp.int32, (t, 1), 0)
        seq_start = ib == 0

        extl[0:CONV_HALO, :] = jnp.where(seq_start, 0.0, ulp_ref[...])
        extl[CONV_HALO:CONV_HALO + t, :] = ul_ref[...]
        c_s[...] = _conv(extl, cw_ref, cb_ref, t)
        _lru_gates(c_s, wa_ref, ba_ref, wx_ref, bx_ref, lam_ref, t_idx, hd, r_s, i_s, a_s, mult_s)
        exth[0:SUBLANES, :] = jnp.where(seq_start, 0.0, hp_ref[...])
        exth[SUBLANES:SUBLANES + t, :] = h_ref[...]

        ug = ug_ref[...]
        dyl = dyl_ref[...]
        dproj_ref[:, 2 * p:3 * p] = (dyl * h_ref[...] * _gelu_grad(ug)).astype(BF16)
        gh_s[...] = dyl * _gelu(ug)

        rows = lax.broadcasted_iota(jnp.int32, (SUBLANES, p), 0)
        nblk = t // SUBLANES

        def block(bi, carry):
            r0 = pl.multiple_of((nblk - 1 - bi) * SUBLANES, SUBLANES)
            at = a_s[pl.ds(r0, SUBLANES), :]
            dt = gh_s[pl.ds(r0, SUBLANES), :]
            out = jnp.zeros((SUBLANES, p), F32)
            for j in range(SUBLANES - 1, -1, -1):
                gh = dt[j:j + 1, :] + carry
                out = jnp.where(rows == j, gh, out)
                carry = at[j:j + 1, :] * gh
            gh_s[pl.ds(r0, SUBLANES), :] = out
            return carry

        ghc[0:1, :] = lax.fori_loop(0, nblk, block, ghc[0:1, :])

        sp = _softplus(-lam_ref[...])
        dsp_dlam = -_sigmoid(-lam_ref[...])
        for h in range(LRU_HEADS):
            hs = slice(h * hd, (h + 1) * hd)
            gh, a, mult, r, ig, c = gh_s[:, hs], a_s[:, hs], mult_s[:, hs], r_s[:, hs], i_s[:, hs], c_s[:, hs]
            hprev = exth[SUBLANES - 1:SUBLANES - 1 + t, hs]
            dmult = gh * (ig * c)
            dlog_a = a * gh * hprev + jnp.where(t_idx == 0, 0.0, -dmult * a * a / mult)
            dlam_ref[:, hs] += jnp.sum(dlog_a * r, axis=0, keepdims=True) * (-LRU_C) * dsp_dlam[:, hs]
            dpa = dlog_a * (-LRU_C * sp[:, hs]) * r * (1.0 - r)
            dpx = gh * mult * c * ig * (1.0 - ig)
            dpab, dpxb, chb = dpa.astype(BF16), dpx.astype(BF16), c.astype(BF16)
            dwa_ref[h] += lax.dot_general(chb, dpab, TN_DIMS, preferred_element_type=F32)
            dwx_ref[h] += lax.dot_general(chb, dpxb, TN_DIMS, preferred_element_type=F32)
            dba_ref[:, hs] += jnp.sum(dpa, axis=0, keepdims=True)
            dbx_ref[:, hs] += jnp.sum(dpx, axis=0, keepdims=True)
            dc = (gh * mult * ig
                  + lax.dot_general(dpab, wa_ref[h], NT_DIMS, preferred_element_type=F32)
                  + lax.dot_general(dpxb, wx_ref[h], NT_DIMS, preferred_element_type=F32))
            extdc[0:t, hs] = dc
            dcb_ref[:, hs] += jnp.sum(dc, axis=0, keepdims=True)
            for kk in range(CONV_WIDTH):
                off = CONV_HALO - (CONV_WIDTH - 1) + kk
                dcw_ref[kk:kk + 1, hs] += jnp.sum(dc * extl[off:off + t, hs], axis=0, keepdims=True)
        du_lru = cw_ref[0:1, :] * extdc[CONV_WIDTH - 1:CONV_WIDTH - 1 + t, :]
        for kk in range(1, CONV_WIDTH):
            off = CONV_WIDTH - 1 - kk
            du_lru = du_lru + cw_ref[kk:kk + 1, :] * extdc[off:off + t, :]
        dproj_ref[:, p:2 * p] = du_lru.astype(BF16)
        extdc[t:t + CONV_HALO, :] = extdc[0:CONV_HALO, :]

        extp[0:POOL_HALO, :] = jnp.where(seq_start, 0.0, upp_ref[...])
        extp[POOL_HALO:POOL_HALO + t, :] = up_ref[...]
        for g, w in enumerate(POOL_WINDOWS):
            cs = slice(g * cg, (g + 1) * cg)
            cnt = jnp.minimum(t_idx + 1, w).astype(F32)
            mixed = (_window_sum(extp, POOL_HALO, t, cs, w, -1) / cnt - up_ref[:, cs]).astype(BF16)
            pre = jnp.dot(mixed, wp_ref[g], preferred_element_type=F32) + bp_ref[:, cs]
            dyp = dyp_ref[:, cs]
            dps_ref[:, cs] += jnp.sum(dyp * pre, axis=0, keepdims=True)
            dpre = dyp * ps_ref[:, cs]
            dpreb = dpre.astype(BF16)
            dbp_ref[:, cs] += jnp.sum(dpre, axis=0, keepdims=True)
            dwp_ref[g] += lax.dot_general(mixed, dpreb, TN_DIMS, preferred_element_type=F32)
            dmixed = lax.dot_general(dpreb, wp_ref[g], NT_DIMS, preferred_element_type=F32)
            extg[0:t, cs] = dmixed / cnt
            dproj_ref[:, cs] = (_window_sum(extg, 0, t, cs, w, 1) - dmixed).astype(BF16)
        extg[t:t + POOL_HALO, :] = extg[0:POOL_HALO, :]

    def col(j):
        return pl.BlockSpec((t, p), lambda i: (nt - 1 - i, j))

    def prev(rows, j):
        per = t // rows
        return pl.BlockSpec((rows, p), lambda i: (jnp.maximum((nt - 1 - i) * per - 1, 0), j))

    def whole(a):
        nd = a.ndim
        return pl.BlockSpec(a.shape, lambda i: (0,) * nd)

    consts = (wp, bp, ps, cw, cb, wa, ba, wx, bx, lam)
    grads = (wp, bp, ps, cw, cb, wa, ba, wx, bx, lam)
    tile = pltpu.VMEM((t, p), F32)
    return _call(
        body, grid=(nt,),
        in_specs=[col(0), col(1), col(0), col(1), col(2), prev(POOL_HALO, 0), prev(CONV_HALO, 1), col(0),
                  prev(SUBLANES, 0)] + [whole(a) for a in consts],
        out_specs=[pl.BlockSpec((t, 3 * p), lambda i: (nt - 1 - i, 0))] + [whole(a) for a in grads],
        out_shape=[jax.ShapeDtypeStruct((s, 3 * p), BF16)] + [jax.ShapeDtypeStruct(a.shape, F32) for a in grads],
        scratch_shapes=[pltpu.VMEM((t + POOL_HALO, p), F32), pltpu.VMEM((t + POOL_HALO, p), F32),
                        pltpu.VMEM((t + CONV_HALO, p), F32), pltpu.VMEM((t + CONV_HALO, p), F32),
                        pltpu.VMEM((t + SUBLANES, p), F32), pltpu.VMEM((SUBLANES, p), F32),
                        tile, tile, tile, tile, tile, tile],
        semantics=("arbitrary",), name=name,
        args=(dycat, dycat, proj, proj, proj, proj, proj, hsave, hsave, *consts), jobs=jobs)


def _pair_add(parts, got, core, name):
    n, r, c = got.shape
    tr = _tile(TILES["add"], r)

    def body(core_ref, a_ref, b_ref, o_ref):
        del core_ref
        o_ref[...] = (a_ref[...].astype(F32) + b_ref[...].astype(F32)).astype(o_ref.dtype)

    blk = pl.BlockSpec((None, tr, c), lambda k, i, core_ref: (k, i, 0))
    mine = pl.BlockSpec((None, tr, c), lambda k, i, core_ref: (2 * k + core_ref[0], i, 0))
    return _call(body, grid=(n, r // tr), in_specs=[mine, blk], out_specs=[blk],
                 out_shape=[jax.ShapeDtypeStruct(got.shape, got.dtype)], semantics=("parallel", "parallel"),
                 name=name, args=(parts, got), index=core)[0]


def _sum_parts(parts, name):
    n, r, c = parts.shape
    tr = _tile(TILES["adam"], r)

    def body(p_ref, o_ref):
        acc = p_ref[0].astype(F32)
        for d in range(1, n):
            acc = acc + p_ref[d].astype(F32)
        o_ref[...] = acc

    return _call(
        body, grid=(r // tr,), in_specs=[pl.BlockSpec((n, tr, c), lambda i: (0, i, 0))],
        out_specs=[pl.BlockSpec((tr, c), lambda i: (i, 0))], out_shape=[jax.ShapeDtypeStruct((r, c), F32)],
        semantics=("parallel",), name=name, args=(parts,))[0]


def _adamw(w, m, v, parts, name, jobs=(), own=None, chip=None):
    r, c = w.shape
    n = parts.shape[0]
    tr = _tile(TILES["adam"], r)

    def body(*refs):
        if own is not None:
            refs = refs[1:]
            own_ref, refs = refs[3], refs[:3] + refs[4:]
        w_ref, m_ref, v_ref, p_ref, g_ref, d_ref, nm_ref, nv_ref = refs
        g = p_ref[0].astype(F32)
        if own is not None:
            g = own_ref[...].astype(F32) + g
        for d in range(1, n):
            g = g + p_ref[d].astype(F32)
        nm = ADAM_B1 * m_ref[...] + (1.0 - ADAM_B1) * g
        nv = ADAM_B2 * v_ref[...] + (1.0 - ADAM_B2) * (g * g)
        m_hat = nm / (1.0 - ADAM_B1 ** ADAM_STEP)
        v_hat = nv / (1.0 - ADAM_B2 ** ADAM_STEP)
        g_ref[...] = g
        d_ref[...] = -ADAM_LR * (m_hat / (jnp.sqrt(v_hat) + ADAM_EPS) + ADAM_WD * w_ref[...])
        nm_ref[...] = nm
        nv_ref[...] = nv

    row = pl.BlockSpec((tr, c), lambda i, *_: (i, 0))
    in_specs, args = [row, row, row], [w, m, v]
    if own is not None:
        in_specs.append(pl.BlockSpec((None, tr, c), lambda i, chip_ref: (chip_ref[0], i, 0)))
        args.append(own)
    in_specs.append(pl.BlockSpec((n, tr, c), lambda i, *_: (0, i, 0)))
    args.append(parts)
    return _call(
        body, grid=(r // tr,), in_specs=in_specs, out_specs=[row] * 4, out_shape=[jax.ShapeDtypeStruct((r, c), F32)] * 4,
        semantics=("parallel",), name=name, args=args, jobs=jobs, index=chip if own is not None else None)


SMALL_ORDER = ("w_a", "w_x", "conv_w", "b_pool", "conv_b", "b_a", "b_x", "lru_lambda", "pool_scale",
               "ln1_g", "ln1_b", "ln2_g", "ln2_b", "ln3_g", "ln3_b")


def _pack_rows(a, p):
    flat = a.reshape(-1, p)
    pad = (-flat.shape[0]) % SUBLANES
    return jnp.pad(flat, ((0, pad), (0, 0))) if pad else flat


def kernel(x, mem, w_in, conv_w, conv_b, w_a, b_a, w_x, b_x, lru_lambda, w_pool, b_pool, pool_scale, w_out, ln1_g, ln1_b, w_q, w_k, w_v, w_o, ln2_g, ln2_b, w_ff1, w_ff2, ln3_g, ln3_b, loss_target, m_w_in, m_conv_w, m_conv_b, m_w_a, m_b_a, m_w_x, m_b_x, m_lru_lambda, m_w_pool, m_b_pool, m_pool_scale, m_w_out, m_ln1_g, m_ln1_b, m_w_q, m_w_k, m_w_v, m_w_o, m_ln2_g, m_ln2_b, m_w_ff1, m_w_ff2, m_ln3_g, m_ln3_b, v_w_in, v_conv_w, v_conv_b, v_w_a, v_b_a, v_w_x, v_b_x, v_lru_lambda, v_w_pool, v_b_pool, v_pool_scale, v_w_out, v_ln1_g, v_ln1_b, v_w_q, v_w_k, v_w_v, v_w_o, v_ln2_g, v_ln2_b, v_w_ff1, v_w_ff2, v_ln3_g, v_ln3_b):
    names = ("w_in", "conv_w", "conv_b", "w_a", "b_a", "w_x", "b_x", "lru_lambda", "w_pool", "b_pool", "pool_scale",
             "w_out", "ln1_g", "ln1_b", "w_q", "w_k", "w_v", "w_o", "ln2_g", "ln2_b", "w_ff1", "w_ff2", "ln3_g", "ln3_b")
    w_loc = dict(zip(names, (w_in, conv_w, conv_b, w_a, b_a, w_x, b_x, lru_lambda, w_pool, b_pool, pool_scale,
                             w_out, ln1_g, ln1_b, w_q, w_k, w_v, w_o, ln2_g, ln2_b, w_ff1, w_ff2, ln3_g, ln3_b)))
    m_loc = dict(zip(names, (m_w_in, m_conv_w, m_conv_b, m_w_a, m_b_a, m_w_x, m_b_x, m_lru_lambda, m_w_pool, m_b_pool,
                             m_pool_scale, m_w_out, m_ln1_g, m_ln1_b, m_w_q, m_w_k, m_w_v, m_w_o, m_ln2_g, m_ln2_b,
                             m_w_ff1, m_w_ff2, m_ln3_g, m_ln3_b)))
    v_loc = dict(zip(names, (v_w_in, v_conv_w, v_conv_b, v_w_a, v_b_a, v_w_x, v_b_x, v_lru_lambda, v_w_pool, v_b_pool,
                             v_pool_scale, v_w_out, v_ln1_g, v_ln1_b, v_w_q, v_w_k, v_w_v, v_w_o, v_ln2_g, v_ln2_b,
                             v_w_ff1, v_w_ff2, v_ln3_g, v_ln3_b)))
    s, d = x.shape[1], x.shape[2]
    p = conv_b.shape[1]
    cg = p // N_POOL_GROUPS
    hd = p // LRU_HEADS
    me = 4 * lax.axis_index("x") + 2 * lax.axis_index("y") + lax.axis_index("c")

    xs, mems, tgt = x[0], mem[0], loss_target[0]
    xb, memb = xs.astype(BF16), mems.astype(BF16)

    gathers = {n: _Job("gather", w_loc[n][0].astype(BF16))
               for n in ("w_in", "w_out", "w_q", "w_k", "w_v", "w_o", "w_ff2", "w_pool")}
    ff1_shard = w_ff1[0].astype(BF16)
    ff1_rows = ff1_shard.shape[0] // FF1_PIECES
    ff1_pieces = [_Job("gather", ff1_shard[i * ff1_rows:(i + 1) * ff1_rows]) for i in range(FF1_PIECES)]
    tiny = jnp.concatenate([_pack_rows(conv_w[0], p // N_DEV),
                            _pack_rows(jnp.pad(b_pool[0], ((0, 0), (0, p // N_DEV - cg // N_DEV))), p // N_DEV)], axis=0)
    gathers["tiny"] = _Job("gather", tiny)

    def gathered(n):
        full = gathers[n].out
        if n == "w_in":
            return jnp.transpose(full, (1, 0, 2)).reshape(1, full.shape[1], -1)
        return full.reshape(1, -1, full.shape[-1])

    W = {"conv_b": conv_b, "b_a": b_a.reshape(1, p), "b_x": b_x.reshape(1, p), "lru_lambda": lru_lambda,
         "pool_scale": pool_scale, "w_a": w_a[0].astype(BF16), "w_x": w_x[0].astype(BF16)}
    for n in ("ln1_g", "ln1_b", "ln2_g", "ln2_b", "ln3_g", "ln3_b"):
        W[n] = w_loc[n]

    out_g, out_d, out_m, out_v = {}, {}, {}, {}
    pairs, quads = {}, {}
    core = lax.axis_index("c").astype(jnp.int32).reshape(1)
    chip = (2 * lax.axis_index("x") + lax.axis_index("y")).astype(jnp.int32).reshape(1)

    def pair(n, partial):
        pairs[n] = _Job("pair", partial.reshape(N_DEV, -1, partial.shape[-1]))
        return pairs[n]

    def quad(n):
        quads[n] = _Job("quad", _pair_add(pairs[n].src, pairs[n].out, core, "add_" + n))
        return quads[n]

    def update(n, parts, jobs=(), own=None):
        shp = w_loc[n].shape
        rows = parts.shape[1]
        w2, m2, v2 = (a.reshape(rows, -1) for a in (w_loc[n], m_loc[n], v_loc[n]))
        res = _adamw(w2, m2, v2, parts.reshape(parts.shape[0], rows, -1), "adamw_" + n, jobs=jobs, own=own, chip=chip)
        out_g[n], out_d[n], out_m[n], out_v[n] = (r.reshape(shp) for r in res)

    _run_jobs([gathers["w_in"], gathers["tiny"], gathers["w_pool"]], "gather_first")
    W["w_pool"] = jnp.transpose(gathers["w_pool"].out, (1, 0, 2, 3)).reshape(N_POOL_GROUPS, cg, cg)
    cwb = gathers["tiny"].out
    W["conv_w"] = jnp.transpose(cwb[:, :CONV_WIDTH, :], (1, 0, 2)).reshape(CONV_WIDTH, p)
    W["b_pool"] = jnp.transpose(cwb[:, SUBLANES:SUBLANES + N_POOL_GROUPS, :cg // N_DEV], (1, 0, 2)).reshape(1, p)
    mixer_consts = (W["w_pool"], W["b_pool"], W["pool_scale"], W["conv_w"], W["conv_b"], W["w_a"], W["b_a"],
                    W["w_x"], W["b_x"], W["lru_lambda"])

    w_in_full = gathered("w_in")
    (proj,) = _mm_nn(xb, w_in_full, [F32], "fwd_proj", jobs=[gathers["w_out"], ff1_pieces[3]])
    ycat, hsave = _mixer_fwd(proj, *mixer_consts, "fwd_mixer", jobs=[gathers["w_q"], gathers["w_k"]])
    (y1,) = _mm_nn(ycat, gathered("w_out"), [F32], "fwd_out", jobs=[gathers["w_v"]])
    x1, x1b, xhat1, rstd1 = _ln_fwd(xs, y1, W["ln1_g"], W["ln1_b"], "fwd_ln1", jobs=[gathers["w_o"]])
    (q,) = _mm_nn(x1b, gathered("w_q"), [BF16], "fwd_q", jobs=[ff1_pieces[0]])
    (k,) = _mm_nn(memb, gathered("w_k"), [BF16], "fwd_k")
    (v,) = _mm_nn(memb, gathered("w_v"), [BF16], "fwd_v")
    o = _attn_fwd(q, k, v, "fwd_attn")
    (y2,) = _mm_nn(o, gathered("w_o"), [F32], "fwd_o", jobs=[ff1_pieces[1]])
    x2, x2b, xhat2, rstd2 = _ln_fwd(x1, y2, W["ln2_g"], W["ln2_b"], "fwd_ln2", jobs=[ff1_pieces[2]])
    w_ff1_full = jnp.concatenate([piece.out for piece in ff1_pieces], axis=1)

    def relu_sq(acc):
        r = jnp.maximum(acc, 0.0)
        return r, r * r

    rb, act = _mm_nn(x2b, w_ff1_full, [BF16, BF16], "fwd_ff1", epilogue=relu_sq, jobs=[gathers["w_ff2"]])
    (y3,) = _mm_nn(act, gathered("w_ff2"), [F32], "fwd_ff2")
    loss_rows, dz3, dz3b, dg3, db3 = _ln_loss(x2, y3, W["ln3_g"], W["ln3_b"], tgt, "ln3_loss")
    loss = lax.psum(loss_rows[0, 0], MESH_AXES)

    small = {"ln3_g": dg3, "ln3_b": db3}

    def add_residual(acc, e):
        return (acc + ALPHA * e,)

    dw_ff2 = _mm_tn(act, dz3b, 1, BF16, "bwd_dw_ff2")
    (dhid,) = _mm_nt(dz3b, gathered("w_ff2"), [BF16], "bwd_dact", extras=(rb,), jobs=[pair("w_ff2", dw_ff2)],
                     epilogue=lambda acc, r: (acc * (2.0 * r.astype(F32)),))
    dw_ff1 = _mm_tn(x2b, dhid, N_DEV, BF16, "bwd_dw_ff1", jobs=[quad("w_ff2")])
    (dx2,) = _mm_nt(dhid, w_ff1_full, [F32], "bwd_dx2", epilogue=add_residual, extras=(dz3,),
                    jobs=[pair("w_ff1", dw_ff1)])
    dz2, dz2b, small["ln2_g"], small["ln2_b"] = _ln_bwd(dx2, xhat2, rstd2, W["ln2_g"], "bwd_ln2")

    dw_o = _mm_tn(o, dz2b, 1, BF16, "bwd_dw_o")
    (do,) = _mm_nt(dz2b, gathered("w_o"), [BF16], "bwd_do", jobs=[pair("w_o", dw_o)])
    dq, dk, dv = _attn_bwd(q, k, v, do, "bwd_attn", jobs=[quad("w_o")])
    dw_q = _mm_tn(x1b, dq, 1, BF16, "bwd_dw_q")
    dw_k = _mm_tn(memb, dk.astype(BF16), 1, BF16, "bwd_dw_k")
    dw_v = _mm_tn(memb, dv.astype(BF16), 1, BF16, "bwd_dw_v")
    (dx1,) = _mm_nt(dq, gathered("w_q"), [F32], "bwd_dx1", epilogue=add_residual, extras=(dz2,),
                    jobs=[pair("w_q", dw_q), pair("w_k", dw_k), pair("w_v", dw_v)])
    dz1, dz1b, small["ln1_g"], small["ln1_b"] = _ln_bwd(dx1, xhat1, rstd1, W["ln1_g"], "bwd_ln1", jobs=[quad("w_q")])

    dw_out = _mm_tn(ycat, dz1b, 1, BF16, "bwd_dw_out", jobs=[quad("w_k")])
    (dycat,) = _mm_nt(dz1b, gathered("w_out"), [F32], "bwd_dycat", jobs=[quad("w_v"), pair("w_out", dw_out)])
    (dproj, dwp, small["b_pool"], small["pool_scale"], small["conv_w"], small["conv_b"], small["w_a"], small["b_a"],
     small["w_x"], small["b_x"], small["lru_lambda"]) = _mixer_bwd(
        dycat, proj, hsave, *mixer_consts, "bwd_mixer", jobs=[quad("w_ff1")])
    dw_pool = jnp.transpose(dwp.astype(BF16).reshape(N_POOL_GROUPS, N_DEV, cg // N_DEV, cg), (1, 0, 2, 3))
    pack = jnp.concatenate([_pack_rows(small[n], p) for n in SMALL_ORDER], axis=0)
    small_gather = _Job("gather", pack)
    dw_in = _mm_tn(xb, dproj, 1, BF16, "bwd_dw_in", jobs=[quad("w_out"), pair("w_pool", dw_pool), small_gather])
    dw_in = jnp.transpose(dw_in.reshape(dw_in.shape[1], N_DEV, -1), (1, 0, 2))
    (grad_x,) = _mm_nt(dproj, w_in_full, [F32], "bwd_dx", epilogue=add_residual, extras=(dz1,),
                       jobs=[pair("w_in", dw_in), quad("w_pool")])

    update("w_ff2", quads["w_ff2"].out, jobs=[quad("w_in")], own=quads["w_ff2"].src)
    for n in ("w_ff1", "w_o", "w_q", "w_k", "w_v", "w_out", "w_pool", "w_in"):
        update(n, quads[n].out, own=quads[n].src)

    total = _sum_parts(small_gather.out, "sum_small")
    row = 0
    for n in SMALL_ORDER:
        size = small[n].size
        nrows = size // p
        g_full = total[row:row + nrows].reshape(small[n].shape)
        row += nrows + (-nrows) % SUBLANES
        if n == "conv_w":
            g_loc = lax.dynamic_slice_in_dim(g_full, me * (p // N_DEV), p // N_DEV, axis=1)
        elif n == "b_pool":
            g_loc = lax.dynamic_slice_in_dim(g_full.reshape(N_POOL_GROUPS, cg), me * (cg // N_DEV), cg // N_DEV, axis=1)
        else:
            g_loc = g_full
        rows = g_loc.shape[0] if n not in ("w_a", "w_x") else LRU_HEADS * hd
        update(n, g_loc.reshape(1, rows, -1))

    order = names
    return (loss, grad_x[None], *[out_g[n] for n in order], *[out_d[n] for n in order],
            *[out_m[n] for n in order], *[out_v[n] for n in order])
```

```python
import functools

import jax
import jax.numpy as jnp
from jax import lax
from jax.experimental import pallas as pl
from jax.experimental.pallas import tpu as pltpu

F32 = jnp.float32
BF16 = jnp.bfloat16

N_DEV = 8
MESH_AXES = ("x", "y", "c")
POOL_WINDOWS = (2, 4, 8, 16)
N_POOL_GROUPS = len(POOL_WINDOWS)
POOL_HALO = 16
CONV_WIDTH = 4
CONV_HALO = 8
FF1_PIECES = 4
LRU_HEADS = 8
LRU_C = 8.0
XATTN_HEADS = 4
LN_EPS = 1e-5
ALPHA = 2.0 ** 0.25
ADAM_LR = 0.001
ADAM_B1 = 0.9
ADAM_B2 = 0.999
ADAM_EPS = 1e-08
ADAM_WD = 0.01
ADAM_STEP = 10
SUBLANES = 8
VMEM_LIMIT = 56 * 1024 * 1024

NT_DIMS = (((1,), (1,)), ((), ()))
TN_DIMS = (((0,), (0,)), ((), ()))


def _params(*sem):
    return pltpu.CompilerParams(dimension_semantics=sem, vmem_limit_bytes=VMEM_LIMIT)


def _place():
    return lax.axis_index("x"), lax.axis_index("y"), lax.axis_index("c")


def _remote(src, dst, send_sem, recv_sem, to):
    return pltpu.make_async_remote_copy(src_ref=src, dst_ref=dst, send_sem=send_sem, recv_sem=recv_sem,
                                        device_id=to, device_id_type=pl.DeviceIdType.MESH)


class _Job:
    def __init__(self, kind, src, window=None, into=None):
        self.kind, self.src, self.out, self.window, self.into = kind, src, None, window, into

    def out_shape(self):
        s = self.src.shape
        if self.window is not None:
            s = (self.window[1],) + s[1:]
        shape = {"gather": (N_DEV,) + s, "pair": (4,) + s[1:], "quad": (3,) + s[1:]}[self.kind]
        return jax.ShapeDtypeStruct(shape, self.src.dtype)

    def scratch(self):
        n = {"gather": 7, "pair": 4, "quad": 3}[self.kind]
        sems = [pltpu.SemaphoreType.DMA((n,)), pltpu.SemaphoreType.DMA((n,))]
        if self.kind == "gather":
            sems += [pltpu.SemaphoreType.DMA((2,)), pltpu.VMEM(self.src.shape, self.src.dtype)]
        return sems

    def ops(self, src, out, *scratch):
        if self.kind == "gather":
            return _gather_ops(src, out, *scratch, first_row=None if self.window is None else self.window[0])
        return {"pair": _pair_ops, "quad": _quad_ops}[self.kind](src, out, *scratch)


def _gather_ops(x_ref, out_ref, send_sems, recv_sems, local_sems, bounce, first_row=None):
    x, y, c = _place()
    me, sibling = (x, y, c), (x, y, 1 - c)
    chips = [(1 - x, y), (x, 1 - y), (1 - x, 1 - y)]

    def slot(px, py, pc):
        block = out_ref.at[4 * px + 2 * py + pc]
        return block if first_row is None else block.at[pl.ds(first_row, x_ref.shape[0])]

    def copy(k, block, to, src=None):
        return _remote(slot(*block) if src is None else src, slot(*block), send_sems.at[k], recv_sems.at[k], to)

    mine_in = pltpu.make_async_copy(x_ref, bounce, local_sems.at[0])
    mine_out = pltpu.make_async_copy(bounce, slot(*me), local_sems.at[1])
    first = [copy(0, me, sibling, src=x_ref)] + [copy(1 + j, me, (*chip, c), src=x_ref) for j, chip in enumerate(chips)]
    passed = [copy(4 + j, (*chip, c), sibling) for j, chip in enumerate(chips)]

    def start():
        mine_in.start()
        for cp in first:
            cp.start()

    def mid():
        mine_in.wait()
        mine_out.start()
        for j, chip in enumerate(chips):
            copy(1 + j, (*chip, c), me).wait_recv()
            passed[j].start()

    def finish():
        copy(0, sibling, me).wait_recv()
        for j, chip in enumerate(chips):
            copy(4 + j, (*chip, 1 - c), me).wait_recv()
        for cp in first + passed:
            cp.wait_send()
        mine_out.wait()

    return start, mid, finish


def _pair_ops(p_ref, got_ref, send_sems, recv_sems):
    x, y, c = _place()
    give = [_remote(p_ref.at[2 * k + 1 - c], got_ref.at[k], send_sems.at[k], recv_sems.at[k], (x, y, 1 - c))
            for k in range(4)]

    def start():
        for cp in give:
            cp.start()

    def finish():
        for cp in give:
            cp.wait_recv()
        for cp in give:
            cp.wait_send()

    return start, None, finish


def _quad_ops(q_ref, out_ref, send_sems, recv_sems):
    x, y, c = _place()
    copies = []
    for rel in range(1, 4):
        px = 1 - x if rel & 2 else x
        py = 1 - y if rel & 1 else y
        copies.append(_remote(q_ref.at[2 * px + py], out_ref.at[rel - 1], send_sems.at[rel - 1], recv_sems.at[rel - 1],
                              (px, py, c)))

    def start():
        for cp in copies:
            cp.start()

    def finish():
        for cp in copies:
            cp.wait_recv()
        for cp in copies:
            cp.wait_send()

    return start, None, finish


def _call(body, *, grid, in_specs, out_specs, out_shape, scratch_shapes=(), semantics, name, args, jobs=(), index=None):
    in_specs, out_specs, out_shape = list(in_specs), list(out_specs), list(out_shape)
    scratch_shapes, jobs = list(scratch_shapes), list(jobs)
    n_in, n_out, n_scr, n_job = len(in_specs), len(out_specs), len(scratch_shapes), len(jobs)
    n_idx = 0 if index is None else 1
    job_scratch = [j.scratch() for j in jobs]
    n_steps = functools.reduce(lambda a, b: a * b, grid, 1)
    early = n_steps - 1 - max(1, n_steps // 8) if n_steps >= 4 else None

    intos = [(k, j.into) for k, j in enumerate(jobs) if j.into is not None]

    def hosted(*refs):
        idx, refs = refs[:n_idx], refs[n_idx:]
        ins, jin = refs[:n_in], refs[n_in:n_in + n_job]
        o0 = n_in + n_job + len(intos)
        outs, jout = refs[o0:o0 + n_out], refs[o0 + n_out:o0 + n_out + n_job]
        s0 = o0 + n_out + n_job
        scr, jscr = refs[s0:s0 + n_scr], refs[s0 + n_scr:]
        ops, at = [], 0
        for k, j in enumerate(jobs):
            ops.append(j.ops(jin[k], jout[k], *jscr[at:at + len(job_scratch[k])]))
            at += len(job_scratch[k])
        step = functools.reduce(lambda acc, a: acc * grid[a] + pl.program_id(a), range(len(grid)), 0)
        mids = [mid for _, mid, _ in ops if mid is not None]

        @pl.when(step == 0)
        def _():
            for start, _, _ in ops:
                start()

        if mids and early is not None:
            @pl.when(step == early)
            def _():
                for mid in mids:
                    mid()

        body(*idx, *ins, *outs, *scr)

        @pl.when(step == n_steps - 1)
        def _():
            if early is None:
                for mid in mids:
                    mid()
            for _, _, finish in ops:
                finish()

    hbm = pl.BlockSpec(memory_space=pl.ANY)
    spec = pltpu.PrefetchScalarGridSpec(
        num_scalar_prefetch=n_idx, grid=grid, in_specs=in_specs + [hbm] * (n_job + len(intos)),
        out_specs=out_specs + [hbm] * n_job, scratch_shapes=scratch_shapes + [s for js in job_scratch for s in js])
    aliases = {n_idx + n_in + n_job + q: n_out + k for q, (k, _) in enumerate(intos)}
    res = pl.pallas_call(
        hosted if jobs else body, grid_spec=spec, out_shape=out_shape + [j.out_shape() for j in jobs],
        input_output_aliases=aliases,
        compiler_params=_params(*(["arbitrary"] * len(grid) if jobs else semantics)), name=name,
    )(*([] if index is None else [index]), *args, *[j.src for j in jobs], *[buf for _, buf in intos])
    for j, o in zip(jobs, res[n_out:]):
        j.out = o
    return res[:n_out]


def _to_bf16(a, name, jobs=()):
    r, c = a.shape
    tr = _tile(TILES["row"], r)

    def body(a_ref, o_ref):
        o_ref[...] = a_ref[...].astype(BF16)

    row = pl.BlockSpec((tr, c), lambda i: (i, 0))
    return _call(body, grid=(r // tr,), in_specs=[row], out_specs=[row], out_shape=[jax.ShapeDtypeStruct((r, c), BF16)],
                 semantics=("parallel",), name=name, args=(a,), jobs=jobs)[0]


TILES = dict(tm=1024, tn=1024, tk=2048, row=256, attn=512, mixer=256, adam=128, add=1024)


def _tile(pref, n):
    for t in range(min(pref, n), 0, -1):
        if n % t == 0 and (t % SUBLANES == 0 or t == n):
            return t
    return n


def _accumulate(acc, step, n_steps, product, write):
    if n_steps == 1:
        write(product)
        return

    @pl.when(step == 0)
    def _():
        acc[...] = product

    @pl.when(jnp.logical_and(step > 0, step < n_steps - 1))
    def _():
        acc[...] += product

    @pl.when(step == n_steps - 1)
    def _():
        write(acc[...] + product)


def _acc_scratch(n_steps, tm, tn):
    return [] if n_steps == 1 else [pltpu.VMEM((tm, tn), F32)]


def _mm_nn(a, b3, out_dtypes, name, *, tm=None, tn=None, tk=None, epilogue=None, extras=(), jobs=()):
    m, k = a.shape
    g, k2, ns = b3.shape
    assert k == k2
    n = g * ns
    tm, tn, tk = _tile(tm or TILES["tm"], m), _tile(tn or TILES["tn"], ns), _tile(tk or TILES["tk"], k)
    nb, nk = ns // tn, k // tk
    n_ex, n_out = len(extras), len(out_dtypes)

    def body(*refs):
        a_ref, b_ref = refs[:2]
        ex = refs[2:2 + n_ex]
        outs = refs[2 + n_ex:2 + n_ex + n_out]
        acc = refs[-1] if nk > 1 else None

        def write(r):
            res = epilogue(r, *[e[...] for e in ex]) if epilogue is not None else (r,)
            for o, v in zip(outs, res):
                o[...] = v.astype(o.dtype)

        _accumulate(acc, pl.program_id(2), nk, jnp.dot(a_ref[...], b_ref[...], preferred_element_type=F32), write)

    tile_out = pl.BlockSpec((tm, tn), lambda i, j, kk: (i, j))
    return _call(
        body, grid=(m // tm, n // tn, nk),
        in_specs=[pl.BlockSpec((tm, tk), lambda i, j, kk: (i, kk)),
                  pl.BlockSpec((None, tk, tn), lambda i, j, kk: (j // nb, kk, j % nb))] + [tile_out] * n_ex,
        out_specs=[tile_out] * n_out,
        out_shape=[jax.ShapeDtypeStruct((m, n), d) for d in out_dtypes],
        scratch_shapes=_acc_scratch(nk, tm, tn),
        semantics=("parallel", "parallel", "arbitrary"), name=name, args=(a, b3, *extras), jobs=jobs)


def _mm_nt(a, b3, out_dtypes, name, *, tm=None, tn=None, tk=None, epilogue=None, extras=(), jobs=()):
    m, n = a.shape
    g, k, ns = b3.shape
    assert n == g * ns
    tm, tn, tk = _tile(tm or TILES["tm"], m), _tile(tn or TILES["tn"], k), _tile(tk or TILES["tk"], ns)
    nb, nc = ns // tk, n // tk
    n_ex, n_out = len(extras), len(out_dtypes)

    def body(*refs):
        a_ref, b_ref = refs[:2]
        ex = refs[2:2 + n_ex]
        outs = refs[2 + n_ex:2 + n_ex + n_out]
        acc = refs[-1] if nc > 1 else None

        def write(r):
            res = epilogue(r, *[e[...] for e in ex]) if epilogue is not None else (r,)
            for o, v in zip(outs, res):
                o[...] = v.astype(o.dtype)

        _accumulate(acc, pl.program_id(2), nc,
                    lax.dot_general(a_ref[...], b_ref[...], NT_DIMS, preferred_element_type=F32), write)

    tile_out = pl.BlockSpec((tm, tn), lambda i, j, cc: (i, j))
    return _call(
        body, grid=(m // tm, k // tn, nc),
        in_specs=[pl.BlockSpec((tm, tk), lambda i, j, cc: (i, cc)),
                  pl.BlockSpec((None, tn, tk), lambda i, j, cc: (cc // nb, j, cc % nb))] + [tile_out] * n_ex,
        out_specs=[tile_out] * n_out,
        out_shape=[jax.ShapeDtypeStruct((m, k), d) for d in out_dtypes],
        scratch_shapes=_acc_scratch(nc, tm, tn),
        semantics=("parallel", "parallel", "arbitrary"), name=name, args=(a, b3, *extras), jobs=jobs)


def _mm_tn(a, b, g, out_dtype, name, *, tm=None, tn=None, tk=None, jobs=()):
    s, m = a.shape
    s2, n = b.shape
    assert s == s2 and n % g == 0
    ns = n // g
    tm, tn, tk = _tile(tm or TILES["tm"], m), _tile(tn or TILES["tn"], ns), _tile(tk or TILES["tk"], s)
    nb, nc = ns // tn, s // tk

    def body(a_ref, b_ref, o_ref, *scratch):
        def write(r):
            o_ref[...] = r.astype(o_ref.dtype)

        _accumulate(scratch[0] if nc > 1 else None, pl.program_id(2), nc,
                    lax.dot_general(a_ref[...], b_ref[...], TN_DIMS, preferred_element_type=F32), write)

    return _call(
        body, grid=(m // tm, n // tn, nc),
        in_specs=[pl.BlockSpec((tk, tm), lambda i, j, cc: (cc, i)),
                  pl.BlockSpec((tk, tn), lambda i, j, cc: (cc, j))],
        out_specs=[pl.BlockSpec((None, tm, tn), lambda i, j, cc: (j // nb, i, j % nb))],
        out_shape=[jax.ShapeDtypeStruct((g, m, ns), out_dtype)],
        scratch_shapes=_acc_scratch(nc, tm, tn),
        semantics=("parallel", "parallel", "arbitrary"), name=name, args=(a, b), jobs=jobs)[0]


def _ln_stats(z):
    mu = jnp.mean(z, axis=-1, keepdims=True)
    zc = z - mu
    var = jnp.mean(zc * zc, axis=-1, keepdims=True)
    rstd = lax.rsqrt(var + LN_EPS)
    return zc * rstd, rstd


def _ln_grad(dout, xhat, rstd, gain):
    dxhat = dout * gain
    m1 = jnp.mean(dxhat, axis=-1, keepdims=True)
    m2 = jnp.mean(dxhat * xhat, axis=-1, keepdims=True)
    return rstd * (dxhat - m1 - xhat * m2)


def _ln_fwd(xres, y, gain, bias, name, jobs=()):
    s, d = xres.shape
    tr = _tile(TILES["row"], s)

    def body(x_ref, y_ref, g_ref, b_ref, xn_ref, xnb_ref, xhat_ref, rstd_ref):
        xhat, rstd = _ln_stats(ALPHA * x_ref[...] + y_ref[...])
        out = xhat * g_ref[...] + b_ref[...]
        xn_ref[...] = out
        xnb_ref[...] = out.astype(BF16)
        xhat_ref[...] = xhat
        rstd_ref[...] = rstd

    row = pl.BlockSpec((tr, d), lambda i: (i, 0))
    vec = pl.BlockSpec((1, d), lambda i: (0, 0))
    return _call(
        body, grid=(s // tr,), in_specs=[row, row, vec, vec],
        out_specs=[row, row, row, pl.BlockSpec((tr, 1), lambda i: (i, 0))],
        out_shape=[jax.ShapeDtypeStruct((s, d), F32), jax.ShapeDtypeStruct((s, d), BF16),
                   jax.ShapeDtypeStruct((s, d), F32), jax.ShapeDtypeStruct((s, 1), F32)],
        semantics=("parallel",), name=name, args=(xres, y, gain, bias), jobs=jobs)


def _ln_bwd(dout, xhat, rstd, gain, name, jobs=()):
    s, d = dout.shape
    tr = _tile(TILES["row"], s)

    def body(d_ref, xhat_ref, rstd_ref, g_ref, dz_ref, dzb_ref, dg_ref, db_ref):
        @pl.when(pl.program_id(0) == 0)
        def _():
            dg_ref[...] = jnp.zeros_like(dg_ref)
            db_ref[...] = jnp.zeros_like(db_ref)

        dout_t, xhat_t = d_ref[...], xhat_ref[...]
        dz = _ln_grad(dout_t, xhat_t, rstd_ref[...], g_ref[...])
        dz_ref[...] = dz
        dzb_ref[...] = dz.astype(BF16)
        dg_ref[...] += jnp.sum(dout_t * xhat_t, axis=0, keepdims=True)
        db_ref[...] += jnp.sum(dout_t, axis=0, keepdims=True)

    row = pl.BlockSpec((tr, d), lambda i: (i, 0))
    vec = pl.BlockSpec((1, d), lambda i: (0, 0))
    return _call(
        body, grid=(s // tr,), in_specs=[row, row, pl.BlockSpec((tr, 1), lambda i: (i, 0)), vec],
        out_specs=[row, row, vec, vec],
        out_shape=[jax.ShapeDtypeStruct((s, d), F32), jax.ShapeDtypeStruct((s, d), BF16),
                   jax.ShapeDtypeStruct((1, d), F32), jax.ShapeDtypeStruct((1, d), F32)],
        semantics=("arbitrary",), name=name, args=(dout, xhat, rstd, gain), jobs=jobs)


def _ln_loss(xres, y, gain, bias, target, name, jobs=()):
    s, d = xres.shape
    tr = _tile(TILES["row"], s)

    def body(x_ref, y_ref, g_ref, b_ref, t_ref, loss_ref, dz_ref, dzb_ref, dg_ref, db_ref):
        @pl.when(pl.program_id(0) == 0)
        def _():
            loss_ref[...] = jnp.zeros_like(loss_ref)
            dg_ref[...] = jnp.zeros_like(dg_ref)
            db_ref[...] = jnp.zeros_like(db_ref)

        xhat, rstd = _ln_stats(ALPHA * x_ref[...] + y_ref[...])
        diff = xhat * g_ref[...] + b_ref[...] - t_ref[...]
        per_row = jnp.mean(diff * diff, axis=-1, keepdims=True)
        loss_ref[...] += 0.5 * jnp.sum(per_row, axis=0, keepdims=True)
        dout = diff * (1.0 / d)
        dz = _ln_grad(dout, xhat, rstd, g_ref[...])
        dz_ref[...] = dz
        dzb_ref[...] = dz.astype(BF16)
        dg_ref[...] += jnp.sum(dout * xhat, axis=0, keepdims=True)
        db_ref[...] += jnp.sum(dout, axis=0, keepdims=True)

    row = pl.BlockSpec((tr, d), lambda i: (i, 0))
    vec = pl.BlockSpec((1, d), lambda i: (0, 0))
    return _call(
        body, grid=(s // tr,), in_specs=[row, row, vec, vec, row],
        out_specs=[pl.BlockSpec((1, 128), lambda i: (0, 0)), row, row, vec, vec],
        out_shape=[jax.ShapeDtypeStruct((1, 128), F32), jax.ShapeDtypeStruct((s, d), F32),
                   jax.ShapeDtypeStruct((s, d), BF16), jax.ShapeDtypeStruct((1, d), F32),
                   jax.ShapeDtypeStruct((1, d), F32)],
        semantics=("arbitrary",), name=name, args=(xres, y, gain, bias, target), jobs=jobs)


def _softmax_rows(s):
    e = jnp.exp(s - jnp.max(s, axis=-1, keepdims=True))
    return e / jnp.sum(e, axis=-1, keepdims=True)


def _attn_fwd(q, k, v, name, jobs=()):
    s, d = q.shape
    m = k.shape[0]
    hd = d // XATTN_HEADS
    ts = _tile(TILES["attn"], s)
    scale = hd ** -0.5

    def body(q_ref, k_ref, v_ref, o_ref):
        for h in range(XATTN_HEADS):
            hs = slice(h * hd, (h + 1) * hd)
            sc = lax.dot_general(q_ref[:, hs], k_ref[:, hs], NT_DIMS, preferred_element_type=F32) * scale
            p = _softmax_rows(sc).astype(BF16)
            o_ref[:, hs] = jnp.dot(p, v_ref[:, hs], preferred_element_type=F32).astype(BF16)

    row = pl.BlockSpec((ts, d), lambda i: (i, 0))
    memb = pl.BlockSpec((m, d), lambda i: (0, 0))
    return _call(
        body, grid=(s // ts,), in_specs=[row, memb, memb], out_specs=[row],
        out_shape=[jax.ShapeDtypeStruct((s, d), BF16)],
        semantics=("parallel",), name=name, args=(q, k, v), jobs=jobs)[0]


def _attn_bwd(q, k, v, do, name, jobs=()):
    s, d = q.shape
    m = k.shape[0]
    hd = d // XATTN_HEADS
    ts = _tile(TILES["attn"], s)
    scale = hd ** -0.5

    def body(q_ref, k_ref, v_ref, do_ref, dq_ref, dk_ref, dv_ref):
        @pl.when(pl.program_id(0) == 0)
        def _():
            dk_ref[...] = jnp.zeros_like(dk_ref)
            dv_ref[...] = jnp.zeros_like(dv_ref)

        for h in range(XATTN_HEADS):
            hs = slice(h * hd, (h + 1) * hd)
            qh, kh, vh, doh = q_ref[:, hs], k_ref[:, hs], v_ref[:, hs], do_ref[:, hs]
            sc = lax.dot_general(qh, kh, NT_DIMS, preferred_element_type=F32) * scale
            p = _softmax_rows(sc)
            pb = p.astype(BF16)
            dp = lax.dot_general(doh, vh, NT_DIMS, preferred_element_type=F32)
            ds = (p * (dp - jnp.sum(dp * p, axis=-1, keepdims=True)) * scale).astype(BF16)
            dq_ref[:, hs] = jnp.dot(ds, kh, preferred_element_type=F32).astype(BF16)
            dk_ref[:, hs] += lax.dot_general(ds, qh, TN_DIMS, preferred_element_type=F32)
            dv_ref[:, hs] += lax.dot_general(pb, doh, TN_DIMS, preferred_element_type=F32)

    row = pl.BlockSpec((ts, d), lambda i: (i, 0))
    memb = pl.BlockSpec((m, d), lambda i: (0, 0))
    return _call(
        body, grid=(s // ts,), in_specs=[row, memb, memb, row], out_specs=[row, memb, memb],
        out_shape=[jax.ShapeDtypeStruct((s, d), BF16), jax.ShapeDtypeStruct((m, d), F32),
                   jax.ShapeDtypeStruct((m, d), F32)],
        semantics=("arbitrary",), name=name, args=(q, k, v, do), jobs=jobs)


def _sigmoid(x):
    return 1.0 / (1.0 + jnp.exp(-x))


def _log1p(x):
    u = 1.0 + x
    return jnp.where(u == 1.0, x, jnp.log(u) * (x / jnp.where(u == 1.0, 1.0, u - 1.0)))


def _softplus(x):
    return jnp.maximum(x, 0.0) + _log1p(jnp.exp(-jnp.abs(x)))


def _expm1(x):
    series = x * (1.0 + x * 0.5 * (1.0 + x * (1.0 / 3.0) * (1.0 + x * 0.25 * (1.0 + x * 0.2 * (1.0 + x * (1.0 / 6.0))))))
    return jnp.where(jnp.abs(x) < 0.1, series, jnp.exp(x) - 1.0)


GELU_K = 0.7978845608028654
GELU_C = 0.044715


def _gelu(x):
    return 0.5 * x * (1.0 + jnp.tanh(GELU_K * (x + GELU_C * (x * x * x))))


def _gelu_grad(x):
    th = jnp.tanh(GELU_K * (x + GELU_C * (x * x * x)))
    return 0.5 * (1.0 + th) + 0.5 * x * (1.0 - th * th) * GELU_K * (1.0 + 3.0 * GELU_C * x * x)


def _window_sum(ext_ref, first, rows, cols, w, step):
    acc = ext_ref[first:first + rows, cols]
    for kk in range(1, w):
        acc = acc + ext_ref[first + step * kk:first + step * kk + rows, cols]
    return acc


def _lru_gates(c_s, wa_ref, ba_ref, wx_ref, bx_ref, lam_ref, t_idx, hd, r_s, i_s, a_s, mult_s):
    sp = _softplus(-lam_ref[...])
    for h in range(LRU_HEADS):
        hs = slice(h * hd, (h + 1) * hd)
        chb = c_s[:, hs].astype(BF16)
        r = _sigmoid(jnp.dot(chb, wa_ref[h], preferred_element_type=F32) + ba_ref[:, hs])
        ig = _sigmoid(jnp.dot(chb, wx_ref[h], preferred_element_type=F32) + bx_ref[:, hs])
        log_a = -LRU_C * r * sp[:, hs]
        mult = jnp.sqrt(-_expm1(2.0 * log_a))
        r_s[:, hs] = r
        i_s[:, hs] = ig
        a_s[:, hs] = jnp.exp(log_a)
        mult_s[:, hs] = jnp.where(t_idx == 0, 1.0, mult)


def _conv(ext_ref, cw_ref, cb_ref, rows):
    acc = cb_ref[...] + cw_ref[0:1, :] * ext_ref[CONV_HALO - 3:CONV_HALO - 3 + rows, :]
    for kk in range(1, CONV_WIDTH):
        off = CONV_HALO - (CONV_WIDTH - 1) + kk
        acc = acc + cw_ref[kk:kk + 1, :] * ext_ref[off:off + rows, :]
    return acc


def _mixer_fwd(proj, wp, bp, ps, cw, cb, wa, ba, wx, bx, lam, name, jobs=()):
    s, p3 = proj.shape
    p = p3 // 3
    cg, hd = p // N_POOL_GROUPS, p // LRU_HEADS
    t = _tile(TILES["mixer"], s)

    def body(up_ref, ul_ref, ug_ref, wp_ref, bp_ref, ps_ref, cw_ref, cb_ref, wa_ref, ba_ref, wx_ref, bx_ref,
             lam_ref, ycat_ref, h_ref, extp, extl, hc, c_s, r_s, i_s, a_s, b_s):
        i = pl.program_id(0)

        @pl.when(i == 0)
        def _():
            extp[0:POOL_HALO, :] = jnp.zeros((POOL_HALO, p), F32)
            extl[0:CONV_HALO, :] = jnp.zeros((CONV_HALO, p), F32)
            hc[...] = jnp.zeros_like(hc)

        t_idx = i * t + lax.broadcasted_iota(jnp.int32, (t, 1), 0)

        extp[POOL_HALO:POOL_HALO + t, :] = up_ref[...]
        for g, w in enumerate(POOL_WINDOWS):
            cs = slice(g * cg, (g + 1) * cg)
            cnt = jnp.minimum(t_idx + 1, w).astype(F32)
            mixed = _window_sum(extp, POOL_HALO, t, cs, w, -1) / cnt - up_ref[:, cs]
            pre = jnp.dot(mixed.astype(BF16), wp_ref[g], preferred_element_type=F32) + bp_ref[:, cs]
            ycat_ref[:, cs] = (pre * ps_ref[:, cs]).astype(BF16)
        extp[0:POOL_HALO, :] = extp[t:t + POOL_HALO, :]

        extl[CONV_HALO:CONV_HALO + t, :] = ul_ref[...]
        c_s[...] = _conv(extl, cw_ref, cb_ref, t)
        extl[0:CONV_HALO, :] = extl[t:t + CONV_HALO, :]
        _lru_gates(c_s, wa_ref, ba_ref, wx_ref, bx_ref, lam_ref, t_idx, hd, r_s, i_s, a_s, b_s)
        b_s[...] = b_s[...] * (i_s[...] * c_s[...])

        rows = lax.broadcasted_iota(jnp.int32, (SUBLANES, p), 0)

        def block(bi, h):
            r0 = pl.multiple_of(bi * SUBLANES, SUBLANES)
            at = a_s[pl.ds(r0, SUBLANES), :]
            bt = b_s[pl.ds(r0, SUBLANES), :]
            out = jnp.zeros((SUBLANES, p), F32)
            for j in range(SUBLANES):
                h = at[j:j + 1, :] * h + bt[j:j + 1, :]
                out = jnp.where(rows == j, h, out)
            h_ref[pl.ds(r0, SUBLANES), :] = out
            return h

        hc[0:1, :] = lax.fori_loop(0, t // SUBLANES, block, hc[0:1, :])
        ycat_ref[:, p:2 * p] = (h_ref[...] * _gelu(ug_ref[...])).astype(BF16)

    def col(j):
        return pl.BlockSpec((t, p), lambda i: (i, j))

    def whole(a):
        nd = a.ndim
        return pl.BlockSpec(a.shape, lambda i: (0,) * nd)

    consts = (wp, bp, ps, cw, cb, wa, ba, wx, bx, lam)
    tile = pltpu.VMEM((t, p), F32)
    return _call(
        body, grid=(s // t,), in_specs=[col(0), col(1), col(2)] + [whole(a) for a in consts],
        out_specs=[pl.BlockSpec((t, 2 * p), lambda i: (i, 0)), pl.BlockSpec((t, p), lambda i: (i, 0))],
        out_shape=[jax.ShapeDtypeStruct((s, 2 * p), BF16), jax.ShapeDtypeStruct((s, p), F32)],
        scratch_shapes=[pltpu.VMEM((t + POOL_HALO, p), F32), pltpu.VMEM((t + CONV_HALO, p), F32),
                        pltpu.VMEM((SUBLANES, p), F32), tile, tile, tile, tile, tile],
        semantics=("arbitrary",), name=name, args=(proj, proj, proj, *consts), jobs=jobs)


def _mixer_bwd(dycat, proj, hsave, wp, bp, ps, cw, cb, wa, ba, wx, bx, lam, name, jobs=()):
    s, p3 = proj.shape
    p = p3 // 3
    cg, hd = p // N_POOL_GROUPS, p // LRU_HEADS
    t = _tile(TILES["mixer"], s)
    nt = s // t

    def body(dyp_ref, dyl_ref, up_ref, ul_ref, ug_ref, upp_ref, ulp_ref, h_ref, hp_ref,
             wp_ref, bp_ref, ps_ref, cw_ref, cb_ref, wa_ref, ba_ref, wx_ref, bx_ref, lam_ref,
             dproj_ref, dwp_ref, dbp_ref, dps_ref, dcw_ref, dcb_ref, dwa_ref, dba_ref, dwx_ref, dbx_ref, dlam_ref,
             extp, extg, extl, extdc, exth, ghc, c_s, r_s, i_s, a_s, mult_s, gh_s):
        i = pl.program_id(0)
        ib = nt - 1 - i

        @pl.when(i == 0)
        def _():
            for ref in (dwp_ref, dbp_ref, dps_ref, dcw_ref, dcb_ref, dwa_ref, dba_ref, dwx_ref, dbx_ref, dlam_ref):
                ref[...] = jnp.zeros_like(ref)
            extg[t:t + POOL_HALO, :] = jnp.zeros((POOL_HALO, p), F32)
            extdc[t:t + CONV_HALO, :] = jnp.zeros((CONV_HALO, p), F32)
            ghc[...] = jnp.zeros_like(ghc)

        t_idx = ib * t + lax.broadcasted_iota(jnp.int32, (t, 1), 0)
        seq_start = ib == 0

        extl[0:CONV_HALO, :] = jnp.where(seq_start, 0.0, ulp_ref[...])
        extl[CONV_HALO:CONV_HALO + t, :] = ul_ref[...]
        c_s[...] = _conv(extl, cw_ref, cb_ref, t)
        _lru_gates(c_s, wa_ref, ba_ref, wx_ref, bx_ref, lam_ref, t_idx, hd, r_s, i_s, a_s, mult_s)
        exth[0:SUBLANES, :] = jnp.where(seq_start, 0.0, hp_ref[...])
        exth[SUBLANES:SUBLANES + t, :] = h_ref[...]

        ug = ug_ref[...]
        dyl = dyl_ref[...]
        dproj_ref[:, 2 * p:3 * p] = (dyl * h_ref[...] * _gelu_grad(ug)).astype(BF16)
        gh_s[...] = dyl * _gelu(ug)

        rows = lax.broadcasted_iota(jnp.int32, (SUBLANES, p), 0)
        nblk = t // SUBLANES

        def block(bi, carry):
            r0 = pl.multiple_of((nblk - 1 - bi) * SUBLANES, SUBLANES)
            at = a_s[pl.ds(r0, SUBLANES), :]
            dt = gh_s[pl.ds(r0, SUBLANES), :]
            out = jnp.zeros((SUBLANES, p), F32)
            for j in range(SUBLANES - 1, -1, -1):
                gh = dt[j:j + 1, :] + carry
                out = jnp.where(rows == j, gh, out)
                carry = at[j:j + 1, :] * gh
            gh_s[pl.ds(r0, SUBLANES), :] = out
            return carry

        ghc[0:1, :] = lax.fori_loop(0, nblk, block, ghc[0:1, :])

        sp = _softplus(-lam_ref[...])
        dsp_dlam = -_sigmoid(-lam_ref[...])
        for h in range(LRU_HEADS):
            hs = slice(h * hd, (h + 1) * hd)
            gh, a, mult, r, ig, c = gh_s[:, hs], a_s[:, hs], mult_s[:, hs], r_s[:, hs], i_s[:, hs], c_s[:, hs]
            hprev = exth[SUBLANES - 1:SUBLANES - 1 + t, hs]
            dmult = gh * (ig * c)
            dlog_a = a * gh * hprev + jnp.where(t_idx == 0, 0.0, -dmult * a * a / mult)
            dlam_ref[:, hs] += jnp.sum(dlog_a * r, axis=0, keepdims=True) * (-LRU_C) * dsp_dlam[:, hs]
            dpa = dlog_a * (-LRU_C * sp[:, hs]) * r * (1.0 - r)
            dpx = gh * mult * c * ig * (1.0 - ig)
            dpab, dpxb, chb = dpa.astype(BF16), dpx.astype(BF16), c.astype(BF16)
            dwa_ref[h] += lax.dot_general(chb, dpab, TN_DIMS, preferred_element_type=F32)
            dwx_ref[h] += lax.dot_general(chb, dpxb, TN_DIMS, preferred_element_type=F32)
            dba_ref[:, hs] += jnp.sum(dpa, axis=0, keepdims=True)
            dbx_ref[:, hs] += jnp.sum(dpx, axis=0, keepdims=True)
            dc = (gh * mult * ig
                  + lax.dot_general(dpab, wa_ref[h], NT_DIMS, preferred_element_type=F32)
                  + lax.dot_general(dpxb, wx_ref[h], NT_DIMS, preferred_element_type=F32))
            extdc[0:t, hs] = dc
            dcb_ref[:, hs] += jnp.sum(dc, axis=0, keepdims=True)
            for kk in range(CONV_WIDTH):
                off = CONV_HALO - (CONV_WIDTH - 1) + kk
                dcw_ref[kk:kk + 1, hs] += jnp.sum(dc * extl[off:off + t, hs], axis=0, keepdims=True)
        du_lru = cw_ref[0:1, :] * extdc[CONV_WIDTH - 1:CONV_WIDTH - 1 + t, :]
        for kk in range(1, CONV_WIDTH):
            off = CONV_WIDTH - 1 - kk
            du_lru = du_lru + cw_ref[kk:kk + 1, :] * extdc[off:off + t, :]
        dproj_ref[:, p:2 * p] = du_lru.astype(BF16)
        extdc[t:t + CONV_HALO, :] = extdc[0:CONV_HALO, :]

        extp[0:POOL_HALO, :] = jnp.where(seq_start, 0.0, upp_ref[...])
        extp[POOL_HALO:POOL_HALO + t, :] = up_ref[...]
        for g, w in enumerate(POOL_WINDOWS):
            cs = slice(g * cg, (g + 1) * cg)
            cnt = jnp.minimum(t_idx + 1, w).astype(F32)
            mixed = (_window_sum(extp, POOL_HALO, t, cs, w, -1) / cnt - up_ref[:, cs]).astype(BF16)
            pre = jnp.dot(mixed, wp_ref[g], preferred_element_type=F32) + bp_ref[:, cs]
            dyp = dyp_ref[:, cs]
            dps_ref[:, cs] += jnp.sum(dyp * pre, axis=0, keepdims=True)
            dpre = dyp * ps_ref[:, cs]
            dpreb = dpre.astype(BF16)
            dbp_ref[:, cs] += jnp.sum(dpre, axis=0, keepdims=True)
            dwp_ref[g] += lax.dot_general(mixed, dpreb, TN_DIMS, preferred_element_type=F32)
            dmixed = lax.dot_general(dpreb, wp_ref[g], NT_DIMS, preferred_element_type=F32)
            extg[0:t, cs] = dmixed / cnt
            dproj_ref[:, cs] = (_window_sum(extg, 0, t, cs, w, 1) - dmixed).astype(BF16)
        extg[t:t + POOL_HALO, :] = extg[0:POOL_HALO, :]

    def col(j):
        return pl.BlockSpec((t, p), lambda i: (nt - 1 - i, j))

    def prev(rows, j):
        per = t // rows
        return pl.BlockSpec((rows, p), lambda i: (jnp.maximum((nt - 1 - i) * per - 1, 0), j))

    def whole(a):
        nd = a.ndim
        return pl.BlockSpec(a.shape, lambda i: (0,) * nd)

    consts = (wp, bp, ps, cw, cb, wa, ba, wx, bx, lam)
    grads = (wp, bp, ps, cw, cb, wa, ba, wx, bx, lam)
    tile = pltpu.VMEM((t, p), F32)
    return _call(
        body, grid=(nt,),
        in_specs=[col(0), col(1), col(0), col(1), col(2), prev(POOL_HALO, 0), prev(CONV_HALO, 1), col(0),
                  prev(SUBLANES, 0)] + [whole(a) for a in consts],
        out_specs=[pl.BlockSpec((t, 3 * p), lambda i: (nt - 1 - i, 0))] + [whole(a) for a in grads],
        out_shape=[jax.ShapeDtypeStruct((s, 3 * p), BF16)] + [jax.ShapeDtypeStruct(a.shape, F32) for a in grads],
        scratch_shapes=[pltpu.VMEM((t + POOL_HALO, p), F32), pltpu.VMEM((t + POOL_HALO, p), F32),
                        pltpu.VMEM((t + CONV_HALO, p), F32), pltpu.VMEM((t + CONV_HALO, p), F32),
                        pltpu.VMEM((t + SUBLANES, p), F32), pltpu.VMEM((SUBLANES, p), F32),
                        tile, tile, tile, tile, tile, tile],
        semantics=("arbitrary",), name=name,
        args=(dycat, dycat, proj, proj, proj, proj, proj, hsave, hsave, *consts), jobs=jobs)


def _pair_add(parts, got, core, name):
    n, r, c = got.shape
    tr = _tile(TILES["add"], r)

    def body(core_ref, a_ref, b_ref, o_ref):
        del core_ref
        o_ref[...] = (a_ref[...].astype(F32) + b_ref[...].astype(F32)).astype(o_ref.dtype)

    blk = pl.BlockSpec((None, tr, c), lambda k, i, core_ref: (k, i, 0))
    mine = pl.BlockSpec((None, tr, c), lambda k, i, core_ref: (2 * k + core_ref[0], i, 0))
    return _call(body, grid=(n, r // tr), in_specs=[mine, blk], out_specs=[blk],
                 out_shape=[jax.ShapeDtypeStruct(got.shape, got.dtype)], semantics=("parallel", "parallel"),
                 name=name, args=(parts, got), index=core)[0]


def _sum_parts(parts, name):
    n, r, c = parts.shape
    tr = _tile(TILES["adam"], r)

    def body(p_ref, o_ref):
        acc = p_ref[0].astype(F32)
        for d in range(1, n):
            acc = acc + p_ref[d].astype(F32)
        o_ref[...] = acc

    return _call(
        body, grid=(r // tr,), in_specs=[pl.BlockSpec((n, tr, c), lambda i: (0, i, 0))],
        out_specs=[pl.BlockSpec((tr, c), lambda i: (i, 0))], out_shape=[jax.ShapeDtypeStruct((r, c), F32)],
        semantics=("parallel",), name=name, args=(parts,))[0]


def _adamw(w, m, v, parts, name, jobs=(), own=None, chip=None):
    r, c = w.shape
    n = parts.shape[0]
    tr = _tile(TILES["adam"], r)

    def body(*refs):
        if own is not None:
            refs = refs[1:]
            own_ref, refs = refs[3], refs[:3] + refs[4:]
        w_ref, m_ref, v_ref, p_ref, g_ref, d_ref, nm_ref, nv_ref = refs
        g = p_ref[0].astype(F32)
        if own is not None:
            g = own_ref[...].astype(F32) + g
        for d in range(1, n):
            g = g + p_ref[d].astype(F32)
        nm = ADAM_B1 * m_ref[...] + (1.0 - ADAM_B1) * g
        nv = ADAM_B2 * v_ref[...] + (1.0 - ADAM_B2) * (g * g)
        m_hat = nm / (1.0 - ADAM_B1 ** ADAM_STEP)
        v_hat = nv / (1.0 - ADAM_B2 ** ADAM_STEP)
        g_ref[...] = g
        d_ref[...] = -ADAM_LR * (m_hat / (jnp.sqrt(v_hat) + ADAM_EPS) + ADAM_WD * w_ref[...])
        nm_ref[...] = nm
        nv_ref[...] = nv

    row = pl.BlockSpec((tr, c), lambda i, *_: (i, 0))
    in_specs, args = [row, row, row], [w, m, v]
    if own is not None:
        in_specs.append(pl.BlockSpec((None, tr, c), lambda i, chip_ref: (chip_ref[0], i, 0)))
        args.append(own)
    in_specs.append(pl.BlockSpec((n, tr, c), lambda i, *_: (0, i, 0)))
    args.append(parts)
    return _call(
        body, grid=(r // tr,), in_specs=in_specs, out_specs=[row] * 4, out_shape=[jax.ShapeDtypeStruct((r, c), F32)] * 4,
        semantics=("parallel",), name=name, args=args, jobs=jobs, index=chip if own is not None else None)


SMALL_ORDER = ("w_a", "w_x", "conv_w", "b_pool", "conv_b", "b_a", "b_x", "lru_lambda", "pool_scale",
               "ln1_g", "ln1_b", "ln2_g", "ln2_b", "ln3_g", "ln3_b")


def _pack_rows(a, p):
    flat = a.reshape(-1, p)
    pad = (-flat.shape[0]) % SUBLANES
    return jnp.pad(flat, ((0, pad), (0, 0))) if pad else flat


def kernel(x, mem, w_in, conv_w, conv_b, w_a, b_a, w_x, b_x, lru_lambda, w_pool, b_pool, pool_scale, w_out, ln1_g, ln1_b, w_q, w_k, w_v, w_o, ln2_g, ln2_b, w_ff1, w_ff2, ln3_g, ln3_b, loss_target, m_w_in, m_conv_w, m_conv_b, m_w_a, m_b_a, m_w_x, m_b_x, m_lru_lambda, m_w_pool, m_b_pool, m_pool_scale, m_w_out, m_ln1_g, m_ln1_b, m_w_q, m_w_k, m_w_v, m_w_o, m_ln2_g, m_ln2_b, m_w_ff1, m_w_ff2, m_ln3_g, m_ln3_b, v_w_in, v_conv_w, v_conv_b, v_w_a, v_b_a, v_w_x, v_b_x, v_lru_lambda, v_w_pool, v_b_pool, v_pool_scale, v_w_out, v_ln1_g, v_ln1_b, v_w_q, v_w_k, v_w_v, v_w_o, v_ln2_g, v_ln2_b, v_w_ff1, v_w_ff2, v_ln3_g, v_ln3_b):
    names = ("w_in", "conv_w", "conv_b", "w_a", "b_a", "w_x", "b_x", "lru_lambda", "w_pool", "b_pool", "pool_scale",
             "w_out", "ln1_g", "ln1_b", "w_q", "w_k", "w_v", "w_o", "ln2_g", "ln2_b", "w_ff1", "w_ff2", "ln3_g", "ln3_b")
    w_loc = dict(zip(names, (w_in, conv_w, conv_b, w_a, b_a, w_x, b_x, lru_lambda, w_pool, b_pool, pool_scale,
                             w_out, ln1_g, ln1_b, w_q, w_k, w_v, w_o, ln2_g, ln2_b, w_ff1, w_ff2, ln3_g, ln3_b)))
    m_loc = dict(zip(names, (m_w_in, m_conv_w, m_conv_b, m_w_a, m_b_a, m_w_x, m_b_x, m_lru_lambda, m_w_pool, m_b_pool,
                             m_pool_scale, m_w_out, m_ln1_g, m_ln1_b, m_w_q, m_w_k, m_w_v, m_w_o, m_ln2_g, m_ln2_b,
                             m_w_ff1, m_w_ff2, m_ln3_g, m_ln3_b)))
    v_loc = dict(zip(names, (v_w_in, v_conv_w, v_conv_b, v_w_a, v_b_a, v_w_x, v_b_x, v_lru_lambda, v_w_pool, v_b_pool,
                             v_pool_scale, v_w_out, v_ln1_g, v_ln1_b, v_w_q, v_w_k, v_w_v, v_w_o, v_ln2_g, v_ln2_b,
                             v_w_ff1, v_w_ff2, v_ln3_g, v_ln3_b)))
    s, d = x.shape[1], x.shape[2]
    p = conv_b.shape[1]
    cg = p // N_POOL_GROUPS
    hd = p // LRU_HEADS
    me = 4 * lax.axis_index("x") + 2 * lax.axis_index("y") + lax.axis_index("c")

    xs, mems, tgt = x[0], mem[0], loss_target[0]
    memb = mems.astype(BF16)

    gathers = {n: _Job("gather", w_loc[n][0].astype(BF16))
               for n in ("w_in", "w_out", "w_q", "w_k", "w_v", "w_o", "w_ff2", "w_pool")}
    ff1_shard = w_ff1[0].astype(BF16)
    ff1_rows = ff1_shard.shape[0] // FF1_PIECES

    def ff1_piece(i, earlier=None):
        return _Job("gather", ff1_shard[i * ff1_rows:(i + 1) * ff1_rows], window=(i * ff1_rows, ff1_shard.shape[0]),
                    into=None if earlier is None else earlier.out)
    tiny = jnp.concatenate([_pack_rows(conv_w[0], p // N_DEV),
                            _pack_rows(jnp.pad(b_pool[0], ((0, 0), (0, p // N_DEV - cg // N_DEV))), p // N_DEV)], axis=0)
    gathers["tiny"] = _Job("gather", tiny)

    def gathered(n):
        full = gathers[n].out
        if n == "w_in":
            return jnp.transpose(full, (1, 0, 2)).reshape(1, full.shape[1], -1)
        return full.reshape(1, -1, full.shape[-1])

    W = {"conv_b": conv_b, "b_a": b_a.reshape(1, p), "b_x": b_x.reshape(1, p), "lru_lambda": lru_lambda,
         "pool_scale": pool_scale, "w_a": w_a[0].astype(BF16), "w_x": w_x[0].astype(BF16)}
    for n in ("ln1_g", "ln1_b", "ln2_g", "ln2_b", "ln3_g", "ln3_b"):
        W[n] = w_loc[n]

    out_g, out_d, out_m, out_v = {}, {}, {}, {}
    pairs, quads = {}, {}
    core = lax.axis_index("c").astype(jnp.int32).reshape(1)
    chip = (2 * lax.axis_index("x") + lax.axis_index("y")).astype(jnp.int32).reshape(1)

    def pair(n, partial):
        pairs[n] = _Job("pair", partial.reshape(N_DEV, -1, partial.shape[-1]))
        return pairs[n]

    def quad(n):
        quads[n] = _Job("quad", _pair_add(pairs[n].src, pairs[n].out, core, "add_" + n))
        return quads[n]

    def update(n, parts, jobs=(), own=None):
        shp = w_loc[n].shape
        rows = parts.shape[1]
        w2, m2, v2 = (a.reshape(rows, -1) for a in (w_loc[n], m_loc[n], v_loc[n]))
        res = _adamw(w2, m2, v2, parts.reshape(parts.shape[0], rows, -1), "adamw_" + n, jobs=jobs, own=own, chip=chip)
        out_g[n], out_d[n], out_m[n], out_v[n] = (r.reshape(shp) for r in res)

    assert FF1_PIECES == 4
    xb = _to_bf16(xs, "cast_x", jobs=[gathers["w_in"], gathers["tiny"], gathers["w_pool"]])
    W["w_pool"] = jnp.transpose(gathers["w_pool"].out, (1, 0, 2, 3)).reshape(N_POOL_GROUPS, cg, cg)
    cwb = gathers["tiny"].out
    W["conv_w"] = jnp.transpose(cwb[:, :CONV_WIDTH, :], (1, 0, 2)).reshape(CONV_WIDTH, p)
    W["b_pool"] = jnp.transpose(cwb[:, SUBLANES:SUBLANES + N_POOL_GROUPS, :cg // N_DEV], (1, 0, 2)).reshape(1, p)
    mixer_consts = (W["w_pool"], W["b_pool"], W["pool_scale"], W["conv_w"], W["conv_b"], W["w_a"], W["b_a"],
                    W["w_x"], W["b_x"], W["lru_lambda"])

    w_in_full = gathered("w_in")
    piece = ff1_piece(FF1_PIECES - 1)
    (proj,) = _mm_nn(xb, w_in_full, [F32], "fwd_proj", jobs=[gathers["w_out"], piece])
    ycat, hsave = _mixer_fwd(proj, *mixer_consts, "fwd_mixer", jobs=[gathers["w_q"], gathers["w_k"]])
    (y1,) = _mm_nn(ycat, gathered("w_out"), [F32], "fwd_out", jobs=[gathers["w_v"]])
    x1, x1b, xhat1, rstd1 = _ln_fwd(xs, y1, W["ln1_g"], W["ln1_b"], "fwd_ln1", jobs=[gathers["w_o"]])
    piece = ff1_piece(0, piece)
    (q,) = _mm_nn(x1b, gathered("w_q"), [BF16], "fwd_q", jobs=[piece])
    (k,) = _mm_nn(memb, gathered("w_k"), [BF16], "fwd_k")
    (v,) = _mm_nn(memb, gathered("w_v"), [BF16], "fwd_v")
    o = _attn_fwd(q, k, v, "fwd_attn")
    piece = ff1_piece(1, piece)
    (y2,) = _mm_nn(o, gathered("w_o"), [F32], "fwd_o", jobs=[piece])
    piece = ff1_piece(2, piece)
    x2, x2b, xhat2, rstd2 = _ln_fwd(x1, y2, W["ln2_g"], W["ln2_b"], "fwd_ln2", jobs=[piece])
    w_ff1_full = piece.out

    def relu_sq(acc):
        r = jnp.maximum(acc, 0.0)
        return r, r * r

    rb, act = _mm_nn(x2b, w_ff1_full, [BF16, BF16], "fwd_ff1", epilogue=relu_sq, jobs=[gathers["w_ff2"]])
    (y3,) = _mm_nn(act, gathered("w_ff2"), [F32], "fwd_ff2")
    loss_rows, dz3, dz3b, dg3, db3 = _ln_loss(x2, y3, W["ln3_g"], W["ln3_b"], tgt, "ln3_loss")
    loss = lax.psum(loss_rows[0, 0], MESH_AXES)

    small = {"ln3_g": dg3, "ln3_b": db3}

    def add_residual(acc, e):
        return (acc + ALPHA * e,)

    dw_ff2 = _mm_tn(act, dz3b, 1, BF16, "bwd_dw_ff2")
    (dhid,) = _mm_nt(dz3b, gathered("w_ff2"), [BF16], "bwd_dact", extras=(rb,), jobs=[pair("w_ff2", dw_ff2)],
                     epilogue=lambda acc, r: (acc * (2.0 * r.astype(F32)),))
    dw_ff1 = _mm_tn(x2b, dhid, N_DEV, BF16, "bwd_dw_ff1", jobs=[quad("w_ff2")])
    (dx2,) = _mm_nt(dhid, w_ff1_full, [F32], "bwd_dx2", epilogue=add_residual, extras=(dz3,), tk=TILES["tk"] // 2,
                    jobs=[pair("w_ff1", dw_ff1)])
    dz2, dz2b, small["ln2_g"], small["ln2_b"] = _ln_bwd(dx2, xhat2, rstd2, W["ln2_g"], "bwd_ln2")

    dw_o = _mm_tn(o, dz2b, 1, BF16, "bwd_dw_o")
    (do,) = _mm_nt(dz2b, gathered("w_o"), [BF16], "bwd_do", jobs=[pair("w_o", dw_o)])
    dq, dk, dv = _attn_bwd(q, k, v, do, "bwd_attn", jobs=[quad("w_o")])
    dw_q = _mm_tn(x1b, dq, 1, BF16, "bwd_dw_q")
    dw_k = _mm_tn(memb, dk.astype(BF16), 1, BF16, "bwd_dw_k")
    dw_v = _mm_tn(memb, dv.astype(BF16), 1, BF16, "bwd_dw_v")
    (dx1,) = _mm_nt(dq, gathered("w_q"), [F32], "bwd_dx1", epilogue=add_residual, extras=(dz2,),
                    jobs=[pair("w_q", dw_q), pair("w_k", dw_k), pair("w_v", dw_v)])
    dz1, dz1b, small["ln1_g"], small["ln1_b"] = _ln_bwd(dx1, xhat1, rstd1, W["ln1_g"], "bwd_ln1", jobs=[quad("w_q")])

    dw_out = _mm_tn(ycat, dz1b, 1, BF16, "bwd_dw_out", jobs=[quad("w_k")])
    (dycat,) = _mm_nt(dz1b, gathered("w_out"), [F32], "bwd_dycat", jobs=[quad("w_v"), pair("w_out", dw_out)])
    (dproj, dwp, small["b_pool"], small["pool_scale"], small["conv_w"], small["conv_b"], small["w_a"], small["b_a"],
     small["w_x"], small["b_x"], small["lru_lambda"]) = _mixer_bwd(
        dycat, proj, hsave, *mixer_consts, "bwd_mixer", jobs=[quad("w_ff1")])
    dw_pool = jnp.transpose(dwp.astype(BF16).reshape(N_POOL_GROUPS, N_DEV, cg // N_DEV, cg), (1, 0, 2, 3))
    pack = jnp.concatenate([_pack_rows(small[n], p) for n in SMALL_ORDER], axis=0)
    small_gather = _Job("gather", pack)
    dw_in = _mm_tn(xb, dproj, 1, BF16, "bwd_dw_in", jobs=[quad("w_out"), pair("w_pool", dw_pool), small_gather])
    dw_in = jnp.transpose(dw_in.reshape(dw_in.shape[1], N_DEV, -1), (1, 0, 2))
    (grad_x,) = _mm_nt(dproj, w_in_full, [F32], "bwd_dx", epilogue=add_residual, extras=(dz1,),
                       jobs=[pair("w_in", dw_in), quad("w_pool")])

    update("w_ff2", quads["w_ff2"].out, jobs=[quad("w_in")], own=quads["w_ff2"].src)
    for n in ("w_ff1", "w_o", "w_q", "w_k", "w_v", "w_out", "w_pool", "w_in"):
        update(n, quads[n].out, own=quads[n].src)

    total = _sum_parts(small_gather.out, "sum_small")
    row = 0
    for n in SMALL_ORDER:
        size = small[n].size
        nrows = size // p
        g_full = total[row:row + nrows].reshape(small[n].shape)
        row += nrows + (-nrows) % SUBLANES
        if n == "conv_w":
            g_loc = lax.dynamic_slice_in_dim(g_full, me * (p // N_DEV), p // N_DEV, axis=1)
        elif n == "b_pool":
            g_loc = lax.dynamic_slice_in_dim(g_full.reshape(N_POOL_GROUPS, cg), me * (cg // N_DEV), cg // N_DEV, axis=1)
        else:
            g_loc = g_full
        rows = g_loc.shape[0] if n not in ("w_a", "w_x") else LRU_HEADS * hd
        update(n, g_loc.reshape(1, rows, -1))

    order = names
    return (loss, grad_x[None], *[out_g[n] for n in order], *[out_d[n] for n in order],
            *[out_m[n] for n in order], *[out_v[n] for n in order])
```

```python
import functools

import jax
import jax.numpy as jnp
from jax import lax
from jax.experimental import pallas as pl
from jax.experimental.pallas import tpu as pltpu

F32 = jnp.float32
BF16 = jnp.bfloat16

N_DEV = 8
MESH_AXES = ("x", "y", "c")
POOL_WINDOWS = (2, 4, 8, 16)
N_POOL_GROUPS = len(POOL_WINDOWS)
POOL_HALO = 16
CONV_WIDTH = 4
CONV_HALO = 8
FF1_PIECES = 4
LRU_HEADS = 8
LRU_C = 8.0
XATTN_HEADS = 4
LN_EPS = 1e-5
ALPHA = 2.0 ** 0.25
ADAM_LR = 0.001
ADAM_B1 = 0.9
ADAM_B2 = 0.999
ADAM_EPS = 1e-08
ADAM_WD = 0.01
ADAM_STEP = 10
SUBLANES = 8
VMEM_LIMIT = 56 * 1024 * 1024

NT_DIMS = (((1,), (1,)), ((), ()))
TN_DIMS = (((0,), (0,)), ((), ()))


def _params(*sem):
    return pltpu.CompilerParams(dimension_semantics=sem, vmem_limit_bytes=VMEM_LIMIT)


def _place():
    return lax.axis_index("x"), lax.axis_index("y"), lax.axis_index("c")


def _remote(src, dst, send_sem, recv_sem, to):
    return pltpu.make_async_remote_copy(src_ref=src, dst_ref=dst, send_sem=send_sem, recv_sem=recv_sem,
                                        device_id=to, device_id_type=pl.DeviceIdType.MESH)


class _Job:
    def __init__(self, kind, src, window=None, into=None):
        self.kind, self.src, self.out, self.window, self.into = kind, src, None, window, into

    def out_shape(self):
        s = self.src.shape
        if self.kind == "gather" and self.window is not None:
            s = (self.window[1],) + s[1:]
        shape = {"gather": (N_DEV,) + s, "pair": (4,) + s[1:], "quad": (3,) + s[1:]}[self.kind]
        return jax.ShapeDtypeStruct(shape, self.src.dtype)

    def scratch(self):
        n = {"gather": 7, "pair": 4, "quad": 3}[self.kind]
        sems = [pltpu.SemaphoreType.DMA((n,)), pltpu.SemaphoreType.DMA((n,))]
        if self.kind == "gather":
            sems += [pltpu.SemaphoreType.DMA((2,)), pltpu.VMEM(self.src.shape, self.src.dtype)]
        return sems

    def ops(self, src, out, *scratch):
        if self.kind == "gather":
            return _gather_ops(src, out, *scratch, first_row=None if self.window is None else self.window[0])
        if self.kind == "quad":
            return _quad_ops(src, out, *scratch, rows=self.window)
        return _pair_ops(src, out, *scratch)


def _gather_ops(x_ref, out_ref, send_sems, recv_sems, local_sems, bounce, first_row=None):
    x, y, c = _place()
    me, sibling = (x, y, c), (x, y, 1 - c)
    chips = [(1 - x, y), (x, 1 - y), (1 - x, 1 - y)]

    def slot(px, py, pc):
        block = out_ref.at[4 * px + 2 * py + pc]
        return block if first_row is None else block.at[pl.ds(first_row, x_ref.shape[0])]

    def copy(k, block, to, src=None):
        return _remote(slot(*block) if src is None else src, slot(*block), send_sems.at[k], recv_sems.at[k], to)

    mine_in = pltpu.make_async_copy(x_ref, bounce, local_sems.at[0])
    mine_out = pltpu.make_async_copy(bounce, slot(*me), local_sems.at[1])
    first = [copy(0, me, sibling, src=x_ref)] + [copy(1 + j, me, (*chip, c), src=x_ref) for j, chip in enumerate(chips)]
    passed = [copy(4 + j, (*chip, c), sibling) for j, chip in enumerate(chips)]

    def start():
        mine_in.start()
        for cp in first:
            cp.start()

    def mid():
        mine_in.wait()
        mine_out.start()
        for j, chip in enumerate(chips):
            copy(1 + j, (*chip, c), me).wait_recv()
            passed[j].start()

    def finish():
        copy(0, sibling, me).wait_recv()
        for j, chip in enumerate(chips):
            copy(4 + j, (*chip, 1 - c), me).wait_recv()
        for cp in first + passed:
            cp.wait_send()
        mine_out.wait()

    return start, mid, finish


def _pair_ops(p_ref, got_ref, send_sems, recv_sems):
    x, y, c = _place()
    give = [_remote(p_ref.at[2 * k + 1 - c], got_ref.at[k], send_sems.at[k], recv_sems.at[k], (x, y, 1 - c))
            for k in range(4)]

    def start():
        for cp in give:
            cp.start()

    def finish():
        for cp in give:
            cp.wait_recv()
        for cp in give:
            cp.wait_send()

    return start, None, finish


def _quad_ops(q_ref, out_ref, send_sems, recv_sems, rows=None):
    x, y, c = _place()

    def part(block):
        return block if rows is None else block.at[pl.ds(rows[0], rows[1])]

    copies = []
    for rel in range(1, 4):
        px = 1 - x if rel & 2 else x
        py = 1 - y if rel & 1 else y
        copies.append(_remote(part(q_ref.at[2 * px + py]), part(out_ref.at[rel - 1]), send_sems.at[rel - 1],
                              recv_sems.at[rel - 1], (px, py, c)))

    def start():
        for cp in copies:
            cp.start()

    def finish():
        for cp in copies:
            cp.wait_recv()
        for cp in copies:
            cp.wait_send()

    return start, None, finish


def _call(body, *, grid, in_specs, out_specs, out_shape, scratch_shapes=(), semantics, name, args, jobs=(), index=None):
    in_specs, out_specs, out_shape = list(in_specs), list(out_specs), list(out_shape)
    scratch_shapes, jobs = list(scratch_shapes), list(jobs)
    n_in, n_out, n_scr, n_job = len(in_specs), len(out_specs), len(scratch_shapes), len(jobs)
    n_idx = 0 if index is None else 1
    job_scratch = [j.scratch() for j in jobs]
    n_steps = functools.reduce(lambda a, b: a * b, grid, 1)
    early = n_steps - 1 - max(1, n_steps // 8) if n_steps >= 4 else None

    intos = [(k, j.into) for k, j in enumerate(jobs) if j.into is not None]

    def hosted(*refs):
        idx, refs = refs[:n_idx], refs[n_idx:]
        ins, jin = refs[:n_in], refs[n_in:n_in + n_job]
        o0 = n_in + n_job + len(intos)
        outs, jout = refs[o0:o0 + n_out], refs[o0 + n_out:o0 + n_out + n_job]
        s0 = o0 + n_out + n_job
        scr, jscr = refs[s0:s0 + n_scr], refs[s0 + n_scr:]
        ops, at = [], 0
        for k, j in enumerate(jobs):
            ops.append(j.ops(jin[k], jout[k], *jscr[at:at + len(job_scratch[k])]))
            at += len(job_scratch[k])
        step = functools.reduce(lambda acc, a: acc * grid[a] + pl.program_id(a), range(len(grid)), 0)
        mids = [mid for _, mid, _ in ops if mid is not None]

        @pl.when(step == 0)
        def _():
            for start, _, _ in ops:
                start()

        if mids and early is not None:
            @pl.when(step == early)
            def _():
                for mid in mids:
                    mid()

        body(*idx, *ins, *outs, *scr)

        @pl.when(step == n_steps - 1)
        def _():
            if early is None:
                for mid in mids:
                    mid()
            for _, _, finish in ops:
                finish()

    hbm = pl.BlockSpec(memory_space=pl.ANY)
    spec = pltpu.PrefetchScalarGridSpec(
        num_scalar_prefetch=n_idx, grid=grid, in_specs=in_specs + [hbm] * (n_job + len(intos)),
        out_specs=out_specs + [hbm] * n_job, scratch_shapes=scratch_shapes + [s for js in job_scratch for s in js])
    aliases = {n_idx + n_in + n_job + q: n_out + k for q, (k, _) in enumerate(intos)}
    res = pl.pallas_call(
        hosted if jobs else body, grid_spec=spec, out_shape=out_shape + [j.out_shape() for j in jobs],
        input_output_aliases=aliases,
        compiler_params=_params(*(["arbitrary"] * len(grid) if jobs else semantics)), name=name,
    )(*([] if index is None else [index]), *args, *[j.src for j in jobs], *[buf for _, buf in intos])
    for j, o in zip(jobs, res[n_out:]):
        j.out = o
    return res[:n_out]


def _to_bf16(a, name, jobs=()):
    r, c = a.shape
    tr = _tile(TILES["row"], r)

    def body(a_ref, o_ref):
        o_ref[...] = a_ref[...].astype(BF16)

    row = pl.BlockSpec((tr, c), lambda i: (i, 0))
    return _call(body, grid=(r // tr,), in_specs=[row], out_specs=[row], out_shape=[jax.ShapeDtypeStruct((r, c), BF16)],
                 semantics=("parallel",), name=name, args=(a,), jobs=jobs)[0]


TILES = dict(tm=1024, tn=1024, tk=2048, row=256, attn=512, mixer=256, adam=128, add=1024)


def _tile(pref, n):
    for t in range(min(pref, n), 0, -1):
        if n % t == 0 and (t % SUBLANES == 0 or t == n):
            return t
    return n


def _accumulate(acc, step, n_steps, product, write):
    if n_steps == 1:
        write(product())
        return

    @pl.when(step == 0)
    def _():
        acc[...] = product()

    @pl.when(jnp.logical_and(step > 0, step < n_steps - 1))
    def _():
        acc[...] += product()

    @pl.when(step == n_steps - 1)
    def _():
        write(acc[...] + product())


def _acc_scratch(n_steps, tm, tn):
    return [] if n_steps == 1 else [pltpu.VMEM((tm, tn), F32)]


def _mm_nn(a, b3, out_dtypes, name, *, tm=None, tn=None, tk=None, epilogue=None, extras=(), jobs=()):
    m, k = a.shape
    g, k2, ns = b3.shape
    assert k == k2
    n = g * ns
    tm, tn, tk = _tile(tm or TILES["tm"], m), _tile(tn or TILES["tn"], ns), _tile(tk or TILES["tk"], k)
    nb, nk = ns // tn, k // tk
    n_ex, n_out = len(extras), len(out_dtypes)

    def body(*refs):
        a_ref, b_ref = refs[:2]
        ex = refs[2:2 + n_ex]
        outs = refs[2 + n_ex:2 + n_ex + n_out]
        acc = refs[-1] if nk > 1 else None

        def write(r):
            res = epilogue(r, *[e[...] for e in ex]) if epilogue is not None else (r,)
            for o, v in zip(outs, res):
                o[...] = v.astype(o.dtype)

        _accumulate(acc, pl.program_id(2), nk,
                    lambda: jnp.dot(a_ref[...], b_ref[...], preferred_element_type=F32), write)

    tile_out = pl.BlockSpec((tm, tn), lambda i, j, kk: (i, j))
    return _call(
        body, grid=(m // tm, n // tn, nk),
        in_specs=[pl.BlockSpec((tm, tk), lambda i, j, kk: (i, kk)),
                  pl.BlockSpec((None, tk, tn), lambda i, j, kk: (j // nb, kk, j % nb))] + [tile_out] * n_ex,
        out_specs=[tile_out] * n_out,
        out_shape=[jax.ShapeDtypeStruct((m, n), d) for d in out_dtypes],
        scratch_shapes=_acc_scratch(nk, tm, tn),
        semantics=("parallel", "parallel", "arbitrary"), name=name, args=(a, b3, *extras), jobs=jobs)


def _mm_nt(a, b3, out_dtypes, name, *, tm=None, tn=None, tk=None, epilogue=None, extras=(), jobs=()):
    m, n = a.shape
    g, k, ns = b3.shape
    assert n == g * ns
    tm, tn, tk = _tile(tm or TILES["tm"], m), _tile(tn or TILES["tn"], k), _tile(tk or TILES["tk"], ns)
    nb, nc = ns // tk, n // tk
    n_ex, n_out = len(extras), len(out_dtypes)

    def body(*refs):
        a_ref, b_ref = refs[:2]
        ex = refs[2:2 + n_ex]
        outs = refs[2 + n_ex:2 + n_ex + n_out]
        acc = refs[-1] if nc > 1 else None

        def write(r):
            res = epilogue(r, *[e[...] for e in ex]) if epilogue is not None else (r,)
            for o, v in zip(outs, res):
                o[...] = v.astype(o.dtype)

        _accumulate(acc, pl.program_id(2), nc,
                    lambda: lax.dot_general(a_ref[...], b_ref[...], NT_DIMS, preferred_element_type=F32), write)

    tile_out = pl.BlockSpec((tm, tn), lambda i, j, cc: (i, j))
    return _call(
        body, grid=(m // tm, k // tn, nc),
        in_specs=[pl.BlockSpec((tm, tk), lambda i, j, cc: (i, cc)),
                  pl.BlockSpec((None, tn, tk), lambda i, j, cc: (cc // nb, j, cc % nb))] + [tile_out] * n_ex,
        out_specs=[tile_out] * n_out,
        out_shape=[jax.ShapeDtypeStruct((m, k), d) for d in out_dtypes],
        scratch_shapes=_acc_scratch(nc, tm, tn),
        semantics=("parallel", "parallel", "arbitrary"), name=name, args=(a, b3, *extras), jobs=jobs)


def _mm_tn(a, b, g, out_dtype, name, *, tm=None, tn=None, tk=None, jobs=()):
    s, m = a.shape
    s2, n = b.shape
    assert s == s2 and n % g == 0
    ns = n // g
    tm, tn, tk = _tile(tm or TILES["tm"], m), _tile(tn or TILES["tn"], ns), _tile(tk or TILES["tk"], s)
    nb, nc = ns // tn, s // tk

    def body(a_ref, b_ref, o_ref, *scratch):
        def write(r):
            o_ref[...] = r.astype(o_ref.dtype)

        _accumulate(scratch[0] if nc > 1 else None, pl.program_id(2), nc,
                    lambda: lax.dot_general(a_ref[...], b_ref[...], TN_DIMS, preferred_element_type=F32), write)

    return _call(
        body, grid=(m // tm, n // tn, nc),
        in_specs=[pl.BlockSpec((tk, tm), lambda i, j, cc: (cc, i)),
                  pl.BlockSpec((tk, tn), lambda i, j, cc: (cc, j))],
        out_specs=[pl.BlockSpec((None, tm, tn), lambda i, j, cc: (j // nb, i, j % nb))],
        out_shape=[jax.ShapeDtypeStruct((g, m, ns), out_dtype)],
        scratch_shapes=_acc_scratch(nc, tm, tn),
        semantics=("parallel", "parallel", "arbitrary"), name=name, args=(a, b), jobs=jobs)[0]


def _ln_stats(z):
    mu = jnp.mean(z, axis=-1, keepdims=True)
    zc = z - mu
    var = jnp.mean(zc * zc, axis=-1, keepdims=True)
    rstd = lax.rsqrt(var + LN_EPS)
    return zc * rstd, rstd


def _ln_grad(dout, xhat, rstd, gain):
    dxhat = dout * gain
    m1 = jnp.mean(dxhat, axis=-1, keepdims=True)
    m2 = jnp.mean(dxhat * xhat, axis=-1, keepdims=True)
    return rstd * (dxhat - m1 - xhat * m2)


def _ln_fwd(xres, y, gain, bias, name, jobs=()):
    s, d = xres.shape
    tr = _tile(TILES["row"], s)

    def body(x_ref, y_ref, g_ref, b_ref, xn_ref, xnb_ref, xhat_ref, rstd_ref):
        xhat, rstd = _ln_stats(ALPHA * x_ref[...] + y_ref[...])
        out = xhat * g_ref[...] + b_ref[...]
        xn_ref[...] = out
        xnb_ref[...] = out.astype(BF16)
        xhat_ref[...] = xhat
        rstd_ref[...] = rstd

    row = pl.BlockSpec((tr, d), lambda i: (i, 0))
    vec = pl.BlockSpec((1, d), lambda i: (0, 0))
    return _call(
        body, grid=(s // tr,), in_specs=[row, row, vec, vec],
        out_specs=[row, row, row, pl.BlockSpec((tr, 1), lambda i: (i, 0))],
        out_shape=[jax.ShapeDtypeStruct((s, d), F32), jax.ShapeDtypeStruct((s, d), BF16),
                   jax.ShapeDtypeStruct((s, d), F32), jax.ShapeDtypeStruct((s, 1), F32)],
        semantics=("parallel",), name=name, args=(xres, y, gain, bias), jobs=jobs)


def _ln_bwd(dout, xhat, rstd, gain, name, jobs=()):
    s, d = dout.shape
    tr = _tile(TILES["row"], s)

    def body(d_ref, xhat_ref, rstd_ref, g_ref, dz_ref, dzb_ref, dg_ref, db_ref):
        @pl.when(pl.program_id(0) == 0)
        def _():
            dg_ref[...] = jnp.zeros_like(dg_ref)
            db_ref[...] = jnp.zeros_like(db_ref)

        dout_t, xhat_t = d_ref[...], xhat_ref[...]
        dz = _ln_grad(dout_t, xhat_t, rstd_ref[...], g_ref[...])
        dz_ref[...] = dz
        dzb_ref[...] = dz.astype(BF16)
        dg_ref[...] += jnp.sum(dout_t * xhat_t, axis=0, keepdims=True)
        db_ref[...] += jnp.sum(dout_t, axis=0, keepdims=True)

    row = pl.BlockSpec((tr, d), lambda i: (i, 0))
    vec = pl.BlockSpec((1, d), lambda i: (0, 0))
    return _call(
        body, grid=(s // tr,), in_specs=[row, row, pl.BlockSpec((tr, 1), lambda i: (i, 0)), vec],
        out_specs=[row, row, vec, vec],
        out_shape=[jax.ShapeDtypeStruct((s, d), F32), jax.ShapeDtypeStruct((s, d), BF16),
                   jax.ShapeDtypeStruct((1, d), F32), jax.ShapeDtypeStruct((1, d), F32)],
        semantics=("arbitrary",), name=name, args=(dout, xhat, rstd, gain), jobs=jobs)


def _ln_loss(xres, y, gain, bias, target, name, jobs=()):
    s, d = xres.shape
    tr = _tile(TILES["row"], s)

    def body(x_ref, y_ref, g_ref, b_ref, t_ref, loss_ref, dz_ref, dzb_ref, dg_ref, db_ref):
        @pl.when(pl.program_id(0) == 0)
        def _():
            loss_ref[...] = jnp.zeros_like(loss_ref)
            dg_ref[...] = jnp.zeros_like(dg_ref)
            db_ref[...] = jnp.zeros_like(db_ref)

        xhat, rstd = _ln_stats(ALPHA * x_ref[...] + y_ref[...])
        diff = xhat * g_ref[...] + b_ref[...] - t_ref[...]
        per_row = jnp.mean(diff * diff, axis=-1, keepdims=True)
        loss_ref[...] += 0.5 * jnp.sum(per_row, axis=0, keepdims=True)
        dout = diff * (1.0 / d)
        dz = _ln_grad(dout, xhat, rstd, g_ref[...])
        dz_ref[...] = dz
        dzb_ref[...] = dz.astype(BF16)
        dg_ref[...] += jnp.sum(dout * xhat, axis=0, keepdims=True)
        db_ref[...] += jnp.sum(dout, axis=0, keepdims=True)

    row = pl.BlockSpec((tr, d), lambda i: (i, 0))
    vec = pl.BlockSpec((1, d), lambda i: (0, 0))
    return _call(
        body, grid=(s // tr,), in_specs=[row, row, vec, vec, row],
        out_specs=[pl.BlockSpec((1, 128), lambda i: (0, 0)), row, row, vec, vec],
        out_shape=[jax.ShapeDtypeStruct((1, 128), F32), jax.ShapeDtypeStruct((s, d), F32),
                   jax.ShapeDtypeStruct((s, d), BF16), jax.ShapeDtypeStruct((1, d), F32),
                   jax.ShapeDtypeStruct((1, d), F32)],
        semantics=("arbitrary",), name=name, args=(xres, y, gain, bias, target), jobs=jobs)


def _softmax_rows(s):
    e = jnp.exp(s - jnp.max(s, axis=-1, keepdims=True))
    return e / jnp.sum(e, axis=-1, keepdims=True)


def _attn_fwd(q, k, v, name, jobs=()):
    s, d = q.shape
    m = k.shape[0]
    hd = d // XATTN_HEADS
    ts = _tile(TILES["attn"], s)
    scale = hd ** -0.5

    def body(q_ref, k_ref, v_ref, o_ref):
        for h in range(XATTN_HEADS):
            hs = slice(h * hd, (h + 1) * hd)
            sc = lax.dot_general(q_ref[:, hs], k_ref[:, hs], NT_DIMS, preferred_element_type=F32) * scale
            p = _softmax_rows(sc).astype(BF16)
            o_ref[:, hs] = jnp.dot(p, v_ref[:, hs], preferred_element_type=F32).astype(BF16)

    row = pl.BlockSpec((ts, d), lambda i: (i, 0))
    memb = pl.BlockSpec((m, d), lambda i: (0, 0))
    return _call(
        body, grid=(s // ts,), in_specs=[row, memb, memb], out_specs=[row],
        out_shape=[jax.ShapeDtypeStruct((s, d), BF16)],
        semantics=("parallel",), name=name, args=(q, k, v), jobs=jobs)[0]


def _attn_bwd(q, k, v, do, name, jobs=()):
    s, d = q.shape
    m = k.shape[0]
    hd = d // XATTN_HEADS
    ts = _tile(TILES["attn"], s)
    scale = hd ** -0.5

    def body(q_ref, k_ref, v_ref, do_ref, dq_ref, dk_ref, dv_ref):
        @pl.when(pl.program_id(0) == 0)
        def _():
            dk_ref[...] = jnp.zeros_like(dk_ref)
            dv_ref[...] = jnp.zeros_like(dv_ref)

        for h in range(XATTN_HEADS):
            hs = slice(h * hd, (h + 1) * hd)
            qh, kh, vh, doh = q_ref[:, hs], k_ref[:, hs], v_ref[:, hs], do_ref[:, hs]
            sc = lax.dot_general(qh, kh, NT_DIMS, preferred_element_type=F32) * scale
            p = _softmax_rows(sc)
            pb = p.astype(BF16)
            dp = lax.dot_general(doh, vh, NT_DIMS, preferred_element_type=F32)
            ds = (p * (dp - jnp.sum(dp * p, axis=-1, keepdims=True)) * scale).astype(BF16)
            dq_ref[:, hs] = jnp.dot(ds, kh, preferred_element_type=F32).astype(BF16)
            dk_ref[:, hs] += lax.dot_general(ds, qh, TN_DIMS, preferred_element_type=F32)
            dv_ref[:, hs] += lax.dot_general(pb, doh, TN_DIMS, preferred_element_type=F32)

    row = pl.BlockSpec((ts, d), lambda i: (i, 0))
    memb = pl.BlockSpec((m, d), lambda i: (0, 0))
    return _call(
        body, grid=(s // ts,), in_specs=[row, memb, memb, row], out_specs=[row, memb, memb],
        out_shape=[jax.ShapeDtypeStruct((s, d), BF16), jax.ShapeDtypeStruct((m, d), F32),
                   jax.ShapeDtypeStruct((m, d), F32)],
        semantics=("arbitrary",), name=name, args=(q, k, v, do), jobs=jobs)


def _sigmoid(x):
    return 1.0 / (1.0 + jnp.exp(-x))


def _log1p(x):
    u = 1.0 + x
    return jnp.where(u == 1.0, x, jnp.log(u) * (x / jnp.where(u == 1.0, 1.0, u - 1.0)))


def _softplus(x):
    return jnp.maximum(x, 0.0) + _log1p(jnp.exp(-jnp.abs(x)))


def _expm1(x):
    series = x * (1.0 + x * 0.5 * (1.0 + x * (1.0 / 3.0) * (1.0 + x * 0.25 * (1.0 + x * 0.2 * (1.0 + x * (1.0 / 6.0))))))
    return jnp.where(jnp.abs(x) < 0.1, series, jnp.exp(x) - 1.0)


GELU_K = 0.7978845608028654
GELU_C = 0.044715


def _gelu(x):
    return 0.5 * x * (1.0 + jnp.tanh(GELU_K * (x + GELU_C * (x * x * x))))


def _gelu_grad(x):
    th = jnp.tanh(GELU_K * (x + GELU_C * (x * x * x)))
    return 0.5 * (1.0 + th) + 0.5 * x * (1.0 - th * th) * GELU_K * (1.0 + 3.0 * GELU_C * x * x)


def _window_sum(ext_ref, first, rows, cols, w, step):
    acc = ext_ref[first:first + rows, cols]
    for kk in range(1, w):
        acc = acc + ext_ref[first + step * kk:first + step * kk + rows, cols]
    return acc


def _lru_gates(c_s, wa_ref, ba_ref, wx_ref, bx_ref, lam_ref, t_idx, hd, r_s, i_s, a_s, mult_s):
    sp = _softplus(-lam_ref[...])
    for h in range(LRU_HEADS):
        hs = slice(h * hd, (h + 1) * hd)
        chb = c_s[:, hs].astype(BF16)
        r = _sigmoid(jnp.dot(chb, wa_ref[h], preferred_element_type=F32) + ba_ref[:, hs])
        ig = _sigmoid(jnp.dot(chb, wx_ref[h], preferred_element_type=F32) + bx_ref[:, hs])
        log_a = -LRU_C * r * sp[:, hs]
        mult = jnp.sqrt(-_expm1(2.0 * log_a))
        r_s[:, hs] = r
        i_s[:, hs] = ig
        a_s[:, hs] = jnp.exp(log_a)
        mult_s[:, hs] = jnp.where(t_idx == 0, 1.0, mult)


def _conv(ext_ref, cw_ref, cb_ref, rows):
    acc = cb_ref[...] + cw_ref[0:1, :] * ext_ref[CONV_HALO - 3:CONV_HALO - 3 + rows, :]
    for kk in range(1, CONV_WIDTH):
        off = CONV_HALO - (CONV_WIDTH - 1) + kk
        acc = acc + cw_ref[kk:kk + 1, :] * ext_ref[off:off + rows, :]
    return acc


def _mixer_fwd(proj, wp, bp, ps, cw, cb, wa, ba, wx, bx, lam, name, jobs=()):
    s, p3 = proj.shape
    p = p3 // 3
    cg, hd = p // N_POOL_GROUPS, p // LRU_HEADS
    t = _tile(TILES["mixer"], s)

    def body(up_ref, ul_ref, ug_ref, wp_ref, bp_ref, ps_ref, cw_ref, cb_ref, wa_ref, ba_ref, wx_ref, bx_ref,
             lam_ref, ycat_ref, h_ref, extp, extl, hc, c_s, r_s, i_s, a_s, b_s):
        i = pl.program_id(0)

        @pl.when(i == 0)
        def _():
            extp[0:POOL_HALO, :] = jnp.zeros((POOL_HALO, p), F32)
            extl[0:CONV_HALO, :] = jnp.zeros((CONV_HALO, p), F32)
            hc[...] = jnp.zeros_like(hc)

        t_idx = i * t + lax.broadcasted_iota(jnp.int32, (t, 1), 0)

        extp[POOL_HALO:POOL_HALO + t, :] = up_ref[...]
        for g, w in enumerate(POOL_WINDOWS):
            cs = slice(g * cg, (g + 1) * cg)
            cnt = jnp.minimum(t_idx + 1, w).astype(F32)
            mixed = _window_sum(extp, POOL_HALO, t, cs, w, -1) / cnt - up_ref[:, cs]
            pre = jnp.dot(mixed.astype(BF16), wp_ref[g], preferred_element_type=F32) + bp_ref[:, cs]
            ycat_ref[:, cs] = (pre * ps_ref[:, cs]).astype(BF16)
        extp[0:POOL_HALO, :] = extp[t:t + POOL_HALO, :]

        extl[CONV_HALO:CONV_HALO + t, :] = ul_ref[...]
        c_s[...] = _conv(extl, cw_ref, cb_ref, t)
        extl[0:CONV_HALO, :] = extl[t:t + CONV_HALO, :]
        _lru_gates(c_s, wa_ref, ba_ref, wx_ref, bx_ref, lam_ref, t_idx, hd, r_s, i_s, a_s, b_s)
        b_s[...] = b_s[...] * (i_s[...] * c_s[...])

        rows = lax.broadcasted_iota(jnp.int32, (SUBLANES, p), 0)

        def block(bi, h):
            r0 = pl.multiple_of(bi * SUBLANES, SUBLANES)
            at = a_s[pl.ds(r0, SUBLANES), :]
            bt = b_s[pl.ds(r0, SUBLANES), :]
            out = jnp.zeros((SUBLANES, p), F32)
            for j in range(SUBLANES):
                h = at[j:j + 1, :] * h + bt[j:j + 1, :]
                out = jnp.where(rows == j, h, out)
            h_ref[pl.ds(r0, SUBLANES), :] = out
            return h

        hc[0:1, :] = lax.fori_loop(0, t // SUBLANES, block, hc[0:1, :])
        ycat_ref[:, p:2 * p] = (h_ref[...] * _gelu(ug_ref[...])).astype(BF16)

    def col(j):
        return pl.BlockSpec((t, p), lambda i: (i, j))

    def whole(a):
        nd = a.ndim
        return pl.BlockSpec(a.shape, lambda i: (0,) * nd)

    consts = (wp, bp, ps, cw, cb, wa, ba, wx, bx, lam)
    tile = pltpu.VMEM((t, p), F32)
    return _call(
        body, grid=(s // t,), in_specs=[col(0), col(1), col(2)] + [whole(a) for a in consts],
        out_specs=[pl.BlockSpec((t, 2 * p), lambda i: (i, 0)), pl.BlockSpec((t, p), lambda i: (i, 0))],
        out_shape=[jax.ShapeDtypeStruct((s, 2 * p), BF16), jax.ShapeDtypeStruct((s, p), F32)],
        scratch_shapes=[pltpu.VMEM((t + POOL_HALO, p), F32), pltpu.VMEM((t + CONV_HALO, p), F32),
                        pltpu.VMEM((SUBLANES, p), F32), tile, tile, tile, tile, tile],
        semantics=("arbitrary",), name=name, args=(proj, proj, proj, *consts), jobs=jobs)


def _mixer_bwd(dycat, proj, hsave, wp, bp, ps, cw, cb, wa, ba, wx, bx, lam, name, jobs=()):
    s, p3 = proj.shape
    p = p3 // 3
    cg, hd = p // N_POOL_GROUPS, p // LRU_HEADS
    t = _tile(TILES["mixer"], s)
    nt = s // t

    def body(dyp_ref, dyl_ref, up_ref, ul_ref, ug_ref, upp_ref, ulp_ref, h_ref, hp_ref,
             wp_ref, bp_ref, ps_ref, cw_ref, cb_ref, wa_ref, ba_ref, wx_ref, bx_ref, lam_ref,
             dproj_ref, dwp_ref, dbp_ref, dps_ref, dcw_ref, dcb_ref, dwa_ref, dba_ref, dwx_ref, dbx_ref, dlam_ref,
             extp, extg, extl, extdc, exth, ghc, c_s, r_s, i_s, a_s, mult_s, gh_s):
        i = pl.program_id(0)
        ib = nt - 1 - i

        @pl.when(i == 0)
        def _():
            for ref in (dwp_ref, dbp_ref, dps_ref, dcw_ref, dcb_ref, dwa_ref, dba_ref, dwx_ref, dbx_ref, dlam_ref):
                ref[...] = jnp.zeros_like(ref)
            extg[t:t + POOL_HALO, :] = jnp.zeros((POOL_HALO, p), F32)
            extdc[t:t + CONV_HALO, :] = jnp.zeros((CONV_HALO, p), F32)
            ghc[...] = jnp.zeros_like(ghc)

        t_idx = ib * t + lax.broadcasted_iota(jnp.int32, (t, 1), 0)
        seq_start = ib == 0

        extl[0:CONV_HALO, :] = jnp.where(seq_start, 0.0, ulp_ref[...])
        extl[CONV_HALO:CONV_HALO + t, :] = ul_ref[...]
        c_s[...] = _conv(extl, cw_ref, cb_ref, t)
        _lru_gates(c_s, wa_ref, ba_ref, wx_ref, bx_ref, lam_ref, t_idx, hd, r_s, i_s, a_s, mult_s)
        exth[0:SUBLANES, :] = jnp.where(seq_start, 0.0, hp_ref[...])
        exth[SUBLANES:SUBLANES + t, :] = h_ref[...]

        ug = ug_ref[...]
        dyl = dyl_ref[...]
        dproj_ref[:, 2 * p:3 * p] = (dyl * h_ref[...] * _gelu_grad(ug)).astype(BF16)
        gh_s[...] = dyl * _gelu(ug)

        rows = lax.broadcasted_iota(jnp.int32, (SUBLANES, p), 0)
        nblk = t // SUBLANES

        def block(bi, carry):
            r0 = pl.multiple_of((nblk - 1 - bi) * SUBLANES, SUBLANES)
            at = a_s[pl.ds(r0, SUBLANES), :]
            dt = gh_s[pl.ds(r0, SUBLANES), :]
            out = jnp.zeros((SUBLANES, p), F32)
            for j in range(SUBLANES - 1, -1, -1):
                gh = dt[j:j + 1, :] + carry
                out = jnp.where(rows == j, gh, out)
                carry = at[j:j + 1, :] * gh
            gh_s[pl.ds(r0, SUBLANES), :] = out
            return carry

        ghc[0:1, :] = lax.fori_loop(0, nblk, block, ghc[0:1, :])

        sp = _softplus(-lam_ref[...])
        dsp_dlam = -_sigmoid(-lam_ref[...])
        for h in range(LRU_HEADS):
            hs = slice(h * hd, (h + 1) * hd)
            gh, a, mult, r, ig, c = gh_s[:, hs], a_s[:, hs], mult_s[:, hs], r_s[:, hs], i_s[:, hs], c_s[:, hs]
            hprev = exth[SUBLANES - 1:SUBLANES - 1 + t, hs]
            dmult = gh * (ig * c)
            dlog_a = a * gh * hprev + jnp.where(t_idx == 0, 0.0, -dmult * a * a / mult)
            dlam_ref[:, hs] += jnp.sum(dlog_a * r, axis=0, keepdims=True) * (-LRU_C) * dsp_dlam[:, hs]
            dpa = dlog_a * (-LRU_C * sp[:, hs]) * r * (1.0 - r)
            dpx = gh * mult * c * ig * (1.0 - ig)
            dpab, dpxb, chb = dpa.astype(BF16), dpx.astype(BF16), c.astype(BF16)
            dwa_ref[h] += lax.dot_general(chb, dpab, TN_DIMS, preferred_element_type=F32)
            dwx_ref[h] += lax.dot_general(chb, dpxb, TN_DIMS, preferred_element_type=F32)
            dba_ref[:, hs] += jnp.sum(dpa, axis=0, keepdims=True)
            dbx_ref[:, hs] += jnp.sum(dpx, axis=0, keepdims=True)
            dc = (gh * mult * ig
                  + lax.dot_general(dpab, wa_ref[h], NT_DIMS, preferred_element_type=F32)
                  + lax.dot_general(dpxb, wx_ref[h], NT_DIMS, preferred_element_type=F32))
            extdc[0:t, hs] = dc
            dcb_ref[:, hs] += jnp.sum(dc, axis=0, keepdims=True)
            for kk in range(CONV_WIDTH):
                off = CONV_HALO - (CONV_WIDTH - 1) + kk
                dcw_ref[kk:kk + 1, hs] += jnp.sum(dc * extl[off:off + t, hs], axis=0, keepdims=True)
        du_lru = cw_ref[0:1, :] * extdc[CONV_WIDTH - 1:CONV_WIDTH - 1 + t, :]
        for kk in range(1, CONV_WIDTH):
            off = CONV_WIDTH - 1 - kk
            du_lru = du_lru + cw_ref[kk:kk + 1, :] * extdc[off:off + t, :]
        dproj_ref[:, p:2 * p] = du_lru.astype(BF16)
        extdc[t:t + CONV_HALO, :] = extdc[0:CONV_HALO, :]

        extp[0:POOL_HALO, :] = jnp.where(seq_start, 0.0, upp_ref[...])
        extp[POOL_HALO:POOL_HALO + t, :] = up_ref[...]
        for g, w in enumerate(POOL_WINDOWS):
            cs = slice(g * cg, (g + 1) * cg)
            cnt = jnp.minimum(t_idx + 1, w).astype(F32)
            mixed = (_window_sum(extp, POOL_HALO, t, cs, w, -1) / cnt - up_ref[:, cs]).astype(BF16)
            pre = jnp.dot(mixed, wp_ref[g], preferred_element_type=F32) + bp_ref[:, cs]
            dyp = dyp_ref[:, cs]
            dps_ref[:, cs] += jnp.sum(dyp * pre, axis=0, keepdims=True)
            dpre = dyp * ps_ref[:, cs]
            dpreb = dpre.astype(BF16)
            dbp_ref[:, cs] += jnp.sum(dpre, axis=0, keepdims=True)
            dwp_ref[g] += lax.dot_general(mixed, dpreb, TN_DIMS, preferred_element_type=F32)
            dmixed = lax.dot_general(dpreb, wp_ref[g], NT_DIMS, preferred_element_type=F32)
            extg[0:t, cs] = dmixed / cnt
            dproj_ref[:, cs] = (_window_sum(extg, 0, t, cs, w, 1) - dmixed).astype(BF16)
        extg[t:t + POOL_HALO, :] = extg[0:POOL_HALO, :]

    def col(j):
        return pl.BlockSpec((t, p), lambda i: (nt - 1 - i, j))

    def prev(rows, j):
        per = t // rows
        return pl.BlockSpec((rows, p), lambda i: (jnp.maximum((nt - 1 - i) * per - 1, 0), j))

    def whole(a):
        nd = a.ndim
        return pl.BlockSpec(a.shape, lambda i: (0,) * nd)

    consts = (wp, bp, ps, cw, cb, wa, ba, wx, bx, lam)
    grads = (wp, bp, ps, cw, cb, wa, ba, wx, bx, lam)
    tile = pltpu.VMEM((t, p), F32)
    return _call(
        body, grid=(nt,),
        in_specs=[col(0), col(1), col(0), col(1), col(2), prev(POOL_HALO, 0), prev(CONV_HALO, 1), col(0),
                  prev(SUBLANES, 0)] + [whole(a) for a in consts],
        out_specs=[pl.BlockSpec((t, 3 * p), lambda i: (nt - 1 - i, 0))] + [whole(a) for a in grads],
        out_shape=[jax.ShapeDtypeStruct((s, 3 * p), BF16)] + [jax.ShapeDtypeStruct(a.shape, F32) for a in grads],
        scratch_shapes=[pltpu.VMEM((t + POOL_HALO, p), F32), pltpu.VMEM((t + POOL_HALO, p), F32),
                        pltpu.VMEM((t + CONV_HALO, p), F32), pltpu.VMEM((t + CONV_HALO, p), F32),
                        pltpu.VMEM((t + SUBLANES, p), F32), pltpu.VMEM((SUBLANES, p), F32),
                        tile, tile, tile, tile, tile, tile],
        semantics=("arbitrary",), name=name,
        args=(dycat, dycat, proj, proj, proj, proj, proj, hsave, hsave, *consts), jobs=jobs)


def _pair_add(parts, got, core, name):
    n, r, c = got.shape
    tr = _tile(TILES["add"], r)

    def body(core_ref, a_ref, b_ref, o_ref):
        del core_ref
        o_ref[...] = (a_ref[...].astype(F32) + b_ref[...].astype(F32)).astype(o_ref.dtype)

    blk = pl.BlockSpec((None, tr, c), lambda k, i, core_ref: (k, i, 0))
    mine = pl.BlockSpec((None, tr, c), lambda k, i, core_ref: (2 * k + core_ref[0], i, 0))
    return _call(body, grid=(n, r // tr), in_specs=[mine, blk], out_specs=[blk],
                 out_shape=[jax.ShapeDtypeStruct(got.shape, got.dtype)], semantics=("parallel", "parallel"),
                 name=name, args=(parts, got), index=core)[0]


def _sum_parts(parts, name):
    n, r, c = parts.shape
    tr = _tile(TILES["adam"], r)

    def body(p_ref, o_ref):
        acc = p_ref[0].astype(F32)
        for d in range(1, n):
            acc = acc + p_ref[d].astype(F32)
        o_ref[...] = acc

    return _call(
        body, grid=(r // tr,), in_specs=[pl.BlockSpec((n, tr, c), lambda i: (0, i, 0))],
        out_specs=[pl.BlockSpec((tr, c), lambda i: (i, 0))], out_shape=[jax.ShapeDtypeStruct((r, c), F32)],
        semantics=("parallel",), name=name, args=(parts,))[0]


def _adamw(w, m, v, parts, name, jobs=(), own=None, chip=None):
    r, c = w.shape
    n = parts.shape[0]
    tr = _tile(TILES["adam"], r)

    def body(*refs):
        if own is not None:
            refs = refs[1:]
            own_ref, refs = refs[3], refs[:3] + refs[4:]
        w_ref, m_ref, v_ref, p_ref, g_ref, d_ref, nm_ref, nv_ref = refs
        g = p_ref[0].astype(F32)
        if own is not None:
            g = own_ref[...].astype(F32) + g
        for d in range(1, n):
            g = g + p_ref[d].astype(F32)
        nm = ADAM_B1 * m_ref[...] + (1.0 - ADAM_B1) * g
        nv = ADAM_B2 * v_ref[...] + (1.0 - ADAM_B2) * (g * g)
        m_hat = nm / (1.0 - ADAM_B1 ** ADAM_STEP)
        v_hat = nv / (1.0 - ADAM_B2 ** ADAM_STEP)
        g_ref[...] = g
        d_ref[...] = -ADAM_LR * (m_hat / (jnp.sqrt(v_hat) + ADAM_EPS) + ADAM_WD * w_ref[...])
        nm_ref[...] = nm
        nv_ref[...] = nv

    row = pl.BlockSpec((tr, c), lambda i, *_: (i, 0))
    in_specs, args = [row, row, row], [w, m, v]
    if own is not None:
        in_specs.append(pl.BlockSpec((None, tr, c), lambda i, chip_ref: (chip_ref[0], i, 0)))
        args.append(own)
    in_specs.append(pl.BlockSpec((n, tr, c), lambda i, *_: (0, i, 0)))
    args.append(parts)
    return _call(
        body, grid=(r // tr,), in_specs=in_specs, out_specs=[row] * 4, out_shape=[jax.ShapeDtypeStruct((r, c), F32)] * 4,
        semantics=("parallel",), name=name, args=args, jobs=jobs, index=chip if own is not None else None)


SMALL_ORDER = ("w_a", "w_x", "conv_w", "b_pool", "conv_b", "b_a", "b_x", "lru_lambda", "pool_scale",
               "ln1_g", "ln1_b", "ln2_g", "ln2_b", "ln3_g", "ln3_b")


def _pack_rows(a, p):
    flat = a.reshape(-1, p)
    pad = (-flat.shape[0]) % SUBLANES
    return jnp.pad(flat, ((0, pad), (0, 0))) if pad else flat


def kernel(x, mem, w_in, conv_w, conv_b, w_a, b_a, w_x, b_x, lru_lambda, w_pool, b_pool, pool_scale, w_out, ln1_g, ln1_b, w_q, w_k, w_v, w_o, ln2_g, ln2_b, w_ff1, w_ff2, ln3_g, ln3_b, loss_target, m_w_in, m_conv_w, m_conv_b, m_w_a, m_b_a, m_w_x, m_b_x, m_lru_lambda, m_w_pool, m_b_pool, m_pool_scale, m_w_out, m_ln1_g, m_ln1_b, m_w_q, m_w_k, m_w_v, m_w_o, m_ln2_g, m_ln2_b, m_w_ff1, m_w_ff2, m_ln3_g, m_ln3_b, v_w_in, v_conv_w, v_conv_b, v_w_a, v_b_a, v_w_x, v_b_x, v_lru_lambda, v_w_pool, v_b_pool, v_pool_scale, v_w_out, v_ln1_g, v_ln1_b, v_w_q, v_w_k, v_w_v, v_w_o, v_ln2_g, v_ln2_b, v_w_ff1, v_w_ff2, v_ln3_g, v_ln3_b):
    names = ("w_in", "conv_w", "conv_b", "w_a", "b_a", "w_x", "b_x", "lru_lambda", "w_pool", "b_pool", "pool_scale",
             "w_out", "ln1_g", "ln1_b", "w_q", "w_k", "w_v", "w_o", "ln2_g", "ln2_b", "w_ff1", "w_ff2", "ln3_g", "ln3_b")
    w_loc = dict(zip(names, (w_in, conv_w, conv_b, w_a, b_a, w_x, b_x, lru_lambda, w_pool, b_pool, pool_scale,
                             w_out, ln1_g, ln1_b, w_q, w_k, w_v, w_o, ln2_g, ln2_b, w_ff1, w_ff2, ln3_g, ln3_b)))
    m_loc = dict(zip(names, (m_w_in, m_conv_w, m_conv_b, m_w_a, m_b_a, m_w_x, m_b_x, m_lru_lambda, m_w_pool, m_b_pool,
                             m_pool_scale, m_w_out, m_ln1_g, m_ln1_b, m_w_q, m_w_k, m_w_v, m_w_o, m_ln2_g, m_ln2_b,
                             m_w_ff1, m_w_ff2, m_ln3_g, m_ln3_b)))
    v_loc = dict(zip(names, (v_w_in, v_conv_w, v_conv_b, v_w_a, v_b_a, v_w_x, v_b_x, v_lru_lambda, v_w_pool, v_b_pool,
                             v_pool_scale, v_w_out, v_ln1_g, v_ln1_b, v_w_q, v_w_k, v_w_v, v_w_o, v_ln2_g, v_ln2_b,
                             v_w_ff1, v_w_ff2, v_ln3_g, v_ln3_b)))
    s, d = x.shape[1], x.shape[2]
    p = conv_b.shape[1]
    cg = p // N_POOL_GROUPS
    hd = p // LRU_HEADS
    me = 4 * lax.axis_index("x") + 2 * lax.axis_index("y") + lax.axis_index("c")

    xs, mems, tgt = x[0], mem[0], loss_target[0]
    memb = mems.astype(BF16)

    gathers = {n: _Job("gather", w_loc[n][0].astype(BF16))
               for n in ("w_in", "w_out", "w_q", "w_k", "w_v", "w_o", "w_ff2", "w_pool")}
    ff1_shard = w_ff1[0].astype(BF16)
    ff1_rows = ff1_shard.shape[0] // FF1_PIECES

    def ff1_piece(i, earlier=None):
        return _Job("gather", ff1_shard[i * ff1_rows:(i + 1) * ff1_rows], window=(i * ff1_rows, ff1_shard.shape[0]),
                    into=None if earlier is None else earlier.out)
    tiny = jnp.concatenate([_pack_rows(conv_w[0], p // N_DEV),
                            _pack_rows(jnp.pad(b_pool[0], ((0, 0), (0, p // N_DEV - cg // N_DEV))), p // N_DEV)], axis=0)
    gathers["tiny"] = _Job("gather", tiny)

    def gathered(n):
        full = gathers[n].out
        if n == "w_in":
            return jnp.transpose(full, (1, 0, 2)).reshape(1, full.shape[1], -1)
        return full.reshape(1, -1, full.shape[-1])

    W = {"conv_b": conv_b, "b_a": b_a.reshape(1, p), "b_x": b_x.reshape(1, p), "lru_lambda": lru_lambda,
         "pool_scale": pool_scale, "w_a": w_a[0].astype(BF16), "w_x": w_x[0].astype(BF16)}
    for n in ("ln1_g", "ln1_b", "ln2_g", "ln2_b", "ln3_g", "ln3_b"):
        W[n] = w_loc[n]

    out_g, out_d, out_m, out_v = {}, {}, {}, {}
    pairs, quads, sums = {}, {}, {}
    core =lax.axis_index("c").astype(jnp.int32).reshape(1)
    chip = (2 * lax.axis_index("x") + lax.axis_index("y")).astype(jnp.int32).reshape(1)

    def pair(n, partial):
        pairs[n] = _Job("pair", partial.reshape(N_DEV, -1, partial.shape[-1]))
        return pairs[n]

    def quad(n, part=0, parts=1):
        if part == 0:
            sums[n] = _pair_add(pairs[n].src, pairs[n].out, core, "add_" + n)
        rows = sums[n].shape[1] // parts
        quads[n] = _Job("quad", sums[n], window=None if parts == 1 else (part * rows, rows),
                        into=None if part == 0 else quads[n].out)
        return quads[n]

    def update(n, parts, jobs=(), own=None):
        shp = w_loc[n].shape
        rows = parts.shape[1]
        w2, m2, v2 = (a.reshape(rows, -1) for a in (w_loc[n], m_loc[n], v_loc[n]))
        res = _adamw(w2, m2, v2, parts.reshape(parts.shape[0], rows, -1), "adamw_" + n, jobs=jobs, own=own, chip=chip)
        out_g[n], out_d[n], out_m[n], out_v[n] = (r.reshape(shp) for r in res)

    assert FF1_PIECES == 4
    xb = _to_bf16(xs, "cast_x", jobs=[gathers["w_in"], gathers["tiny"], gathers["w_pool"]])
    W["w_pool"] = jnp.transpose(gathers["w_pool"].out, (1, 0, 2, 3)).reshape(N_POOL_GROUPS, cg, cg)
    cwb = gathers["tiny"].out
    W["conv_w"] = jnp.transpose(cwb[:, :CONV_WIDTH, :], (1, 0, 2)).reshape(CONV_WIDTH, p)
    W["b_pool"] = jnp.transpose(cwb[:, SUBLANES:SUBLANES + N_POOL_GROUPS, :cg // N_DEV], (1, 0, 2)).reshape(1, p)
    mixer_consts = (W["w_pool"], W["b_pool"], W["pool_scale"], W["conv_w"], W["conv_b"], W["w_a"], W["b_a"],
                    W["w_x"], W["b_x"], W["lru_lambda"])

    w_in_full = gathered("w_in")
    piece = ff1_piece(FF1_PIECES - 1)
    (proj,) = _mm_nn(xb, w_in_full, [F32], "fwd_proj", jobs=[gathers["w_out"], piece])
    ycat, hsave = _mixer_fwd(proj, *mixer_consts, "fwd_mixer", jobs=[gathers["w_q"], gathers["w_k"]])
    (y1,) = _mm_nn(ycat, gathered("w_out"), [F32], "fwd_out", jobs=[gathers["w_v"]])
    x1, x1b, xhat1, rstd1 = _ln_fwd(xs, y1, W["ln1_g"], W["ln1_b"], "fwd_ln1", jobs=[gathers["w_o"]])
    piece = ff1_piece(0, piece)
    (q,) = _mm_nn(x1b, gathered("w_q"), [BF16], "fwd_q", jobs=[piece])
    (k,) = _mm_nn(memb, gathered("w_k"), [BF16], "fwd_k")
    (v,) = _mm_nn(memb, gathered("w_v"), [BF16], "fwd_v")
    o = _attn_fwd(q, k, v, "fwd_attn")
    piece = ff1_piece(1, piece)
    (y2,) = _mm_nn(o, gathered("w_o"), [F32], "fwd_o", jobs=[piece])
    piece = ff1_piece(2, piece)
    x2, x2b, xhat2, rstd2 = _ln_fwd(x1, y2, W["ln2_g"], W["ln2_b"], "fwd_ln2", jobs=[piece])
    w_ff1_full = piece.out

    def relu_sq(acc):
        r = jnp.maximum(acc, 0.0)
        return r, r * r

    rb, act = _mm_nn(x2b, w_ff1_full, [BF16, BF16], "fwd_ff1", epilogue=relu_sq, jobs=[gathers["w_ff2"]])
    (y3,) = _mm_nn(act, gathered("w_ff2"), [F32], "fwd_ff2")
    loss_rows, dz3, dz3b, dg3, db3 = _ln_loss(x2, y3, W["ln3_g"], W["ln3_b"], tgt, "ln3_loss")
    loss = lax.psum(loss_rows[0, 0], MESH_AXES)

    small = {"ln3_g": dg3, "ln3_b": db3}

    def add_residual(acc, e):
        return (acc + ALPHA * e,)

    dw_ff2 = _mm_tn(act, dz3b, 1, BF16, "bwd_dw_ff2")
    (dhid,) = _mm_nt(dz3b, gathered("w_ff2"), [BF16], "bwd_dact", extras=(rb,), jobs=[pair("w_ff2", dw_ff2)],
                     epilogue=lambda acc, r: (acc * (2.0 * r.astype(F32)),))
    dw_ff1 = _mm_tn(x2b, dhid, N_DEV, BF16, "bwd_dw_ff1", jobs=[quad("w_ff2", 0, 2)])
    (dx2,) = _mm_nt(dhid, w_ff1_full, [F32], "bwd_dx2", epilogue=add_residual, extras=(dz3,), tk=TILES["tk"] // 2,
                    jobs=[pair("w_ff1", dw_ff1), quad("w_ff2", 1, 2)])
    dz2, dz2b, small["ln2_g"], small["ln2_b"] = _ln_bwd(dx2, xhat2, rstd2, W["ln2_g"], "bwd_ln2",
                                                        jobs=[quad("w_ff1", 0, 4)])

    dw_o = _mm_tn(o, dz2b, 1, BF16, "bwd_dw_o", jobs=[quad("w_ff1", 1, 4)])
    (do,) = _mm_nt(dz2b, gathered("w_o"), [BF16], "bwd_do", jobs=[pair("w_o", dw_o)])
    dq, dk, dv = _attn_bwd(q, k, v, do, "bwd_attn", jobs=[quad("w_o")])
    dw_q = _mm_tn(x1b, dq, 1, BF16, "bwd_dw_q", jobs=[quad("w_ff1", 2, 4)])
    dw_k = _mm_tn(memb, dk.astype(BF16), 1, BF16, "bwd_dw_k")
    dw_v = _mm_tn(memb, dv.astype(BF16), 1, BF16, "bwd_dw_v")
    (dx1,) = _mm_nt(dq, gathered("w_q"), [F32], "bwd_dx1", epilogue=add_residual, extras=(dz2,),
                    jobs=[pair("w_q", dw_q), pair("w_k", dw_k), pair("w_v", dw_v), quad("w_ff1", 3, 4)])
    dz1, dz1b, small["ln1_g"], small["ln1_b"] = _ln_bwd(dx1, xhat1, rstd1, W["ln1_g"], "bwd_ln1", jobs=[quad("w_q")])

    dw_out = _mm_tn(ycat, dz1b, 1, BF16, "bwd_dw_out", jobs=[quad("w_k")])
    (dycat,) = _mm_nt(dz1b, gathered("w_out"), [F32], "bwd_dycat", jobs=[quad("w_v"), pair("w_out", dw_out)])
    (dproj, dwp, small["b_pool"], small["pool_scale"], small["conv_w"], small["conv_b"], small["w_a"], small["b_a"],
     small["w_x"], small["b_x"], small["lru_lambda"]) = _mixer_bwd(
        dycat, proj, hsave, *mixer_consts, "bwd_mixer")
    dw_pool = jnp.transpose(dwp.astype(BF16).reshape(N_POOL_GROUPS, N_DEV, cg // N_DEV, cg), (1, 0, 2, 3))
    pack = jnp.concatenate([_pack_rows(small[n], p) for n in SMALL_ORDER], axis=0)
    small_gather = _Job("gather", pack)
    dw_in = _mm_tn(xb, dproj, 1, BF16, "bwd_dw_in", jobs=[quad("w_out"), pair("w_pool", dw_pool), small_gather])
    dw_in = jnp.transpose(dw_in.reshape(dw_in.shape[1], N_DEV, -1), (1, 0, 2))
    (grad_x,) = _mm_nt(dproj, w_in_full, [F32], "bwd_dx", epilogue=add_residual, extras=(dz1,),
                       jobs=[pair("w_in", dw_in), quad("w_pool")])

    update("w_ff2", quads["w_ff2"].out, jobs=[quad("w_in")], own=quads["w_ff2"].src)
    for n in ("w_ff1", "w_o", "w_q", "w_k", "w_v", "w_out", "w_pool", "w_in"):
        update(n, quads[n].out, own=quads[n].src)

    total = _sum_parts(small_gather.out, "sum_small")
    row = 0
    for n in SMALL_ORDER:
        size = small[n].size
        nrows = size // p
        g_full = total[row:row + nrows].reshape(small[n].shape)
        row += nrows + (-nrows) % SUBLANES
        if n == "conv_w":
            g_loc = lax.dynamic_slice_in_dim(g_full, me * (p // N_DEV), p // N_DEV, axis=1)
        elif n == "b_pool":
            g_loc = lax.dynamic_slice_in_dim(g_full.reshape(N_POOL_GROUPS, cg), me * (cg // N_DEV), cg // N_DEV, axis=1)
        else:
            g_loc = g_full
        rows = g_loc.shape[0] if n not in ("w_a", "w_x") else LRU_HEADS * hd
        update(n, g_loc.reshape(1, rows, -1))

    order = names
    return (loss, grad_x[None], *[out_g[n] for n in order], *[out_d[n] for n in order],
            *[out_m[n] for n in order], *[out_v[n] for n in order])
```

```python
import functools

import jax
import jax.numpy as jnp
from jax import lax
from jax.experimental import pallas as pl
from jax.experimental.pallas import tpu as pltpu

F32 = jnp.float32
BF16 = jnp.bfloat16

N_DEV = 8
MESH_AXES = ("x", "y", "c")
POOL_WINDOWS = (2, 4, 8, 16)
N_POOL_GROUPS = len(POOL_WINDOWS)
POOL_HALO = 16
CONV_WIDTH = 4
CONV_HALO = 8
FF1_PIECES = 4
LRU_HEADS = 8
LRU_C = 8.0
XATTN_HEADS = 4
LN_EPS = 1e-5
ALPHA = 2.0 ** 0.25
ADAM_LR = 0.001
ADAM_B1 = 0.9
ADAM_B2 = 0.999
ADAM_EPS = 1e-08
ADAM_WD = 0.01
ADAM_STEP = 10
SUBLANES = 8
VMEM_LIMIT = 56 * 1024 * 1024

NT_DIMS = (((1,), (1,)), ((), ()))
TN_DIMS = (((0,), (0,)), ((), ()))


def _params(*sem):
    return pltpu.CompilerParams(dimension_semantics=sem, vmem_limit_bytes=VMEM_LIMIT)


def _place():
    return lax.axis_index("x"), lax.axis_index("y"), lax.axis_index("c")


def _remote(src, dst, send_sem, recv_sem, to):
    return pltpu.make_async_remote_copy(src_ref=src, dst_ref=dst, send_sem=send_sem, recv_sem=recv_sem,
                                        device_id=to, device_id_type=pl.DeviceIdType.MESH)


class _Job:
    def __init__(self, kind, src, window=None, into=None):
        self.kind, self.src, self.out, self.window, self.into = kind, src, None, window, into

    def out_shape(self):
        s = self.src.shape
        if self.kind == "gather" and self.window is not None:
            s = (self.window[1],) + s[1:]
        shape = {"gather": (N_DEV,) + s, "pair": (4,) + s[1:], "quad": (3,) + s[1:]}[self.kind]
        return jax.ShapeDtypeStruct(shape, self.src.dtype)

    def scratch(self):
        n = {"gather": 7, "pair": 4, "quad": 3}[self.kind]
        sems = [pltpu.SemaphoreType.DMA((n,)), pltpu.SemaphoreType.DMA((n,))]
        if self.kind == "gather":
            sems += [pltpu.SemaphoreType.DMA((2,)), pltpu.VMEM(self.src.shape, self.src.dtype)]
        return sems

    def ops(self, src, out, *scratch):
        if self.kind == "gather":
            return _gather_ops(src, out, *scratch, first_row=None if self.window is None else self.window[0])
        if self.kind == "quad":
            return _quad_ops(src, out, *scratch, rows=self.window)
        return _pair_ops(src, out, *scratch)


def _gather_ops(x_ref, out_ref, send_sems, recv_sems, local_sems, bounce, first_row=None):
    x, y, c = _place()
    me, sibling = (x, y, c), (x, y, 1 - c)
    chips = [(1 - x, y), (x, 1 - y), (1 - x, 1 - y)]

    def slot(px, py, pc):
        block = out_ref.at[4 * px + 2 * py + pc]
        return block if first_row is None else block.at[pl.ds(first_row, x_ref.shape[0])]

    def copy(k, block, to, src=None):
        return _remote(slot(*block) if src is None else src, slot(*block), send_sems.at[k], recv_sems.at[k], to)

    mine_in = pltpu.make_async_copy(x_ref, bounce, local_sems.at[0])
    mine_out = pltpu.make_async_copy(bounce, slot(*me), local_sems.at[1])
    first = [copy(0, me, sibling, src=x_ref)] + [copy(1 + j, me, (*chip, c), src=x_ref) for j, chip in enumerate(chips)]
    passed = [copy(4 + j, (*chip, c), sibling) for j, chip in enumerate(chips)]

    def start():
        mine_in.start()
        for cp in first:
            cp.start()

    def mid():
        mine_in.wait()
        mine_out.start()
        for j, chip in enumerate(chips):
            copy(1 + j, (*chip, c), me).wait_recv()
            passed[j].start()

    def finish():
        copy(0, sibling, me).wait_recv()
        for j, chip in enumerate(chips):
            copy(4 + j, (*chip, 1 - c), me).wait_recv()
        for cp in first + passed:
            cp.wait_send()
        mine_out.wait()

    return start, mid, finish


def _pair_ops(p_ref, got_ref, send_sems, recv_sems):
    x, y, c = _place()
    give = [_remote(p_ref.at[2 * k + 1 - c], got_ref.at[k], send_sems.at[k], recv_sems.at[k], (x, y, 1 - c))
            for k in range(4)]

    def start():
        for cp in give:
            cp.start()

    def finish():
        for cp in give:
            cp.wait_recv()
        for cp in give:
            cp.wait_send()

    return start, None, finish


def _quad_ops(q_ref, out_ref, send_sems, recv_sems, rows=None):
    x, y, c = _place()

    def part(block):
        return block if rows is None else block.at[pl.ds(rows[0], rows[1])]

    copies = []
    for rel in range(1, 4):
        px = 1 - x if rel & 2 else x
        py = 1 - y if rel & 1 else y
        copies.append(_remote(part(q_ref.at[2 * px + py]), part(out_ref.at[rel - 1]), send_sems.at[rel - 1],
                              recv_sems.at[rel - 1], (px, py, c)))

    def start():
        for cp in copies:
            cp.start()

    def finish():
        for cp in copies:
            cp.wait_recv()
        for cp in copies:
            cp.wait_send()

    return start, None, finish


def _call(body, *, grid, in_specs, out_specs, out_shape, scratch_shapes=(), semantics, name, args, jobs=(), index=None):
    in_specs, out_specs, out_shape = list(in_specs), list(out_specs), list(out_shape)
    scratch_shapes, jobs = list(scratch_shapes), list(jobs)
    n_in, n_out, n_scr, n_job = len(in_specs), len(out_specs), len(scratch_shapes), len(jobs)
    n_idx = 0 if index is None else 1
    job_scratch = [j.scratch() for j in jobs]
    n_steps = functools.reduce(lambda a, b: a * b, grid, 1)
    early = n_steps - 1 - max(1, n_steps // 8) if n_steps >= 4 else None

    intos = [(k, j.into) for k, j in enumerate(jobs) if j.into is not None]

    def hosted(*refs):
        idx, refs = refs[:n_idx], refs[n_idx:]
        ins, jin = refs[:n_in], refs[n_in:n_in + n_job]
        o0 = n_in + n_job + len(intos)
        outs, jout = refs[o0:o0 + n_out], refs[o0 + n_out:o0 + n_out + n_job]
        s0 = o0 + n_out + n_job
        scr, jscr = refs[s0:s0 + n_scr], refs[s0 + n_scr:]
        ops, at = [], 0
        for k, j in enumerate(jobs):
            ops.append(j.ops(jin[k], jout[k], *jscr[at:at + len(job_scratch[k])]))
            at += len(job_scratch[k])
        step = functools.reduce(lambda acc, a: acc * grid[a] + pl.program_id(a), range(len(grid)), 0)
        mids = [mid for _, mid, _ in ops if mid is not None]

        @pl.when(step == 0)
        def _():
            for start, _, _ in ops:
                start()

        if mids and early is not None:
            @pl.when(step == early)
            def _():
                for mid in mids:
                    mid()

        body(*idx, *ins, *outs, *scr)

        @pl.when(step == n_steps - 1)
        def _():
            if early is None:
                for mid in mids:
                    mid()
            for _, _, finish in ops:
                finish()

    hbm = pl.BlockSpec(memory_space=pl.ANY)
    spec = pltpu.PrefetchScalarGridSpec(
        num_scalar_prefetch=n_idx, grid=grid, in_specs=in_specs + [hbm] * (n_job + len(intos)),
        out_specs=out_specs + [hbm] * n_job, scratch_shapes=scratch_shapes + [s for js in job_scratch for s in js])
    aliases = {n_idx + n_in + n_job + q: n_out + k for q, (k, _) in enumerate(intos)}
    res = pl.pallas_call(
        hosted if jobs else body, grid_spec=spec, out_shape=out_shape + [j.out_shape() for j in jobs],
        input_output_aliases=aliases,
        compiler_params=_params(*(["arbitrary"] * len(grid) if jobs else semantics)), name=name,
    )(*([] if index is None else [index]), *args, *[j.src for j in jobs], *[buf for _, buf in intos])
    for j, o in zip(jobs, res[n_out:]):
        j.out = o
    return res[:n_out]


def _to_bf16(a, name, jobs=()):
    r, c = a.shape
    tr = _tile(TILES["row"], r)

    def body(a_ref, o_ref):
        o_ref[...] = a_ref[...].astype(BF16)

    row = pl.BlockSpec((tr, c), lambda i: (i, 0))
    return _call(body, grid=(r // tr,), in_specs=[row], out_specs=[row], out_shape=[jax.ShapeDtypeStruct((r, c), BF16)],
                 semantics=("parallel",), name=name, args=(a,), jobs=jobs)[0]


TILES = dict(tm=1024, tn=1024, tk=2048, row=256, attn=512, mixer=256, adam=128, add=1024)


def _tile(pref, n):
    for t in range(min(pref, n), 0, -1):
        if n % t == 0 and (t % SUBLANES == 0 or t == n):
            return t
    return n


def _accumulate(acc, step, n_steps, product, write):
    if n_steps == 1:
        write(product())
        return

    @pl.when(step == 0)
    def _():
        acc[...] = product()

    @pl.when(jnp.logical_and(step > 0, step < n_steps - 1))
    def _():
        acc[...] += product()

    @pl.when(step == n_steps - 1)
    def _():
        write(acc[...] + product())


def _acc_scratch(n_steps, tm, tn):
    return [] if n_steps == 1 else [pltpu.VMEM((tm, tn), F32)]


def _mm_nn(a, b3, out_dtypes, name, *, tm=None, tn=None, tk=None, epilogue=None, extras=(), jobs=()):
    m, k = a.shape
    g, k2, ns = b3.shape
    assert k == k2
    n = g * ns
    tm, tn, tk = _tile(tm or TILES["tm"], m), _tile(tn or TILES["tn"], ns), _tile(tk or TILES["tk"], k)
    nb, nk = ns // tn, k // tk
    n_ex, n_out = len(extras), len(out_dtypes)

    def body(*refs):
        a_ref, b_ref = refs[:2]
        ex = refs[2:2 + n_ex]
        outs = refs[2 + n_ex:2 + n_ex + n_out]
        acc = refs[-1] if nk > 1 else None

        def write(r):
            res = epilogue(r, *[e[...] for e in ex]) if epilogue is not None else (r,)
            for o, v in zip(outs, res):
                o[...] = v.astype(o.dtype)

        _accumulate(acc, pl.program_id(2), nk,
                    lambda: jnp.dot(a_ref[...], b_ref[...], preferred_element_type=F32), write)

    tile_out = pl.BlockSpec((tm, tn), lambda i, j, kk: (i, j))
    return _call(
        body, grid=(m // tm, n // tn, nk),
        in_specs=[pl.BlockSpec((tm, tk), lambda i, j, kk: (i, kk)),
                  pl.BlockSpec((None, tk, tn), lambda i, j, kk: (j // nb, kk, j % nb))] + [tile_out] * n_ex,
        out_specs=[tile_out] * n_out,
        out_shape=[jax.ShapeDtypeStruct((m, n), d) for d in out_dtypes],
        scratch_shapes=_acc_scratch(nk, tm, tn),
        semantics=("parallel", "parallel", "arbitrary"), name=name, args=(a, b3, *extras), jobs=jobs)


def _mm_nt(a, b3, out_dtypes, name, *, tm=None, tn=None, tk=None, epilogue=None, extras=(), jobs=()):
    m, n = a.shape
    g, k, ns = b3.shape
    assert n == g * ns
    tm, tn, tk = _tile(tm or TILES["tm"], m), _tile(tn or TILES["tn"], k), _tile(tk or TILES["tk"], ns)
    nb, nc = ns // tk, n // tk
    n_ex, n_out = len(extras), len(out_dtypes)

    def body(*refs):
        a_ref, b_ref = refs[:2]
        ex = refs[2:2 + n_ex]
        outs = refs[2 + n_ex:2 + n_ex + n_out]
        acc = refs[-1] if nc > 1 else None

        def write(r):
            res = epilogue(r, *[e[...] for e in ex]) if epilogue is not None else (r,)
            for o, v in zip(outs, res):
                o[...] = v.astype(o.dtype)

        _accumulate(acc, pl.program_id(2), nc,
                    lambda: lax.dot_general(a_ref[...], b_ref[...], NT_DIMS, preferred_element_type=F32), write)

    tile_out = pl.BlockSpec((tm, tn), lambda i, j, cc: (i, j))
    return _call(
        body, grid=(m // tm, k // tn, nc),
        in_specs=[pl.BlockSpec((tm, tk), lambda i, j, cc: (i, cc)),
                  pl.BlockSpec((None, tn, tk), lambda i, j, cc: (cc // nb, j, cc % nb))] + [tile_out] * n_ex,
        out_specs=[tile_out] * n_out,
        out_shape=[jax.ShapeDtypeStruct((m, k), d) for d in out_dtypes],
        scratch_shapes=_acc_scratch(nc, tm, tn),
        semantics=("parallel", "parallel", "arbitrary"), name=name, args=(a, b3, *extras), jobs=jobs)


def _mm_tn(a, b, g, out_dtype, name, *, tm=None, tn=None, tk=None, jobs=()):
    s, m = a.shape
    s2, n = b.shape
    assert s == s2 and n % g == 0
    ns = n // g
    tm, tn, tk = _tile(tm or TILES["tm"], m), _tile(tn or TILES["tn"], ns), _tile(tk or TILES["tk"], s)
    nb, nc = ns // tn, s // tk

    def body(a_ref, b_ref, o_ref, *scratch):
        def write(r):
            o_ref[...] = r.astype(o_ref.dtype)

        _accumulate(scratch[0] if nc > 1 else None, pl.program_id(2), nc,
                    lambda: lax.dot_general(a_ref[...], b_ref[...], TN_DIMS, preferred_element_type=F32), write)

    return _call(
        body, grid=(m // tm, n // tn, nc),
        in_specs=[pl.BlockSpec((tk, tm), lambda i, j, cc: (cc, i)),
                  pl.BlockSpec((tk, tn), lambda i, j, cc: (cc, j))],
        out_specs=[pl.BlockSpec((None, tm, tn), lambda i, j, cc: (j // nb, i, j % nb))],
        out_shape=[jax.ShapeDtypeStruct((g, m, ns), out_dtype)],
        scratch_shapes=_acc_scratch(nc, tm, tn),
        semantics=("parallel", "parallel", "arbitrary"), name=name, args=(a, b), jobs=jobs)[0]


def _ln_stats(z):
    mu = jnp.mean(z, axis=-1, keepdims=True)
    zc = z - mu
    var = jnp.mean(zc * zc, axis=-1, keepdims=True)
    rstd = lax.rsqrt(var + LN_EPS)
    return zc * rstd, rstd


def _ln_grad(dout, xhat, rstd, gain):
    dxhat = dout * gain
    m1 = jnp.mean(dxhat, axis=-1, keepdims=True)
    m2 = jnp.mean(dxhat * xhat, axis=-1, keepdims=True)
    return rstd * (dxhat - m1 - xhat * m2)


def _ln_fwd(xres, y, gain, bias, name, jobs=()):
    s, d = xres.shape
    tr = _tile(TILES["row"], s)

    def body(x_ref, y_ref, g_ref, b_ref, xn_ref, xnb_ref, xhat_ref, rstd_ref):
        xhat, rstd = _ln_stats(ALPHA * x_ref[...] + y_ref[...])
        out = xhat * g_ref[...] + b_ref[...]
        xn_ref[...] = out
        xnb_ref[...] = out.astype(BF16)
        xhat_ref[...] = xhat
        rstd_ref[...] = rstd

    row = pl.BlockSpec((tr, d), lambda i: (i, 0))
    vec = pl.BlockSpec((1, d), lambda i: (0, 0))
    return _call(
        body, grid=(s // tr,), in_specs=[row, row, vec, vec],
        out_specs=[row, row, row, pl.BlockSpec((tr, 1), lambda i: (i, 0))],
        out_shape=[jax.ShapeDtypeStruct((s, d), F32), jax.ShapeDtypeStruct((s, d), BF16),
                   jax.ShapeDtypeStruct((s, d), F32), jax.ShapeDtypeStruct((s, 1), F32)],
        semantics=("parallel",), name=name, args=(xres, y, gain, bias), jobs=jobs)


def _ln_bwd(dout, xhat, rstd, gain, name, jobs=()):
    s, d = dout.shape
    tr = _tile(TILES["row"], s)

    def body(d_ref, xhat_ref, rstd_ref, g_ref, dz_ref, dzb_ref, dg_ref, db_ref):
        @pl.when(pl.program_id(0) == 0)
        def _():
            dg_ref[...] = jnp.zeros_like(dg_ref)
            db_ref[...] = jnp.zeros_like(db_ref)

        dout_t, xhat_t = d_ref[...], xhat_ref[...]
        dz = _ln_grad(dout_t, xhat_t, rstd_ref[...], g_ref[...])
        dz_ref[...] = dz
        dzb_ref[...] = dz.astype(BF16)
        dg_ref[...] += jnp.sum(dout_t * xhat_t, axis=0, keepdims=True)
        db_ref[...] += jnp.sum(dout_t, axis=0, keepdims=True)

    row = pl.BlockSpec((tr, d), lambda i: (i, 0))
    vec = pl.BlockSpec((1, d), lambda i: (0, 0))
    return _call(
        body, grid=(s // tr,), in_specs=[row, row, pl.BlockSpec((tr, 1), lambda i: (i, 0)), vec],
        out_specs=[row, row, vec, vec],
        out_shape=[jax.ShapeDtypeStruct((s, d), F32), jax.ShapeDtypeStruct((s, d), BF16),
                   jax.ShapeDtypeStruct((1, d), F32), jax.ShapeDtypeStruct((1, d), F32)],
        semantics=("arbitrary",), name=name, args=(dout, xhat, rstd, gain), jobs=jobs)


def _ln_loss(xres, y, gain, bias, target, name, jobs=()):
    s, d = xres.shape
    tr = _tile(TILES["row"], s)

    def body(x_ref, y_ref, g_ref, b_ref, t_ref, loss_ref, dz_ref, dzb_ref, dg_ref, db_ref):
        @pl.when(pl.program_id(0) == 0)
        def _():
            loss_ref[...] = jnp.zeros_like(loss_ref)
            dg_ref[...] = jnp.zeros_like(dg_ref)
            db_ref[...] = jnp.zeros_like(db_ref)

        xhat, rstd = _ln_stats(ALPHA * x_ref[...] + y_ref[...])
        diff = xhat * g_ref[...] + b_ref[...] - t_ref[...]
        per_row = jnp.mean(diff * diff, axis=-1, keepdims=True)
        loss_ref[...] += 0.5 * jnp.sum(per_row, axis=0, keepdims=True)
        dout = diff * (1.0 / d)
        dz = _ln_grad(dout, xhat, rstd, g_ref[...])
        dz_ref[...] = dz
        dzb_ref[...] = dz.astype(BF16)
        dg_ref[...] += jnp.sum(dout * xhat, axis=0, keepdims=True)
        db_ref[...] += jnp.sum(dout, axis=0, keepdims=True)

    row = pl.BlockSpec((tr, d), lambda i: (i, 0))
    vec = pl.BlockSpec((1, d), lambda i: (0, 0))
    return _call(
        body, grid=(s // tr,), in_specs=[row, row, vec, vec, row],
        out_specs=[pl.BlockSpec((1, 128), lambda i: (0, 0)), row, row, vec, vec],
        out_shape=[jax.ShapeDtypeStruct((1, 128), F32), jax.ShapeDtypeStruct((s, d), F32),
                   jax.ShapeDtypeStruct((s, d), BF16), jax.ShapeDtypeStruct((1, d), F32),
                   jax.ShapeDtypeStruct((1, d), F32)],
        semantics=("arbitrary",), name=name, args=(xres, y, gain, bias, target), jobs=jobs)


def _softmax_rows(s):
    e = jnp.exp(s - jnp.max(s, axis=-1, keepdims=True))
    return e / jnp.sum(e, axis=-1, keepdims=True)


def _attn_fwd(q, k, v, name, jobs=()):
    s, d = q.shape
    m = k.shape[0]
    hd = d // XATTN_HEADS
    ts = _tile(TILES["attn"], s)
    scale = hd ** -0.5

    def body(q_ref, k_ref, v_ref, o_ref):
        for h in range(XATTN_HEADS):
            hs = slice(h * hd, (h + 1) * hd)
            sc = lax.dot_general(q_ref[:, hs], k_ref[:, hs], NT_DIMS, preferred_element_type=F32) * scale
            p = _softmax_rows(sc).astype(BF16)
            o_ref[:, hs] = jnp.dot(p, v_ref[:, hs], preferred_element_type=F32).astype(BF16)

    row = pl.BlockSpec((ts, d), lambda i: (i, 0))
    memb = pl.BlockSpec((m, d), lambda i: (0, 0))
    return _call(
        body, grid=(s // ts,), in_specs=[row, memb, memb], out_specs=[row],
        out_shape=[jax.ShapeDtypeStruct((s, d), BF16)],
        semantics=("parallel",), name=name, args=(q, k, v), jobs=jobs)[0]


def _attn_bwd(q, k, v, do, name, jobs=()):
    s, d = q.shape
    m = k.shape[0]
    hd = d // XATTN_HEADS
    ts = _tile(TILES["attn"], s)
    scale = hd ** -0.5

    def body(q_ref, k_ref, v_ref, do_ref, dq_ref, dk_ref, dv_ref):
        @pl.when(pl.program_id(0) == 0)
        def _():
            dk_ref[...] = jnp.zeros_like(dk_ref)
            dv_ref[...] = jnp.zeros_like(dv_ref)

        for h in range(XATTN_HEADS):
            hs = slice(h * hd, (h + 1) * hd)
            qh, kh, vh, doh = q_ref[:, hs], k_ref[:, hs], v_ref[:, hs], do_ref[:, hs]
            sc = lax.dot_general(qh, kh, NT_DIMS, preferred_element_type=F32) * scale
            p = _softmax_rows(sc)
            pb = p.astype(BF16)
            dp = lax.dot_general(doh, vh, NT_DIMS, preferred_element_type=F32)
            ds = (p * (dp - jnp.sum(dp * p, axis=-1, keepdims=True)) * scale).astype(BF16)
            dq_ref[:, hs] = jnp.dot(ds, kh, preferred_element_type=F32).astype(BF16)
            dk_ref[:, hs] += lax.dot_general(ds, qh, TN_DIMS, preferred_element_type=F32)
            dv_ref[:, hs] += lax.dot_general(pb, doh, TN_DIMS, preferred_element_type=F32)

    row = pl.BlockSpec((ts, d), lambda i: (i, 0))
    memb = pl.BlockSpec((m, d), lambda i: (0, 0))
    return _call(
        body, grid=(s // ts,), in_specs=[row, memb, memb, row], out_specs=[row, memb, memb],
        out_shape=[jax.ShapeDtypeStruct((s, d), BF16), jax.ShapeDtypeStruct((m, d), F32),
                   jax.ShapeDtypeStruct((m, d), F32)],
        semantics=("arbitrary",), name=name, args=(q, k, v, do), jobs=jobs)


def _sigmoid(x):
    return 1.0 / (1.0 + jnp.exp(-x))


def _log1p(x):
    u = 1.0 + x
    return jnp.where(u == 1.0, x, jnp.log(u) * (x / jnp.where(u == 1.0, 1.0, u - 1.0)))


def _softplus(x):
    return jnp.maximum(x, 0.0) + _log1p(jnp.exp(-jnp.abs(x)))


def _expm1(x):
    series = x * (1.0 + x * 0.5 * (1.0 + x * (1.0 / 3.0) * (1.0 + x * 0.25 * (1.0 + x * 0.2 * (1.0 + x * (1.0 / 6.0))))))
    return jnp.where(jnp.abs(x) < 0.1, series, jnp.exp(x) - 1.0)


GELU_K = 0.7978845608028654
GELU_C = 0.044715


def _gelu(x):
    return 0.5 * x * (1.0 + jnp.tanh(GELU_K * (x + GELU_C * (x * x * x))))


def _gelu_grad(x):
    th = jnp.tanh(GELU_K * (x + GELU_C * (x * x * x)))
    return 0.5 * (1.0 + th) + 0.5 * x * (1.0 - th * th) * GELU_K * (1.0 + 3.0 * GELU_C * x * x)


def _window_sum(ext_ref, first, rows, cols, w, step):
    acc = ext_ref[first:first + rows, cols]
    for kk in range(1, w):
        acc = acc + ext_ref[first + step * kk:first + step * kk + rows, cols]
    return acc


def _lru_gates(c_s, wa_ref, ba_ref, wx_ref, bx_ref, lam_ref, t_idx, hd, r_s, i_s, a_s, mult_s):
    sp = _softplus(-lam_ref[...])
    for h in range(LRU_HEADS):
        hs = slice(h * hd, (h + 1) * hd)
        chb = c_s[:, hs].astype(BF16)
        r = _sigmoid(jnp.dot(chb, wa_ref[h], preferred_element_type=F32) + ba_ref[:, hs])
        ig = _sigmoid(jnp.dot(chb, wx_ref[h], preferred_element_type=F32) + bx_ref[:, hs])
        log_a = -LRU_C * r * sp[:, hs]
        mult = jnp.sqrt(-_expm1(2.0 * log_a))
        r_s[:, hs] = r
        i_s[:, hs] = ig
        a_s[:, hs] = jnp.exp(log_a)
        mult_s[:, hs] = jnp.where(t_idx == 0, 1.0, mult)


def _conv(ext_ref, cw_ref, cb_ref, rows):
    acc = cb_ref[...] + cw_ref[0:1, :] * ext_ref[CONV_HALO - 3:CONV_HALO - 3 + rows, :]
    for kk in range(1, CONV_WIDTH):
        off = CONV_HALO - (CONV_WIDTH - 1) + kk
        acc = acc + cw_ref[kk:kk + 1, :] * ext_ref[off:off + rows, :]
    return acc


def _mixer_fwd(proj, wp, bp, ps, cw, cb, wa, ba, wx, bx, lam, name, jobs=()):
    s, p3 = proj.shape
    p = p3 // 3
    cg, hd = p // N_POOL_GROUPS, p // LRU_HEADS
    t = _tile(TILES["mixer"], s)

    def body(up_ref, ul_ref, ug_ref, wp_ref, bp_ref, ps_ref, cw_ref, cb_ref, wa_ref, ba_ref, wx_ref, bx_ref,
             lam_ref, ycat_ref, h_ref, extp, extl, hc, c_s, r_s, i_s, a_s, b_s):
        i = pl.program_id(0)

        @pl.when(i == 0)
        def _():
            extp[0:POOL_HALO, :] = jnp.zeros((POOL_HALO, p), F32)
            extl[0:CONV_HALO, :] = jnp.zeros((CONV_HALO, p), F32)
            hc[...] = jnp.zeros_like(hc)

        t_idx = i * t + lax.broadcasted_iota(jnp.int32, (t, 1), 0)

        extp[POOL_HALO:POOL_HALO + t, :] = up_ref[...]
        for g, w in enumerate(POOL_WINDOWS):
            cs = slice(g * cg, (g + 1) * cg)
            cnt = jnp.minimum(t_idx + 1, w).astype(F32)
            mixed = _window_sum(extp, POOL_HALO, t, cs, w, -1) / cnt - up_ref[:, cs]
            pre = jnp.dot(mixed.astype(BF16), wp_ref[g], preferred_element_type=F32) + bp_ref[:, cs]
            ycat_ref[:, cs] = (pre * ps_ref[:, cs]).astype(BF16)
        extp[0:POOL_HALO, :] = extp[t:t + POOL_HALO, :]

        extl[CONV_HALO:CONV_HALO + t, :] = ul_ref[...]
        c_s[...] = _conv(extl, cw_ref, cb_ref, t)
        extl[0:CONV_HALO, :] = extl[t:t + CONV_HALO, :]
        _lru_gates(c_s, wa_ref, ba_ref, wx_ref, bx_ref, lam_ref, t_idx, hd, r_s, i_s, a_s, b_s)
        b_s[...] = b_s[...] * (i_s[...] * c_s[...])

        rows = lax.broadcasted_iota(jnp.int32, (SUBLANES, p), 0)

        def block(bi, h):
            r0 = pl.multiple_of(bi * SUBLANES, SUBLANES)
            at = a_s[pl.ds(r0, SUBLANES), :]
            bt = b_s[pl.ds(r0, SUBLANES), :]
            out = jnp.zeros((SUBLANES, p), F32)
            for j in range(SUBLANES):
                h = at[j:j + 1, :] * h + bt[j:j + 1, :]
                out = jnp.where(rows == j, h, out)
            h_ref[pl.ds(r0, SUBLANES), :] = out
            return h

        hc[0:1, :] = lax.fori_loop(0, t // SUBLANES, block, hc[0:1, :])
        ycat_ref[:, p:2 * p] = (h_ref[...] * _gelu(ug_ref[...])).astype(BF16)

    def col(j):
        return pl.BlockSpec((t, p), lambda i: (i, j))

    def whole(a):
        nd = a.ndim
        return pl.BlockSpec(a.shape, lambda i: (0,) * nd)

    consts = (wp, bp, ps, cw, cb, wa, ba, wx, bx, lam)
    tile = pltpu.VMEM((t, p), F32)
    return _call(
        body, grid=(s // t,), in_specs=[col(0), col(1), col(2)] + [whole(a) for a in consts],
        out_specs=[pl.BlockSpec((t, 2 * p), lambda i: (i, 0)), pl.BlockSpec((t, p), lambda i: (i, 0))],
        out_shape=[jax.ShapeDtypeStruct((s, 2 * p), BF16), jax.ShapeDtypeStruct((s, p), F32)],
        scratch_shapes=[pltpu.VMEM((t + POOL_HALO, p), F32), pltpu.VMEM((t + CONV_HALO, p), F32),
                        pltpu.VMEM((SUBLANES, p), F32), tile, tile, tile, tile, tile],
        semantics=("arbitrary",), name=name, args=(proj, proj, proj, *consts), jobs=jobs)


def _mixer_bwd(dycat, proj, hsave, wp, bp, ps, cw, cb, wa, ba, wx, bx, lam, name, jobs=()):
    s, p3 = proj.shape
    p = p3 // 3
    cg, hd = p // N_POOL_GROUPS, p // LRU_HEADS
    t = _tile(TILES["mixer"], s)
    nt = s // t

    def body(dyp_ref, dyl_ref, up_ref, ul_ref, ug_ref, upp_ref, ulp_ref, h_ref, hp_ref,
             wp_ref, bp_ref, ps_ref, cw_ref, cb_ref, wa_ref, ba_ref, wx_ref, bx_ref, lam_ref,
             dproj_ref, dwp_ref, dbp_ref, dps_ref, dcw_ref, dcb_ref, dwa_ref, dba_ref, dwx_ref, dbx_ref, dlam_ref,
             extp, extg, extl, extdc, exth, ghc, c_s, r_s, i_s, a_s, mult_s, gh_s):
        i = pl.program_id(0)
        ib = nt - 1 - i

        @pl.when(i == 0)
        def _():
            for ref in (dwp_ref, dbp_ref, dps_ref, dcw_ref, dcb_ref, dwa_ref, dba_ref, dwx_ref, dbx_ref, dlam_ref):
                ref[...] = jnp.zeros_like(ref)
            extg[t:t + POOL_HALO, :] = jnp.zeros((POOL_HALO, p), F32)
            extdc[t:t + CONV_HALO, :] = jnp.zeros((CONV_HALO, p), F32)
            ghc[...] = jnp.zeros_like(ghc)

        t_idx = ib * t + lax.broadcasted_iota(jnp.int32, (t, 1), 0)
        seq_start = ib == 0

        extl[0:CONV_HALO, :] = jnp.where(seq_start, 0.0, ulp_ref[...])
        extl[CONV_HALO:CONV_HALO + t, :] = ul_ref[...]
        c_s[...] = _conv(extl, cw_ref, cb_ref, t)
        _lru_gates(c_s, wa_ref, ba_ref, wx_ref, bx_ref, lam_ref, t_idx, hd, r_s, i_s, a_s, mult_s)
        exth[0:SUBLANES, :] = jnp.where(seq_start, 0.0, hp_ref[...])
        exth[SUBLANES:SUBLANES + t, :] = h_ref[...]

        ug = ug_ref[...]
        dyl = dyl_ref[...]
        dproj_ref[:, 2 * p:3 * p] = (dyl * h_ref[...] * _gelu_grad(ug)).astype(BF16)
        gh_s[...] = dyl * _gelu(ug)

        rows = lax.broadcasted_iota(jnp.int32, (SUBLANES, p), 0)
        nblk = t // SUBLANES

        def block(bi, carry):
            r0 = pl.multiple_of((nblk - 1 - bi) * SUBLANES, SUBLANES)
            at = a_s[pl.ds(r0, SUBLANES), :]
            dt = gh_s[pl.ds(r0, SUBLANES), :]
            out = jnp.zeros((SUBLANES, p), F32)
            for j in range(SUBLANES - 1, -1, -1):
                gh = dt[j:j + 1, :] + carry
                out = jnp.where(rows == j, gh, out)
                carry = at[j:j + 1, :] * gh
            gh_s[pl.ds(r0, SUBLANES), :] = out
            return carry

        ghc[0:1, :] = lax.fori_loop(0, nblk, block, ghc[0:1, :])

        sp = _softplus(-lam_ref[...])
        dsp_dlam = -_sigmoid(-lam_ref[...])
        for h in range(LRU_HEADS):
            hs = slice(h * hd, (h + 1) * hd)
            gh, a, mult, r, ig, c = gh_s[:, hs], a_s[:, hs], mult_s[:, hs], r_s[:, hs], i_s[:, hs], c_s[:, hs]
            hprev = exth[SUBLANES - 1:SUBLANES - 1 + t, hs]
            dmult = gh * (ig * c)
            dlog_a = a * gh * hprev + jnp.where(t_idx == 0, 0.0, -dmult * a * a / mult)
            dlam_ref[:, hs] += jnp.sum(dlog_a * r, axis=0, keepdims=True) * (-LRU_C) * dsp_dlam[:, hs]
            dpa = dlog_a * (-LRU_C * sp[:, hs]) * r * (1.0 - r)
            dpx = gh * mult * c * ig * (1.0 - ig)
            dpab, dpxb, chb = dpa.astype(BF16), dpx.astype(BF16), c.astype(BF16)
            dwa_ref[h] += lax.dot_general(chb, dpab, TN_DIMS, preferred_element_type=F32)
            dwx_ref[h] += lax.dot_general(chb, dpxb, TN_DIMS, preferred_element_type=F32)
            dba_ref[:, hs] += jnp.sum(dpa, axis=0, keepdims=True)
            dbx_ref[:, hs] += jnp.sum(dpx, axis=0, keepdims=True)
            dc = (gh * mult * ig
                  + lax.dot_general(dpab, wa_ref[h], NT_DIMS, preferred_element_type=F32)
                  + lax.dot_general(dpxb, wx_ref[h], NT_DIMS, preferred_element_type=F32))
            extdc[0:t, hs] = dc
            dcb_ref[:, hs] += jnp.sum(dc, axis=0, keepdims=True)
            for kk in range(CONV_WIDTH):
                off = CONV_HALO - (CONV_WIDTH - 1) + kk
                dcw_ref[kk:kk + 1, hs] += jnp.sum(dc * extl[off:off + t, hs], axis=0, keepdims=True)
        du_lru = cw_ref[0:1, :] * extdc[CONV_WIDTH - 1:CONV_WIDTH - 1 + t, :]
        for kk in range(1, CONV_WIDTH):
            off = CONV_WIDTH - 1 - kk
            du_lru = du_lru + cw_ref[kk:kk + 1, :] * extdc[off:off + t, :]
        dproj_ref[:, p:2 * p] = du_lru.astype(BF16)
        extdc[t:t + CONV_HALO, :] = extdc[0:CONV_HALO, :]

        extp[0:POOL_HALO, :] = jnp.where(seq_start, 0.0, upp_ref[...])
        extp[POOL_HALO:POOL_HALO + t, :] = up_ref[...]
        for g, w in enumerate(POOL_WINDOWS):
            cs = slice(g * cg, (g + 1) * cg)
            cnt = jnp.minimum(t_idx + 1, w).astype(F32)
            mixed = (_window_sum(extp, POOL_HALO, t, cs, w, -1) / cnt - up_ref[:, cs]).astype(BF16)
            pre = jnp.dot(mixed, wp_ref[g], preferred_element_type=F32) + bp_ref[:, cs]
            dyp = dyp_ref[:, cs]
            dps_ref[:, cs] += jnp.sum(dyp * pre, axis=0, keepdims=True)
            dpre = dyp * ps_ref[:, cs]
            dpreb = dpre.astype(BF16)
            dbp_ref[:, cs] += jnp.sum(dpre, axis=0, keepdims=True)
            dwp_ref[g] += lax.dot_general(mixed, dpreb, TN_DIMS, preferred_element_type=F32)
            dmixed = lax.dot_general(dpreb, wp_ref[g], NT_DIMS, preferred_element_type=F32)
            extg[0:t, cs] = dmixed / cnt
            dproj_ref[:, cs] = (_window_sum(extg, 0, t, cs, w, 1) - dmixed).astype(BF16)
        extg[t:t + POOL_HALO, :] = extg[0:POOL_HALO, :]

    def col(j):
        return pl.BlockSpec((t, p), lambda i: (nt - 1 - i, j))

    def prev(rows, j):
        per = t // rows
        return pl.BlockSpec((rows, p), lambda i: (jnp.maximum((nt - 1 - i) * per - 1, 0), j))

    def whole(a):
        nd = a.ndim
        return pl.BlockSpec(a.shape, lambda i: (0,) * nd)

    consts = (wp, bp, ps, cw, cb, wa, ba, wx, bx, lam)
    grads = (wp, bp, ps, cw, cb, wa, ba, wx, bx, lam)
    tile = pltpu.VMEM((t, p), F32)
    return _call(
        body, grid=(nt,),
        in_specs=[col(0), col(1), col(0), col(1), col(2), prev(POOL_HALO, 0), prev(CONV_HALO, 1), col(0),
                  prev(SUBLANES, 0)] + [whole(a) for a in consts],
        out_specs=[pl.BlockSpec((t, 3 * p), lambda i: (nt - 1 - i, 0))] + [whole(a) for a in grads],
        out_shape=[jax.ShapeDtypeStruct((s, 3 * p), BF16)] + [jax.ShapeDtypeStruct(a.shape, F32) for a in grads],
        scratch_shapes=[pltpu.VMEM((t + POOL_HALO, p), F32), pltpu.VMEM((t + POOL_HALO, p), F32),
                        pltpu.VMEM((t + CONV_HALO, p), F32), pltpu.VMEM((t + CONV_HALO, p), F32),
                        pltpu.VMEM((t + SUBLANES, p), F32), pltpu.VMEM((SUBLANES, p), F32),
                        tile, tile, tile, tile, tile, tile],
        semantics=("arbitrary",), name=name,
        args=(dycat, dycat, proj, proj, proj, proj, proj, hsave, hsave, *consts), jobs=jobs)


def _pair_add(parts, got, core, name):
    n, r, c = got.shape
    tr = _tile(TILES["add"], r)

    def body(core_ref, a_ref, b_ref, o_ref):
        del core_ref
        o_ref[...] = (a_ref[...].astype(F32) + b_ref[...].astype(F32)).astype(o_ref.dtype)

    blk = pl.BlockSpec((None, tr, c), lambda k, i, core_ref: (k, i, 0))
    mine = pl.BlockSpec((None, tr, c), lambda k, i, core_ref: (2 * k + core_ref[0], i, 0))
    return _call(body, grid=(n, r // tr), in_specs=[mine, blk], out_specs=[blk],
                 out_shape=[jax.ShapeDtypeStruct(got.shape, got.dtype)], semantics=("parallel", "parallel"),
                 name=name, args=(parts, got), index=core)[0]


def _sum_parts(parts, name):
    n, r, c = parts.shape
    tr = _tile(TILES["adam"], r)

    def body(p_ref, o_ref):
        acc = p_ref[0].astype(F32)
        for d in range(1, n):
            acc = acc + p_ref[d].astype(F32)
        o_ref[...] = acc

    return _call(
        body, grid=(r // tr,), in_specs=[pl.BlockSpec((n, tr, c), lambda i: (0, i, 0))],
        out_specs=[pl.BlockSpec((tr, c), lambda i: (i, 0))], out_shape=[jax.ShapeDtypeStruct((r, c), F32)],
        semantics=("parallel",), name=name, args=(parts,))[0]


def _adamw(w, m, v, parts, name, jobs=(), own=None, chip=None):
    r, c = w.shape
    n = parts.shape[0]
    tr = _tile(TILES["adam"], r)

    def body(*refs):
        if own is not None:
            refs = refs[1:]
            own_ref, refs = refs[3], refs[:3] + refs[4:]
        w_ref, m_ref, v_ref, p_ref, g_ref, d_ref, nm_ref, nv_ref = refs
        g = p_ref[0].astype(F32)
        if own is not None:
            g = own_ref[...].astype(F32) + g
        for d in range(1, n):
            g = g + p_ref[d].astype(F32)
        nm = ADAM_B1 * m_ref[...] + (1.0 - ADAM_B1) * g
        nv = ADAM_B2 * v_ref[...] + (1.0 - ADAM_B2) * (g * g)
        m_hat = nm / (1.0 - ADAM_B1 ** ADAM_STEP)
        v_hat = nv / (1.0 - ADAM_B2 ** ADAM_STEP)
        g_ref[...] = g
        d_ref[...] = -ADAM_LR * (m_hat / (jnp.sqrt(v_hat) + ADAM_EPS) + ADAM_WD * w_ref[...])
        nm_ref[...] = nm
        nv_ref[...] = nv

    row = pl.BlockSpec((tr, c), lambda i, *_: (i, 0))
    in_specs, args = [row, row, row], [w, m, v]
    if own is not None:
        in_specs.append(pl.BlockSpec((None, tr, c), lambda i, chip_ref: (chip_ref[0], i, 0)))
        args.append(own)
    in_specs.append(pl.BlockSpec((n, tr, c), lambda i, *_: (0, i, 0)))
    args.append(parts)
    return _call(
        body, grid=(r // tr,), in_specs=in_specs, out_specs=[row] * 4, out_shape=[jax.ShapeDtypeStruct((r, c), F32)] * 4,
        semantics=("parallel",), name=name, args=args, jobs=jobs, index=chip if own is not None else None)


SMALL_ORDER = ("w_a", "w_x", "conv_w", "b_pool", "conv_b", "b_a", "b_x", "lru_lambda", "pool_scale",
               "ln1_g", "ln1_b", "ln2_g", "ln2_b", "ln3_g", "ln3_b")


def _pack_rows(a, p):
    flat = a.reshape(-1, p)
    pad = (-flat.shape[0]) % SUBLANES
    return jnp.pad(flat, ((0, pad), (0, 0))) if pad else flat


def kernel(x, mem, w_in, conv_w, conv_b, w_a, b_a, w_x, b_x, lru_lambda, w_pool, b_pool, pool_scale, w_out, ln1_g, ln1_b, w_q, w_k, w_v, w_o, ln2_g, ln2_b, w_ff1, w_ff2, ln3_g, ln3_b, loss_target, m_w_in, m_conv_w, m_conv_b, m_w_a, m_b_a, m_w_x, m_b_x, m_lru_lambda, m_w_pool, m_b_pool, m_pool_scale, m_w_out, m_ln1_g, m_ln1_b, m_w_q, m_w_k, m_w_v, m_w_o, m_ln2_g, m_ln2_b, m_w_ff1, m_w_ff2, m_ln3_g, m_ln3_b, v_w_in, v_conv_w, v_conv_b, v_w_a, v_b_a, v_w_x, v_b_x, v_lru_lambda, v_w_pool, v_b_pool, v_pool_scale, v_w_out, v_ln1_g, v_ln1_b, v_w_q, v_w_k, v_w_v, v_w_o, v_ln2_g, v_ln2_b, v_w_ff1, v_w_ff2, v_ln3_g, v_ln3_b):
    names = ("w_in", "conv_w", "conv_b", "w_a", "b_a", "w_x", "b_x", "lru_lambda", "w_pool", "b_pool", "pool_scale",
             "w_out", "ln1_g", "ln1_b", "w_q", "w_k", "w_v", "w_o", "ln2_g", "ln2_b", "w_ff1", "w_ff2", "ln3_g", "ln3_b")
    w_loc = dict(zip(names, (w_in, conv_w, conv_b, w_a, b_a, w_x, b_x, lru_lambda, w_pool, b_pool, pool_scale,
                             w_out, ln1_g, ln1_b, w_q, w_k, w_v, w_o, ln2_g, ln2_b, w_ff1, w_ff2, ln3_g, ln3_b)))
    m_loc = dict(zip(names, (m_w_in, m_conv_w, m_conv_b, m_w_a, m_b_a, m_w_x, m_b_x, m_lru_lambda, m_w_pool, m_b_pool,
                             m_pool_scale, m_w_out, m_ln1_g, m_ln1_b, m_w_q, m_w_k, m_w_v, m_w_o, m_ln2_g, m_ln2_b,
                             m_w_ff1, m_w_ff2, m_ln3_g, m_ln3_b)))
    v_loc = dict(zip(names, (v_w_in, v_conv_w, v_conv_b, v_w_a, v_b_a, v_w_x, v_b_x, v_lru_lambda, v_w_pool, v_b_pool,
                             v_pool_scale, v_w_out, v_ln1_g, v_ln1_b, v_w_q, v_w_k, v_w_v, v_w_o, v_ln2_g, v_ln2_b,
                             v_w_ff1, v_w_ff2, v_ln3_g, v_ln3_b)))
    s, d = x.shape[1], x.shape[2]
    p = conv_b.shape[1]
    cg = p // N_POOL_GROUPS
    hd = p // LRU_HEADS
    me = 4 * lax.axis_index("x") + 2 * lax.axis_index("y") + lax.axis_index("c")

    xs, mems, tgt = x[0], mem[0], loss_target[0]
    memb = mems.astype(BF16)

    gathers = {n: _Job("gather", w_loc[n][0].astype(BF16))
               for n in ("w_in", "w_out", "w_q", "w_k", "w_v", "w_o", "w_ff2", "w_pool")}
    ff1_shard = w_ff1[0].astype(BF16)
    ff1_rows = ff1_shard.shape[0] // FF1_PIECES

    def ff1_piece(i, earlier=None):
        return _Job("gather", ff1_shard[i * ff1_rows:(i + 1) * ff1_rows], window=(i * ff1_rows, ff1_shard.shape[0]),
                    into=None if earlier is None else earlier.out)
    tiny = jnp.concatenate([_pack_rows(conv_w[0], p // N_DEV),
                            _pack_rows(jnp.pad(b_pool[0], ((0, 0), (0, p // N_DEV - cg // N_DEV))), p // N_DEV)], axis=0)
    gathers["tiny"] = _Job("gather", tiny)

    def gathered(n):
        full = gathers[n].out
        if n == "w_in":
            return jnp.transpose(full, (1, 0, 2)).reshape(1, full.shape[1], -1)
        return full.reshape(1, -1, full.shape[-1])

    W = {"conv_b": conv_b, "b_a": b_a.reshape(1, p), "b_x": b_x.reshape(1, p), "lru_lambda": lru_lambda,
         "pool_scale": pool_scale, "w_a": w_a[0].astype(BF16), "w_x": w_x[0].astype(BF16)}
    for n in ("ln1_g", "ln1_b", "ln2_g", "ln2_b", "ln3_g", "ln3_b"):
        W[n] = w_loc[n]

    out_g, out_d, out_m, out_v = {}, {}, {}, {}
    pairs, quads, sums = {}, {}, {}
    core =lax.axis_index("c").astype(jnp.int32).reshape(1)
    chip = (2 * lax.axis_index("x") + lax.axis_index("y")).astype(jnp.int32).reshape(1)

    def pair(n, partial):
        pairs[n] = _Job("pair", partial.reshape(N_DEV, -1, partial.shape[-1]))
        return pairs[n]

    def quad(n, lo=0, hi=1, of=1):
        if lo == 0:
            sums[n] = _pair_add(pairs[n].src, pairs[n].out, core, "add_" + n)
        rows = sums[n].shape[1] // of
        quads[n] = _Job("quad", sums[n], window=None if (lo, hi) == (0, of) else (lo * rows, (hi - lo) * rows),
                        into=None if lo == 0 else quads[n].out)
        return quads[n]

    def update(n, parts, jobs=(), own=None):
        shp = w_loc[n].shape
        rows = parts.shape[1]
        w2, m2, v2 = (a.reshape(rows, -1) for a in (w_loc[n], m_loc[n], v_loc[n]))
        res = _adamw(w2, m2, v2, parts.reshape(parts.shape[0], rows, -1), "adamw_" + n, jobs=jobs, own=own, chip=chip)
        out_g[n], out_d[n], out_m[n], out_v[n] = (r.reshape(shp) for r in res)

    assert FF1_PIECES == 4
    xb = _to_bf16(xs, "cast_x", jobs=[gathers["w_in"], gathers["tiny"], gathers["w_pool"]])
    W["w_pool"] = jnp.transpose(gathers["w_pool"].out, (1, 0, 2, 3)).reshape(N_POOL_GROUPS, cg, cg)
    cwb = gathers["tiny"].out
    W["conv_w"] = jnp.transpose(cwb[:, :CONV_WIDTH, :], (1, 0, 2)).reshape(CONV_WIDTH, p)
    W["b_pool"] = jnp.transpose(cwb[:, SUBLANES:SUBLANES + N_POOL_GROUPS, :cg // N_DEV], (1, 0, 2)).reshape(1, p)
    mixer_consts = (W["w_pool"], W["b_pool"], W["pool_scale"], W["conv_w"], W["conv_b"], W["w_a"], W["b_a"],
                    W["w_x"], W["b_x"], W["lru_lambda"])

    w_in_full = gathered("w_in")
    piece = ff1_piece(FF1_PIECES - 1)
    (proj,) = _mm_nn(xb, w_in_full, [F32], "fwd_proj", jobs=[gathers["w_out"], piece])
    ycat, hsave = _mixer_fwd(proj, *mixer_consts, "fwd_mixer", jobs=[gathers["w_q"], gathers["w_k"]])
    (y1,) = _mm_nn(ycat, gathered("w_out"), [F32], "fwd_out", jobs=[gathers["w_v"]])
    x1, x1b, xhat1, rstd1 = _ln_fwd(xs, y1, W["ln1_g"], W["ln1_b"], "fwd_ln1", jobs=[gathers["w_o"]])
    piece = ff1_piece(0, piece)
    (q,) = _mm_nn(x1b, gathered("w_q"), [BF16], "fwd_q", jobs=[piece])
    (k,) = _mm_nn(memb, gathered("w_k"), [BF16], "fwd_k")
    (v,) = _mm_nn(memb, gathered("w_v"), [BF16], "fwd_v")
    o = _attn_fwd(q, k, v, "fwd_attn")
    piece = ff1_piece(1, piece)
    (y2,) = _mm_nn(o, gathered("w_o"), [F32], "fwd_o", jobs=[piece])
    piece = ff1_piece(2, piece)
    x2, x2b, xhat2, rstd2 = _ln_fwd(x1, y2, W["ln2_g"], W["ln2_b"], "fwd_ln2", jobs=[piece])
    w_ff1_full = piece.out

    def relu_sq(acc):
        r = jnp.maximum(acc, 0.0)
        return r, r * r

    rb, act = _mm_nn(x2b, w_ff1_full, [BF16, BF16], "fwd_ff1", epilogue=relu_sq, jobs=[gathers["w_ff2"]])
    (y3,) = _mm_nn(act, gathered("w_ff2"), [F32], "fwd_ff2")
    loss_rows, dz3, dz3b, dg3, db3 = _ln_loss(x2, y3, W["ln3_g"], W["ln3_b"], tgt, "ln3_loss")
    loss = lax.psum(loss_rows[0, 0], MESH_AXES)

    small = {"ln3_g": dg3, "ln3_b": db3}

    def add_residual(acc, e):
        return (acc + ALPHA * e,)

    dw_ff2 = _mm_tn(act, dz3b, 1, BF16, "bwd_dw_ff2")
    (dhid,) = _mm_nt(dz3b, gathered("w_ff2"), [BF16], "bwd_dact", extras=(rb,), jobs=[pair("w_ff2", dw_ff2)],
                     epilogue=lambda acc, r: (acc * (2.0 * r.astype(F32)),))
    dw_ff1 = _mm_tn(x2b, dhid, N_DEV, BF16, "bwd_dw_ff1", jobs=[quad("w_ff2", 0, 1, 2)])
    (dx2,) = _mm_nt(dhid, w_ff1_full, [F32], "bwd_dx2", epilogue=add_residual, extras=(dz3,), tk=TILES["tk"] // 2,
                    jobs=[pair("w_ff1", dw_ff1), quad("w_ff2", 1, 2, 2)])
    dz2, dz2b, small["ln2_g"], small["ln2_b"] = _ln_bwd(dx2, xhat2, rstd2, W["ln2_g"], "bwd_ln2")

    dw_o = _mm_tn(o, dz2b, 1, BF16, "bwd_dw_o")
    (do,) = _mm_nt(dz2b, gathered("w_o"), [BF16], "bwd_do", jobs=[pair("w_o", dw_o)])
    dq, dk, dv = _attn_bwd(q, k, v, do, "bwd_attn", jobs=[quad("w_o")])
    dw_q = _mm_tn(x1b, dq, 1, BF16, "bwd_dw_q")
    dw_k = _mm_tn(memb, dk.astype(BF16), 1, BF16, "bwd_dw_k")
    dw_v = _mm_tn(memb, dv.astype(BF16), 1, BF16, "bwd_dw_v")
    (dx1,) = _mm_nt(dq, gathered("w_q"), [F32], "bwd_dx1", epilogue=add_residual, extras=(dz2,),
                    jobs=[pair("w_q", dw_q), pair("w_k", dw_k), pair("w_v", dw_v), quad("w_ff1", 0, 1, 4)])
    dz1, dz1b, small["ln1_g"], small["ln1_b"] = _ln_bwd(dx1, xhat1, rstd1, W["ln1_g"], "bwd_ln1", jobs=[quad("w_q")])

    dw_out = _mm_tn(ycat, dz1b, 1, BF16, "bwd_dw_out")
    (dycat,) = _mm_nt(dz1b, gathered("w_out"), [F32], "bwd_dycat", jobs=[pair("w_out", dw_out)])
    (dproj, dwp, small["b_pool"], small["pool_scale"], small["conv_w"], small["conv_b"], small["w_a"], small["b_a"],
     small["w_x"], small["b_x"], small["lru_lambda"]) = _mixer_bwd(
        dycat, proj, hsave, *mixer_consts, "bwd_mixer", jobs=[quad("w_ff1", 1, 3, 4), quad("w_k"), quad("w_v")])
    dw_pool = jnp.transpose(dwp.astype(BF16).reshape(N_POOL_GROUPS, N_DEV, cg // N_DEV, cg), (1, 0, 2, 3))
    pack = jnp.concatenate([_pack_rows(small[n], p) for n in SMALL_ORDER], axis=0)
    small_gather = _Job("gather", pack)
    dw_in = _mm_tn(xb, dproj, 1, BF16, "bwd_dw_in", jobs=[quad("w_out"), pair("w_pool", dw_pool), small_gather])
    dw_in = jnp.transpose(dw_in.reshape(dw_in.shape[1], N_DEV, -1), (1, 0, 2))
    (grad_x,) = _mm_nt(dproj, w_in_full, [F32], "bwd_dx", epilogue=add_residual, extras=(dz1,),
                       jobs=[pair("w_in", dw_in), quad("w_pool"), quad("w_ff1", 3, 4, 4)])

    update("w_ff2", quads["w_ff2"].out, jobs=[quad("w_in")], own=quads["w_ff2"].src)
    for n in ("w_ff1", "w_o", "w_q", "w_k", "w_v", "w_out", "w_pool", "w_in"):
        update(n, quads[n].out, own=quads[n].src)

    total = _sum_parts(small_gather.out, "sum_small")
    row = 0
    for n in SMALL_ORDER:
        size = small[n].size
        nrows = size // p
        g_full = total[row:row + nrows].reshape(small[n].shape)
        row += nrows + (-nrows) % SUBLANES
        if n == "conv_w":
            g_loc = lax.dynamic_slice_in_dim(g_full, me * (p // N_DEV), p // N_DEV, axis=1)
        elif n == "b_pool":
            g_loc = lax.dynamic_slice_in_dim(g_full.reshape(N_POOL_GROUPS, cg), me * (cg // N_DEV), cg // N_DEV, axis=1)
        else:
            g_loc = g_full
        rows = g_loc.shape[0] if n not in ("w_a", "w_x") else LRU_HEADS * hd
        update(n, g_loc.reshape(1, rows, -1))

    order = names
    return (loss, grad_x[None], *[out_g[n] for n in order], *[out_d[n] for n in order],
            *[out_m[n] for n in order], *[out_v[n] for n in order])
```

```python
import functools

import jax
import jax.numpy as jnp
from jax import lax
from jax.experimental import pallas as pl
from jax.experimental.pallas import tpu as pltpu

F32 = jnp.float32
BF16 = jnp.bfloat16

N_DEV = 8
MESH_AXES = ("x", "y", "c")
POOL_WINDOWS = (2, 4, 8, 16)
N_POOL_GROUPS = len(POOL_WINDOWS)
POOL_HALO = 16
CONV_WIDTH = 4
CONV_HALO = 8
FF1_PIECES = 4
LRU_HEADS = 8
LRU_C = 8.0
XATTN_HEADS = 4
LN_EPS = 1e-5
ALPHA = 2.0 ** 0.25
ADAM_LR = 0.001
ADAM_B1 = 0.9
ADAM_B2 = 0.999
ADAM_EPS = 1e-08
ADAM_WD = 0.01
ADAM_STEP = 10
SUBLANES = 8
VMEM_LIMIT = 56 * 1024 * 1024

NT_DIMS = (((1,), (1,)), ((), ()))
TN_DIMS = (((0,), (0,)), ((), ()))


def _params(*sem):
    return pltpu.CompilerParams(dimension_semantics=sem, vmem_limit_bytes=VMEM_LIMIT)


def _place():
    return lax.axis_index("x"), lax.axis_index("y"), lax.axis_index("c")


def _remote(src, dst, send_sem, recv_sem, to):
    return pltpu.make_async_remote_copy(src_ref=src, dst_ref=dst, send_sem=send_sem, recv_sem=recv_sem,
                                        device_id=to, device_id_type=pl.DeviceIdType.MESH)


class _Job:
    def __init__(self, kind, src, window=None, into=None):
        self.kind, self.src, self.out, self.window, self.into = kind, src, None, window, into

    def out_shape(self):
        s = self.src.shape
        if self.kind == "gather" and self.window is not None:
            s = (self.window[1],) + s[1:]
        shape = {"gather": (N_DEV,) + s, "pair": (4,) + s[1:], "quad": (3,) + s[1:]}[self.kind]
        return jax.ShapeDtypeStruct(shape, self.src.dtype)

    def scratch(self):
        n = {"gather": 7, "pair": 4, "quad": 3}[self.kind]
        sems = [pltpu.SemaphoreType.DMA((n,)), pltpu.SemaphoreType.DMA((n,))]
        if self.kind == "gather":
            sems += [pltpu.SemaphoreType.DMA((2,)), pltpu.VMEM(self.src.shape, self.src.dtype)]
        return sems

    def ops(self, src, out, *scratch):
        if self.kind == "gather":
            return _gather_ops(src, out, *scratch, first_row=None if self.window is None else self.window[0])
        if self.kind == "quad":
            return _quad_ops(src, out, *scratch, rows=self.window)
        return _pair_ops(src, out, *scratch)


def _gather_ops(x_ref, out_ref, send_sems, recv_sems, local_sems, bounce, first_row=None):
    x, y, c = _place()
    me, sibling = (x, y, c), (x, y, 1 - c)
    chips = [(1 - x, y), (x, 1 - y), (1 - x, 1 - y)]

    def slot(px, py, pc):
        block = out_ref.at[4 * px + 2 * py + pc]
        return block if first_row is None else block.at[pl.ds(first_row, x_ref.shape[0])]

    def copy(k, block, to, src=None):
        return _remote(slot(*block) if src is None else src, slot(*block), send_sems.at[k], recv_sems.at[k], to)

    mine_in = pltpu.make_async_copy(x_ref, bounce, local_sems.at[0])
    mine_out = pltpu.make_async_copy(bounce, slot(*me), local_sems.at[1])
    first = [copy(0, me, sibling, src=x_ref)] + [copy(1 + j, me, (*chip, c), src=x_ref) for j, chip in enumerate(chips)]
    passed = [copy(4 + j, (*chip, c), sibling) for j, chip in enumerate(chips)]

    def start():
        mine_in.start()
        for cp in first:
            cp.start()

    def mid():
        mine_in.wait()
        mine_out.start()
        for j, chip in enumerate(chips):
            copy(1 + j, (*chip, c), me).wait_recv()
            passed[j].start()

    def finish():
        copy(0, sibling, me).wait_recv()
        for j, chip in enumerate(chips):
            copy(4 + j, (*chip, 1 - c), me).wait_recv()
        for cp in first + passed:
            cp.wait_send()
        mine_out.wait()

    return start, mid, finish


def _pair_ops(p_ref, got_ref, send_sems, recv_sems):
    x, y, c = _place()
    give = [_remote(p_ref.at[2 * k + 1 - c], got_ref.at[k], send_sems.at[k], recv_sems.at[k], (x, y, 1 - c))
            for k in range(4)]

    def start():
        for cp in give:
            cp.start()

    def finish():
        for cp in give:
            cp.wait_recv()
        for cp in give:
            cp.wait_send()

    return start, None, finish


def _quad_ops(q_ref, out_ref, send_sems, recv_sems, rows=None):
    x, y, c = _place()

    def part(block):
        return block if rows is None else block.at[pl.ds(rows[0], rows[1])]

    copies = []
    for rel in range(1, 4):
        px = 1 - x if rel & 2 else x
        py = 1 - y if rel & 1 else y
        copies.append(_remote(part(q_ref.at[2 * px + py]), part(out_ref.at[rel - 1]), send_sems.at[rel - 1],
                              recv_sems.at[rel - 1], (px, py, c)))

    def start():
        for cp in copies:
            cp.start()

    def finish():
        for cp in copies:
            cp.wait_recv()
        for cp in copies:
            cp.wait_send()

    return start, None, finish


def _call(body, *, grid, in_specs, out_specs, out_shape, scratch_shapes=(), semantics, name, args, jobs=(), index=None):
    in_specs, out_specs, out_shape = list(in_specs), list(out_specs), list(out_shape)
    scratch_shapes, jobs = list(scratch_shapes), list(jobs)
    n_in, n_out, n_scr, n_job = len(in_specs), len(out_specs), len(scratch_shapes), len(jobs)
    n_idx = 0 if index is None else 1
    job_scratch = [j.scratch() for j in jobs]
    n_steps = functools.reduce(lambda a, b: a * b, grid, 1)
    early = n_steps - 1 - max(1, n_steps // 8) if n_steps >= 4 else None

    intos = [(k, j.into) for k, j in enumerate(jobs) if j.into is not None]

    def hosted(*refs):
        idx, refs = refs[:n_idx], refs[n_idx:]
        ins, jin = refs[:n_in], refs[n_in:n_in + n_job]
        o0 = n_in + n_job + len(intos)
        outs, jout = refs[o0:o0 + n_out], refs[o0 + n_out:o0 + n_out + n_job]
        s0 = o0 + n_out + n_job
        scr, jscr = refs[s0:s0 + n_scr], refs[s0 + n_scr:]
        ops, at = [], 0
        for k, j in enumerate(jobs):
            ops.append(j.ops(jin[k], jout[k], *jscr[at:at + len(job_scratch[k])]))
            at += len(job_scratch[k])
        step = functools.reduce(lambda acc, a: acc * grid[a] + pl.program_id(a), range(len(grid)), 0)
        mids = [mid for _, mid, _ in ops if mid is not None]

        @pl.when(step == 0)
        def _():
            for start, _, _ in ops:
                start()

        if mids and early is not None:
            @pl.when(step == early)
            def _():
                for mid in mids:
                    mid()

        body(*idx, *ins, *outs, *scr)

        @pl.when(step == n_steps - 1)
        def _():
            if early is None:
                for mid in mids:
                    mid()
            for _, _, finish in ops:
                finish()

    hbm = pl.BlockSpec(memory_space=pl.ANY)
    spec = pltpu.PrefetchScalarGridSpec(
        num_scalar_prefetch=n_idx, grid=grid, in_specs=in_specs + [hbm] * (n_job + len(intos)),
        out_specs=out_specs + [hbm] * n_job, scratch_shapes=scratch_shapes + [s for js in job_scratch for s in js])
    aliases = {n_idx + n_in + n_job + q: n_out + k for q, (k, _) in enumerate(intos)}
    res = pl.pallas_call(
        hosted if jobs else body, grid_spec=spec, out_shape=out_shape + [j.out_shape() for j in jobs],
        input_output_aliases=aliases,
        compiler_params=_params(*(["arbitrary"] * len(grid) if jobs else semantics)), name=name,
    )(*([] if index is None else [index]), *args, *[j.src for j in jobs], *[buf for _, buf in intos])
    for j, o in zip(jobs, res[n_out:]):
        j.out = o
    return res[:n_out]


def _to_bf16(a, name, jobs=()):
    r, c = a.shape
    tr = _tile(TILES["row"], r)

    def body(a_ref, o_ref):
        o_ref[...] = a_ref[...].astype(BF16)

    row = pl.BlockSpec((tr, c), lambda i: (i, 0))
    return _call(body, grid=(r // tr,), in_specs=[row], out_specs=[row], out_shape=[jax.ShapeDtypeStruct((r, c), BF16)],
                 semantics=("parallel",), name=name, args=(a,), jobs=jobs)[0]


TILES = dict(tm=1024, tn=1024, tk=2048, row=256, attn=512, mixer=256, adam=128, add=1024)


def _tile(pref, n):
    for t in range(min(pref, n), 0, -1):
        if n % t == 0 and (t % SUBLANES == 0 or t == n):
            return t
    return n


def _accumulate(acc, step, n_steps, product, write):
    if n_steps == 1:
        write(product())
        return

    @pl.when(step == 0)
    def _():
        acc[...] = product()

    @pl.when(jnp.logical_and(step > 0, step < n_steps - 1))
    def _():
        acc[...] += product()

    @pl.when(step == n_steps - 1)
    def _():
        write(acc[...] + product())


def _acc_scratch(n_steps, tm, tn):
    return [] if n_steps == 1 else [pltpu.VMEM((tm, tn), F32)]


def _mm_nn(a, b3, out_dtypes, name, *, tm=None, tn=None, tk=None, epilogue=None, extras=(), jobs=()):
    m, k = a.shape
    g, k2, ns = b3.shape
    assert k == k2
    n = g * ns
    tm, tn, tk = _tile(tm or TILES["tm"], m), _tile(tn or TILES["tn"], ns), _tile(tk or TILES["tk"], k)
    nb, nk = ns // tn, k // tk
    n_ex, n_out = len(extras), len(out_dtypes)

    def body(*refs):
        a_ref, b_ref = refs[:2]
        ex = refs[2:2 + n_ex]
        outs = refs[2 + n_ex:2 + n_ex + n_out]
        acc = refs[-1] if nk > 1 else None

        def write(r):
            res = epilogue(r, *[e[...] for e in ex]) if epilogue is not None else (r,)
            for o, v in zip(outs, res):
                o[...] = v.astype(o.dtype)

        _accumulate(acc, pl.program_id(2), nk,
                    lambda: jnp.dot(a_ref[...], b_ref[...], preferred_element_type=F32), write)

    tile_out = pl.BlockSpec((tm, tn), lambda i, j, kk: (i, j))
    return _call(
        body, grid=(m // tm, n // tn, nk),
        in_specs=[pl.BlockSpec((tm, tk), lambda i, j, kk: (i, kk)),
                  pl.BlockSpec((None, tk, tn), lambda i, j, kk: (j // nb, kk, j % nb))] + [tile_out] * n_ex,
        out_specs=[tile_out] * n_out,
        out_shape=[jax.ShapeDtypeStruct((m, n), d) for d in out_dtypes],
        scratch_shapes=_acc_scratch(nk, tm, tn),
        semantics=("parallel", "parallel", "arbitrary"), name=name, args=(a, b3, *extras), jobs=jobs)


def _mm_nt(a, b3, out_dtypes, name, *, tm=None, tn=None, tk=None, epilogue=None, extras=(), jobs=(), part=(0, 1)):
    m, n = a.shape
    g, k, ns = b3.shape
    assert n == g * ns
    tm, tn, tk = _tile(tm or TILES["tm"], m // part[1]), _tile(tn or TILES["tn"], k), _tile(tk or TILES["tk"], ns)
    nb, nc = ns // tk, n // tk
    n_ex, n_out = len(extras), len(out_dtypes)
    m_blocks = m // tm // part[1]
    first = part[0] * m_blocks

    def body(*refs):
        a_ref, b_ref = refs[:2]
        ex = refs[2:2 + n_ex]
        outs = refs[2 + n_ex:2 + n_ex + n_out]
        acc = refs[-1] if nc > 1 else None

        def write(r):
            res = epilogue(r, *[e[...] for e in ex]) if epilogue is not None else (r,)
            for o, v in zip(outs, res):
                o[...] = v.astype(o.dtype)

        _accumulate(acc, pl.program_id(2), nc,
                    lambda: lax.dot_general(a_ref[...], b_ref[...], NT_DIMS, preferred_element_type=F32), write)

    tile_out = pl.BlockSpec((tm, tn), lambda i, j, cc: (i, j))
    tile_ex = pl.BlockSpec((tm, tn), lambda i, j, cc: (first + i, j))
    return _call(
        body, grid=(m_blocks, k // tn, nc),
        in_specs=[pl.BlockSpec((tm, tk), lambda i, j, cc: (first + i, cc)),
                  pl.BlockSpec((None, tn, tk), lambda i, j, cc: (cc // nb, j, cc % nb))] + [tile_ex] * n_ex,
        out_specs=[tile_out] * n_out,
        out_shape=[jax.ShapeDtypeStruct((m_blocks * tm, k), d) for d in out_dtypes],
        scratch_shapes=_acc_scratch(nc, tm, tn),
        semantics=("parallel", "parallel", "arbitrary"), name=name, args=(a, b3, *extras), jobs=jobs)


def _mm_tn(a, b, g, out_dtype, name, *, tm=None, tn=None, tk=None, jobs=()):
    s, m = a.shape
    s2, n = b.shape
    assert s == s2 and n % g == 0
    ns = n // g
    tm, tn, tk = _tile(tm or TILES["tm"], m), _tile(tn or TILES["tn"], ns), _tile(tk or TILES["tk"], s)
    nb, nc = ns // tn, s // tk

    def body(a_ref, b_ref, o_ref, *scratch):
        def write(r):
            o_ref[...] = r.astype(o_ref.dtype)

        _accumulate(scratch[0] if nc > 1 else None, pl.program_id(2), nc,
                    lambda: lax.dot_general(a_ref[...], b_ref[...], TN_DIMS, preferred_element_type=F32), write)

    return _call(
        body, grid=(m // tm, n // tn, nc),
        in_specs=[pl.BlockSpec((tk, tm), lambda i, j, cc: (cc, i)),
                  pl.BlockSpec((tk, tn), lambda i, j, cc: (cc, j))],
        out_specs=[pl.BlockSpec((None, tm, tn), lambda i, j, cc: (j // nb, i, j % nb))],
        out_shape=[jax.ShapeDtypeStruct((g, m, ns), out_dtype)],
        scratch_shapes=_acc_scratch(nc, tm, tn),
        semantics=("parallel", "parallel", "arbitrary"), name=name, args=(a, b), jobs=jobs)[0]


def _ln_stats(z):
    mu = jnp.mean(z, axis=-1, keepdims=True)
    zc = z - mu
    var = jnp.mean(zc * zc, axis=-1, keepdims=True)
    rstd = lax.rsqrt(var + LN_EPS)
    return zc * rstd, rstd


def _ln_grad(dout, xhat, rstd, gain):
    dxhat = dout * gain
    m1 = jnp.mean(dxhat, axis=-1, keepdims=True)
    m2 = jnp.mean(dxhat * xhat, axis=-1, keepdims=True)
    return rstd * (dxhat - m1 - xhat * m2)


def _ln_fwd(xres, y, gain, bias, name, jobs=()):
    s, d = xres.shape
    tr = _tile(TILES["row"], s)

    def body(x_ref, y_ref, g_ref, b_ref, xn_ref, xnb_ref, xhat_ref, rstd_ref):
        xhat, rstd = _ln_stats(ALPHA * x_ref[...] + y_ref[...])
        out = xhat * g_ref[...] + b_ref[...]
        xn_ref[...] = out
        xnb_ref[...] = out.astype(BF16)
        xhat_ref[...] = xhat
        rstd_ref[...] = rstd

    row = pl.BlockSpec((tr, d), lambda i: (i, 0))
    vec = pl.BlockSpec((1, d), lambda i: (0, 0))
    return _call(
        body, grid=(s // tr,), in_specs=[row, row, vec, vec],
        out_specs=[row, row, row, pl.BlockSpec((tr, 1), lambda i: (i, 0))],
        out_shape=[jax.ShapeDtypeStruct((s, d), F32), jax.ShapeDtypeStruct((s, d), BF16),
                   jax.ShapeDtypeStruct((s, d), F32), jax.ShapeDtypeStruct((s, 1), F32)],
        semantics=("parallel",), name=name, args=(xres, y, gain, bias), jobs=jobs)


def _ln_bwd(dout, xhat, rstd, gain, name, jobs=()):
    s, d = dout.shape
    tr = _tile(TILES["row"], s)

    def body(d_ref, xhat_ref, rstd_ref, g_ref, dz_ref, dzb_ref, dg_ref, db_ref):
        @pl.when(pl.program_id(0) == 0)
        def _():
            dg_ref[...] = jnp.zeros_like(dg_ref)
            db_ref[...] = jnp.zeros_like(db_ref)

        dout_t, xhat_t = d_ref[...], xhat_ref[...]
        dz = _ln_grad(dout_t, xhat_t, rstd_ref[...], g_ref[...])
        dz_ref[...] = dz
        dzb_ref[...] = dz.astype(BF16)
        dg_ref[...] += jnp.sum(dout_t * xhat_t, axis=0, keepdims=True)
        db_ref[...] += jnp.sum(dout_t, axis=0, keepdims=True)

    row = pl.BlockSpec((tr, d), lambda i: (i, 0))
    vec = pl.BlockSpec((1, d), lambda i: (0, 0))
    return _call(
        body, grid=(s // tr,), in_specs=[row, row, pl.BlockSpec((tr, 1), lambda i: (i, 0)), vec],
        out_specs=[row, row, vec, vec],
        out_shape=[jax.ShapeDtypeStruct((s, d), F32), jax.ShapeDtypeStruct((s, d), BF16),
                   jax.ShapeDtypeStruct((1, d), F32), jax.ShapeDtypeStruct((1, d), F32)],
        semantics=("arbitrary",), name=name, args=(dout, xhat, rstd, gain), jobs=jobs)


def _ln_loss(xres, y, gain, bias, target, name, jobs=()):
    s, d = xres.shape
    tr = _tile(TILES["row"], s)

    def body(x_ref, y_ref, g_ref, b_ref, t_ref, loss_ref, dz_ref, dzb_ref, dg_ref, db_ref):
        @pl.when(pl.program_id(0) == 0)
        def _():
            loss_ref[...] = jnp.zeros_like(loss_ref)
            dg_ref[...] = jnp.zeros_like(dg_ref)
            db_ref[...] = jnp.zeros_like(db_ref)

        xhat, rstd = _ln_stats(ALPHA * x_ref[...] + y_ref[...])
        diff = xhat * g_ref[...] + b_ref[...] - t_ref[...]
        per_row = jnp.mean(diff * diff, axis=-1, keepdims=True)
        loss_ref[...] += 0.5 * jnp.sum(per_row, axis=0, keepdims=True)
        dout = diff * (1.0 / d)
        dz = _ln_grad(dout, xhat, rstd, g_ref[...])
        dz_ref[...] = dz
        dzb_ref[...] = dz.astype(BF16)
        dg_ref[...] += jnp.sum(dout * xhat, axis=0, keepdims=True)
        db_ref[...] += jnp.sum(dout, axis=0, keepdims=True)

    row = pl.BlockSpec((tr, d), lambda i: (i, 0))
    vec = pl.BlockSpec((1, d), lambda i: (0, 0))
    return _call(
        body, grid=(s // tr,), in_specs=[row, row, vec, vec, row],
        out_specs=[pl.BlockSpec((1, 128), lambda i: (0, 0)), row, row, vec, vec],
        out_shape=[jax.ShapeDtypeStruct((1, 128), F32), jax.ShapeDtypeStruct((s, d), F32),
                   jax.ShapeDtypeStruct((s, d), BF16), jax.ShapeDtypeStruct((1, d), F32),
                   jax.ShapeDtypeStruct((1, d), F32)],
        semantics=("arbitrary",), name=name, args=(xres, y, gain, bias, target), jobs=jobs)


def _softmax_rows(s):
    e = jnp.exp(s - jnp.max(s, axis=-1, keepdims=True))
    return e / jnp.sum(e, axis=-1, keepdims=True)


def _attn_fwd(q, k, v, name, jobs=()):
    s, d = q.shape
    m = k.shape[0]
    hd = d // XATTN_HEADS
    ts = _tile(TILES["attn"], s)
    scale = hd ** -0.5

    def body(q_ref, k_ref, v_ref, o_ref):
        for h in range(XATTN_HEADS):
            hs = slice(h * hd, (h + 1) * hd)
            sc = lax.dot_general(q_ref[:, hs], k_ref[:, hs], NT_DIMS, preferred_element_type=F32) * scale
            p = _softmax_rows(sc).astype(BF16)
            o_ref[:, hs] = jnp.dot(p, v_ref[:, hs], preferred_element_type=F32).astype(BF16)

    row = pl.BlockSpec((ts, d), lambda i: (i, 0))
    memb = pl.BlockSpec((m, d), lambda i: (0, 0))
    return _call(
        body, grid=(s // ts,), in_specs=[row, memb, memb], out_specs=[row],
        out_shape=[jax.ShapeDtypeStruct((s, d), BF16)],
        semantics=("parallel",), name=name, args=(q, k, v), jobs=jobs)[0]


def _attn_bwd(q, k, v, do, name, jobs=()):
    s, d = q.shape
    m = k.shape[0]
    hd = d // XATTN_HEADS
    ts = _tile(TILES["attn"], s)
    scale = hd ** -0.5

    def body(q_ref, k_ref, v_ref, do_ref, dq_ref, dk_ref, dv_ref):
        @pl.when(pl.program_id(0) == 0)
        def _():
            dk_ref[...] = jnp.zeros_like(dk_ref)
            dv_ref[...] = jnp.zeros_like(dv_ref)

        for h in range(XATTN_HEADS):
            hs = slice(h * hd, (h + 1) * hd)
            qh, kh, vh, doh = q_ref[:, hs], k_ref[:, hs], v_ref[:, hs], do_ref[:, hs]
            sc = lax.dot_general(qh, kh, NT_DIMS, preferred_element_type=F32) * scale
            p = _softmax_rows(sc)
            pb = p.astype(BF16)
            dp = lax.dot_general(doh, vh, NT_DIMS, preferred_element_type=F32)
            ds = (p * (dp - jnp.sum(dp * p, axis=-1, keepdims=True)) * scale).astype(BF16)
            dq_ref[:, hs] = jnp.dot(ds, kh, preferred_element_type=F32).astype(BF16)
            dk_ref[:, hs] += lax.dot_general(ds, qh, TN_DIMS, preferred_element_type=F32)
            dv_ref[:, hs] += lax.dot_general(pb, doh, TN_DIMS, preferred_element_type=F32)

    row = pl.BlockSpec((ts, d), lambda i: (i, 0))
    memb = pl.BlockSpec((m, d), lambda i: (0, 0))
    return _call(
        body, grid=(s // ts,), in_specs=[row, memb, memb, row], out_specs=[row, memb, memb],
        out_shape=[jax.ShapeDtypeStruct((s, d), BF16), jax.ShapeDtypeStruct((m, d), F32),
                   jax.ShapeDtypeStruct((m, d), F32)],
        semantics=("arbitrary",), name=name, args=(q, k, v, do), jobs=jobs)


def _sigmoid(x):
    return 1.0 / (1.0 + jnp.exp(-x))


def _log1p(x):
    u = 1.0 + x
    return jnp.where(u == 1.0, x, jnp.log(u) * (x / jnp.where(u == 1.0, 1.0, u - 1.0)))


def _softplus(x):
    return jnp.maximum(x, 0.0) + _log1p(jnp.exp(-jnp.abs(x)))


def _expm1(x):
    series = x * (1.0 + x * 0.5 * (1.0 + x * (1.0 / 3.0) * (1.0 + x * 0.25 * (1.0 + x * 0.2 * (1.0 + x * (1.0 / 6.0))))))
    return jnp.where(jnp.abs(x) < 0.1, series, jnp.exp(x) - 1.0)


GELU_K = 0.7978845608028654
GELU_C = 0.044715


def _gelu(x):
    return 0.5 * x * (1.0 + jnp.tanh(GELU_K * (x + GELU_C * (x * x * x))))


def _gelu_grad(x):
    th = jnp.tanh(GELU_K * (x + GELU_C * (x * x * x)))
    return 0.5 * (1.0 + th) + 0.5 * x * (1.0 - th * th) * GELU_K * (1.0 + 3.0 * GELU_C * x * x)


def _window_sum(ext_ref, first, rows, cols, w, step):
    acc = ext_ref[first:first + rows, cols]
    for kk in range(1, w):
        acc = acc + ext_ref[first + step * kk:first + step * kk + rows, cols]
    return acc


def _lru_gates(c_s, wa_ref, ba_ref, wx_ref, bx_ref, lam_ref, t_idx, hd, r_s, i_s, a_s, mult_s):
    sp = _softplus(-lam_ref[...])
    for h in range(LRU_HEADS):
        hs = slice(h * hd, (h + 1) * hd)
        chb = c_s[:, hs].astype(BF16)
        r = _sigmoid(jnp.dot(chb, wa_ref[h], preferred_element_type=F32) + ba_ref[:, hs])
        ig = _sigmoid(jnp.dot(chb, wx_ref[h], preferred_element_type=F32) + bx_ref[:, hs])
        log_a = -LRU_C * r * sp[:, hs]
        mult = jnp.sqrt(-_expm1(2.0 * log_a))
        r_s[:, hs] = r
        i_s[:, hs] = ig
        a_s[:, hs] = jnp.exp(log_a)
        mult_s[:, hs] = jnp.where(t_idx == 0, 1.0, mult)


def _conv(ext_ref, cw_ref, cb_ref, rows):
    acc = cb_ref[...] + cw_ref[0:1, :] * ext_ref[CONV_HALO - 3:CONV_HALO - 3 + rows, :]
    for kk in range(1, CONV_WIDTH):
        off = CONV_HALO - (CONV_WIDTH - 1) + kk
        acc = acc + cw_ref[kk:kk + 1, :] * ext_ref[off:off + rows, :]
    return acc


def _mixer_fwd(proj, wp, bp, ps, cw, cb, wa, ba, wx, bx, lam, name, jobs=()):
    s, p3 = proj.shape
    p = p3 // 3
    cg, hd = p // N_POOL_GROUPS, p // LRU_HEADS
    t = _tile(TILES["mixer"], s)

    def body(up_ref, ul_ref, ug_ref, wp_ref, bp_ref, ps_ref, cw_ref, cb_ref, wa_ref, ba_ref, wx_ref, bx_ref,
             lam_ref, ycat_ref, h_ref, extp, extl, hc, c_s, r_s, i_s, a_s, b_s):
        i = pl.program_id(0)

        @pl.when(i == 0)
        def _():
            extp[0:POOL_HALO, :] = jnp.zeros((POOL_HALO, p), F32)
            extl[0:CONV_HALO, :] = jnp.zeros((CONV_HALO, p), F32)
            hc[...] = jnp.zeros_like(hc)

        t_idx = i * t + lax.broadcasted_iota(jnp.int32, (t, 1), 0)

        extp[POOL_HALO:POOL_HALO + t, :] = up_ref[...]
        for g, w in enumerate(POOL_WINDOWS):
            cs = slice(g * cg, (g + 1) * cg)
            cnt = jnp.minimum(t_idx + 1, w).astype(F32)
            mixed = _window_sum(extp, POOL_HALO, t, cs, w, -1) / cnt - up_ref[:, cs]
            pre = jnp.dot(mixed.astype(BF16), wp_ref[g], preferred_element_type=F32) + bp_ref[:, cs]
            ycat_ref[:, cs] = (pre * ps_ref[:, cs]).astype(BF16)
        extp[0:POOL_HALO, :] = extp[t:t + POOL_HALO, :]

        extl[CONV_HALO:CONV_HALO + t, :] = ul_ref[...]
        c_s[...] = _conv(extl, cw_ref, cb_ref, t)
        extl[0:CONV_HALO, :] = extl[t:t + CONV_HALO, :]
        _lru_gates(c_s, wa_ref, ba_ref, wx_ref, bx_ref, lam_ref, t_idx, hd, r_s, i_s, a_s, b_s)
        b_s[...] = b_s[...] * (i_s[...] * c_s[...])

        rows = lax.broadcasted_iota(jnp.int32, (SUBLANES, p), 0)

        def block(bi, h):
            r0 = pl.multiple_of(bi * SUBLANES, SUBLANES)
            at = a_s[pl.ds(r0, SUBLANES), :]
            bt = b_s[pl.ds(r0, SUBLANES), :]
            out = jnp.zeros((SUBLANES, p), F32)
            for j in range(SUBLANES):
                h = at[j:j + 1, :] * h + bt[j:j + 1, :]
                out = jnp.where(rows == j, h, out)
            h_ref[pl.ds(r0, SUBLANES), :] = out
            return h

        hc[0:1, :] = lax.fori_loop(0, t // SUBLANES, block, hc[0:1, :])
        ycat_ref[:, p:2 * p] = (h_ref[...] * _gelu(ug_ref[...])).astype(BF16)

    def col(j):
        return pl.BlockSpec((t, p), lambda i: (i, j))

    def whole(a):
        nd = a.ndim
        return pl.BlockSpec(a.shape, lambda i: (0,) * nd)

    consts = (wp, bp, ps, cw, cb, wa, ba, wx, bx, lam)
    tile = pltpu.VMEM((t, p), F32)
    return _call(
        body, grid=(s // t,), in_specs=[col(0), col(1), col(2)] + [whole(a) for a in consts],
        out_specs=[pl.BlockSpec((t, 2 * p), lambda i: (i, 0)), pl.BlockSpec((t, p), lambda i: (i, 0))],
        out_shape=[jax.ShapeDtypeStruct((s, 2 * p), BF16), jax.ShapeDtypeStruct((s, p), F32)],
        scratch_shapes=[pltpu.VMEM((t + POOL_HALO, p), F32), pltpu.VMEM((t + CONV_HALO, p), F32),
                        pltpu.VMEM((SUBLANES, p), F32), tile, tile, tile, tile, tile],
        semantics=("arbitrary",), name=name, args=(proj, proj, proj, *consts), jobs=jobs)


def _mixer_bwd(dycat, proj, hsave, wp, bp, ps, cw, cb, wa, ba, wx, bx, lam, name, jobs=()):
    s, p3 = proj.shape
    p = p3 // 3
    cg, hd = p // N_POOL_GROUPS, p // LRU_HEADS
    t = _tile(TILES["mixer"], s)
    nt = s // t

    def body(dyp_ref, dyl_ref, up_ref, ul_ref, ug_ref, upp_ref, ulp_ref, h_ref, hp_ref,
             wp_ref, bp_ref, ps_ref, cw_ref, cb_ref, wa_ref, ba_ref, wx_ref, bx_ref, lam_ref,
             dproj_ref, dwp_ref, dbp_ref, dps_ref, dcw_ref, dcb_ref, dwa_ref, dba_ref, dwx_ref, dbx_ref, dlam_ref,
             extp, extg, extl, extdc, exth, ghc, c_s, r_s, i_s, a_s, mult_s, gh_s):
        i = pl.program_id(0)
        ib = nt - 1 - i

        @pl.when(i == 0)
        def _():
            for ref in (dwp_ref, dbp_ref, dps_ref, dcw_ref, dcb_ref, dwa_ref, dba_ref, dwx_ref, dbx_ref, dlam_ref):
                ref[...] = jnp.zeros_like(ref)
            extg[t:t + POOL_HALO, :] = jnp.zeros((POOL_HALO, p), F32)
            extdc[t:t + CONV_HALO, :] = jnp.zeros((CONV_HALO, p), F32)
            ghc[...] = jnp.zeros_like(ghc)

        t_idx = ib * t + lax.broadcasted_iota(jnp.int32, (t, 1), 0)
        seq_start = ib == 0

        extl[0:CONV_HALO, :] = jnp.where(seq_start, 0.0, ulp_ref[...])
        extl[CONV_HALO:CONV_HALO + t, :] = ul_ref[...]
        c_s[...] = _conv(extl, cw_ref, cb_ref, t)
        _lru_gates(c_s, wa_ref, ba_ref, wx_ref, bx_ref, lam_ref, t_idx, hd, r_s, i_s, a_s, mult_s)
        exth[0:SUBLANES, :] = jnp.where(seq_start, 0.0, hp_ref[...])
        exth[SUBLANES:SUBLANES + t, :] = h_ref[...]

        ug = ug_ref[...]
        dyl = dyl_ref[...]
        dproj_ref[:, 2 * p:3 * p] = (dyl * h_ref[...] * _gelu_grad(ug)).astype(BF16)
        gh_s[...] = dyl * _gelu(ug)

        rows = lax.broadcasted_iota(jnp.int32, (SUBLANES, p), 0)
        nblk = t // SUBLANES

        def block(bi, carry):
            r0 = pl.multiple_of((nblk - 1 - bi) * SUBLANES, SUBLANES)
            at = a_s[pl.ds(r0, SUBLANES), :]
            dt = gh_s[pl.ds(r0, SUBLANES), :]
            out = jnp.zeros((SUBLANES, p), F32)
            for j in range(SUBLANES - 1, -1, -1):
                gh = dt[j:j + 1, :] + carry
                out = jnp.where(rows == j, gh, out)
                carry = at[j:j + 1, :] * gh
            gh_s[pl.ds(r0, SUBLANES), :] = out
            return carry

        ghc[0:1, :] = lax.fori_loop(0, nblk, block, ghc[0:1, :])

        sp = _softplus(-lam_ref[...])
        dsp_dlam = -_sigmoid(-lam_ref[...])
        for h in range(LRU_HEADS):
            hs = slice(h * hd, (h + 1) * hd)
            gh, a, mult, r, ig, c = gh_s[:, hs], a_s[:, hs], mult_s[:, hs], r_s[:, hs], i_s[:, hs], c_s[:, hs]
            hprev = exth[SUBLANES - 1:SUBLANES - 1 + t, hs]
            dmult = gh * (ig * c)
            dlog_a = a * gh * hprev + jnp.where(t_idx == 0, 0.0, -dmult * a * a / mult)
            dlam_ref[:, hs] += jnp.sum(dlog_a * r, axis=0, keepdims=True) * (-LRU_C) * dsp_dlam[:, hs]
            dpa = dlog_a * (-LRU_C * sp[:, hs]) * r * (1.0 - r)
            dpx = gh * mult * c * ig * (1.0 - ig)
            dpab, dpxb, chb = dpa.astype(BF16), dpx.astype(BF16), c.astype(BF16)
            dwa_ref[h] += lax.dot_general(chb, dpab, TN_DIMS, preferred_element_type=F32)
            dwx_ref[h] += lax.dot_general(chb, dpxb, TN_DIMS, preferred_element_type=F32)
            dba_ref[:, hs] += jnp.sum(dpa, axis=0, keepdims=True)
            dbx_ref[:, hs] += jnp.sum(dpx, axis=0, keepdims=True)
            dc = (gh * mult * ig
                  + lax.dot_general(dpab, wa_ref[h], NT_DIMS, preferred_element_type=F32)
                  + lax.dot_general(dpxb, wx_ref[h], NT_DIMS, preferred_element_type=F32))
            extdc[0:t, hs] = dc
            dcb_ref[:, hs] += jnp.sum(dc, axis=0, keepdims=True)
            for kk in range(CONV_WIDTH):
                off = CONV_HALO - (CONV_WIDTH - 1) + kk
                dcw_ref[kk:kk + 1, hs] += jnp.sum(dc * extl[off:off + t, hs], axis=0, keepdims=True)
        du_lru = cw_ref[0:1, :] * extdc[CONV_WIDTH - 1:CONV_WIDTH - 1 + t, :]
        for kk in range(1, CONV_WIDTH):
            off = CONV_WIDTH - 1 - kk
            du_lru = du_lru + cw_ref[kk:kk + 1, :] * extdc[off:off + t, :]
        dproj_ref[:, p:2 * p] = du_lru.astype(BF16)
        extdc[t:t + CONV_HALO, :] = extdc[0:CONV_HALO, :]

        extp[0:POOL_HALO, :] = jnp.where(seq_start, 0.0, upp_ref[...])
        extp[POOL_HALO:POOL_HALO + t, :] = up_ref[...]
        for g, w in enumerate(POOL_WINDOWS):
            cs = slice(g * cg, (g + 1) * cg)
            cnt = jnp.minimum(t_idx + 1, w).astype(F32)
            mixed = (_window_sum(extp, POOL_HALO, t, cs, w, -1) / cnt - up_ref[:, cs]).astype(BF16)
            pre = jnp.dot(mixed, wp_ref[g], preferred_element_type=F32) + bp_ref[:, cs]
            dyp = dyp_ref[:, cs]
            dps_ref[:, cs] += jnp.sum(dyp * pre, axis=0, keepdims=True)
            dpre = dyp * ps_ref[:, cs]
            dpreb = dpre.astype(BF16)
            dbp_ref[:, cs] += jnp.sum(dpre, axis=0, keepdims=True)
            dwp_ref[g] += lax.dot_general(mixed, dpreb, TN_DIMS, preferred_element_type=F32)
            dmixed = lax.dot_general(dpreb, wp_ref[g], NT_DIMS, preferred_element_type=F32)
            extg[0:t, cs] = dmixed / cnt
            dproj_ref[:, cs] = (_window_sum(extg, 0, t, cs, w, 1) - dmixed).astype(BF16)
        extg[t:t + POOL_HALO, :] = extg[0:POOL_HALO, :]

    def col(j):
        return pl.BlockSpec((t, p), lambda i: (nt - 1 - i, j))

    def prev(rows, j):
        per = t // rows
        return pl.BlockSpec((rows, p), lambda i: (jnp.maximum((nt - 1 - i) * per - 1, 0), j))

    def whole(a):
        nd = a.ndim
        return pl.BlockSpec(a.shape, lambda i: (0,) * nd)

    consts = (wp, bp, ps, cw, cb, wa, ba, wx, bx, lam)
    grads = (wp, bp, ps, cw, cb, wa, ba, wx, bx, lam)
    tile = pltpu.VMEM((t, p), F32)
    return _call(
        body, grid=(nt,),
        in_specs=[col(0), col(1), col(0), col(1), col(2), prev(POOL_HALO, 0), prev(CONV_HALO, 1), col(0),
                  prev(SUBLANES, 0)] + [whole(a) for a in consts],
        out_specs=[pl.BlockSpec((t, 3 * p), lambda i: (nt - 1 - i, 0))] + [whole(a) for a in grads],
        out_shape=[jax.ShapeDtypeStruct((s, 3 * p), BF16)] + [jax.ShapeDtypeStruct(a.shape, F32) for a in grads],
        scratch_shapes=[pltpu.VMEM((t + POOL_HALO, p), F32), pltpu.VMEM((t + POOL_HALO, p), F32),
                        pltpu.VMEM((t + CONV_HALO, p), F32), pltpu.VMEM((t + CONV_HALO, p), F32),
                        pltpu.VMEM((t + SUBLANES, p), F32), pltpu.VMEM((SUBLANES, p), F32),
                        tile, tile, tile, tile, tile, tile],
        semantics=("arbitrary",), name=name,
        args=(dycat, dycat, proj, proj, proj, proj, proj, hsave, hsave, *consts), jobs=jobs)


def _pair_add(parts, got, core, name):
    n, r, c = got.shape
    tr = _tile(TILES["add"], r)

    def body(core_ref, a_ref, b_ref, o_ref):
        del core_ref
        o_ref[...] = (a_ref[...].astype(F32) + b_ref[...].astype(F32)).astype(o_ref.dtype)

    blk = pl.BlockSpec((None, tr, c), lambda k, i, core_ref: (k, i, 0))
    mine = pl.BlockSpec((None, tr, c), lambda k, i, core_ref: (2 * k + core_ref[0], i, 0))
    return _call(body, grid=(n, r // tr), in_specs=[mine, blk], out_specs=[blk],
                 out_shape=[jax.ShapeDtypeStruct(got.shape, got.dtype)], semantics=("parallel", "parallel"),
                 name=name, args=(parts, got), index=core)[0]


def _sum_parts(parts, name):
    n, r, c = parts.shape
    tr = _tile(TILES["adam"], r)

    def body(p_ref, o_ref):
        acc = p_ref[0].astype(F32)
        for d in range(1, n):
            acc = acc + p_ref[d].astype(F32)
        o_ref[...] = acc

    return _call(
        body, grid=(r // tr,), in_specs=[pl.BlockSpec((n, tr, c), lambda i: (0, i, 0))],
        out_specs=[pl.BlockSpec((tr, c), lambda i: (i, 0))], out_shape=[jax.ShapeDtypeStruct((r, c), F32)],
        semantics=("parallel",), name=name, args=(parts,))[0]


def _adamw(w, m, v, parts, name, jobs=(), own=None, chip=None):
    r, c = w.shape
    n = parts.shape[0]
    tr = _tile(TILES["adam"], r)

    def body(*refs):
        if own is not None:
            refs = refs[1:]
            own_ref, refs = refs[3], refs[:3] + refs[4:]
        w_ref, m_ref, v_ref, p_ref, g_ref, d_ref, nm_ref, nv_ref = refs
        g = p_ref[0].astype(F32)
        if own is not None:
            g = own_ref[...].astype(F32) + g
        for d in range(1, n):
            g = g + p_ref[d].astype(F32)
        nm = ADAM_B1 * m_ref[...] + (1.0 - ADAM_B1) * g
        nv = ADAM_B2 * v_ref[...] + (1.0 - ADAM_B2) * (g * g)
        m_hat = nm / (1.0 - ADAM_B1 ** ADAM_STEP)
        v_hat = nv / (1.0 - ADAM_B2 ** ADAM_STEP)
        g_ref[...] = g
        d_ref[...] = -ADAM_LR * (m_hat / (jnp.sqrt(v_hat) + ADAM_EPS) + ADAM_WD * w_ref[...])
        nm_ref[...] = nm
        nv_ref[...] = nv

    row = pl.BlockSpec((tr, c), lambda i, *_: (i, 0))
    in_specs, args = [row, row, row], [w, m, v]
    if own is not None:
        in_specs.append(pl.BlockSpec((None, tr, c), lambda i, chip_ref: (chip_ref[0], i, 0)))
        args.append(own)
    in_specs.append(pl.BlockSpec((n, tr, c), lambda i, *_: (0, i, 0)))
    args.append(parts)
    return _call(
        body, grid=(r // tr,), in_specs=in_specs, out_specs=[row] * 4, out_shape=[jax.ShapeDtypeStruct((r, c), F32)] * 4,
        semantics=("parallel",), name=name, args=args, jobs=jobs, index=chip if own is not None else None)


SMALL_ORDER = ("w_a", "w_x", "conv_w", "b_pool", "conv_b", "b_a", "b_x", "lru_lambda", "pool_scale",
               "ln1_g", "ln1_b", "ln2_g", "ln2_b", "ln3_g", "ln3_b")


def _pack_rows(a, p):
    flat = a.reshape(-1, p)
    pad = (-flat.shape[0]) % SUBLANES
    return jnp.pad(flat, ((0, pad), (0, 0))) if pad else flat


def kernel(x, mem, w_in, conv_w, conv_b, w_a, b_a, w_x, b_x, lru_lambda, w_pool, b_pool, pool_scale, w_out, ln1_g, ln1_b, w_q, w_k, w_v, w_o, ln2_g, ln2_b, w_ff1, w_ff2, ln3_g, ln3_b, loss_target, m_w_in, m_conv_w, m_conv_b, m_w_a, m_b_a, m_w_x, m_b_x, m_lru_lambda, m_w_pool, m_b_pool, m_pool_scale, m_w_out, m_ln1_g, m_ln1_b, m_w_q, m_w_k, m_w_v, m_w_o, m_ln2_g, m_ln2_b, m_w_ff1, m_w_ff2, m_ln3_g, m_ln3_b, v_w_in, v_conv_w, v_conv_b, v_w_a, v_b_a, v_w_x, v_b_x, v_lru_lambda, v_w_pool, v_b_pool, v_pool_scale, v_w_out, v_ln1_g, v_ln1_b, v_w_q, v_w_k, v_w_v, v_w_o, v_ln2_g, v_ln2_b, v_w_ff1, v_w_ff2, v_ln3_g, v_ln3_b):
    names = ("w_in", "conv_w", "conv_b", "w_a", "b_a", "w_x", "b_x", "lru_lambda", "w_pool", "b_pool", "pool_scale",
             "w_out", "ln1_g", "ln1_b", "w_q", "w_k", "w_v", "w_o", "ln2_g", "ln2_b", "w_ff1", "w_ff2", "ln3_g", "ln3_b")
    w_loc = dict(zip(names, (w_in, conv_w, conv_b, w_a, b_a, w_x, b_x, lru_lambda, w_pool, b_pool, pool_scale,
                             w_out, ln1_g, ln1_b, w_q, w_k, w_v, w_o, ln2_g, ln2_b, w_ff1, w_ff2, ln3_g, ln3_b)))
    m_loc = dict(zip(names, (m_w_in, m_conv_w, m_conv_b, m_w_a, m_b_a, m_w_x, m_b_x, m_lru_lambda, m_w_pool, m_b_pool,
                             m_pool_scale, m_w_out, m_ln1_g, m_ln1_b, m_w_q, m_w_k, m_w_v, m_w_o, m_ln2_g, m_ln2_b,
                             m_w_ff1, m_w_ff2, m_ln3_g, m_ln3_b)))
    v_loc = dict(zip(names, (v_w_in, v_conv_w, v_conv_b, v_w_a, v_b_a, v_w_x, v_b_x, v_lru_lambda, v_w_pool, v_b_pool,
                             v_pool_scale, v_w_out, v_ln1_g, v_ln1_b, v_w_q, v_w_k, v_w_v, v_w_o, v_ln2_g, v_ln2_b,
                             v_w_ff1, v_w_ff2, v_ln3_g, v_ln3_b)))
    s, d = x.shape[1], x.shape[2]
    p = conv_b.shape[1]
    cg = p // N_POOL_GROUPS
    hd = p // LRU_HEADS
    me = 4 * lax.axis_index("x") + 2 * lax.axis_index("y") + lax.axis_index("c")

    xs, mems, tgt = x[0], mem[0], loss_target[0]
    memb = mems.astype(BF16)

    gathers = {n: _Job("gather", w_loc[n][0].astype(BF16))
               for n in ("w_in", "w_out", "w_q", "w_k", "w_v", "w_o", "w_ff2", "w_pool")}
    ff1_shard = w_ff1[0].astype(BF16)
    ff1_rows = ff1_shard.shape[0] // FF1_PIECES

    def ff1_piece(i, earlier=None):
        return _Job("gather", ff1_shard[i * ff1_rows:(i + 1) * ff1_rows], window=(i * ff1_rows, ff1_shard.shape[0]),
                    into=None if earlier is None else earlier.out)
    tiny = jnp.concatenate([_pack_rows(conv_w[0], p // N_DEV),
                            _pack_rows(jnp.pad(b_pool[0], ((0, 0), (0, p // N_DEV - cg // N_DEV))), p // N_DEV)], axis=0)
    gathers["tiny"] = _Job("gather", tiny)

    def gathered(n):
        full = gathers[n].out
        if n == "w_in":
            return jnp.transpose(full, (1, 0, 2)).reshape(1, full.shape[1], -1)
        return full.reshape(1, -1, full.shape[-1])

    W = {"conv_b": conv_b, "b_a": b_a.reshape(1, p), "b_x": b_x.reshape(1, p), "lru_lambda": lru_lambda,
         "pool_scale": pool_scale, "w_a": w_a[0].astype(BF16), "w_x": w_x[0].astype(BF16)}
    for n in ("ln1_g", "ln1_b", "ln2_g", "ln2_b", "ln3_g", "ln3_b"):
        W[n] = w_loc[n]

    out_g, out_d, out_m, out_v = {}, {}, {}, {}
    pairs, quads, sums = {}, {}, {}
    core =lax.axis_index("c").astype(jnp.int32).reshape(1)
    chip = (2 * lax.axis_index("x") + lax.axis_index("y")).astype(jnp.int32).reshape(1)

    def pair(n, partial):
        pairs[n] = _Job("pair", partial.reshape(N_DEV, -1, partial.shape[-1]))
        return pairs[n]

    def quad(n, lo=0, hi=1, of=1):
        if lo == 0:
            sums[n] = _pair_add(pairs[n].src, pairs[n].out, core, "add_" + n)
        rows = sums[n].shape[1] // of
        quads[n] = _Job("quad", sums[n], window=None if (lo, hi) == (0, of) else (lo * rows, (hi - lo) * rows),
                        into=None if lo == 0 else quads[n].out)
        return quads[n]

    def update(n, parts, jobs=(), own=None):
        shp = w_loc[n].shape
        rows = parts.shape[1]
        w2, m2, v2 = (a.reshape(rows, -1) for a in (w_loc[n], m_loc[n], v_loc[n]))
        res = _adamw(w2, m2, v2, parts.reshape(parts.shape[0], rows, -1), "adamw_" + n, jobs=jobs, own=own, chip=chip)
        out_g[n], out_d[n], out_m[n], out_v[n] = (r.reshape(shp) for r in res)

    assert FF1_PIECES == 4
    xb = _to_bf16(xs, "cast_x", jobs=[gathers["w_in"], gathers["tiny"], gathers["w_pool"]])
    W["w_pool"] = jnp.transpose(gathers["w_pool"].out, (1, 0, 2, 3)).reshape(N_POOL_GROUPS, cg, cg)
    cwb = gathers["tiny"].out
    W["conv_w"] = jnp.transpose(cwb[:, :CONV_WIDTH, :], (1, 0, 2)).reshape(CONV_WIDTH, p)
    W["b_pool"] = jnp.transpose(cwb[:, SUBLANES:SUBLANES + N_POOL_GROUPS, :cg // N_DEV], (1, 0, 2)).reshape(1, p)
    mixer_consts = (W["w_pool"], W["b_pool"], W["pool_scale"], W["conv_w"], W["conv_b"], W["w_a"], W["b_a"],
                    W["w_x"], W["b_x"], W["lru_lambda"])

    w_in_full = gathered("w_in")
    piece = ff1_piece(FF1_PIECES - 1)
    (proj,) = _mm_nn(xb, w_in_full, [F32], "fwd_proj", jobs=[gathers["w_out"], piece])
    ycat, hsave = _mixer_fwd(proj, *mixer_consts, "fwd_mixer", jobs=[gathers["w_q"], gathers["w_k"]])
    (y1,) = _mm_nn(ycat, gathered("w_out"), [F32], "fwd_out", jobs=[gathers["w_v"]])
    x1, x1b, xhat1, rstd1 = _ln_fwd(xs, y1, W["ln1_g"], W["ln1_b"], "fwd_ln1", jobs=[gathers["w_o"]])
    piece = ff1_piece(0, piece)
    (q,) = _mm_nn(x1b, gathered("w_q"), [BF16], "fwd_q", jobs=[piece])
    (k,) = _mm_nn(memb, gathered("w_k"), [BF16], "fwd_k")
    (v,) = _mm_nn(memb, gathered("w_v"), [BF16], "fwd_v")
    o = _attn_fwd(q, k, v, "fwd_attn")
    piece = ff1_piece(1, piece)
    (y2,) = _mm_nn(o, gathered("w_o"), [F32], "fwd_o", jobs=[piece])
    piece = ff1_piece(2, piece)
    x2, x2b, xhat2, rstd2 = _ln_fwd(x1, y2, W["ln2_g"], W["ln2_b"], "fwd_ln2", jobs=[piece])
    w_ff1_full = piece.out

    def relu_sq(acc):
        r = jnp.maximum(acc, 0.0)
        return r, r * r

    rb, act = _mm_nn(x2b, w_ff1_full, [BF16, BF16], "fwd_ff1", epilogue=relu_sq, jobs=[gathers["w_ff2"]])
    (y3,) = _mm_nn(act, gathered("w_ff2"), [F32], "fwd_ff2")
    loss_rows, dz3, dz3b, dg3, db3 = _ln_loss(x2, y3, W["ln3_g"], W["ln3_b"], tgt, "ln3_loss")
    loss = lax.psum(loss_rows[0, 0], MESH_AXES)

    small = {"ln3_g": dg3, "ln3_b": db3}

    def add_residual(acc, e):
        return (acc + ALPHA * e,)

    dw_ff2 = _mm_tn(act, dz3b, 1, BF16, "bwd_dw_ff2")
    (dhid,) = _mm_nt(dz3b, gathered("w_ff2"), [BF16], "bwd_dact", extras=(rb,), jobs=[pair("w_ff2", dw_ff2)],
                     epilogue=lambda acc, r: (acc * (2.0 * r.astype(F32)),))
    dw_ff1 = _mm_tn(x2b, dhid, N_DEV, BF16, "bwd_dw_ff1", jobs=[quad("w_ff2", 0, 1, 2)])
    (dx2,) = _mm_nt(dhid, w_ff1_full, [F32], "bwd_dx2", epilogue=add_residual, extras=(dz3,), tk=TILES["tk"] // 2,
                    jobs=[pair("w_ff1", dw_ff1), quad("w_ff2", 1, 2, 2)])
    dz2, dz2b, small["ln2_g"], small["ln2_b"] = _ln_bwd(dx2, xhat2, rstd2, W["ln2_g"], "bwd_ln2")

    dw_o = _mm_tn(o, dz2b, 1, BF16, "bwd_dw_o")
    (do,) = _mm_nt(dz2b, gathered("w_o"), [BF16], "bwd_do", jobs=[pair("w_o", dw_o)])
    dq, dk, dv = _attn_bwd(q, k, v, do, "bwd_attn", jobs=[quad("w_o")])
    dw_q = _mm_tn(x1b, dq, 1, BF16, "bwd_dw_q")
    dw_k = _mm_tn(memb, dk.astype(BF16), 1, BF16, "bwd_dw_k")
    dw_v = _mm_tn(memb, dv.astype(BF16), 1, BF16, "bwd_dw_v")
    (dx1,) = _mm_nt(dq, gathered("w_q"), [F32], "bwd_dx1", epilogue=add_residual, extras=(dz2,),
                    jobs=[pair("w_q", dw_q), pair("w_k", dw_k), pair("w_v", dw_v), quad("w_ff1", 0, 1, 4)])
    dz1, dz1b, small["ln1_g"], small["ln1_b"] = _ln_bwd(dx1, xhat1, rstd1, W["ln1_g"], "bwd_ln1", jobs=[quad("w_q")])

    dw_out = _mm_tn(ycat, dz1b, 1, BF16, "bwd_dw_out")
    (dycat,) = _mm_nt(dz1b, gathered("w_out"), [F32], "bwd_dycat", jobs=[pair("w_out", dw_out)])
    (dproj, dwp, small["b_pool"], small["pool_scale"], small["conv_w"], small["conv_b"], small["w_a"], small["b_a"],
     small["w_x"], small["b_x"], small["lru_lambda"]) = _mixer_bwd(
        dycat, proj, hsave, *mixer_consts, "bwd_mixer", jobs=[quad("w_ff1", 1, 3, 4), quad("w_k"), quad("w_v")])
    dw_pool = jnp.transpose(dwp.astype(BF16).reshape(N_POOL_GROUPS, N_DEV, cg // N_DEV, cg), (1, 0, 2, 3))
    pack = jnp.concatenate([_pack_rows(small[n], p) for n in SMALL_ORDER], axis=0)
    small_gather = _Job("gather", pack)
    dw_in = _mm_tn(xb, dproj, 1, BF16, "bwd_dw_in", jobs=[quad("w_out"), pair("w_pool", dw_pool), small_gather])
    dw_in = jnp.transpose(dw_in.reshape(dw_in.shape[1], N_DEV, -1), (1, 0, 2))
    (gx_lo,) = _mm_nt(dproj, w_in_full, [F32], "bwd_dx_lo", epilogue=add_residual, extras=(dz1,), part=(0, 2),
                      jobs=[pair("w_in", dw_in), quad("w_pool"), quad("w_ff1", 3, 4, 4)])
    (gx_hi,) = _mm_nt(dproj, w_in_full, [F32], "bwd_dx_hi", epilogue=add_residual, extras=(dz1,), part=(1, 2),
                      jobs=[quad("w_in", 0, 1, 2)])
    grad_x = jnp.concatenate([gx_lo, gx_hi], axis=0)

    update("w_ff2", quads["w_ff2"].out, jobs=[quad("w_in", 1, 2, 2)], own=quads["w_ff2"].src)
    for n in ("w_ff1", "w_o", "w_q", "w_k", "w_v", "w_out", "w_pool", "w_in"):
        update(n, quads[n].out, own=quads[n].src)

    total = _sum_parts(small_gather.out, "sum_small")
    row = 0
    for n in SMALL_ORDER:
        size = small[n].size
        nrows = size // p
        g_full = total[row:row + nrows].reshape(small[n].shape)
        row += nrows + (-nrows) % SUBLANES
        if n == "conv_w":
            g_loc = lax.dynamic_slice_in_dim(g_full, me * (p // N_DEV), p // N_DEV, axis=1)
        elif n == "b_pool":
            g_loc = lax.dynamic_slice_in_dim(g_full.reshape(N_POOL_GROUPS, cg), me * (cg // N_DEV), cg // N_DEV, axis=1)
        else:
            g_loc = g_full
        rows = g_loc.shape[0] if n not in ("w_a", "w_x") else LRU_HEADS * hd
        update(n, g_loc.reshape(1, rows, -1))

    order = names
    return (loss, grad_x[None], *[out_g[n] for n in order], *[out_d[n] for n in order],
            *[out_m[n] for n in order], *[out_v[n] for n in order])
```

```python
import functools

import jax
import jax.numpy as jnp
from jax import lax
from jax.experimental import pallas as pl
from jax.experimental.pallas import tpu as pltpu

F32 = jnp.float32
BF16 = jnp.bfloat16

N_DEV = 8
MESH_AXES = ("x", "y", "c")
POOL_WINDOWS = (2, 4, 8, 16)
N_POOL_GROUPS = len(POOL_WINDOWS)
POOL_HALO = 16
CONV_WIDTH = 4
CONV_HALO = 8
FF1_PIECES = 4
LRU_HEADS = 8
LRU_C = 8.0
XATTN_HEADS = 4
LN_EPS = 1e-5
ALPHA = 2.0 ** 0.25
ADAM_LR = 0.001
ADAM_B1 = 0.9
ADAM_B2 = 0.999
ADAM_EPS = 1e-08
ADAM_WD = 0.01
ADAM_STEP = 10
SUBLANES = 8
VMEM_LIMIT = 56 * 1024 * 1024

NT_DIMS = (((1,), (1,)), ((), ()))
TN_DIMS = (((0,), (0,)), ((), ()))


def _params(*sem):
    return pltpu.CompilerParams(dimension_semantics=sem, vmem_limit_bytes=VMEM_LIMIT)


def _place():
    return lax.axis_index("x"), lax.axis_index("y"), lax.axis_index("c")


def _remote(src, dst, send_sem, recv_sem, to):
    return pltpu.make_async_remote_copy(src_ref=src, dst_ref=dst, send_sem=send_sem, recv_sem=recv_sem,
                                        device_id=to, device_id_type=pl.DeviceIdType.MESH)


class _Job:
    def __init__(self, kind, src, window=None, into=None):
        self.kind, self.src, self.out, self.window, self.into = kind, src, None, window, into

    def out_shape(self):
        s = self.src.shape
        if self.kind == "gather" and self.window is not None:
            s = (self.window[1],) + s[1:]
        shape = {"gather": (N_DEV,) + s, "pair": (4,) + s[1:], "quad": (3,) + s[1:]}[self.kind]
        return jax.ShapeDtypeStruct(shape, self.src.dtype)

    def scratch(self):
        n = {"gather": 7, "pair": 4, "quad": 3}[self.kind]
        sems = [pltpu.SemaphoreType.DMA((n,)), pltpu.SemaphoreType.DMA((n,))]
        if self.kind == "gather":
            sems += [pltpu.SemaphoreType.DMA((2,)), pltpu.VMEM(self.src.shape, self.src.dtype)]
        return sems

    def ops(self, src, out, *scratch):
        if self.kind == "gather":
            return _gather_ops(src, out, *scratch, first_row=None if self.window is None else self.window[0])
        if self.kind == "quad":
            return _quad_ops(src, out, *scratch, rows=self.window)
        return _pair_ops(src, out, *scratch)


def _gather_ops(x_ref, out_ref, send_sems, recv_sems, local_sems, bounce, first_row=None):
    x, y, c = _place()
    me, sibling = (x, y, c), (x, y, 1 - c)
    chips = [(1 - x, y), (x, 1 - y), (1 - x, 1 - y)]

    def slot(px, py, pc):
        block = out_ref.at[4 * px + 2 * py + pc]
        return block if first_row is None else block.at[pl.ds(first_row, x_ref.shape[0])]

    def copy(k, block, to, src=None):
        return _remote(slot(*block) if src is None else src, slot(*block), send_sems.at[k], recv_sems.at[k], to)

    mine_in = pltpu.make_async_copy(x_ref, bounce, local_sems.at[0])
    mine_out = pltpu.make_async_copy(bounce, slot(*me), local_sems.at[1])
    first = [copy(0, me, sibling, src=x_ref)] + [copy(1 + j, me, (*chip, c), src=x_ref) for j, chip in enumerate(chips)]
    passed = [copy(4 + j, (*chip, c), sibling) for j, chip in enumerate(chips)]

    def start():
        mine_in.start()
        for cp in first:
            cp.start()

    def mid():
        mine_in.wait()
        mine_out.start()
        for j, chip in enumerate(chips):
            copy(1 + j, (*chip, c), me).wait_recv()
            passed[j].start()

    def finish():
        copy(0, sibling, me).wait_recv()
        for j, chip in enumerate(chips):
            copy(4 + j, (*chip, 1 - c), me).wait_recv()
        for cp in first + passed:
            cp.wait_send()
        mine_out.wait()

    return start, mid, finish


def _pair_ops(p_ref, got_ref, send_sems, recv_sems):
    x, y, c = _place()
    give = [_remote(p_ref.at[2 * k + 1 - c], got_ref.at[k], send_sems.at[k], recv_sems.at[k], (x, y, 1 - c))
            for k in range(4)]

    def start():
        for cp in give:
            cp.start()

    def finish():
        for cp in give:
            cp.wait_recv()
        for cp in give:
            cp.wait_send()

    return start, None, finish


def _quad_ops(q_ref, out_ref, send_sems, recv_sems, rows=None):
    x, y, c = _place()

    def part(block):
        return block if rows is None else block.at[pl.ds(rows[0], rows[1])]

    copies = []
    for rel in range(1, 4):
        px = 1 - x if rel & 2 else x
        py = 1 - y if rel & 1 else y
        copies.append(_remote(part(q_ref.at[2 * px + py]), part(out_ref.at[rel - 1]), send_sems.at[rel - 1],
                              recv_sems.at[rel - 1], (px, py, c)))

    def start():
        for cp in copies:
            cp.start()

    def finish():
        for cp in copies:
            cp.wait_recv()
        for cp in copies:
            cp.wait_send()

    return start, None, finish


def _call(body, *, grid, in_specs, out_specs, out_shape, scratch_shapes=(), semantics, name, args, jobs=(), index=None):
    in_specs, out_specs, out_shape = list(in_specs), list(out_specs), list(out_shape)
    scratch_shapes, jobs = list(scratch_shapes), list(jobs)
    n_in, n_out, n_scr, n_job = len(in_specs), len(out_specs), len(scratch_shapes), len(jobs)
    n_idx = 0 if index is None else 1
    job_scratch = [j.scratch() for j in jobs]
    n_steps = functools.reduce(lambda a, b: a * b, grid, 1)
    early = n_steps - 1 - max(1, n_steps // 8) if n_steps >= 4 else None

    intos = [(k, j.into) for k, j in enumerate(jobs) if j.into is not None]

    def hosted(*refs):
        idx, refs = refs[:n_idx], refs[n_idx:]
        ins, jin = refs[:n_in], refs[n_in:n_in + n_job]
        o0 = n_in + n_job + len(intos)
        outs, jout = refs[o0:o0 + n_out], refs[o0 + n_out:o0 + n_out + n_job]
        s0 = o0 + n_out + n_job
        scr, jscr = refs[s0:s0 + n_scr], refs[s0 + n_scr:]
        ops, at = [], 0
        for k, j in enumerate(jobs):
            ops.append(j.ops(jin[k], jout[k], *jscr[at:at + len(job_scratch[k])]))
            at += len(job_scratch[k])
        step = functools.reduce(lambda acc, a: acc * grid[a] + pl.program_id(a), range(len(grid)), 0)
        mids = [mid for _, mid, _ in ops if mid is not None]

        @pl.when(step == 0)
        def _():
            for start, _, _ in ops:
                start()

        if mids and early is not None:
            @pl.when(step == early)
            def _():
                for mid in mids:
                    mid()

        body(*idx, *ins, *outs, *scr)

        @pl.when(step == n_steps - 1)
        def _():
            if early is None:
                for mid in mids:
                    mid()
            for _, _, finish in ops:
                finish()

    hbm = pl.BlockSpec(memory_space=pl.ANY)
    spec = pltpu.PrefetchScalarGridSpec(
        num_scalar_prefetch=n_idx, grid=grid, in_specs=in_specs + [hbm] * (n_job + len(intos)),
        out_specs=out_specs + [hbm] * n_job, scratch_shapes=scratch_shapes + [s for js in job_scratch for s in js])
    aliases = {n_idx + n_in + n_job + q: n_out + k for q, (k, _) in enumerate(intos)}
    res = pl.pallas_call(
        hosted if jobs else body, grid_spec=spec, out_shape=out_shape + [j.out_shape() for j in jobs],
        input_output_aliases=aliases,
        compiler_params=_params(*(["arbitrary"] * len(grid) if jobs else semantics)), name=name,
    )(*([] if index is None else [index]), *args, *[j.src for j in jobs], *[buf for _, buf in intos])
    for j, o in zip(jobs, res[n_out:]):
        j.out = o
    return res[:n_out]


def _to_bf16(a, name, jobs=()):
    r, c = a.shape
    tr = _tile(TILES["row"], r)

    def body(a_ref, o_ref):
        o_ref[...] = a_ref[...].astype(BF16)

    row = pl.BlockSpec((tr, c), lambda i: (i, 0))
    return _call(body, grid=(r // tr,), in_specs=[row], out_specs=[row], out_shape=[jax.ShapeDtypeStruct((r, c), BF16)],
                 semantics=("parallel",), name=name, args=(a,), jobs=jobs)[0]


TILES = dict(tm=1024, tn=1024, tk=2048, row=256, attn=512, mixer=256, adam=128, add=1024)


def _tile(pref, n):
    for t in range(min(pref, n), 0, -1):
        if n % t == 0 and (t % SUBLANES == 0 or t == n):
            return t
    return n


def _accumulate(acc, step, n_steps, product, write):
    if n_steps == 1:
        write(product())
        return

    @pl.when(step == 0)
    def _():
        acc[...] = product()

    @pl.when(jnp.logical_and(step > 0, step < n_steps - 1))
    def _():
        acc[...] += product()

    @pl.when(step == n_steps - 1)
    def _():
        write(acc[...] + product())


def _acc_scratch(n_steps, tm, tn):
    return [] if n_steps == 1 else [pltpu.VMEM((tm, tn), F32)]


def _mm_nn(a, b3, out_dtypes, name, *, tm=None, tn=None, tk=None, epilogue=None, extras=(), jobs=()):
    m, k = a.shape
    g, k2, ns = b3.shape
    assert k == k2
    n = g * ns
    tm, tn, tk = _tile(tm or TILES["tm"], m), _tile(tn or TILES["tn"], ns), _tile(tk or TILES["tk"], k)
    nb, nk = ns // tn, k // tk
    n_ex, n_out = len(extras), len(out_dtypes)

    def body(*refs):
        a_ref, b_ref = refs[:2]
        ex = refs[2:2 + n_ex]
        outs = refs[2 + n_ex:2 + n_ex + n_out]
        acc = refs[-1] if nk > 1 else None

        def write(r):
            res = epilogue(r, *[e[...] for e in ex]) if epilogue is not None else (r,)
            for o, v in zip(outs, res):
                o[...] = v.astype(o.dtype)

        _accumulate(acc, pl.program_id(2), nk,
                    lambda: jnp.dot(a_ref[...], b_ref[...], preferred_element_type=F32), write)

    tile_out = pl.BlockSpec((tm, tn), lambda i, j, kk: (i, j))
    return _call(
        body, grid=(m // tm, n // tn, nk),
        in_specs=[pl.BlockSpec((tm, tk), lambda i, j, kk: (i, kk)),
                  pl.BlockSpec((None, tk, tn), lambda i, j, kk: (j // nb, kk, j % nb))] + [tile_out] * n_ex,
        out_specs=[tile_out] * n_out,
        out_shape=[jax.ShapeDtypeStruct((m, n), d) for d in out_dtypes],
        scratch_shapes=_acc_scratch(nk, tm, tn),
        semantics=("parallel", "parallel", "arbitrary"), name=name, args=(a, b3, *extras), jobs=jobs)


def _mm_nt(a, b3, out_dtypes, name, *, tm=None, tn=None, tk=None, epilogue=None, extras=(), jobs=(), part=(0, 1)):
    m, n = a.shape
    g, k, ns = b3.shape
    assert n == g * ns
    tm, tn, tk = _tile(tm or TILES["tm"], m // part[1]), _tile(tn or TILES["tn"], k), _tile(tk or TILES["tk"], ns)
    nb, nc = ns // tk, n // tk
    n_ex, n_out = len(extras), len(out_dtypes)
    m_blocks = m // tm // part[1]
    first = part[0] * m_blocks

    def body(*refs):
        a_ref, b_ref = refs[:2]
        ex = refs[2:2 + n_ex]
        outs = refs[2 + n_ex:2 + n_ex + n_out]
        acc = refs[-1] if nc > 1 else None

        def write(r):
            res = epilogue(r, *[e[...] for e in ex]) if epilogue is not None else (r,)
            for o, v in zip(outs, res):
                o[...] = v.astype(o.dtype)

        _accumulate(acc, pl.program_id(2), nc,
                    lambda: lax.dot_general(a_ref[...], b_ref[...], NT_DIMS, preferred_element_type=F32), write)

    tile_out = pl.BlockSpec((tm, tn), lambda i, j, cc: (i, j))
    tile_ex = pl.BlockSpec((tm, tn), lambda i, j, cc: (first + i, j))
    return _call(
        body, grid=(m_blocks, k // tn, nc),
        in_specs=[pl.BlockSpec((tm, tk), lambda i, j, cc: (first + i, cc)),
                  pl.BlockSpec((None, tn, tk), lambda i, j, cc: (cc // nb, j, cc % nb))] + [tile_ex] * n_ex,
        out_specs=[tile_out] * n_out,
        out_shape=[jax.ShapeDtypeStruct((m_blocks * tm, k), d) for d in out_dtypes],
        scratch_shapes=_acc_scratch(nc, tm, tn),
        semantics=("parallel", "parallel", "arbitrary"), name=name, args=(a, b3, *extras), jobs=jobs)


def _mm_tn(a, b, g, out_dtype, name, *, tm=None, tn=None, tk=None, jobs=()):
    s, m = a.shape
    s2, n = b.shape
    assert s == s2 and n % g == 0
    ns = n // g
    tm, tn, tk = _tile(tm or TILES["tm"], m), _tile(tn or TILES["tn"], ns), _tile(tk or TILES["tk"], s)
    nb, nc = ns // tn, s // tk

    def body(a_ref, b_ref, o_ref, *scratch):
        def write(r):
            o_ref[...] = r.astype(o_ref.dtype)

        _accumulate(scratch[0] if nc > 1 else None, pl.program_id(2), nc,
                    lambda: lax.dot_general(a_ref[...], b_ref[...], TN_DIMS, preferred_element_type=F32), write)

    return _call(
        body, grid=(m // tm, n // tn, nc),
        in_specs=[pl.BlockSpec((tk, tm), lambda i, j, cc: (cc, i)),
                  pl.BlockSpec((tk, tn), lambda i, j, cc: (cc, j))],
        out_specs=[pl.BlockSpec((None, tm, tn), lambda i, j, cc: (j // nb, i, j % nb))],
        out_shape=[jax.ShapeDtypeStruct((g, m, ns), out_dtype)],
        scratch_shapes=_acc_scratch(nc, tm, tn),
        semantics=("parallel", "parallel", "arbitrary"), name=name, args=(a, b), jobs=jobs)[0]


def _ln_stats(z):
    mu = jnp.mean(z, axis=-1, keepdims=True)
    zc = z - mu
    var = jnp.mean(zc * zc, axis=-1, keepdims=True)
    rstd = lax.rsqrt(var + LN_EPS)
    return zc * rstd, rstd


def _ln_grad(dout, xhat, rstd, gain):
    dxhat = dout * gain
    m1 = jnp.mean(dxhat, axis=-1, keepdims=True)
    m2 = jnp.mean(dxhat * xhat, axis=-1, keepdims=True)
    return rstd * (dxhat - m1 - xhat * m2)


def _ln_fwd(xres, y, gain, bias, name, jobs=()):
    s, d = xres.shape
    tr = _tile(TILES["row"], s)

    def body(x_ref, y_ref, g_ref, b_ref, xn_ref, xnb_ref, xhat_ref, rstd_ref):
        xhat, rstd = _ln_stats(ALPHA * x_ref[...] + y_ref[...])
        out = xhat * g_ref[...] + b_ref[...]
        xn_ref[...] = out
        xnb_ref[...] = out.astype(BF16)
        xhat_ref[...] = xhat
        rstd_ref[...] = rstd

    row = pl.BlockSpec((tr, d), lambda i: (i, 0))
    vec = pl.BlockSpec((1, d), lambda i: (0, 0))
    return _call(
        body, grid=(s // tr,), in_specs=[row, row, vec, vec],
        out_specs=[row, row, row, pl.BlockSpec((tr, 1), lambda i: (i, 0))],
        out_shape=[jax.ShapeDtypeStruct((s, d), F32), jax.ShapeDtypeStruct((s, d), BF16),
                   jax.ShapeDtypeStruct((s, d), F32), jax.ShapeDtypeStruct((s, 1), F32)],
        semantics=("parallel",), name=name, args=(xres, y, gain, bias), jobs=jobs)


def _ln_bwd(dout, xhat, rstd, gain, name, jobs=()):
    s, d = dout.shape
    tr = _tile(TILES["row"], s)

    def body(d_ref, xhat_ref, rstd_ref, g_ref, dz_ref, dzb_ref, dg_ref, db_ref):
        @pl.when(pl.program_id(0) == 0)
        def _():
            dg_ref[...] = jnp.zeros_like(dg_ref)
            db_ref[...] = jnp.zeros_like(db_ref)

        dout_t, xhat_t = d_ref[...], xhat_ref[...]
        dz = _ln_grad(dout_t, xhat_t, rstd_ref[...], g_ref[...])
        dz_ref[...] = dz
        dzb_ref[...] = dz.astype(BF16)
        dg_ref[...] += jnp.sum(dout_t * xhat_t, axis=0, keepdims=True)
        db_ref[...] += jnp.sum(dout_t, axis=0, keepdims=True)

    row = pl.BlockSpec((tr, d), lambda i: (i, 0))
    vec = pl.BlockSpec((1, d), lambda i: (0, 0))
    return _call(
        body, grid=(s // tr,), in_specs=[row, row, pl.BlockSpec((tr, 1), lambda i: (i, 0)), vec],
        out_specs=[row, row, vec, vec],
        out_shape=[jax.ShapeDtypeStruct((s, d), F32), jax.ShapeDtypeStruct((s, d), BF16),
                   jax.ShapeDtypeStruct((1, d), F32), jax.ShapeDtypeStruct((1, d), F32)],
        semantics=("arbitrary",), name=name, args=(dout, xhat, rstd, gain), jobs=jobs)


def _ln_loss(xres, y, gain, bias, target, name, jobs=()):
    s, d = xres.shape
    tr = _tile(TILES["row"], s)

    def body(x_ref, y_ref, g_ref, b_ref, t_ref, loss_ref, dz_ref, dzb_ref, dg_ref, db_ref):
        @pl.when(pl.program_id(0) == 0)
        def _():
            loss_ref[...] = jnp.zeros_like(loss_ref)
            dg_ref[...] = jnp.zeros_like(dg_ref)
            db_ref[...] = jnp.zeros_like(db_ref)

        xhat, rstd = _ln_stats(ALPHA * x_ref[...] + y_ref[...])
        diff = xhat * g_ref[...] + b_ref[...] - t_ref[...]
        per_row = jnp.mean(diff * diff, axis=-1, keepdims=True)
        loss_ref[...] += 0.5 * jnp.sum(per_row, axis=0, keepdims=True)
        dout = diff * (1.0 / d)
        dz = _ln_grad(dout, xhat, rstd, g_ref[...])
        dz_ref[...] = dz
        dzb_ref[...] = dz.astype(BF16)
        dg_ref[...] += jnp.sum(dout * xhat, axis=0, keepdims=True)
        db_ref[...] += jnp.sum(dout, axis=0, keepdims=True)

    row = pl.BlockSpec((tr, d), lambda i: (i, 0))
    vec = pl.BlockSpec((1, d), lambda i: (0, 0))
    return _call(
        body, grid=(s // tr,), in_specs=[row, row, vec, vec, row],
        out_specs=[pl.BlockSpec((1, 128), lambda i: (0, 0)), row, row, vec, vec],
        out_shape=[jax.ShapeDtypeStruct((1, 128), F32), jax.ShapeDtypeStruct((s, d), F32),
                   jax.ShapeDtypeStruct((s, d), BF16), jax.ShapeDtypeStruct((1, d), F32),
                   jax.ShapeDtypeStruct((1, d), F32)],
        semantics=("arbitrary",), name=name, args=(xres, y, gain, bias, target), jobs=jobs)


def _softmax_rows(s):
    e = jnp.exp(s - jnp.max(s, axis=-1, keepdims=True))
    return e / jnp.sum(e, axis=-1, keepdims=True)


def _attn_fwd(q, k, v, name, jobs=()):
    s, d = q.shape
    m = k.shape[0]
    hd = d // XATTN_HEADS
    ts = _tile(TILES["attn"], s)
    scale = hd ** -0.5

    def body(q_ref, k_ref, v_ref, o_ref):
        for h in range(XATTN_HEADS):
            hs = slice(h * hd, (h + 1) * hd)
            sc = lax.dot_general(q_ref[:, hs], k_ref[:, hs], NT_DIMS, preferred_element_type=F32) * scale
            p = _softmax_rows(sc).astype(BF16)
            o_ref[:, hs] = jnp.dot(p, v_ref[:, hs], preferred_element_type=F32).astype(BF16)

    row = pl.BlockSpec((ts, d), lambda i: (i, 0))
    memb = pl.BlockSpec((m, d), lambda i: (0, 0))
    return _call(
        body, grid=(s // ts,), in_specs=[row, memb, memb], out_specs=[row],
        out_shape=[jax.ShapeDtypeStruct((s, d), BF16)],
        semantics=("parallel",), name=name, args=(q, k, v), jobs=jobs)[0]


def _attn_bwd(q, k, v, do, name, jobs=()):
    s, d = q.shape
    m = k.shape[0]
    hd = d // XATTN_HEADS
    ts = _tile(TILES["attn"], s)
    scale = hd ** -0.5

    def body(q_ref, k_ref, v_ref, do_ref, dq_ref, dk_ref, dv_ref):
        @pl.when(pl.program_id(0) == 0)
        def _():
            dk_ref[...] = jnp.zeros_like(dk_ref)
            dv_ref[...] = jnp.zeros_like(dv_ref)

        for h in range(XATTN_HEADS):
            hs = slice(h * hd, (h + 1) * hd)
            qh, kh, vh, doh = q_ref[:, hs], k_ref[:, hs], v_ref[:, hs], do_ref[:, hs]
            sc = lax.dot_general(qh, kh, NT_DIMS, preferred_element_type=F32) * scale
            p = _softmax_rows(sc)
            pb = p.astype(BF16)
            dp = lax.dot_general(doh, vh, NT_DIMS, preferred_element_type=F32)
            ds = (p * (dp - jnp.sum(dp * p, axis=-1, keepdims=True)) * scale).astype(BF16)
            dq_ref[:, hs] = jnp.dot(ds, kh, preferred_element_type=F32).astype(BF16)
            dk_ref[:, hs] += lax.dot_general(ds, qh, TN_DIMS, preferred_element_type=F32)
            dv_ref[:, hs] += lax.dot_general(pb, doh, TN_DIMS, preferred_element_type=F32)

    row = pl.BlockSpec((ts, d), lambda i: (i, 0))
    memb = pl.BlockSpec((m, d), lambda i: (0, 0))
    return _call(
        body, grid=(s // ts,), in_specs=[row, memb, memb, row], out_specs=[row, memb, memb],
        out_shape=[jax.ShapeDtypeStruct((s, d), BF16), jax.ShapeDtypeStruct((m, d), F32),
                   jax.ShapeDtypeStruct((m, d), F32)],
        semantics=("arbitrary",), name=name, args=(q, k, v, do), jobs=jobs)


def _sigmoid(x):
    return 1.0 / (1.0 + jnp.exp(-x))


def _log1p(x):
    u = 1.0 + x
    return jnp.where(u == 1.0, x, jnp.log(u) * (x / jnp.where(u == 1.0, 1.0, u - 1.0)))


def _softplus(x):
    return jnp.maximum(x, 0.0) + _log1p(jnp.exp(-jnp.abs(x)))


def _expm1(x):
    series = x * (1.0 + x * 0.5 * (1.0 + x * (1.0 / 3.0) * (1.0 + x * 0.25 * (1.0 + x * 0.2 * (1.0 + x * (1.0 / 6.0))))))
    return jnp.where(jnp.abs(x) < 0.1, series, jnp.exp(x) - 1.0)


GELU_K = 0.7978845608028654
GELU_C = 0.044715


def _gelu(x):
    return 0.5 * x * (1.0 + jnp.tanh(GELU_K * (x + GELU_C * (x * x * x))))


def _gelu_grad(x):
    th = jnp.tanh(GELU_K * (x + GELU_C * (x * x * x)))
    return 0.5 * (1.0 + th) + 0.5 * x * (1.0 - th * th) * GELU_K * (1.0 + 3.0 * GELU_C * x * x)


def _window_sum(ext_ref, first, rows, cols, w, step):
    acc = ext_ref[first:first + rows, cols]
    for kk in range(1, w):
        acc = acc + ext_ref[first + step * kk:first + step * kk + rows, cols]
    return acc


def _lru_gates(c_s, wa_ref, ba_ref, wx_ref, bx_ref, lam_ref, t_idx, hd, r_s, i_s, a_s, mult_s):
    sp = _softplus(-lam_ref[...])
    for h in range(LRU_HEADS):
        hs = slice(h * hd, (h + 1) * hd)
        chb = c_s[:, hs].astype(BF16)
        r = _sigmoid(jnp.dot(chb, wa_ref[h], preferred_element_type=F32) + ba_ref[:, hs])
        ig = _sigmoid(jnp.dot(chb, wx_ref[h], preferred_element_type=F32) + bx_ref[:, hs])
        log_a = -LRU_C * r * sp[:, hs]
        mult = jnp.sqrt(-_expm1(2.0 * log_a))
        r_s[:, hs] = r
        i_s[:, hs] = ig
        a_s[:, hs] = jnp.exp(log_a)
        mult_s[:, hs] = jnp.where(t_idx == 0, 1.0, mult)


def _conv(ext_ref, cw_ref, cb_ref, rows):
    acc = cb_ref[...] + cw_ref[0:1, :] * ext_ref[CONV_HALO - 3:CONV_HALO - 3 + rows, :]
    for kk in range(1, CONV_WIDTH):
        off = CONV_HALO - (CONV_WIDTH - 1) + kk
        acc = acc + cw_ref[kk:kk + 1, :] * ext_ref[off:off + rows, :]
    return acc


def _mixer_fwd(proj, wp, bp, ps, cw, cb, wa, ba, wx, bx, lam, name, jobs=()):
    s, p3 = proj.shape
    p = p3 // 3
    cg, hd = p // N_POOL_GROUPS, p // LRU_HEADS
    t = _tile(TILES["mixer"], s)

    def body(up_ref, ul_ref, ug_ref, wp_ref, bp_ref, ps_ref, cw_ref, cb_ref, wa_ref, ba_ref, wx_ref, bx_ref,
             lam_ref, ycat_ref, h_ref, extp, extl, hc, c_s, r_s, i_s, a_s, b_s):
        i = pl.program_id(0)

        @pl.when(i == 0)
        def _():
            extp[0:POOL_HALO, :] = jnp.zeros((POOL_HALO, p), F32)
            extl[0:CONV_HALO, :] = jnp.zeros((CONV_HALO, p), F32)
            hc[...] = jnp.zeros_like(hc)

        t_idx = i * t + lax.broadcasted_iota(jnp.int32, (t, 1), 0)

        extp[POOL_HALO:POOL_HALO + t, :] = up_ref[...]
        for g, w in enumerate(POOL_WINDOWS):
            cs = slice(g * cg, (g + 1) * cg)
            cnt = jnp.minimum(t_idx + 1, w).astype(F32)
            mixed = _window_sum(extp, POOL_HALO, t, cs, w, -1) / cnt - up_ref[:, cs]
            pre = jnp.dot(mixed.astype(BF16), wp_ref[g], preferred_element_type=F32) + bp_ref[:, cs]
            ycat_ref[:, cs] = (pre * ps_ref[:, cs]).astype(BF16)
        extp[0:POOL_HALO, :] = extp[t:t + POOL_HALO, :]

        extl[CONV_HALO:CONV_HALO + t, :] = ul_ref[...]
        c_s[...] = _conv(extl, cw_ref, cb_ref, t)
        extl[0:CONV_HALO, :] = extl[t:t + CONV_HALO, :]
        _lru_gates(c_s, wa_ref, ba_ref, wx_ref, bx_ref, lam_ref, t_idx, hd, r_s, i_s, a_s, b_s)
        b_s[...] = b_s[...] * (i_s[...] * c_s[...])

        rows = lax.broadcasted_iota(jnp.int32, (SUBLANES, p), 0)

        def block(bi, h):
            r0 = pl.multiple_of(bi * SUBLANES, SUBLANES)
            at = a_s[pl.ds(r0, SUBLANES), :]
            bt = b_s[pl.ds(r0, SUBLANES), :]
            out = jnp.zeros((SUBLANES, p), F32)
            for j in range(SUBLANES):
                h = at[j:j + 1, :] * h + bt[j:j + 1, :]
                out = jnp.where(rows == j, h, out)
            h_ref[pl.ds(r0, SUBLANES), :] = out
            return h

        hc[0:1, :] = lax.fori_loop(0, t // SUBLANES, block, hc[0:1, :])
        ycat_ref[:, p:2 * p] = (h_ref[...] * _gelu(ug_ref[...])).astype(BF16)

    def col(j):
        return pl.BlockSpec((t, p), lambda i: (i, j))

    def whole(a):
        nd = a.ndim
        return pl.BlockSpec(a.shape, lambda i: (0,) * nd)

    consts = (wp, bp, ps, cw, cb, wa, ba, wx, bx, lam)
    tile = pltpu.VMEM((t, p), F32)
    return _call(
        body, grid=(s // t,), in_specs=[col(0), col(1), col(2)] + [whole(a) for a in consts],
        out_specs=[pl.BlockSpec((t, 2 * p), lambda i: (i, 0)), pl.BlockSpec((t, p), lambda i: (i, 0))],
        out_shape=[jax.ShapeDtypeStruct((s, 2 * p), BF16), jax.ShapeDtypeStruct((s, p), F32)],
        scratch_shapes=[pltpu.VMEM((t + POOL_HALO, p), F32), pltpu.VMEM((t + CONV_HALO, p), F32),
                        pltpu.VMEM((SUBLANES, p), F32), tile, tile, tile, tile, tile],
        semantics=("arbitrary",), name=name, args=(proj, proj, proj, *consts), jobs=jobs)


def _mixer_bwd(dycat, proj, hsave, wp, bp, ps, cw, cb, wa, ba, wx, bx, lam, name, jobs=()):
    s, p3 = proj.shape
    p = p3 // 3
    cg, hd = p // N_POOL_GROUPS, p // LRU_HEADS
    t = _tile(TILES["mixer"], s)
    nt = s // t

    def body(dyp_ref, dyl_ref, up_ref, ul_ref, ug_ref, upp_ref, ulp_ref, h_ref, hp_ref,
             wp_ref, bp_ref, ps_ref, cw_ref, cb_ref, wa_ref, ba_ref, wx_ref, bx_ref, lam_ref,
             dproj_ref, dwp_ref, dbp_ref, dps_ref, dcw_ref, dcb_ref, dwa_ref, dba_ref, dwx_ref, dbx_ref, dlam_ref,
             extp, extg, extl, extdc, exth, ghc, c_s, r_s, i_s, a_s, mult_s, gh_s):
        i = pl.program_id(0)
        ib = nt - 1 - i

        @pl.when(i == 0)
        def _():
            for ref in (dwp_ref, dbp_ref, dps_ref, dcw_ref, dcb_ref, dwa_ref, dba_ref, dwx_ref, dbx_ref, dlam_ref):
                ref[...] = jnp.zeros_like(ref)
            extg[t:t + POOL_HALO, :] = jnp.zeros((POOL_HALO, p), F32)
            extdc[t:t + CONV_HALO, :] = jnp.zeros((CONV_HALO, p), F32)
            ghc[...] = jnp.zeros_like(ghc)

        t_idx = ib * t + lax.broadcasted_iota(jnp.int32, (t, 1), 0)
        seq_start = ib == 0

        extl[0:CONV_HALO, :] = jnp.where(seq_start, 0.0, ulp_ref[...])
        extl[CONV_HALO:CONV_HALO + t, :] = ul_ref[...]
        c_s[...] = _conv(extl, cw_ref, cb_ref, t)
        _lru_gates(c_s, wa_ref, ba_ref, wx_ref, bx_ref, lam_ref, t_idx, hd, r_s, i_s, a_s, mult_s)
        exth[0:SUBLANES, :] = jnp.where(seq_start, 0.0, hp_ref[...])
        exth[SUBLANES:SUBLANES + t, :] = h_ref[...]

        ug = ug_ref[...]
        dyl = dyl_ref[...]
        dproj_ref[:, 2 * p:3 * p] = (dyl * h_ref[...] * _gelu_grad(ug)).astype(BF16)
        gh_s[...] = dyl * _gelu(ug)

        rows = lax.broadcasted_iota(jnp.int32, (SUBLANES, p), 0)
        nblk = t // SUBLANES

        def block(bi, carry):
            r0 = pl.multiple_of((nblk - 1 - bi) * SUBLANES, SUBLANES)
            at = a_s[pl.ds(r0, SUBLANES), :]
            dt = gh_s[pl.ds(r0, SUBLANES), :]
            out = jnp.zeros((SUBLANES, p), F32)
            for j in range(SUBLANES - 1, -1, -1):
                gh = dt[j:j + 1, :] + carry
                out = jnp.where(rows == j, gh, out)
                carry = at[j:j + 1, :] * gh
            gh_s[pl.ds(r0, SUBLANES), :] = out
            return carry

        ghc[0:1, :] = lax.fori_loop(0, nblk, block, ghc[0:1, :])

        sp = _softplus(-lam_ref[...])
        dsp_dlam = -_sigmoid(-lam_ref[...])
        for h in range(LRU_HEADS):
            hs = slice(h * hd, (h + 1) * hd)
            gh, a, mult, r, ig, c = gh_s[:, hs], a_s[:, hs], mult_s[:, hs], r_s[:, hs], i_s[:, hs], c_s[:, hs]
            hprev = exth[SUBLANES - 1:SUBLANES - 1 + t, hs]
            dmult = gh * (ig * c)
            dlog_a = a * gh * hprev + jnp.where(t_idx == 0, 0.0, -dmult * a * a / mult)
            dlam_ref[:, hs] += jnp.sum(dlog_a * r, axis=0, keepdims=True) * (-LRU_C) * dsp_dlam[:, hs]
            dpa = dlog_a * (-LRU_C * sp[:, hs]) * r * (1.0 - r)
            dpx = gh * mult * c * ig * (1.0 - ig)
            dpab, dpxb, chb = dpa.astype(BF16), dpx.astype(BF16), c.astype(BF16)
            dwa_ref[h] += lax.dot_general(chb, dpab, TN_DIMS, preferred_element_type=F32)
            dwx_ref[h] += lax.dot_general(chb, dpxb, TN_DIMS, preferred_element_type=F32)
            dba_ref[:, hs] += jnp.sum(dpa, axis=0, keepdims=True)
            dbx_ref[:, hs] += jnp.sum(dpx, axis=0, keepdims=True)
            dc = (gh * mult * ig
                  + lax.dot_general(dpab, wa_ref[h], NT_DIMS, preferred_element_type=F32)
                  + lax.dot_general(dpxb, wx_ref[h], NT_DIMS, preferred_element_type=F32))
            extdc[0:t, hs] = dc
            dcb_ref[:, hs] += jnp.sum(dc, axis=0, keepdims=True)
            for kk in range(CONV_WIDTH):
                off = CONV_HALO - (CONV_WIDTH - 1) + kk
                dcw_ref[kk:kk + 1, hs] += jnp.sum(dc * extl[off:off + t, hs], axis=0, keepdims=True)
        du_lru = cw_ref[0:1, :] * extdc[CONV_WIDTH - 1:CONV_WIDTH - 1 + t, :]
        for kk in range(1, CONV_WIDTH):
            off = CONV_WIDTH - 1 - kk
            du_lru = du_lru + cw_ref[kk:kk + 1, :] * extdc[off:off + t, :]
        dproj_ref[:, p:2 * p] = du_lru.astype(BF16)
        extdc[t:t + CONV_HALO, :] = extdc[0:CONV_HALO, :]

        extp[0:POOL_HALO, :] = jnp.where(seq_start, 0.0, upp_ref[...])
        extp[POOL_HALO:POOL_HALO + t, :] = up_ref[...]
        for g, w in enumerate(POOL_WINDOWS):
            cs = slice(g * cg, (g + 1) * cg)
            cnt = jnp.minimum(t_idx + 1, w).astype(F32)
            mixed = (_window_sum(extp, POOL_HALO, t, cs, w, -1) / cnt - up_ref[:, cs]).astype(BF16)
            pre = jnp.dot(mixed, wp_ref[g], preferred_element_type=F32) + bp_ref[:, cs]
            dyp = dyp_ref[:, cs]
            dps_ref[:, cs] += jnp.sum(dyp * pre, axis=0, keepdims=True)
            dpre = dyp * ps_ref[:, cs]
            dpreb = dpre.astype(BF16)
            dbp_ref[:, cs] += jnp.sum(dpre, axis=0, keepdims=True)
            dwp_ref[g] += lax.dot_general(mixed, dpreb, TN_DIMS, preferred_element_type=F32)
            dmixed = lax.dot_general(dpreb, wp_ref[g], NT_DIMS, preferred_element_type=F32)
            extg[0:t, cs] = dmixed / cnt
            dproj_ref[:, cs] = (_window_sum(extg, 0, t, cs, w, 1) - dmixed).astype(BF16)
        extg[t:t + POOL_HALO, :] = extg[0:POOL_HALO, :]

    def col(j):
        return pl.BlockSpec((t, p), lambda i: (nt - 1 - i, j))

    def prev(rows, j):
        per = t // rows
        return pl.BlockSpec((rows, p), lambda i: (jnp.maximum((nt - 1 - i) * per - 1, 0), j))

    def whole(a):
        nd = a.ndim
        return pl.BlockSpec(a.shape, lambda i: (0,) * nd)

    consts = (wp, bp, ps, cw, cb, wa, ba, wx, bx, lam)
    grads = (wp, bp, ps, cw, cb, wa, ba, wx, bx, lam)
    tile = pltpu.VMEM((t, p), F32)
    return _call(
        body, grid=(nt,),
        in_specs=[col(0), col(1), col(0), col(1), col(2), prev(POOL_HALO, 0), prev(CONV_HALO, 1), col(0),
                  prev(SUBLANES, 0)] + [whole(a) for a in consts],
        out_specs=[pl.BlockSpec((t, 3 * p), lambda i: (nt - 1 - i, 0))] + [whole(a) for a in grads],
        out_shape=[jax.ShapeDtypeStruct((s, 3 * p), BF16)] + [jax.ShapeDtypeStruct(a.shape, F32) for a in grads],
        scratch_shapes=[pltpu.VMEM((t + POOL_HALO, p), F32), pltpu.VMEM((t + POOL_HALO, p), F32),
                        pltpu.VMEM((t + CONV_HALO, p), F32), pltpu.VMEM((t + CONV_HALO, p), F32),
                        pltpu.VMEM((t + SUBLANES, p), F32), pltpu.VMEM((SUBLANES, p), F32),
                        tile, tile, tile, tile, tile, tile],
        semantics=("arbitrary",), name=name,
        args=(dycat, dycat, proj, proj, proj, proj, proj, hsave, hsave, *consts), jobs=jobs)


def _pair_add(parts, got, core, name):
    n, r, c = got.shape
    tr = _tile(TILES["add"], r)

    def body(core_ref, a_ref, b_ref, o_ref):
        del core_ref
        o_ref[...] = (a_ref[...].astype(F32) + b_ref[...].astype(F32)).astype(o_ref.dtype)

    blk = pl.BlockSpec((None, tr, c), lambda k, i, core_ref: (k, i, 0))
    mine = pl.BlockSpec((None, tr, c), lambda k, i, core_ref: (2 * k + core_ref[0], i, 0))
    return _call(body, grid=(n, r // tr), in_specs=[mine, blk], out_specs=[blk],
                 out_shape=[jax.ShapeDtypeStruct(got.shape, got.dtype)], semantics=("parallel", "parallel"),
                 name=name, args=(parts, got), index=core)[0]


def _sum_parts(parts, name):
    n, r, c = parts.shape
    tr = _tile(TILES["adam"], r)

    def body(p_ref, o_ref):
        acc = p_ref[0].astype(F32)
        for d in range(1, n):
            acc = acc + p_ref[d].astype(F32)
        o_ref[...] = acc

    return _call(
        body, grid=(r // tr,), in_specs=[pl.BlockSpec((n, tr, c), lambda i: (0, i, 0))],
        out_specs=[pl.BlockSpec((tr, c), lambda i: (i, 0))], out_shape=[jax.ShapeDtypeStruct((r, c), F32)],
        semantics=("parallel",), name=name, args=(parts,))[0]


def _adamw(w, m, v, parts, name, jobs=(), own=None, chip=None):
    r, c = w.shape
    n = parts.shape[0]
    tr = _tile(TILES["adam"], r)

    def body(*refs):
        if own is not None:
            refs = refs[1:]
            own_ref, refs = refs[3], refs[:3] + refs[4:]
        w_ref, m_ref, v_ref, p_ref, g_ref, d_ref, nm_ref, nv_ref = refs
        g = p_ref[0].astype(F32)
        if own is not None:
            g = own_ref[...].astype(F32) + g
        for d in range(1, n):
            g = g + p_ref[d].astype(F32)
        nm = ADAM_B1 * m_ref[...] + (1.0 - ADAM_B1) * g
        nv = ADAM_B2 * v_ref[...] + (1.0 - ADAM_B2) * (g * g)
        m_hat = nm / (1.0 - ADAM_B1 ** ADAM_STEP)
        v_hat = nv / (1.0 - ADAM_B2 ** ADAM_STEP)
        g_ref[...] = g
        d_ref[...] = -ADAM_LR * (m_hat / (jnp.sqrt(v_hat) + ADAM_EPS) + ADAM_WD * w_ref[...])
        nm_ref[...] = nm
        nv_ref[...] = nv

    row = pl.BlockSpec((tr, c), lambda i, *_: (i, 0))
    in_specs, args = [row, row, row], [w, m, v]
    if own is not None:
        in_specs.append(pl.BlockSpec((None, tr, c), lambda i, chip_ref: (chip_ref[0], i, 0)))
        args.append(own)
    in_specs.append(pl.BlockSpec((n, tr, c), lambda i, *_: (0, i, 0)))
    args.append(parts)
    return _call(
        body, grid=(r // tr,), in_specs=in_specs, out_specs=[row] * 4, out_shape=[jax.ShapeDtypeStruct((r, c), F32)] * 4,
        semantics=("parallel",), name=name, args=args, jobs=jobs, index=chip if own is not None else None)


SMALL_ORDER = ("w_a", "w_x", "conv_w", "b_pool", "conv_b", "b_a", "b_x", "lru_lambda", "pool_scale",
               "ln1_g", "ln1_b", "ln2_g", "ln2_b", "ln3_g", "ln3_b")


def _pack_rows(a, p):
    flat = a.reshape(-1, p)
    pad = (-flat.shape[0]) % SUBLANES
    return jnp.pad(flat, ((0, pad), (0, 0))) if pad else flat


def kernel(x, mem, w_in, conv_w, conv_b, w_a, b_a, w_x, b_x, lru_lambda, w_pool, b_pool, pool_scale, w_out, ln1_g, ln1_b, w_q, w_k, w_v, w_o, ln2_g, ln2_b, w_ff1, w_ff2, ln3_g, ln3_b, loss_target, m_w_in, m_conv_w, m_conv_b, m_w_a, m_b_a, m_w_x, m_b_x, m_lru_lambda, m_w_pool, m_b_pool, m_pool_scale, m_w_out, m_ln1_g, m_ln1_b, m_w_q, m_w_k, m_w_v, m_w_o, m_ln2_g, m_ln2_b, m_w_ff1, m_w_ff2, m_ln3_g, m_ln3_b, v_w_in, v_conv_w, v_conv_b, v_w_a, v_b_a, v_w_x, v_b_x, v_lru_lambda, v_w_pool, v_b_pool, v_pool_scale, v_w_out, v_ln1_g, v_ln1_b, v_w_q, v_w_k, v_w_v, v_w_o, v_ln2_g, v_ln2_b, v_w_ff1, v_w_ff2, v_ln3_g, v_ln3_b):
    names = ("w_in", "conv_w", "conv_b", "w_a", "b_a", "w_x", "b_x", "lru_lambda", "w_pool", "b_pool", "pool_scale",
             "w_out", "ln1_g", "ln1_b", "w_q", "w_k", "w_v", "w_o", "ln2_g", "ln2_b", "w_ff1", "w_ff2", "ln3_g", "ln3_b")
    w_loc = dict(zip(names, (w_in, conv_w, conv_b, w_a, b_a, w_x, b_x, lru_lambda, w_pool, b_pool, pool_scale,
                             w_out, ln1_g, ln1_b, w_q, w_k, w_v, w_o, ln2_g, ln2_b, w_ff1, w_ff2, ln3_g, ln3_b)))
    m_loc = dict(zip(names, (m_w_in, m_conv_w, m_conv_b, m_w_a, m_b_a, m_w_x, m_b_x, m_lru_lambda, m_w_pool, m_b_pool,
                             m_pool_scale, m_w_out, m_ln1_g, m_ln1_b, m_w_q, m_w_k, m_w_v, m_w_o, m_ln2_g, m_ln2_b,
                             m_w_ff1, m_w_ff2, m_ln3_g, m_ln3_b)))
    v_loc = dict(zip(names, (v_w_in, v_conv_w, v_conv_b, v_w_a, v_b_a, v_w_x, v_b_x, v_lru_lambda, v_w_pool, v_b_pool,
                             v_pool_scale, v_w_out, v_ln1_g, v_ln1_b, v_w_q, v_w_k, v_w_v, v_w_o, v_ln2_g, v_ln2_b,
                             v_w_ff1, v_w_ff2, v_ln3_g, v_ln3_b)))
    s, d = x.shape[1], x.shape[2]
    p = conv_b.shape[1]
    cg = p // N_POOL_GROUPS
    hd = p // LRU_HEADS
    me = 4 * lax.axis_index("x") + 2 * lax.axis_index("y") + lax.axis_index("c")

    xs, mems, tgt = x[0], mem[0], loss_target[0]
    memb = mems.astype(BF16)

    gathers = {n: _Job("gather", w_loc[n][0].astype(BF16))
               for n in ("w_in", "w_out", "w_q", "w_k", "w_v", "w_o", "w_ff2", "w_pool")}
    ff1_shard = w_ff1[0].astype(BF16)
    ff1_rows = ff1_shard.shape[0] // FF1_PIECES

    def ff1_piece(i, earlier=None):
        return _Job("gather", ff1_shard[i * ff1_rows:(i + 1) * ff1_rows], window=(i * ff1_rows, ff1_shard.shape[0]),
                    into=None if earlier is None else earlier.out)
    tiny = jnp.concatenate([_pack_rows(conv_w[0], p // N_DEV),
                            _pack_rows(jnp.pad(b_pool[0], ((0, 0), (0, p // N_DEV - cg // N_DEV))), p // N_DEV)], axis=0)
    gathers["tiny"] = _Job("gather", tiny)

    def gathered(n):
        full = gathers[n].out
        if n == "w_in":
            return jnp.transpose(full, (1, 0, 2)).reshape(1, full.shape[1], -1)
        return full.reshape(1, -1, full.shape[-1])

    W = {"conv_b": conv_b, "b_a": b_a.reshape(1, p), "b_x": b_x.reshape(1, p), "lru_lambda": lru_lambda,
         "pool_scale": pool_scale, "w_a": w_a[0].astype(BF16), "w_x": w_x[0].astype(BF16)}
    for n in ("ln1_g", "ln1_b", "ln2_g", "ln2_b", "ln3_g", "ln3_b"):
        W[n] = w_loc[n]

    out_g, out_d, out_m, out_v = {}, {}, {}, {}
    pairs, quads, sums = {}, {}, {}
    core =lax.axis_index("c").astype(jnp.int32).reshape(1)
    chip = (2 * lax.axis_index("x") + lax.axis_index("y")).astype(jnp.int32).reshape(1)

    def pair(n, partial):
        pairs[n] = _Job("pair", partial.reshape(N_DEV, -1, partial.shape[-1]))
        return pairs[n]

    def quad(n, lo=0, hi=1, of=1):
        if lo == 0:
            sums[n] = _pair_add(pairs[n].src, pairs[n].out, core, "add_" + n)
        rows = sums[n].shape[1] // of
        quads[n] = _Job("quad", sums[n], window=None if (lo, hi) == (0, of) else (lo * rows, (hi - lo) * rows),
                        into=None if lo == 0 else quads[n].out)
        return quads[n]

    def update(n, parts, jobs=(), own=None):
        shp = w_loc[n].shape
        rows = parts.shape[1]
        w2, m2, v2 = (a.reshape(rows, -1) for a in (w_loc[n], m_loc[n], v_loc[n]))
        res = _adamw(w2, m2, v2, parts.reshape(parts.shape[0], rows, -1), "adamw_" + n, jobs=jobs, own=own, chip=chip)
        out_g[n], out_d[n], out_m[n], out_v[n] = (r.reshape(shp) for r in res)

    assert FF1_PIECES == 4
    xb = _to_bf16(xs, "cast_x", jobs=[gathers["w_in"], gathers["tiny"], gathers["w_pool"]])
    W["w_pool"] = jnp.transpose(gathers["w_pool"].out, (1, 0, 2, 3)).reshape(N_POOL_GROUPS, cg, cg)
    cwb = gathers["tiny"].out
    W["conv_w"] = jnp.transpose(cwb[:, :CONV_WIDTH, :], (1, 0, 2)).reshape(CONV_WIDTH, p)
    W["b_pool"] = jnp.transpose(cwb[:, SUBLANES:SUBLANES + N_POOL_GROUPS, :cg // N_DEV], (1, 0, 2)).reshape(1, p)
    mixer_consts = (W["w_pool"], W["b_pool"], W["pool_scale"], W["conv_w"], W["conv_b"], W["w_a"], W["b_a"],
                    W["w_x"], W["b_x"], W["lru_lambda"])

    w_in_full = gathered("w_in")
    piece = ff1_piece(FF1_PIECES - 1)
    (proj,) = _mm_nn(xb, w_in_full, [F32], "fwd_proj", jobs=[gathers["w_out"], piece])
    ycat, hsave = _mixer_fwd(proj, *mixer_consts, "fwd_mixer", jobs=[gathers["w_q"], gathers["w_k"]])
    (y1,) = _mm_nn(ycat, gathered("w_out"), [F32], "fwd_out", jobs=[gathers["w_v"]])
    x1, x1b, xhat1, rstd1 = _ln_fwd(xs, y1, W["ln1_g"], W["ln1_b"], "fwd_ln1", jobs=[gathers["w_o"]])
    piece = ff1_piece(0, piece)
    (q,) = _mm_nn(x1b, gathered("w_q"), [BF16], "fwd_q", jobs=[piece])
    (k,) = _mm_nn(memb, gathered("w_k"), [BF16], "fwd_k")
    (v,) = _mm_nn(memb, gathered("w_v"), [BF16], "fwd_v")
    o = _attn_fwd(q, k, v, "fwd_attn")
    piece = ff1_piece(1, piece)
    (y2,) = _mm_nn(o, gathered("w_o"), [F32], "fwd_o", jobs=[piece])
    piece = ff1_piece(2, piece)
    x2, x2b, xhat2, rstd2 = _ln_fwd(x1, y2, W["ln2_g"], W["ln2_b"], "fwd_ln2", jobs=[piece])
    w_ff1_full = piece.out

    def relu_sq(acc):
        r = jnp.maximum(acc, 0.0)
        return r, r * r

    rb, act = _mm_nn(x2b, w_ff1_full, [BF16, BF16], "fwd_ff1", epilogue=relu_sq, jobs=[gathers["w_ff2"]])
    (y3,) = _mm_nn(act, gathered("w_ff2"), [F32], "fwd_ff2")
    loss_rows, dz3, dz3b, dg3, db3 = _ln_loss(x2, y3, W["ln3_g"], W["ln3_b"], tgt, "ln3_loss")
    loss = lax.psum(loss_rows[0, 0], MESH_AXES)

    small = {"ln3_g": dg3, "ln3_b": db3}

    def add_residual(acc, e):
        return (acc + ALPHA * e,)

    dw_ff2 = _mm_tn(act, dz3b, 1, BF16, "bwd_dw_ff2")
    (dhid,) = _mm_nt(dz3b, gathered("w_ff2"), [BF16], "bwd_dact", extras=(rb,), jobs=[pair("w_ff2", dw_ff2)],
                     epilogue=lambda acc, r: (acc * (2.0 * r.astype(F32)),))
    dw_ff1 = _mm_tn(x2b, dhid, N_DEV, BF16, "bwd_dw_ff1", jobs=[quad("w_ff2", 0, 1, 2)])
    (dx2,) = _mm_nt(dhid, w_ff1_full, [F32], "bwd_dx2", epilogue=add_residual, extras=(dz3,), tk=TILES["tk"] // 2,
                    jobs=[pair("w_ff1", dw_ff1), quad("w_ff2", 1, 2, 2)])
    dz2, dz2b, small["ln2_g"], small["ln2_b"] = _ln_bwd(dx2, xhat2, rstd2, W["ln2_g"], "bwd_ln2")

    dw_o = _mm_tn(o, dz2b, 1, BF16, "bwd_dw_o")
    (do,) = _mm_nt(dz2b, gathered("w_o"), [BF16], "bwd_do", jobs=[pair("w_o", dw_o)])
    dq, dk, dv = _attn_bwd(q, k, v, do, "bwd_attn", jobs=[quad("w_o")])
    dw_q = _mm_tn(x1b, dq, 1, BF16, "bwd_dw_q")
    dw_k = _mm_tn(memb, dk.astype(BF16), 1, BF16, "bwd_dw_k")
    dw_v = _mm_tn(memb, dv.astype(BF16), 1, BF16, "bwd_dw_v")
    (dx1,) = _mm_nt(dq, gathered("w_q"), [F32], "bwd_dx1", epilogue=add_residual, extras=(dz2,),
                    jobs=[pair("w_q", dw_q), pair("w_k", dw_k), pair("w_v", dw_v), quad("w_ff1", 0, 2, 8)])
    dz1, dz1b, small["ln1_g"], small["ln1_b"] = _ln_bwd(dx1, xhat1, rstd1, W["ln1_g"], "bwd_ln1", jobs=[quad("w_q")])

    dw_out = _mm_tn(ycat, dz1b, 1, BF16, "bwd_dw_out")
    (dycat,) = _mm_nt(dz1b, gathered("w_out"), [F32], "bwd_dycat", jobs=[pair("w_out", dw_out)])
    (dproj, dwp, small["b_pool"], small["pool_scale"], small["conv_w"], small["conv_b"], small["w_a"], small["b_a"],
     small["w_x"], small["b_x"], small["lru_lambda"]) = _mixer_bwd(
        dycat, proj, hsave, *mixer_consts, "bwd_mixer", jobs=[quad("w_ff1", 2, 6, 8), quad("w_k"), quad("w_v")])
    dw_pool = jnp.transpose(dwp.astype(BF16).reshape(N_POOL_GROUPS, N_DEV, cg // N_DEV, cg), (1, 0, 2, 3))
    pack = jnp.concatenate([_pack_rows(small[n], p) for n in SMALL_ORDER], axis=0)
    small_gather = _Job("gather", pack)
    dw_in = _mm_tn(xb, dproj, 1, BF16, "bwd_dw_in",
                   jobs=[quad("w_out"), pair("w_pool", dw_pool), small_gather, quad("w_ff1", 6, 7, 8)])
    dw_in = jnp.transpose(dw_in.reshape(dw_in.shape[1], N_DEV, -1), (1, 0, 2))
    (gx_lo,) = _mm_nt(dproj, w_in_full, [F32], "bwd_dx_lo", epilogue=add_residual, extras=(dz1,), part=(0, 2),
                      jobs=[pair("w_in", dw_in), quad("w_pool"), quad("w_ff1", 7, 8, 8)])
    (gx_hi,) = _mm_nt(dproj, w_in_full, [F32], "bwd_dx_hi", epilogue=add_residual, extras=(dz1,), part=(1, 2),
                      jobs=[quad("w_in", 0, 1, 2)])
    grad_x = jnp.concatenate([gx_lo, gx_hi], axis=0)

    update("w_ff2", quads["w_ff2"].out, jobs=[quad("w_in", 1, 2, 2)], own=quads["w_ff2"].src)
    for n in ("w_ff1", "w_o", "w_q", "w_k", "w_v", "w_out", "w_pool", "w_in"):
        update(n, quads[n].out, own=quads[n].src)

    total = _sum_parts(small_gather.out, "sum_small")
    row = 0
    for n in SMALL_ORDER:
        size = small[n].size
        nrows = size // p
        g_full = total[row:row + nrows].reshape(small[n].shape)
        row += nrows + (-nrows) % SUBLANES
        if n == "conv_w":
            g_loc = lax.dynamic_slice_in_dim(g_full, me * (p // N_DEV), p // N_DEV, axis=1)
        elif n == "b_pool":
            g_loc = lax.dynamic_slice_in_dim(g_full.reshape(N_POOL_GROUPS, cg), me * (cg // N_DEV), cg // N_DEV, axis=1)
        else:
            g_loc = g_full
        rows = g_loc.shape[0] if n not in ("w_a", "w_x") else LRU_HEADS * hd
        update(n, g_loc.reshape(1, rows, -1))

    order = names
    return (loss, grad_x[None], *[out_g[n] for n in order], *[out_d[n] for n in order],
            *[out_m[n] for n in order], *[out_v[n] for n in order])
```

```python
import functools

import jax
import jax.numpy as jnp
from jax import lax
from jax.experimental import pallas as pl
from jax.experimental.pallas import tpu as pltpu

F32 = jnp.float32
BF16 = jnp.bfloat16

N_DEV = 8
MESH_AXES = ("x", "y", "c")
POOL_WINDOWS = (2, 4, 8, 16)
N_POOL_GROUPS = len(POOL_WINDOWS)
POOL_HALO = 16
CONV_WIDTH = 4
CONV_HALO = 8
FF1_PIECES = 4
LRU_HEADS = 8
LRU_C = 8.0
XATTN_HEADS = 4
LN_EPS = 1e-5
ALPHA = 2.0 ** 0.25
ADAM_LR = 0.001
ADAM_B1 = 0.9
ADAM_B2 = 0.999
ADAM_EPS = 1e-08
ADAM_WD = 0.01
ADAM_STEP = 10
SUBLANES = 8
VMEM_LIMIT = 56 * 1024 * 1024

NT_DIMS = (((1,), (1,)), ((), ()))
TN_DIMS = (((0,), (0,)), ((), ()))


def _params(*sem):
    return pltpu.CompilerParams(dimension_semantics=sem, vmem_limit_bytes=VMEM_LIMIT)


def _place():
    return lax.axis_index("x"), lax.axis_index("y"), lax.axis_index("c")


def _remote(src, dst, send_sem, recv_sem, to):
    return pltpu.make_async_remote_copy(src_ref=src, dst_ref=dst, send_sem=send_sem, recv_sem=recv_sem,
                                        device_id=to, device_id_type=pl.DeviceIdType.MESH)


class _Job:
    def __init__(self, kind, src, window=None, into=None):
        self.kind, self.src, self.out, self.window, self.into = kind, src, None, window, into

    def out_shape(self):
        s = self.src.shape
        if self.kind == "gather" and self.window is not None:
            s = (self.window[1],) + s[1:]
        shape = {"gather": (N_DEV,) + s, "pair": (4,) + s[1:], "quad": (3,) + s[1:]}[self.kind]
        return jax.ShapeDtypeStruct(shape, self.src.dtype)

    def scratch(self):
        n = {"gather": 7, "pair": 4, "quad": 3}[self.kind]
        sems = [pltpu.SemaphoreType.DMA((n,)), pltpu.SemaphoreType.DMA((n,))]
        if self.kind == "gather":
            sems += [pltpu.SemaphoreType.DMA((2,)), pltpu.VMEM(self.src.shape, self.src.dtype)]
        return sems

    def ops(self, src, out, *scratch):
        if self.kind == "gather":
            return _gather_ops(src, out, *scratch, first_row=None if self.window is None else self.window[0])
        if self.kind == "quad":
            return _quad_ops(src, out, *scratch, rows=self.window)
        return _pair_ops(src, out, *scratch)


def _gather_ops(x_ref, out_ref, send_sems, recv_sems, local_sems, bounce, first_row=None):
    x, y, c = _place()
    me, sibling = (x, y, c), (x, y, 1 - c)
    chips = [(1 - x, y), (x, 1 - y), (1 - x, 1 - y)]

    def slot(px, py, pc):
        block = out_ref.at[4 * px + 2 * py + pc]
        return block if first_row is None else block.at[pl.ds(first_row, x_ref.shape[0])]

    def copy(k, block, to, src=None):
        return _remote(slot(*block) if src is None else src, slot(*block), send_sems.at[k], recv_sems.at[k], to)

    mine_in = pltpu.make_async_copy(x_ref, bounce, local_sems.at[0])
    mine_out = pltpu.make_async_copy(bounce, slot(*me), local_sems.at[1])
    first = [copy(0, me, sibling, src=x_ref)] + [copy(1 + j, me, (*chip, c), src=x_ref) for j, chip in enumerate(chips)]
    passed = [copy(4 + j, (*chip, c), sibling) for j, chip in enumerate(chips)]

    def start():
        mine_in.start()
        for cp in first:
            cp.start()

    def mid():
        mine_in.wait()
        mine_out.start()
        for j, chip in enumerate(chips):
            copy(1 + j, (*chip, c), me).wait_recv()
            passed[j].start()

    def finish():
        copy(0, sibling, me).wait_recv()
        for j, chip in enumerate(chips):
            copy(4 + j, (*chip, 1 - c), me).wait_recv()
        for cp in first + passed:
            cp.wait_send()
        mine_out.wait()

    return start, mid, finish


def _pair_ops(p_ref, got_ref, send_sems, recv_sems):
    x, y, c = _place()
    give = [_remote(p_ref.at[2 * k + 1 - c], got_ref.at[k], send_sems.at[k], recv_sems.at[k], (x, y, 1 - c))
            for k in range(4)]

    def start():
        for cp in give:
            cp.start()

    def finish():
        for cp in give:
            cp.wait_recv()
        for cp in give:
            cp.wait_send()

    return start, None, finish


def _quad_ops(q_ref, out_ref, send_sems, recv_sems, rows=None):
    x, y, c = _place()

    def part(block):
        return block if rows is None else block.at[pl.ds(rows[0], rows[1])]

    copies = []
    for rel in range(1, 4):
        px = 1 - x if rel & 2 else x
        py = 1 - y if rel & 1 else y
        copies.append(_remote(part(q_ref.at[2 * px + py]), part(out_ref.at[rel - 1]), send_sems.at[rel - 1],
                              recv_sems.at[rel - 1], (px, py, c)))

    def start():
        for cp in copies:
            cp.start()

    def finish():
        for cp in copies:
            cp.wait_recv()
        for cp in copies:
            cp.wait_send()

    return start, None, finish


def _call(body, *, grid, in_specs, out_specs, out_shape, scratch_shapes=(), semantics, name, args, jobs=(), index=None):
    in_specs, out_specs, out_shape = list(in_specs), list(out_specs), list(out_shape)
    scratch_shapes, jobs = list(scratch_shapes), list(jobs)
    n_in, n_out, n_scr, n_job = len(in_specs), len(out_specs), len(scratch_shapes), len(jobs)
    n_idx = 0 if index is None else 1
    job_scratch = [j.scratch() for j in jobs]
    n_steps = functools.reduce(lambda a, b: a * b, grid, 1)
    early = n_steps - 1 - max(1, n_steps // 8) if n_steps >= 4 else None

    intos = [(k, j.into) for k, j in enumerate(jobs) if j.into is not None]

    def hosted(*refs):
        idx, refs = refs[:n_idx], refs[n_idx:]
        ins, jin = refs[:n_in], refs[n_in:n_in + n_job]
        o0 = n_in + n_job + len(intos)
        outs, jout = refs[o0:o0 + n_out], refs[o0 + n_out:o0 + n_out + n_job]
        s0 = o0 + n_out + n_job
        scr, jscr = refs[s0:s0 + n_scr], refs[s0 + n_scr:]
        ops, at = [], 0
        for k, j in enumerate(jobs):
            ops.append(j.ops(jin[k], jout[k], *jscr[at:at + len(job_scratch[k])]))
            at += len(job_scratch[k])
        step = functools.reduce(lambda acc, a: acc * grid[a] + pl.program_id(a), range(len(grid)), 0)
        mids = [mid for _, mid, _ in ops if mid is not None]

        @pl.when(step == 0)
        def _():
            for start, _, _ in ops:
                start()

        if mids and early is not None:
            @pl.when(step == early)
            def _():
                for mid in mids:
                    mid()

        body(*idx, *ins, *outs, *scr)

        @pl.when(step == n_steps - 1)
        def _():
            if early is None:
                for mid in mids:
                    mid()
            for _, _, finish in ops:
                finish()

    hbm = pl.BlockSpec(memory_space=pl.ANY)
    spec = pltpu.PrefetchScalarGridSpec(
        num_scalar_prefetch=n_idx, grid=grid, in_specs=in_specs + [hbm] * (n_job + len(intos)),
        out_specs=out_specs + [hbm] * n_job, scratch_shapes=scratch_shapes + [s for js in job_scratch for s in js])
    aliases = {n_idx + n_in + n_job + q: n_out + k for q, (k, _) in enumerate(intos)}
    res = pl.pallas_call(
        hosted if jobs else body, grid_spec=spec, out_shape=out_shape + [j.out_shape() for j in jobs],
        input_output_aliases=aliases,
        compiler_params=_params(*(["arbitrary"] * len(grid) if jobs else semantics)), name=name,
    )(*([] if index is None else [index]), *args, *[j.src for j in jobs], *[buf for _, buf in intos])
    for j, o in zip(jobs, res[n_out:]):
        j.out = o
    return res[:n_out]


def _to_bf16(a, name, jobs=()):
    r, c = a.shape
    tr = _tile(TILES["row"], r)

    def body(a_ref, o_ref):
        o_ref[...] = a_ref[...].astype(BF16)

    row = pl.BlockSpec((tr, c), lambda i: (i, 0))
    return _call(body, grid=(r // tr,), in_specs=[row], out_specs=[row], out_shape=[jax.ShapeDtypeStruct((r, c), BF16)],
                 semantics=("parallel",), name=name, args=(a,), jobs=jobs)[0]


TILES = dict(tm=1024, tn=1024, tk=2048, row=256, attn=512, mixer=256, adam=128, add=1024)


def _tile(pref, n):
    for t in range(min(pref, n), 0, -1):
        if n % t == 0 and (t % SUBLANES == 0 or t == n):
            return t
    return n


def _accumulate(acc, step, n_steps, product, write):
    if n_steps == 1:
        write(product())
        return

    @pl.when(step == 0)
    def _():
        acc[...] = product()

    @pl.when(jnp.logical_and(step > 0, step < n_steps - 1))
    def _():
        acc[...] += product()

    @pl.when(step == n_steps - 1)
    def _():
        write(acc[...] + product())


def _acc_scratch(n_steps, tm, tn):
    return [] if n_steps == 1 else [pltpu.VMEM((tm, tn), F32)]


def _mm_nn(a, b3, out_dtypes, name, *, tm=None, tn=None, tk=None, epilogue=None, extras=(), jobs=()):
    m, k = a.shape
    g, k2, ns = b3.shape
    assert k == k2
    n = g * ns
    tm, tn, tk = _tile(tm or TILES["tm"], m), _tile(tn or TILES["tn"], ns), _tile(tk or TILES["tk"], k)
    nb, nk = ns // tn, k // tk
    n_ex, n_out = len(extras), len(out_dtypes)

    def body(*refs):
        a_ref, b_ref = refs[:2]
        ex = refs[2:2 + n_ex]
        outs = refs[2 + n_ex:2 + n_ex + n_out]
        acc = refs[-1] if nk > 1 else None

        def write(r):
            res = epilogue(r, *[e[...] for e in ex]) if epilogue is not None else (r,)
            for o, v in zip(outs, res):
                o[...] = v.astype(o.dtype)

        _accumulate(acc, pl.program_id(2), nk,
                    lambda: jnp.dot(a_ref[...], b_ref[...], preferred_element_type=F32), write)

    tile_out = pl.BlockSpec((tm, tn), lambda i, j, kk: (i, j))
    return _call(
        body, grid=(m // tm, n // tn, nk),
        in_specs=[pl.BlockSpec((tm, tk), lambda i, j, kk: (i, kk)),
                  pl.BlockSpec((None, tk, tn), lambda i, j, kk: (j // nb, kk, j % nb))] + [tile_out] * n_ex,
        out_specs=[tile_out] * n_out,
        out_shape=[jax.ShapeDtypeStruct((m, n), d) for d in out_dtypes],
        scratch_shapes=_acc_scratch(nk, tm, tn),
        semantics=("parallel", "parallel", "arbitrary"), name=name, args=(a, b3, *extras), jobs=jobs)


def _mm_nt(a, b3, out_dtypes, name, *, tm=None, tn=None, tk=None, epilogue=None, extras=(), jobs=(), part=(0, 1)):
    m, n = a.shape
    g, k, ns = b3.shape
    assert n == g * ns
    tm, tn, tk = _tile(tm or TILES["tm"], m // part[1]), _tile(tn or TILES["tn"], k), _tile(tk or TILES["tk"], ns)
    nb, nc = ns // tk, n // tk
    n_ex, n_out = len(extras), len(out_dtypes)
    m_blocks = m // tm // part[1]
    first = part[0] * m_blocks

    def body(*refs):
        a_ref, b_ref = refs[:2]
        ex = refs[2:2 + n_ex]
        outs = refs[2 + n_ex:2 + n_ex + n_out]
        acc = refs[-1] if nc > 1 else None

        def write(r):
            res = epilogue(r, *[e[...] for e in ex]) if epilogue is not None else (r,)
            for o, v in zip(outs, res):
                o[...] = v.astype(o.dtype)

        _accumulate(acc, pl.program_id(2), nc,
                    lambda: lax.dot_general(a_ref[...], b_ref[...], NT_DIMS, preferred_element_type=F32), write)

    tile_out = pl.BlockSpec((tm, tn), lambda i, j, cc: (i, j))
    tile_ex = pl.BlockSpec((tm, tn), lambda i, j, cc: (first + i, j))
    return _call(
        body, grid=(m_blocks, k // tn, nc),
        in_specs=[pl.BlockSpec((tm, tk), lambda i, j, cc: (first + i, cc)),
                  pl.BlockSpec((None, tn, tk), lambda i, j, cc: (cc // nb, j, cc % nb))] + [tile_ex] * n_ex,
        out_specs=[tile_out] * n_out,
        out_shape=[jax.ShapeDtypeStruct((m_blocks * tm, k), d) for d in out_dtypes],
        scratch_shapes=_acc_scratch(nc, tm, tn),
        semantics=("parallel", "parallel", "arbitrary"), name=name, args=(a, b3, *extras), jobs=jobs)


def _mm_tn(a, b, g, out_dtype, name, *, tm=None, tn=None, tk=None, jobs=()):
    s, m = a.shape
    s2, n = b.shape
    assert s == s2 and n % g == 0
    ns = n // g
    tm, tn, tk = _tile(tm or TILES["tm"], m), _tile(tn or TILES["tn"], ns), _tile(tk or TILES["tk"], s)
    nb, nc = ns // tn, s // tk

    def body(a_ref, b_ref, o_ref, *scratch):
        def write(r):
            o_ref[...] = r.astype(o_ref.dtype)

        _accumulate(scratch[0] if nc > 1 else None, pl.program_id(2), nc,
                    lambda: lax.dot_general(a_ref[...], b_ref[...], TN_DIMS, preferred_element_type=F32), write)

    return _call(
        body, grid=(m // tm, n // tn, nc),
        in_specs=[pl.BlockSpec((tk, tm), lambda i, j, cc: (cc, i)),
                  pl.BlockSpec((tk, tn), lambda i, j, cc: (cc, j))],
        out_specs=[pl.BlockSpec((None, tm, tn), lambda i, j, cc: (j // nb, i, j % nb))],
        out_shape=[jax.ShapeDtypeStruct((g, m, ns), out_dtype)],
        scratch_shapes=_acc_scratch(nc, tm, tn),
        semantics=("parallel", "parallel", "arbitrary"), name=name, args=(a, b), jobs=jobs)[0]


def _ln_stats(z):
    mu = jnp.mean(z, axis=-1, keepdims=True)
    zc = z - mu
    var = jnp.mean(zc * zc, axis=-1, keepdims=True)
    rstd = lax.rsqrt(var + LN_EPS)
    return zc * rstd, rstd


def _ln_grad(dout, xhat, rstd, gain):
    dxhat = dout * gain
    m1 = jnp.mean(dxhat, axis=-1, keepdims=True)
    m2 = jnp.mean(dxhat * xhat, axis=-1, keepdims=True)
    return rstd * (dxhat - m1 - xhat * m2)


def _ln_fwd(xres, y, gain, bias, name, jobs=()):
    s, d = xres.shape
    tr = _tile(TILES["row"], s)

    def body(x_ref, y_ref, g_ref, b_ref, xn_ref, xnb_ref, xhat_ref, rstd_ref):
        xhat, rstd = _ln_stats(ALPHA * x_ref[...] + y_ref[...])
        out = xhat * g_ref[...] + b_ref[...]
        xn_ref[...] = out
        xnb_ref[...] = out.astype(BF16)
        xhat_ref[...] = xhat
        rstd_ref[...] = rstd

    row = pl.BlockSpec((tr, d), lambda i: (i, 0))
    vec = pl.BlockSpec((1, d), lambda i: (0, 0))
    return _call(
        body, grid=(s // tr,), in_specs=[row, row, vec, vec],
        out_specs=[row, row, row, pl.BlockSpec((tr, 1), lambda i: (i, 0))],
        out_shape=[jax.ShapeDtypeStruct((s, d), F32), jax.ShapeDtypeStruct((s, d), BF16),
                   jax.ShapeDtypeStruct((s, d), F32), jax.ShapeDtypeStruct((s, 1), F32)],
        semantics=("parallel",), name=name, args=(xres, y, gain, bias), jobs=jobs)


def _ln_bwd(dout, xhat, rstd, gain, name, jobs=()):
    s, d = dout.shape
    tr = _tile(TILES["row"], s)

    def body(d_ref, xhat_ref, rstd_ref, g_ref, dz_ref, dzb_ref, dg_ref, db_ref):
        @pl.when(pl.program_id(0) == 0)
        def _():
            dg_ref[...] = jnp.zeros_like(dg_ref)
            db_ref[...] = jnp.zeros_like(db_ref)

        dout_t, xhat_t = d_ref[...], xhat_ref[...]
        dz = _ln_grad(dout_t, xhat_t, rstd_ref[...], g_ref[...])
        dz_ref[...] = dz
        dzb_ref[...] = dz.astype(BF16)
        dg_ref[...] += jnp.sum(dout_t * xhat_t, axis=0, keepdims=True)
        db_ref[...] += jnp.sum(dout_t, axis=0, keepdims=True)

    row = pl.BlockSpec((tr, d), lambda i: (i, 0))
    vec = pl.BlockSpec((1, d), lambda i: (0, 0))
    return _call(
        body, grid=(s // tr,), in_specs=[row, row, pl.BlockSpec((tr, 1), lambda i: (i, 0)), vec],
        out_specs=[row, row, vec, vec],
        out_shape=[jax.ShapeDtypeStruct((s, d), F32), jax.ShapeDtypeStruct((s, d), BF16),
                   jax.ShapeDtypeStruct((1, d), F32), jax.ShapeDtypeStruct((1, d), F32)],
        semantics=("arbitrary",), name=name, args=(dout, xhat, rstd, gain), jobs=jobs)


def _ln_loss(xres, y, gain, bias, target, name, jobs=()):
    s, d = xres.shape
    tr = _tile(TILES["row"], s)

    def body(x_ref, y_ref, g_ref, b_ref, t_ref, loss_ref, dz_ref, dzb_ref, dg_ref, db_ref):
        @pl.when(pl.program_id(0) == 0)
        def _():
            loss_ref[...] = jnp.zeros_like(loss_ref)
            dg_ref[...] = jnp.zeros_like(dg_ref)
            db_ref[...] = jnp.zeros_like(db_ref)

        xhat, rstd = _ln_stats(ALPHA * x_ref[...] + y_ref[...])
        diff = xhat * g_ref[...] + b_ref[...] - t_ref[...]
        per_row = jnp.mean(diff * diff, axis=-1, keepdims=True)
        loss_ref[...] += 0.5 * jnp.sum(per_row, axis=0, keepdims=True)
        dout = diff * (1.0 / d)
        dz = _ln_grad(dout, xhat, rstd, g_ref[...])
        dz_ref[...] = dz
        dzb_ref[...] = dz.astype(BF16)
        dg_ref[...] += jnp.sum(dout * xhat, axis=0, keepdims=True)
        db_ref[...] += jnp.sum(dout, axis=0, keepdims=True)

    row = pl.BlockSpec((tr, d), lambda i: (i, 0))
    vec = pl.BlockSpec((1, d), lambda i: (0, 0))
    return _call(
        body, grid=(s // tr,), in_specs=[row, row, vec, vec, row],
        out_specs=[pl.BlockSpec((1, 128), lambda i: (0, 0)), row, row, vec, vec],
        out_shape=[jax.ShapeDtypeStruct((1, 128), F32), jax.ShapeDtypeStruct((s, d), F32),
                   jax.ShapeDtypeStruct((s, d), BF16), jax.ShapeDtypeStruct((1, d), F32),
                   jax.ShapeDtypeStruct((1, d), F32)],
        semantics=("arbitrary",), name=name, args=(xres, y, gain, bias, target), jobs=jobs)


def _softmax_rows(s):
    e = jnp.exp(s - jnp.max(s, axis=-1, keepdims=True))
    return e / jnp.sum(e, axis=-1, keepdims=True)


def _attn_fwd(q, k, v, name, jobs=()):
    s, d = q.shape
    m = k.shape[0]
    hd = d // XATTN_HEADS
    ts = _tile(TILES["attn"], s)
    scale = hd ** -0.5

    def body(q_ref, k_ref, v_ref, o_ref):
        for h in range(XATTN_HEADS):
            hs = slice(h * hd, (h + 1) * hd)
            sc = lax.dot_general(q_ref[:, hs], k_ref[:, hs], NT_DIMS, preferred_element_type=F32) * scale
            p = _softmax_rows(sc).astype(BF16)
            o_ref[:, hs] = jnp.dot(p, v_ref[:, hs], preferred_element_type=F32).astype(BF16)

    row = pl.BlockSpec((ts, d), lambda i: (i, 0))
    memb = pl.BlockSpec((m, d), lambda i: (0, 0))
    return _call(
        body, grid=(s // ts,), in_specs=[row, memb, memb], out_specs=[row],
        out_shape=[jax.ShapeDtypeStruct((s, d), BF16)],
        semantics=("parallel",), name=name, args=(q, k, v), jobs=jobs)[0]


def _attn_bwd(q, k, v, do, name, jobs=()):
    s, d = q.shape
    m = k.shape[0]
    hd = d // XATTN_HEADS
    ts = _tile(TILES["attn"], s)
    scale = hd ** -0.5

    def body(q_ref, k_ref, v_ref, do_ref, dq_ref, dk_ref, dv_ref):
        @pl.when(pl.program_id(0) == 0)
        def _():
            dk_ref[...] = jnp.zeros_like(dk_ref)
            dv_ref[...] = jnp.zeros_like(dv_ref)

        for h in range(XATTN_HEADS):
            hs = slice(h * hd, (h + 1) * hd)
            qh, kh, vh, doh = q_ref[:, hs], k_ref[:, hs], v_ref[:, hs], do_ref[:, hs]
            sc = lax.dot_general(qh, kh, NT_DIMS, preferred_element_type=F32) * scale
            p = _softmax_rows(sc)
            pb = p.astype(BF16)
            dp = lax.dot_general(doh, vh, NT_DIMS, preferred_element_type=F32)
            ds = (p * (dp - jnp.sum(dp * p, axis=-1, keepdims=True)) * scale).astype(BF16)
            dq_ref[:, hs] = jnp.dot(ds, kh, preferred_element_type=F32).astype(BF16)
            dk_ref[:, hs] += lax.dot_general(ds, qh, TN_DIMS, preferred_element_type=F32)
            dv_ref[:, hs] += lax.dot_general(pb, doh, TN_DIMS, preferred_element_type=F32)

    row = pl.BlockSpec((ts, d), lambda i: (i, 0))
    memb = pl.BlockSpec((m, d), lambda i: (0, 0))
    return _call(
        body, grid=(s // ts,), in_specs=[row, memb, memb, row], out_specs=[row, memb, memb],
        out_shape=[jax.ShapeDtypeStruct((s, d), BF16), jax.ShapeDtypeStruct((m, d), F32),
                   jax.ShapeDtypeStruct((m, d), F32)],
        semantics=("arbitrary",), name=name, args=(q, k, v, do), jobs=jobs)


def _sigmoid(x):
    return 1.0 / (1.0 + jnp.exp(-x))


def _log1p(x):
    u = 1.0 + x
    return jnp.where(u == 1.0, x, jnp.log(u) * (x / jnp.where(u == 1.0, 1.0, u - 1.0)))


def _softplus(x):
    return jnp.maximum(x, 0.0) + _log1p(jnp.exp(-jnp.abs(x)))


def _expm1(x):
    series = x * (1.0 + x * 0.5 * (1.0 + x * (1.0 / 3.0) * (1.0 + x * 0.25 * (1.0 + x * 0.2 * (1.0 + x * (1.0 / 6.0))))))
    return jnp.where(jnp.abs(x) < 0.1, series, jnp.exp(x) - 1.0)


GELU_K = 0.7978845608028654
GELU_C = 0.044715


def _gelu(x):
    return 0.5 * x * (1.0 + jnp.tanh(GELU_K * (x + GELU_C * (x * x * x))))


def _gelu_grad(x):
    th = jnp.tanh(GELU_K * (x + GELU_C * (x * x * x)))
    return 0.5 * (1.0 + th) + 0.5 * x * (1.0 - th * th) * GELU_K * (1.0 + 3.0 * GELU_C * x * x)


def _window_sum(ext_ref, first, rows, cols, w, step):
    acc = ext_ref[first:first + rows, cols]
    for kk in range(1, w):
        acc = acc + ext_ref[first + step * kk:first + step * kk + rows, cols]
    return acc


def _lru_gates(c_s, wa_ref, ba_ref, wx_ref, bx_ref, lam_ref, t_idx, hd, r_s, i_s, a_s, mult_s):
    sp = _softplus(-lam_ref[...])
    for h in range(LRU_HEADS):
        hs = slice(h * hd, (h + 1) * hd)
        chb = c_s[:, hs].astype(BF16)
        r = _sigmoid(jnp.dot(chb, wa_ref[h], preferred_element_type=F32) + ba_ref[:, hs])
        ig = _sigmoid(jnp.dot(chb, wx_ref[h], preferred_element_type=F32) + bx_ref[:, hs])
        log_a = -LRU_C * r * sp[:, hs]
        mult = jnp.sqrt(-_expm1(2.0 * log_a))
        r_s[:, hs] = r
        i_s[:, hs] = ig
        a_s[:, hs] = jnp.exp(log_a)
        mult_s[:, hs] = jnp.where(t_idx == 0, 1.0, mult)


def _conv(ext_ref, cw_ref, cb_ref, rows):
    acc = cb_ref[...] + cw_ref[0:1, :] * ext_ref[CONV_HALO - 3:CONV_HALO - 3 + rows, :]
    for kk in range(1, CONV_WIDTH):
        off = CONV_HALO - (CONV_WIDTH - 1) + kk
        acc = acc + cw_ref[kk:kk + 1, :] * ext_ref[off:off + rows, :]
    return acc


def _mixer_fwd(proj, wp, bp, ps, cw, cb, wa, ba, wx, bx, lam, name, jobs=()):
    s, p3 = proj.shape
    p = p3 // 3
    cg, hd = p // N_POOL_GROUPS, p // LRU_HEADS
    t = _tile(TILES["mixer"], s)

    def body(up_ref, ul_ref, ug_ref, wp_ref, bp_ref, ps_ref, cw_ref, cb_ref, wa_ref, ba_ref, wx_ref, bx_ref,
             lam_ref, ycat_ref, h_ref, extp, extl, hc, c_s, r_s, i_s, a_s, b_s):
        i = pl.program_id(0)

        @pl.when(i == 0)
        def _():
            extp[0:POOL_HALO, :] = jnp.zeros((POOL_HALO, p), F32)
            extl[0:CONV_HALO, :] = jnp.zeros((CONV_HALO, p), F32)
            hc[...] = jnp.zeros_like(hc)

        t_idx = i * t + lax.broadcasted_iota(jnp.int32, (t, 1), 0)

        extp[POOL_HALO:POOL_HALO + t, :] = up_ref[...]
        for g, w in enumerate(POOL_WINDOWS):
            cs = slice(g * cg, (g + 1) * cg)
            cnt = jnp.minimum(t_idx + 1, w).astype(F32)
            mixed = _window_sum(extp, POOL_HALO, t, cs, w, -1) / cnt - up_ref[:, cs]
            pre = jnp.dot(mixed.astype(BF16), wp_ref[g], preferred_element_type=F32) + bp_ref[:, cs]
            ycat_ref[:, cs] = (pre * ps_ref[:, cs]).astype(BF16)
        extp[0:POOL_HALO, :] = extp[t:t + POOL_HALO, :]

        extl[CONV_HALO:CONV_HALO + t, :] = ul_ref[...]
        c_s[...] = _conv(extl, cw_ref, cb_ref, t)
        extl[0:CONV_HALO, :] = extl[t:t + CONV_HALO, :]
        _lru_gates(c_s, wa_ref, ba_ref, wx_ref, bx_ref, lam_ref, t_idx, hd, r_s, i_s, a_s, b_s)
        b_s[...] = b_s[...] * (i_s[...] * c_s[...])

        rows = lax.broadcasted_iota(jnp.int32, (SUBLANES, p), 0)

        def block(bi, h):
            r0 = pl.multiple_of(bi * SUBLANES, SUBLANES)
            at = a_s[pl.ds(r0, SUBLANES), :]
            bt = b_s[pl.ds(r0, SUBLANES), :]
            out = jnp.zeros((SUBLANES, p), F32)
            for j in range(SUBLANES):
                h = at[j:j + 1, :] * h + bt[j:j + 1, :]
                out = jnp.where(rows == j, h, out)
            h_ref[pl.ds(r0, SUBLANES), :] = out
            return h

        hc[0:1, :] = lax.fori_loop(0, t // SUBLANES, block, hc[0:1, :])
        ycat_ref[:, p:2 * p] = (h_ref[...] * _gelu(ug_ref[...])).astype(BF16)

    def col(j):
        return pl.BlockSpec((t, p), lambda i: (i, j))

    def whole(a):
        nd = a.ndim
        return pl.BlockSpec(a.shape, lambda i: (0,) * nd)

    consts = (wp, bp, ps, cw, cb, wa, ba, wx, bx, lam)
    tile = pltpu.VMEM((t, p), F32)
    return _call(
        body, grid=(s // t,), in_specs=[col(0), col(1), col(2)] + [whole(a) for a in consts],
        out_specs=[pl.BlockSpec((t, 2 * p), lambda i: (i, 0)), pl.BlockSpec((t, p), lambda i: (i, 0))],
        out_shape=[jax.ShapeDtypeStruct((s, 2 * p), BF16), jax.ShapeDtypeStruct((s, p), F32)],
        scratch_shapes=[pltpu.VMEM((t + POOL_HALO, p), F32), pltpu.VMEM((t + CONV_HALO, p), F32),
                        pltpu.VMEM((SUBLANES, p), F32), tile, tile, tile, tile, tile],
        semantics=("arbitrary",), name=name, args=(proj, proj, proj, *consts), jobs=jobs)


def _mixer_bwd(dycat, proj, hsave, wp, bp, ps, cw, cb, wa, ba, wx, bx, lam, name, jobs=()):
    s, p3 = proj.shape
    p = p3 // 3
    cg, hd = p // N_POOL_GROUPS, p // LRU_HEADS
    t = _tile(TILES["mixer"], s)
    nt = s // t

    def body(dyp_ref, dyl_ref, up_ref, ul_ref, ug_ref, upp_ref, ulp_ref, h_ref, hp_ref,
             wp_ref, bp_ref, ps_ref, cw_ref, cb_ref, wa_ref, ba_ref, wx_ref, bx_ref, lam_ref,
             dproj_ref, dwp_ref, dbp_ref, dps_ref, dcw_ref, dcb_ref, dwa_ref, dba_ref, dwx_ref, dbx_ref, dlam_ref,
             extp, extg, extl, extdc, exth, ghc, c_s, r_s, i_s, a_s, mult_s, gh_s):
        i = pl.program_id(0)
        ib = nt - 1 - i

        @pl.when(i == 0)
        def _():
            for ref in (dwp_ref, dbp_ref, dps_ref, dcw_ref, dcb_ref, dwa_ref, dba_ref, dwx_ref, dbx_ref, dlam_ref):
                ref[...] = jnp.zeros_like(ref)
            extg[t:t + POOL_HALO, :] = jnp.zeros((POOL_HALO, p), F32)
            extdc[t:t + CONV_HALO, :] = jnp.zeros((CONV_HALO, p), F32)
            ghc[...] = jnp.zeros_like(ghc)

        t_idx = ib * t + lax.broadcasted_iota(jnp.int32, (t, 1), 0)
        seq_start = ib == 0

        extl[0:CONV_HALO, :] = jnp.where(seq_start, 0.0, ulp_ref[...])
        extl[CONV_HALO:CONV_HALO + t, :] = ul_ref[...]
        c_s[...] = _conv(extl, cw_ref, cb_ref, t)
        _lru_gates(c_s, wa_ref, ba_ref, wx_ref, bx_ref, lam_ref, t_idx, hd, r_s, i_s, a_s, mult_s)
        exth[0:SUBLANES, :] = jnp.where(seq_start, 0.0, hp_ref[...])
        exth[SUBLANES:SUBLANES + t, :] = h_ref[...]

        ug = ug_ref[...]
        dyl = dyl_ref[...]
        dproj_ref[:, 2 * p:3 * p] = (dyl * h_ref[...] * _gelu_grad(ug)).astype(BF16)
        gh_s[...] = dyl * _gelu(ug)

        rows = lax.broadcasted_iota(jnp.int32, (SUBLANES, p), 0)
        nblk = t // SUBLANES

        def block(bi, carry):
            r0 = pl.multiple_of((nblk - 1 - bi) * SUBLANES, SUBLANES)
            at = a_s[pl.ds(r0, SUBLANES), :]
            dt = gh_s[pl.ds(r0, SUBLANES), :]
            out = jnp.zeros((SUBLANES, p), F32)
            for j in range(SUBLANES - 1, -1, -1):
                gh = dt[j:j + 1, :] + carry
                out = jnp.where(rows == j, gh, out)
                carry = at[j:j + 1, :] * gh
            gh_s[pl.ds(r0, SUBLANES), :] = out
            return carry

        ghc[0:1, :] = lax.fori_loop(0, nblk, block, ghc[0:1, :])

        sp = _softplus(-lam_ref[...])
        dsp_dlam = -_sigmoid(-lam_ref[...])
        for h in range(LRU_HEADS):
            hs = slice(h * hd, (h + 1) * hd)
            gh, a, mult, r, ig, c = gh_s[:, hs], a_s[:, hs], mult_s[:, hs], r_s[:, hs], i_s[:, hs], c_s[:, hs]
            hprev = exth[SUBLANES - 1:SUBLANES - 1 + t, hs]
            dmult = gh * (ig * c)
            dlog_a = a * gh * hprev + jnp.where(t_idx == 0, 0.0, -dmult * a * a / mult)
            dlam_ref[:, hs] += jnp.sum(dlog_a * r, axis=0, keepdims=True) * (-LRU_C) * dsp_dlam[:, hs]
            dpa = dlog_a * (-LRU_C * sp[:, hs]) * r * (1.0 - r)
            dpx = gh * mult * c * ig * (1.0 - ig)
            dpab, dpxb, chb = dpa.astype(BF16), dpx.astype(BF16), c.astype(BF16)
            dwa_ref[h] += lax.dot_general(chb, dpab, TN_DIMS, preferred_element_type=F32)
            dwx_ref[h] += lax.dot_general(chb, dpxb, TN_DIMS, preferred_element_type=F32)
            dba_ref[:, hs] += jnp.sum(dpa, axis=0, keepdims=True)
            dbx_ref[:, hs] += jnp.sum(dpx, axis=0, keepdims=True)
            dc = (gh * mult * ig
                  + lax.dot_general(dpab, wa_ref[h], NT_DIMS, preferred_element_type=F32)
                  + lax.dot_general(dpxb, wx_ref[h], NT_DIMS, preferred_element_type=F32))
            extdc[0:t, hs] = dc
            dcb_ref[:, hs] += jnp.sum(dc, axis=0, keepdims=True)
            for kk in range(CONV_WIDTH):
                off = CONV_HALO - (CONV_WIDTH - 1) + kk
                dcw_ref[kk:kk + 1, hs] += jnp.sum(dc * extl[off:off + t, hs], axis=0, keepdims=True)
        du_lru = cw_ref[0:1, :] * extdc[CONV_WIDTH - 1:CONV_WIDTH - 1 + t, :]
        for kk in range(1, CONV_WIDTH):
            off = CONV_WIDTH - 1 - kk
            du_lru = du_lru + cw_ref[kk:kk + 1, :] * extdc[off:off + t, :]
        dproj_ref[:, p:2 * p] = du_lru.astype(BF16)
        extdc[t:t + CONV_HALO, :] = extdc[0:CONV_HALO, :]

        extp[0:POOL_HALO, :] = jnp.where(seq_start, 0.0, upp_ref[...])
        extp[POOL_HALO:POOL_HALO + t, :] = up_ref[...]
        for g, w in enumerate(POOL_WINDOWS):
            cs = slice(g * cg, (g + 1) * cg)
            cnt = jnp.minimum(t_idx + 1, w).astype(F32)
            mixed = (_window_sum(extp, POOL_HALO, t, cs, w, -1) / cnt - up_ref[:, cs]).astype(BF16)
            pre = jnp.dot(mixed, wp_ref[g], preferred_element_type=F32) + bp_ref[:, cs]
            dyp = dyp_ref[:, cs]
            dps_ref[:, cs] += jnp.sum(dyp * pre, axis=0, keepdims=True)
            dpre = dyp * ps_ref[:, cs]
            dpreb = dpre.astype(BF16)
            dbp_ref[:, cs] += jnp.sum(dpre, axis=0, keepdims=True)
            dwp_ref[g] += lax.dot_general(mixed, dpreb, TN_DIMS, preferred_element_type=F32)
            dmixed = lax.dot_general(dpreb, wp_ref[g], NT_DIMS, preferred_element_type=F32)
            extg[0:t, cs] = dmixed / cnt
            dproj_ref[:, cs] = (_window_sum(extg, 0, t, cs, w, 1) - dmixed).astype(BF16)
        extg[t:t + POOL_HALO, :] = extg[0:POOL_HALO, :]

    def col(j):
        return pl.BlockSpec((t, p), lambda i: (nt - 1 - i, j))

    def prev(rows, j):
        per = t // rows
        return pl.BlockSpec((rows, p), lambda i: (jnp.maximum((nt - 1 - i) * per - 1, 0), j))

    def whole(a):
        nd = a.ndim
        return pl.BlockSpec(a.shape, lambda i: (0,) * nd)

    consts = (wp, bp, ps, cw, cb, wa, ba, wx, bx, lam)
    grads = (wp, bp, ps, cw, cb, wa, ba, wx, bx, lam)
    tile = pltpu.VMEM((t, p), F32)
    return _call(
        body, grid=(nt,),
        in_specs=[col(0), col(1), col(0), col(1), col(2), prev(POOL_HALO, 0), prev(CONV_HALO, 1), col(0),
                  prev(SUBLANES, 0)] + [whole(a) for a in consts],
        out_specs=[pl.BlockSpec((t, 3 * p), lambda i: (nt - 1 - i, 0))] + [whole(a) for a in grads],
        out_shape=[jax.ShapeDtypeStruct((s, 3 * p), BF16)] + [jax.ShapeDtypeStruct(a.shape, F32) for a in grads],
        scratch_shapes=[pltpu.VMEM((t + POOL_HALO, p), F32), pltpu.VMEM((t + POOL_HALO, p), F32),
                        pltpu.VMEM((t + CONV_HALO, p), F32), pltpu.VMEM((t + CONV_HALO, p), F32),
                        pltpu.VMEM((t + SUBLANES, p), F32), pltpu.VMEM((SUBLANES, p), F32),
                        tile, tile, tile, tile, tile, tile],
        semantics=("arbitrary",), name=name,
        args=(dycat, dycat, proj, proj, proj, proj, proj, hsave, hsave, *consts), jobs=jobs)


def _pair_add(parts, got, core, name):
    n, r, c = got.shape
    tr = _tile(TILES["add"], r)

    def body(core_ref, a_ref, b_ref, o_ref):
        del core_ref
        o_ref[...] = (a_ref[...].astype(F32) + b_ref[...].astype(F32)).astype(o_ref.dtype)

    blk = pl.BlockSpec((None, tr, c), lambda k, i, core_ref: (k, i, 0))
    mine = pl.BlockSpec((None, tr, c), lambda k, i, core_ref: (2 * k + core_ref[0], i, 0))
    return _call(body, grid=(n, r // tr), in_specs=[mine, blk], out_specs=[blk],
                 out_shape=[jax.ShapeDtypeStruct(got.shape, got.dtype)], semantics=("parallel", "parallel"),
                 name=name, args=(parts, got), index=core)[0]


def _sum_parts(parts, name):
    n, r, c = parts.shape
    tr = _tile(TILES["adam"], r)

    def body(p_ref, o_ref):
        acc = p_ref[0].astype(F32)
        for d in range(1, n):
            acc = acc + p_ref[d].astype(F32)
        o_ref[...] = acc

    return _call(
        body, grid=(r // tr,), in_specs=[pl.BlockSpec((n, tr, c), lambda i: (0, i, 0))],
        out_specs=[pl.BlockSpec((tr, c), lambda i: (i, 0))], out_shape=[jax.ShapeDtypeStruct((r, c), F32)],
        semantics=("parallel",), name=name, args=(parts,))[0]


def _adamw(w, m, v, parts, name, jobs=(), own=None, chip=None):
    r, c = w.shape
    n = parts.shape[0]
    tr = _tile(TILES["adam"], r)

    def body(*refs):
        if own is not None:
            refs = refs[1:]
            own_ref, refs = refs[3], refs[:3] + refs[4:]
        w_ref, m_ref, v_ref, p_ref, g_ref, d_ref, nm_ref, nv_ref = refs
        g = p_ref[0].astype(F32)
        if own is not None:
            g = own_ref[...].astype(F32) + g
        for d in range(1, n):
            g = g + p_ref[d].astype(F32)
        nm = ADAM_B1 * m_ref[...] + (1.0 - ADAM_B1) * g
        nv = ADAM_B2 * v_ref[...] + (1.0 - ADAM_B2) * (g * g)
        m_hat = nm / (1.0 - ADAM_B1 ** ADAM_STEP)
        v_hat = nv / (1.0 - ADAM_B2 ** ADAM_STEP)
        g_ref[...] = g
        d_ref[...] = -ADAM_LR * (m_hat / (jnp.sqrt(v_hat) + ADAM_EPS) + ADAM_WD * w_ref[...])
        nm_ref[...] = nm
        nv_ref[...] = nv

    row = pl.BlockSpec((tr, c), lambda i, *_: (i, 0))
    in_specs, args = [row, row, row], [w, m, v]
    if own is not None:
        in_specs.append(pl.BlockSpec((None, tr, c), lambda i, chip_ref: (chip_ref[0], i, 0)))
        args.append(own)
    in_specs.append(pl.BlockSpec((n, tr, c), lambda i, *_: (0, i, 0)))
    args.append(parts)
    return _call(
        body, grid=(r // tr,), in_specs=in_specs, out_specs=[row] * 4, out_shape=[jax.ShapeDtypeStruct((r, c), F32)] * 4,
        semantics=("parallel",), name=name, args=args, jobs=jobs, index=chip if own is not None else None)


SMALL_ORDER = ("w_a", "w_x", "conv_w", "b_pool", "conv_b", "b_a", "b_x", "lru_lambda", "pool_scale",
               "ln1_g", "ln1_b", "ln2_g", "ln2_b", "ln3_g", "ln3_b")


def _pack_rows(a, p):
    flat = a.reshape(-1, p)
    pad = (-flat.shape[0]) % SUBLANES
    return jnp.pad(flat, ((0, pad), (0, 0))) if pad else flat


def kernel(x, mem, w_in, conv_w, conv_b, w_a, b_a, w_x, b_x, lru_lambda, w_pool, b_pool, pool_scale, w_out, ln1_g, ln1_b, w_q, w_k, w_v, w_o, ln2_g, ln2_b, w_ff1, w_ff2, ln3_g, ln3_b, loss_target, m_w_in, m_conv_w, m_conv_b, m_w_a, m_b_a, m_w_x, m_b_x, m_lru_lambda, m_w_pool, m_b_pool, m_pool_scale, m_w_out, m_ln1_g, m_ln1_b, m_w_q, m_w_k, m_w_v, m_w_o, m_ln2_g, m_ln2_b, m_w_ff1, m_w_ff2, m_ln3_g, m_ln3_b, v_w_in, v_conv_w, v_conv_b, v_w_a, v_b_a, v_w_x, v_b_x, v_lru_lambda, v_w_pool, v_b_pool, v_pool_scale, v_w_out, v_ln1_g, v_ln1_b, v_w_q, v_w_k, v_w_v, v_w_o, v_ln2_g, v_ln2_b, v_w_ff1, v_w_ff2, v_ln3_g, v_ln3_b):
    names = ("w_in", "conv_w", "conv_b", "w_a", "b_a", "w_x", "b_x", "lru_lambda", "w_pool", "b_pool", "pool_scale",
             "w_out", "ln1_g", "ln1_b", "w_q", "w_k", "w_v", "w_o", "ln2_g", "ln2_b", "w_ff1", "w_ff2", "ln3_g", "ln3_b")
    w_loc = dict(zip(names, (w_in, conv_w, conv_b, w_a, b_a, w_x, b_x, lru_lambda, w_pool, b_pool, pool_scale,
                             w_out, ln1_g, ln1_b, w_q, w_k, w_v, w_o, ln2_g, ln2_b, w_ff1, w_ff2, ln3_g, ln3_b)))
    m_loc = dict(zip(names, (m_w_in, m_conv_w, m_conv_b, m_w_a, m_b_a, m_w_x, m_b_x, m_lru_lambda, m_w_pool, m_b_pool,
                             m_pool_scale, m_w_out, m_ln1_g, m_ln1_b, m_w_q, m_w_k, m_w_v, m_w_o, m_ln2_g, m_ln2_b,
                             m_w_ff1, m_w_ff2, m_ln3_g, m_ln3_b)))
    v_loc = dict(zip(names, (v_w_in, v_conv_w, v_conv_b, v_w_a, v_b_a, v_w_x, v_b_x, v_lru_lambda, v_w_pool, v_b_pool,
                             v_pool_scale, v_w_out, v_ln1_g, v_ln1_b, v_w_q, v_w_k, v_w_v, v_w_o, v_ln2_g, v_ln2_b,
                             v_w_ff1, v_w_ff2, v_ln3_g, v_ln3_b)))
    s, d = x.shape[1], x.shape[2]
    p = conv_b.shape[1]
    cg = p // N_POOL_GROUPS
    hd = p // LRU_HEADS
    me = 4 * lax.axis_index("x") + 2 * lax.axis_index("y") + lax.axis_index("c")

    xs, mems, tgt = x[0], mem[0], loss_target[0]
    memb = mems.astype(BF16)

    gathers = {n: _Job("gather", w_loc[n][0].astype(BF16))
               for n in ("w_in", "w_out", "w_q", "w_k", "w_v", "w_o", "w_ff2", "w_pool")}
    ff1_shard = w_ff1[0].astype(BF16)
    ff1_rows = ff1_shard.shape[0] // FF1_PIECES

    def ff1_piece(i, earlier=None):
        return _Job("gather", ff1_shard[i * ff1_rows:(i + 1) * ff1_rows], window=(i * ff1_rows, ff1_shard.shape[0]),
                    into=None if earlier is None else earlier.out)
    tiny = jnp.concatenate([_pack_rows(conv_w[0], p // N_DEV),
                            _pack_rows(jnp.pad(b_pool[0], ((0, 0), (0, p // N_DEV - cg // N_DEV))), p // N_DEV)], axis=0)
    gathers["tiny"] = _Job("gather", tiny)

    def gathered(n):
        full = gathers[n].out
        if n == "w_in":
            return jnp.transpose(full, (1, 0, 2)).reshape(1, full.shape[1], -1)
        return full.reshape(1, -1, full.shape[-1])

    W = {"conv_b": conv_b, "b_a": b_a.reshape(1, p), "b_x": b_x.reshape(1, p), "lru_lambda": lru_lambda,
         "pool_scale": pool_scale, "w_a": w_a[0].astype(BF16), "w_x": w_x[0].astype(BF16)}
    for n in ("ln1_g", "ln1_b", "ln2_g", "ln2_b", "ln3_g", "ln3_b"):
        W[n] = w_loc[n]

    out_g, out_d, out_m, out_v = {}, {}, {}, {}
    pairs, quads, sums = {}, {}, {}
    core =lax.axis_index("c").astype(jnp.int32).reshape(1)
    chip = (2 * lax.axis_index("x") + lax.axis_index("y")).astype(jnp.int32).reshape(1)

    def pair(n, partial):
        pairs[n] = _Job("pair", partial.reshape(N_DEV, -1, partial.shape[-1]))
        return pairs[n]

    def quad(n, lo=0, hi=1, of=1):
        if lo == 0:
            sums[n] = _pair_add(pairs[n].src, pairs[n].out, core, "add_" + n)
        rows = sums[n].shape[1] // of
        quads[n] = _Job("quad", sums[n], window=None if (lo, hi) == (0, of) else (lo * rows, (hi - lo) * rows),
                        into=None if lo == 0 else quads[n].out)
        return quads[n]

    def update(n, parts, jobs=(), own=None):
        shp = w_loc[n].shape
        rows = parts.shape[1]
        w2, m2, v2 = (a.reshape(rows, -1) for a in (w_loc[n], m_loc[n], v_loc[n]))
        res = _adamw(w2, m2, v2, parts.reshape(parts.shape[0], rows, -1), "adamw_" + n, jobs=jobs, own=own, chip=chip)
        out_g[n], out_d[n], out_m[n], out_v[n] = (r.reshape(shp) for r in res)

    assert FF1_PIECES == 4
    xb = _to_bf16(xs, "cast_x", jobs=[gathers["w_in"], gathers["tiny"], gathers["w_pool"]])
    W["w_pool"] = jnp.transpose(gathers["w_pool"].out, (1, 0, 2, 3)).reshape(N_POOL_GROUPS, cg, cg)
    cwb = gathers["tiny"].out
    W["conv_w"] = jnp.transpose(cwb[:, :CONV_WIDTH, :], (1, 0, 2)).reshape(CONV_WIDTH, p)
    W["b_pool"] = jnp.transpose(cwb[:, SUBLANES:SUBLANES + N_POOL_GROUPS, :cg // N_DEV], (1, 0, 2)).reshape(1, p)
    mixer_consts = (W["w_pool"], W["b_pool"], W["pool_scale"], W["conv_w"], W["conv_b"], W["w_a"], W["b_a"],
                    W["w_x"], W["b_x"], W["lru_lambda"])

    w_in_full = gathered("w_in")
    piece = ff1_piece(FF1_PIECES - 1)
    (proj,) = _mm_nn(xb, w_in_full, [F32], "fwd_proj", jobs=[gathers["w_out"], piece])
    ycat, hsave = _mixer_fwd(proj, *mixer_consts, "fwd_mixer", jobs=[gathers["w_q"], gathers["w_k"]])
    (y1,) = _mm_nn(ycat, gathered("w_out"), [F32], "fwd_out", jobs=[gathers["w_v"]])
    x1, x1b, xhat1, rstd1 = _ln_fwd(xs, y1, W["ln1_g"], W["ln1_b"], "fwd_ln1", jobs=[gathers["w_o"]])
    piece = ff1_piece(0, piece)
    (q,) = _mm_nn(x1b, gathered("w_q"), [BF16], "fwd_q", jobs=[piece])
    (k,) = _mm_nn(memb, gathered("w_k"), [BF16], "fwd_k")
    (v,) = _mm_nn(memb, gathered("w_v"), [BF16], "fwd_v")
    o = _attn_fwd(q, k, v, "fwd_attn")
    piece = ff1_piece(1, piece)
    (y2,) = _mm_nn(o, gathered("w_o"), [F32], "fwd_o", jobs=[piece])
    piece = ff1_piece(2, piece)
    x2, x2b, xhat2, rstd2 = _ln_fwd(x1, y2, W["ln2_g"], W["ln2_b"], "fwd_ln2", jobs=[piece])
    w_ff1_full = piece.out

    def relu_sq(acc):
        r = jnp.maximum(acc, 0.0)
        return r, r * r

    rb, act = _mm_nn(x2b, w_ff1_full, [BF16, BF16], "fwd_ff1", epilogue=relu_sq, jobs=[gathers["w_ff2"]])
    (y3,) = _mm_nn(act, gathered("w_ff2"), [F32], "fwd_ff2")
    loss_rows, dz3, dz3b, dg3, db3 = _ln_loss(x2, y3, W["ln3_g"], W["ln3_b"], tgt, "ln3_loss")
    loss = lax.psum(loss_rows[0, 0], MESH_AXES)

    small = {"ln3_g": dg3, "ln3_b": db3}

    def add_residual(acc, e):
        return (acc + ALPHA * e,)

    dw_ff2 = _mm_tn(act, dz3b, 1, BF16, "bwd_dw_ff2")
    (dhid,) = _mm_nt(dz3b, gathered("w_ff2"), [BF16], "bwd_dact", extras=(rb,), jobs=[pair("w_ff2", dw_ff2)],
                     epilogue=lambda acc, r: (acc * (2.0 * r.astype(F32)),))
    dw_ff1 = _mm_tn(x2b, dhid, N_DEV, BF16, "bwd_dw_ff1", jobs=[quad("w_ff2", 0, 1, 2)])
    (dx2,) = _mm_nt(dhid, w_ff1_full, [F32], "bwd_dx2", epilogue=add_residual, extras=(dz3,), tk=TILES["tk"] // 2,
                    jobs=[pair("w_ff1", dw_ff1), quad("w_ff2", 1, 2, 2)])
    dz2, dz2b, small["ln2_g"], small["ln2_b"] = _ln_bwd(dx2, xhat2, rstd2, W["ln2_g"], "bwd_ln2")

    dw_o = _mm_tn(o, dz2b, 1, BF16, "bwd_dw_o")
    (do,) = _mm_nt(dz2b, gathered("w_o"), [BF16], "bwd_do", jobs=[pair("w_o", dw_o)])
    dq, dk, dv = _attn_bwd(q, k, v, do, "bwd_attn", jobs=[quad("w_o")])
    dw_q = _mm_tn(x1b, dq, 1, BF16, "bwd_dw_q")
    dw_k = _mm_tn(memb, dk.astype(BF16), 1, BF16, "bwd_dw_k")
    dw_v = _mm_tn(memb, dv.astype(BF16), 1, BF16, "bwd_dw_v")
    (dx1,) = _mm_nt(dq, gathered("w_q"), [F32], "bwd_dx1", epilogue=add_residual, extras=(dz2,),
                    jobs=[pair("w_q", dw_q), pair("w_k", dw_k), pair("w_v", dw_v), quad("w_ff1", 0, 2, 8)])
    dz1, dz1b, small["ln1_g"], small["ln1_b"] = _ln_bwd(dx1, xhat1, rstd1, W["ln1_g"], "bwd_ln1", jobs=[quad("w_q")])

    dw_out = _mm_tn(ycat, dz1b, 1, BF16, "bwd_dw_out")
    (dycat,) = _mm_nt(dz1b, gathered("w_out"), [F32], "bwd_dycat", jobs=[pair("w_out", dw_out), quad("w_ff1", 2, 3, 8)])
    (dproj, dwp, small["b_pool"], small["pool_scale"], small["conv_w"], small["conv_b"], small["w_a"], small["b_a"],
     small["w_x"], small["b_x"], small["lru_lambda"]) = _mixer_bwd(
        dycat, proj, hsave, *mixer_consts, "bwd_mixer", jobs=[quad("w_ff1", 3, 7, 8), quad("w_k"), quad("w_v")])
    dw_pool = jnp.transpose(dwp.astype(BF16).reshape(N_POOL_GROUPS, N_DEV, cg // N_DEV, cg), (1, 0, 2, 3))
    pack = jnp.concatenate([_pack_rows(small[n], p) for n in SMALL_ORDER], axis=0)
    small_gather = _Job("gather", pack)
    dw_in = _mm_tn(xb, dproj, 1, BF16, "bwd_dw_in", jobs=[quad("w_out"), pair("w_pool", dw_pool), small_gather])
    dw_in = jnp.transpose(dw_in.reshape(dw_in.shape[1], N_DEV, -1), (1, 0, 2))
    (gx_lo,) = _mm_nt(dproj, w_in_full, [F32], "bwd_dx_lo", epilogue=add_residual, extras=(dz1,), part=(0, 2),
                      jobs=[pair("w_in", dw_in), quad("w_pool"), quad("w_ff1", 7, 8, 8)])
    (gx_hi,) = _mm_nt(dproj, w_in_full, [F32], "bwd_dx_hi", epilogue=add_residual, extras=(dz1,), part=(1, 2),
                      jobs=[quad("w_in", 0, 1, 2)])
    grad_x = jnp.concatenate([gx_lo, gx_hi], axis=0)

    update("w_ff2", quads["w_ff2"].out, jobs=[quad("w_in", 1, 2, 2)], own=quads["w_ff2"].src)
    for n in ("w_ff1", "w_o", "w_q", "w_k", "w_v", "w_out", "w_pool", "w_in"):
        update(n, quads[n].out, own=quads[n].src)

    total = _sum_parts(small_gather.out, "sum_small")
    row = 0
    for n in SMALL_ORDER:
        size = small[n].size
        nrows = size // p
        g_full = total[row:row + nrows].reshape(small[n].shape)
        row += nrows + (-nrows) % SUBLANES
        if n == "conv_w":
            g_loc = lax.dynamic_slice_in_dim(g_full, me * (p // N_DEV), p // N_DEV, axis=1)
        elif n == "b_pool":
            g_loc = lax.dynamic_slice_in_dim(g_full.reshape(N_POOL_GROUPS, cg), me * (cg // N_DEV), cg // N_DEV, axis=1)
        else:
            g_loc = g_full
        rows = g_loc.shape[0] if n not in ("w_a", "w_x") else LRU_HEADS * hd
        update(n, g_loc.reshape(1, rows, -1))

    order = names
    return (loss, grad_x[None], *[out_g[n] for n in order], *[out_d[n] for n in order],
            *[out_m[n] for n in order], *[out_v[n] for n in order])
```

```python
import functools

import jax
import jax.numpy as jnp
from jax import lax
from jax.experimental import pallas as pl
from jax.experimental.pallas import tpu as pltpu

F32 = jnp.float32
BF16 = jnp.bfloat16

N_DEV = 8
MESH_AXES = ("x", "y", "c")
POOL_WINDOWS = (2, 4, 8, 16)
N_POOL_GROUPS = len(POOL_WINDOWS)
POOL_HALO = 16
CONV_WIDTH = 4
CONV_HALO = 8
FF1_PIECES = 4
LRU_HEADS = 8
LRU_C = 8.0
XATTN_HEADS = 4
LN_EPS = 1e-5
ALPHA = 2.0 ** 0.25
ADAM_LR = 0.001
ADAM_B1 = 0.9
ADAM_B2 = 0.999
ADAM_EPS = 1e-08
ADAM_WD = 0.01
ADAM_STEP = 10
SUBLANES = 8
VMEM_LIMIT = 56 * 1024 * 1024

NT_DIMS = (((1,), (1,)), ((), ()))
TN_DIMS = (((0,), (0,)), ((), ()))


def _params(*sem):
    return pltpu.CompilerParams(dimension_semantics=sem, vmem_limit_bytes=VMEM_LIMIT)


def _place():
    return lax.axis_index("x"), lax.axis_index("y"), lax.axis_index("c")


def _remote(src, dst, send_sem, recv_sem, to):
    return pltpu.make_async_remote_copy(src_ref=src, dst_ref=dst, send_sem=send_sem, recv_sem=recv_sem,
                                        device_id=to, device_id_type=pl.DeviceIdType.MESH)


class _Job:
    def __init__(self, kind, src, window=None, into=None):
        self.kind, self.src, self.out, self.window, self.into = kind, src, None, window, into

    def out_shape(self):
        s = self.src.shape
        if self.kind == "gather" and self.window is not None:
            s = (self.window[1],) + s[1:]
        shape = {"gather": (N_DEV,) + s, "pair": (4,) + s[1:], "quad": (3,) + s[1:]}[self.kind]
        return jax.ShapeDtypeStruct(shape, self.src.dtype)

    def scratch(self):
        n = {"gather": 7, "pair": 4, "quad": 3}[self.kind]
        sems = [pltpu.SemaphoreType.DMA((n,)), pltpu.SemaphoreType.DMA((n,))]
        if self.kind == "gather":
            sems += [pltpu.SemaphoreType.DMA((2,)), pltpu.VMEM(self.src.shape, self.src.dtype)]
        return sems

    def ops(self, src, out, *scratch):
        if self.kind == "gather":
            return _gather_ops(src, out, *scratch, first_row=None if self.window is None else self.window[0])
        if self.kind == "quad":
            return _quad_ops(src, out, *scratch, rows=self.window)
        return _pair_ops(src, out, *scratch)


def _gather_ops(x_ref, out_ref, send_sems, recv_sems, local_sems, bounce, first_row=None):
    x, y, c = _place()
    me, sibling = (x, y, c), (x, y, 1 - c)
    chips = [(1 - x, y), (x, 1 - y), (1 - x, 1 - y)]

    def slot(px, py, pc):
        block = out_ref.at[4 * px + 2 * py + pc]
        return block if first_row is None else block.at[pl.ds(first_row, x_ref.shape[0])]

    def copy(k, block, to, src=None):
        return _remote(slot(*block) if src is None else src, slot(*block), send_sems.at[k], recv_sems.at[k], to)

    mine_in = pltpu.make_async_copy(x_ref, bounce, local_sems.at[0])
    mine_out = pltpu.make_async_copy(bounce, slot(*me), local_sems.at[1])
    first = [copy(0, me, sibling, src=x_ref)] + [copy(1 + j, me, (*chip, c), src=x_ref) for j, chip in enumerate(chips)]
    passed = [copy(4 + j, (*chip, c), sibling) for j, chip in enumerate(chips)]

    def start():
        mine_in.start()
        for cp in first:
            cp.start()

    def mid():
        mine_in.wait()
        mine_out.start()
        for j, chip in enumerate(chips):
            copy(1 + j, (*chip, c), me).wait_recv()
            passed[j].start()

    def finish():
        copy(0, sibling, me).wait_recv()
        for j, chip in enumerate(chips):
            copy(4 + j, (*chip, 1 - c), me).wait_recv()
        for cp in first + passed:
            cp.wait_send()
        mine_out.wait()

    return start, mid, finish


def _pair_ops(p_ref, got_ref, send_sems, recv_sems):
    x, y, c = _place()
    give = [_remote(p_ref.at[2 * k + 1 - c], got_ref.at[k], send_sems.at[k], recv_sems.at[k], (x, y, 1 - c))
            for k in range(4)]

    def start():
        for cp in give:
            cp.start()

    def finish():
        for cp in give:
            cp.wait_recv()
        for cp in give:
            cp.wait_send()

    return start, None, finish


def _quad_ops(q_ref, out_ref, send_sems, recv_sems, rows=None):
    x, y, c = _place()

    def part(block):
        return block if rows is None else block.at[pl.ds(rows[0], rows[1])]

    copies = []
    for rel in range(1, 4):
        px = 1 - x if rel & 2 else x
        py = 1 - y if rel & 1 else y
        copies.append(_remote(part(q_ref.at[2 * px + py]), part(out_ref.at[rel - 1]), send_sems.at[rel - 1],
                              recv_sems.at[rel - 1], (px, py, c)))

    def start():
        for cp in copies:
            cp.start()

    def finish():
        for cp in copies:
            cp.wait_recv()
        for cp in copies:
            cp.wait_send()

    return start, None, finish


def _call(body, *, grid, in_specs, out_specs, out_shape, scratch_shapes=(), semantics, name, args, jobs=(), index=None):
    in_specs, out_specs, out_shape = list(in_specs), list(out_specs), list(out_shape)
    scratch_shapes, jobs = list(scratch_shapes), list(jobs)
    n_in, n_out, n_scr, n_job = len(in_specs), len(out_specs), len(scratch_shapes), len(jobs)
    n_idx = 0 if index is None else 1
    job_scratch = [j.scratch() for j in jobs]
    n_steps = functools.reduce(lambda a, b: a * b, grid, 1)
    early = n_steps - 1 - max(1, n_steps // 8) if n_steps >= 4 else None

    intos = [(k, j.into) for k, j in enumerate(jobs) if j.into is not None]

    def hosted(*refs):
        idx, refs = refs[:n_idx], refs[n_idx:]
        ins, jin = refs[:n_in], refs[n_in:n_in + n_job]
        o0 = n_in + n_job + len(intos)
        outs, jout = refs[o0:o0 + n_out], refs[o0 + n_out:o0 + n_out + n_job]
        s0 = o0 + n_out + n_job
        scr, jscr = refs[s0:s0 + n_scr], refs[s0 + n_scr:]
        ops, at = [], 0
        for k, j in enumerate(jobs):
            ops.append(j.ops(jin[k], jout[k], *jscr[at:at + len(job_scratch[k])]))
            at += len(job_scratch[k])
        step = functools.reduce(lambda acc, a: acc * grid[a] + pl.program_id(a), range(len(grid)), 0)
        mids = [mid for _, mid, _ in ops if mid is not None]

        @pl.when(step == 0)
        def _():
            for start, _, _ in ops:
                start()

        if mids and early is not None:
            @pl.when(step == early)
            def _():
                for mid in mids:
                    mid()

        body(*idx, *ins, *outs, *scr)

        @pl.when(step == n_steps - 1)
        def _():
            if early is None:
                for mid in mids:
                    mid()
            for _, _, finish in ops:
                finish()

    hbm = pl.BlockSpec(memory_space=pl.ANY)
    spec = pltpu.PrefetchScalarGridSpec(
        num_scalar_prefetch=n_idx, grid=grid, in_specs=in_specs + [hbm] * (n_job + len(intos)),
        out_specs=out_specs + [hbm] * n_job, scratch_shapes=scratch_shapes + [s for js in job_scratch for s in js])
    aliases = {n_idx + n_in + n_job + q: n_out + k for q, (k, _) in enumerate(intos)}
    res = pl.pallas_call(
        hosted if jobs else body, grid_spec=spec, out_shape=out_shape + [j.out_shape() for j in jobs],
        input_output_aliases=aliases,
        compiler_params=_params(*(["arbitrary"] * len(grid) if jobs else semantics)), name=name,
    )(*([] if index is None else [index]), *args, *[j.src for j in jobs], *[buf for _, buf in intos])
    for j, o in zip(jobs, res[n_out:]):
        j.out = o
    return res[:n_out]


def _to_bf16(a, name, jobs=()):
    r, c = a.shape
    tr = _tile(TILES["row"], r)

    def body(a_ref, o_ref):
        o_ref[...] = a_ref[...].astype(BF16)

    row = pl.BlockSpec((tr, c), lambda i: (i, 0))
    return _call(body, grid=(r // tr,), in_specs=[row], out_specs=[row], out_shape=[jax.ShapeDtypeStruct((r, c), BF16)],
                 semantics=("parallel",), name=name, args=(a,), jobs=jobs)[0]


TILES = dict(tm=1024, tn=1024, tk=2048, row=512, attn=512, mixer=256, adam=256, add=1024)


def _tile(pref, n):
    for t in range(min(pref, n), 0, -1):
        if n % t == 0 and (t % SUBLANES == 0 or t == n):
            return t
    return n


def _accumulate(acc, step, n_steps, product, write):
    if n_steps == 1:
        write(product())
        return

    @pl.when(step == 0)
    def _():
        acc[...] = product()

    @pl.when(jnp.logical_and(step > 0, step < n_steps - 1))
    def _():
        acc[...] += product()

    @pl.when(step == n_steps - 1)
    def _():
        write(acc[...] + product())


def _acc_scratch(n_steps, tm, tn):
    return [] if n_steps == 1 else [pltpu.VMEM((tm, tn), F32)]


def _mm_nn(a, b3, out_dtypes, name, *, tm=None, tn=None, tk=None, epilogue=None, extras=(), jobs=()):
    m, k = a.shape
    g, k2, ns = b3.shape
    assert k == k2
    n = g * ns
    tm, tn, tk = _tile(tm or TILES["tm"], m), _tile(tn or TILES["tn"], ns), _tile(tk or TILES["tk"], k)
    nb, nk = ns // tn, k // tk
    n_ex, n_out = len(extras), len(out_dtypes)

    def body(*refs):
        a_ref, b_ref = refs[:2]
        ex = refs[2:2 + n_ex]
        outs = refs[2 + n_ex:2 + n_ex + n_out]
        acc = refs[-1] if nk > 1 else None

        def write(r):
            res = epilogue(r, *[e[...] for e in ex]) if epilogue is not None else (r,)
            for o, v in zip(outs, res):
                o[...] = v.astype(o.dtype)

        _accumulate(acc, pl.program_id(2), nk,
                    lambda: jnp.dot(a_ref[...], b_ref[...], preferred_element_type=F32), write)

    tile_out = pl.BlockSpec((tm, tn), lambda i, j, kk: (i, j))
    return _call(
        body, grid=(m // tm, n // tn, nk),
        in_specs=[pl.BlockSpec((tm, tk), lambda i, j, kk: (i, kk)),
                  pl.BlockSpec((None, tk, tn), lambda i, j, kk: (j // nb, kk, j % nb))] + [tile_out] * n_ex,
        out_specs=[tile_out] * n_out,
        out_shape=[jax.ShapeDtypeStruct((m, n), d) for d in out_dtypes],
        scratch_shapes=_acc_scratch(nk, tm, tn),
        semantics=("parallel", "parallel", "arbitrary"), name=name, args=(a, b3, *extras), jobs=jobs)


def _mm_nt(a, b3, out_dtypes, name, *, tm=None, tn=None, tk=None, epilogue=None, extras=(), jobs=(), part=(0, 1)):
    m, n = a.shape
    g, k, ns = b3.shape
    assert n == g * ns
    tm, tn, tk = _tile(tm or TILES["tm"], m // part[1]), _tile(tn or TILES["tn"], k), _tile(tk or TILES["tk"], ns)
    nb, nc = ns // tk, n // tk
    n_ex, n_out = len(extras), len(out_dtypes)
    m_blocks = m // tm // part[1]
    first = part[0] * m_blocks

    def body(*refs):
        a_ref, b_ref = refs[:2]
        ex = refs[2:2 + n_ex]
        outs = refs[2 + n_ex:2 + n_ex + n_out]
        acc = refs[-1] if nc > 1 else None

        def write(r):
            res = epilogue(r, *[e[...] for e in ex]) if epilogue is not None else (r,)
            for o, v in zip(outs, res):
                o[...] = v.astype(o.dtype)

        _accumulate(acc, pl.program_id(2), nc,
                    lambda: lax.dot_general(a_ref[...], b_ref[...], NT_DIMS, preferred_element_type=F32), write)

    tile_out = pl.BlockSpec((tm, tn), lambda i, j, cc: (i, j))
    tile_ex = pl.BlockSpec((tm, tn), lambda i, j, cc: (first + i, j))
    return _call(
        body, grid=(m_blocks, k // tn, nc),
        in_specs=[pl.BlockSpec((tm, tk), lambda i, j, cc: (first + i, cc)),
                  pl.BlockSpec((None, tn, tk), lambda i, j, cc: (cc // nb, j, cc % nb))] + [tile_ex] * n_ex,
        out_specs=[tile_out] * n_out,
        out_shape=[jax.ShapeDtypeStruct((m_blocks * tm, k), d) for d in out_dtypes],
        scratch_shapes=_acc_scratch(nc, tm, tn),
        semantics=("parallel", "parallel", "arbitrary"), name=name, args=(a, b3, *extras), jobs=jobs)


def _mm_tn(a, b, g, out_dtype, name, *, tm=None, tn=None, tk=None, jobs=()):
    s, m = a.shape
    s2, n = b.shape
    assert s == s2 and n % g == 0
    ns = n // g
    tm, tn, tk = _tile(tm or TILES["tm"], m), _tile(tn or TILES["tn"], ns), _tile(tk or TILES["tk"], s)
    nb, nc = ns // tn, s // tk

    def body(a_ref, b_ref, o_ref, *scratch):
        def write(r):
            o_ref[...] = r.astype(o_ref.dtype)

        _accumulate(scratch[0] if nc > 1 else None, pl.program_id(2), nc,
                    lambda: lax.dot_general(a_ref[...], b_ref[...], TN_DIMS, preferred_element_type=F32), write)

    return _call(
        body, grid=(m // tm, n // tn, nc),
        in_specs=[pl.BlockSpec((tk, tm), lambda i, j, cc: (cc, i)),
                  pl.BlockSpec((tk, tn), lambda i, j, cc: (cc, j))],
        out_specs=[pl.BlockSpec((None, tm, tn), lambda i, j, cc: (j // nb, i, j % nb))],
        out_shape=[jax.ShapeDtypeStruct((g, m, ns), out_dtype)],
        scratch_shapes=_acc_scratch(nc, tm, tn),
        semantics=("parallel", "parallel", "arbitrary"), name=name, args=(a, b), jobs=jobs)[0]


def _ln_stats(z):
    mu = jnp.mean(z, axis=-1, keepdims=True)
    zc = z - mu
    var = jnp.mean(zc * zc, axis=-1, keepdims=True)
    rstd = lax.rsqrt(var + LN_EPS)
    return zc * rstd, rstd


def _ln_grad(dout, xhat, rstd, gain):
    dxhat = dout * gain
    m1 = jnp.mean(dxhat, axis=-1, keepdims=True)
    m2 = jnp.mean(dxhat * xhat, axis=-1, keepdims=True)
    return rstd * (dxhat - m1 - xhat * m2)


def _ln_fwd(xres, y, gain, bias, name, jobs=()):
    s, d = xres.shape
    tr = _tile(TILES["row"], s)

    def body(x_ref, y_ref, g_ref, b_ref, xn_ref, xnb_ref, xhat_ref, rstd_ref):
        xhat, rstd = _ln_stats(ALPHA * x_ref[...] + y_ref[...])
        out = xhat * g_ref[...] + b_ref[...]
        xn_ref[...] = out
        xnb_ref[...] = out.astype(BF16)
        xhat_ref[...] = xhat
        rstd_ref[...] = rstd

    row = pl.BlockSpec((tr, d), lambda i: (i, 0))
    vec = pl.BlockSpec((1, d), lambda i: (0, 0))
    return _call(
        body, grid=(s // tr,), in_specs=[row, row, vec, vec],
        out_specs=[row, row, row, pl.BlockSpec((tr, 1), lambda i: (i, 0))],
        out_shape=[jax.ShapeDtypeStruct((s, d), F32), jax.ShapeDtypeStruct((s, d), BF16),
                   jax.ShapeDtypeStruct((s, d), F32), jax.ShapeDtypeStruct((s, 1), F32)],
        semantics=("parallel",), name=name, args=(xres, y, gain, bias), jobs=jobs)


def _ln_bwd(dout, xhat, rstd, gain, name, jobs=()):
    s, d = dout.shape
    tr = _tile(TILES["row"], s)

    def body(d_ref, xhat_ref, rstd_ref, g_ref, dz_ref, dzb_ref, dg_ref, db_ref):
        @pl.when(pl.program_id(0) == 0)
        def _():
            dg_ref[...] = jnp.zeros_like(dg_ref)
            db_ref[...] = jnp.zeros_like(db_ref)

        dout_t, xhat_t = d_ref[...], xhat_ref[...]
        dz = _ln_grad(dout_t, xhat_t, rstd_ref[...], g_ref[...])
        dz_ref[...] = dz
        dzb_ref[...] = dz.astype(BF16)
        dg_ref[...] += jnp.sum(dout_t * xhat_t, axis=0, keepdims=True)
        db_ref[...] += jnp.sum(dout_t, axis=0, keepdims=True)

    row = pl.BlockSpec((tr, d), lambda i: (i, 0))
    vec = pl.BlockSpec((1, d), lambda i: (0, 0))
    return _call(
        body, grid=(s // tr,), in_specs=[row, row, pl.BlockSpec((tr, 1), lambda i: (i, 0)), vec],
        out_specs=[row, row, vec, vec],
        out_shape=[jax.ShapeDtypeStruct((s, d), F32), jax.ShapeDtypeStruct((s, d), BF16),
                   jax.ShapeDtypeStruct((1, d), F32), jax.ShapeDtypeStruct((1, d), F32)],
        semantics=("arbitrary",), name=name, args=(dout, xhat, rstd, gain), jobs=jobs)


def _ln_loss(xres, y, gain, bias, target, name, jobs=()):
    s, d = xres.shape
    tr = _tile(TILES["row"], s)

    def body(x_ref, y_ref, g_ref, b_ref, t_ref, loss_ref, dz_ref, dzb_ref, dg_ref, db_ref):
        @pl.when(pl.program_id(0) == 0)
        def _():
            loss_ref[...] = jnp.zeros_like(loss_ref)
            dg_ref[...] = jnp.zeros_like(dg_ref)
            db_ref[...] = jnp.zeros_like(db_ref)

        xhat, rstd = _ln_stats(ALPHA * x_ref[...] + y_ref[...])
        diff = xhat * g_ref[...] + b_ref[...] - t_ref[...]
        per_row = jnp.mean(diff * diff, axis=-1, keepdims=True)
        loss_ref[...] += 0.5 * jnp.sum(per_row, axis=0, keepdims=True)
        dout = diff * (1.0 / d)
        dz = _ln_grad(dout, xhat, rstd, g_ref[...])
        dz_ref[...] = dz
        dzb_ref[...] = dz.astype(BF16)
        dg_ref[...] += jnp.sum(dout * xhat, axis=0, keepdims=True)
        db_ref[...] += jnp.sum(dout, axis=0, keepdims=True)

    row = pl.BlockSpec((tr, d), lambda i: (i, 0))
    vec = pl.BlockSpec((1, d), lambda i: (0, 0))
    return _call(
        body, grid=(s // tr,), in_specs=[row, row, vec, vec, row],
        out_specs=[pl.BlockSpec((1, 128), lambda i: (0, 0)), row, row, vec, vec],
        out_shape=[jax.ShapeDtypeStruct((1, 128), F32), jax.ShapeDtypeStruct((s, d), F32),
                   jax.ShapeDtypeStruct((s, d), BF16), jax.ShapeDtypeStruct((1, d), F32),
                   jax.ShapeDtypeStruct((1, d), F32)],
        semantics=("arbitrary",), name=name, args=(xres, y, gain, bias, target), jobs=jobs)


def _softmax_rows(s):
    e = jnp.exp(s - jnp.max(s, axis=-1, keepdims=True))
    return e / jnp.sum(e, axis=-1, keepdims=True)


def _attn_fwd(q, k, v, name, jobs=()):
    s, d = q.shape
    m = k.shape[0]
    hd = d // XATTN_HEADS
    ts = _tile(TILES["attn"], s)
    scale = hd ** -0.5

    def body(q_ref, k_ref, v_ref, o_ref):
        for h in range(XATTN_HEADS):
            hs = slice(h * hd, (h + 1) * hd)
            sc = lax.dot_general(q_ref[:, hs], k_ref[:, hs], NT_DIMS, preferred_element_type=F32) * scale
            p = _softmax_rows(sc).astype(BF16)
            o_ref[:, hs] = jnp.dot(p, v_ref[:, hs], preferred_element_type=F32).astype(BF16)

    row = pl.BlockSpec((ts, d), lambda i: (i, 0))
    memb = pl.BlockSpec((m, d), lambda i: (0, 0))
    return _call(
        body, grid=(s // ts,), in_specs=[row, memb, memb], out_specs=[row],
        out_shape=[jax.ShapeDtypeStruct((s, d), BF16)],
        semantics=("parallel",), name=name, args=(q, k, v), jobs=jobs)[0]


def _attn_bwd(q, k, v, do, name, jobs=()):
    s, d = q.shape
    m = k.shape[0]
    hd = d // XATTN_HEADS
    ts = _tile(TILES["attn"], s)
    scale = hd ** -0.5

    def body(q_ref, k_ref, v_ref, do_ref, dq_ref, dk_ref, dv_ref):
        @pl.when(pl.program_id(0) == 0)
        def _():
            dk_ref[...] = jnp.zeros_like(dk_ref)
            dv_ref[...] = jnp.zeros_like(dv_ref)

        for h in range(XATTN_HEADS):
            hs = slice(h * hd, (h + 1) * hd)
            qh, kh, vh, doh = q_ref[:, hs], k_ref[:, hs], v_ref[:, hs], do_ref[:, hs]
            sc = lax.dot_general(qh, kh, NT_DIMS, preferred_element_type=F32) * scale
            p = _softmax_rows(sc)
            pb = p.astype(BF16)
            dp = lax.dot_general(doh, vh, NT_DIMS, preferred_element_type=F32)
            ds = (p * (dp - jnp.sum(dp * p, axis=-1, keepdims=True)) * scale).astype(BF16)
            dq_ref[:, hs] = jnp.dot(ds, kh, preferred_element_type=F32).astype(BF16)
            dk_ref[:, hs] += lax.dot_general(ds, qh, TN_DIMS, preferred_element_type=F32)
            dv_ref[:, hs] += lax.dot_general(pb, doh, TN_DIMS, preferred_element_type=F32)

    row = pl.BlockSpec((ts, d), lambda i: (i, 0))
    memb = pl.BlockSpec((m, d), lambda i: (0, 0))
    return _call(
        body, grid=(s // ts,), in_specs=[row, memb, memb, row], out_specs=[row, memb, memb],
        out_shape=[jax.ShapeDtypeStruct((s, d), BF16), jax.ShapeDtypeStruct((m, d), F32),
                   jax.ShapeDtypeStruct((m, d), F32)],
        semantics=("arbitrary",), name=name, args=(q, k, v, do), jobs=jobs)


def _sigmoid(x):
    return 1.0 / (1.0 + jnp.exp(-x))


def _log1p(x):
    u = 1.0 + x
    return jnp.where(u == 1.0, x, jnp.log(u) * (x / jnp.where(u == 1.0, 1.0, u - 1.0)))


def _softplus(x):
    return jnp.maximum(x, 0.0) + _log1p(jnp.exp(-jnp.abs(x)))


def _expm1(x):
    series = x * (1.0 + x * 0.5 * (1.0 + x * (1.0 / 3.0) * (1.0 + x * 0.25 * (1.0 + x * 0.2 * (1.0 + x * (1.0 / 6.0))))))
    return jnp.where(jnp.abs(x) < 0.1, series, jnp.exp(x) - 1.0)


GELU_K = 0.7978845608028654
GELU_C = 0.044715


def _gelu(x):
    return 0.5 * x * (1.0 + jnp.tanh(GELU_K * (x + GELU_C * (x * x * x))))


def _gelu_grad(x):
    th = jnp.tanh(GELU_K * (x + GELU_C * (x * x * x)))
    return 0.5 * (1.0 + th) + 0.5 * x * (1.0 - th * th) * GELU_K * (1.0 + 3.0 * GELU_C * x * x)


def _window_sum(ext_ref, first, rows, cols, w, step):
    acc = ext_ref[first:first + rows, cols]
    for kk in range(1, w):
        acc = acc + ext_ref[first + step * kk:first + step * kk + rows, cols]
    return acc


def _lru_gates(c_s, wa_ref, ba_ref, wx_ref, bx_ref, lam_ref, t_idx, hd, r_s, i_s, a_s, mult_s):
    sp = _softplus(-lam_ref[...])
    for h in range(LRU_HEADS):
        hs = slice(h * hd, (h + 1) * hd)
        chb = c_s[:, hs].astype(BF16)
        r = _sigmoid(jnp.dot(chb, wa_ref[h], preferred_element_type=F32) + ba_ref[:, hs])
        ig = _sigmoid(jnp.dot(chb, wx_ref[h], preferred_element_type=F32) + bx_ref[:, hs])
        log_a = -LRU_C * r * sp[:, hs]
        mult = jnp.sqrt(-_expm1(2.0 * log_a))
        r_s[:, hs] = r
        i_s[:, hs] = ig
        a_s[:, hs] = jnp.exp(log_a)
        mult_s[:, hs] = jnp.where(t_idx == 0, 1.0, mult)


def _conv(ext_ref, cw_ref, cb_ref, rows):
    acc = cb_ref[...] + cw_ref[0:1, :] * ext_ref[CONV_HALO - 3:CONV_HALO - 3 + rows, :]
    for kk in range(1, CONV_WIDTH):
        off = CONV_HALO - (CONV_WIDTH - 1) + kk
        acc = acc + cw_ref[kk:kk + 1, :] * ext_ref[off:off + rows, :]
    return acc


def _mixer_fwd(proj, wp, bp, ps, cw, cb, wa, ba, wx, bx, lam, name, jobs=()):
    s, p3 = proj.shape
    p = p3 // 3
    cg, hd = p // N_POOL_GROUPS, p // LRU_HEADS
    t = _tile(TILES["mixer"], s)

    def body(up_ref, ul_ref, ug_ref, wp_ref, bp_ref, ps_ref, cw_ref, cb_ref, wa_ref, ba_ref, wx_ref, bx_ref,
             lam_ref, ycat_ref, h_ref, extp, extl, hc, c_s, r_s, i_s, a_s, b_s):
        i = pl.program_id(0)

        @pl.when(i == 0)
        def _():
            extp[0:POOL_HALO, :] = jnp.zeros((POOL_HALO, p), F32)
            extl[0:CONV_HALO, :] = jnp.zeros((CONV_HALO, p), F32)
            hc[...] = jnp.zeros_like(hc)

        t_idx = i * t + lax.broadcasted_iota(jnp.int32, (t, 1), 0)

        extp[POOL_HALO:POOL_HALO + t, :] = up_ref[...]
        for g, w in enumerate(POOL_WINDOWS):
            cs = slice(g * cg, (g + 1) * cg)
            cnt = jnp.minimum(t_idx + 1, w).astype(F32)
            mixed = _window_sum(extp, POOL_HALO, t, cs, w, -1) / cnt - up_ref[:, cs]
            pre = jnp.dot(mixed.astype(BF16), wp_ref[g], preferred_element_type=F32) + bp_ref[:, cs]
            ycat_ref[:, cs] = (pre * ps_ref[:, cs]).astype(BF16)
        extp[0:POOL_HALO, :] = extp[t:t + POOL_HALO, :]

        extl[CONV_HALO:CONV_HALO + t, :] = ul_ref[...]
        c_s[...] = _conv(extl, cw_ref, cb_ref, t)
        extl[0:CONV_HALO, :] = extl[t:t + CONV_HALO, :]
        _lru_gates(c_s, wa_ref, ba_ref, wx_ref, bx_ref, lam_ref, t_idx, hd, r_s, i_s, a_s, b_s)
        b_s[...] = b_s[...] * (i_s[...] * c_s[...])

        rows = lax.broadcasted_iota(jnp.int32, (SUBLANES, p), 0)

        def block(bi, h):
            r0 = pl.multiple_of(bi * SUBLANES, SUBLANES)
            at = a_s[pl.ds(r0, SUBLANES), :]
            bt = b_s[pl.ds(r0, SUBLANES), :]
            out = jnp.zeros((SUBLANES, p), F32)
            for j in range(SUBLANES):
                h = at[j:j + 1, :] * h + bt[j:j + 1, :]
                out = jnp.where(rows == j, h, out)
            h_ref[pl.ds(r0, SUBLANES), :] = out
            return h

        hc[0:1, :] = lax.fori_loop(0, t // SUBLANES, block, hc[0:1, :])
        ycat_ref[:, p:2 * p] = (h_ref[...] * _gelu(ug_ref[...])).astype(BF16)

    def col(j):
        return pl.BlockSpec((t, p), lambda i: (i, j))

    def whole(a):
        nd = a.ndim
        return pl.BlockSpec(a.shape, lambda i: (0,) * nd)

    consts = (wp, bp, ps, cw, cb, wa, ba, wx, bx, lam)
    tile = pltpu.VMEM((t, p), F32)
    return _call(
        body, grid=(s // t,), in_specs=[col(0), col(1), col(2)] + [whole(a) for a in consts],
        out_specs=[pl.BlockSpec((t, 2 * p), lambda i: (i, 0)), pl.BlockSpec((t, p), lambda i: (i, 0))],
        out_shape=[jax.ShapeDtypeStruct((s, 2 * p), BF16), jax.ShapeDtypeStruct((s, p), F32)],
        scratch_shapes=[pltpu.VMEM((t + POOL_HALO, p), F32), pltpu.VMEM((t + CONV_HALO, p), F32),
                        pltpu.VMEM((SUBLANES, p), F32), tile, tile, tile, tile, tile],
        semantics=("arbitrary",), name=name, args=(proj, proj, proj, *consts), jobs=jobs)


def _mixer_bwd(dycat, proj, hsave, wp, bp, ps, cw, cb, wa, ba, wx, bx, lam, name, jobs=()):
    s, p3 = proj.shape
    p = p3 // 3
    cg, hd = p // N_POOL_GROUPS, p // LRU_HEADS
    t = _tile(TILES["mixer"], s)
    nt = s // t

    def body(dyp_ref, dyl_ref, up_ref, ul_ref, ug_ref, upp_ref, ulp_ref, h_ref, hp_ref,
             wp_ref, bp_ref, ps_ref, cw_ref, cb_ref, wa_ref, ba_ref, wx_ref, bx_ref, lam_ref,
             dproj_ref, dwp_ref, dbp_ref, dps_ref, dcw_ref, dcb_ref, dwa_ref, dba_ref, dwx_ref, dbx_ref, dlam_ref,
             extp, extg, extl, extdc, exth, ghc, c_s, r_s, i_s, a_s, mult_s, gh_s):
        i = pl.program_id(0)
        ib = nt - 1 - i

        @pl.when(i == 0)
        def _():
            for ref in (dwp_ref, dbp_ref, dps_ref, dcw_ref, dcb_ref, dwa_ref, dba_ref, dwx_ref, dbx_ref, dlam_ref):
                ref[...] = jnp.zeros_like(ref)
            extg[t:t + POOL_HALO, :] = jnp.zeros((POOL_HALO, p), F32)
            extdc[t:t + CONV_HALO, :] = jnp.zeros((CONV_HALO, p), F32)
            ghc[...] = jnp.zeros_like(ghc)

        t_idx = ib * t + lax.broadcasted_iota(jnp.int32, (t, 1), 0)
        seq_start = ib == 0

        extl[0:CONV_HALO, :] = jnp.where(seq_start, 0.0, ulp_ref[...])
        extl[CONV_HALO:CONV_HALO + t, :] = ul_ref[...]
        c_s[...] = _conv(extl, cw_ref, cb_ref, t)
        _lru_gates(c_s, wa_ref, ba_ref, wx_ref, bx_ref, lam_ref, t_idx, hd, r_s, i_s, a_s, mult_s)
        exth[0:SUBLANES, :] = jnp.where(seq_start, 0.0, hp_ref[...])
        exth[SUBLANES:SUBLANES + t, :] = h_ref[...]

        ug = ug_ref[...]
        dyl = dyl_ref[...]
        dproj_ref[:, 2 * p:3 * p] = (dyl * h_ref[...] * _gelu_grad(ug)).astype(BF16)
        gh_s[...] = dyl * _gelu(ug)

        rows = lax.broadcasted_iota(jnp.int32, (SUBLANES, p), 0)
        nblk = t // SUBLANES

        def block(bi, carry):
            r0 = pl.multiple_of((nblk - 1 - bi) * SUBLANES, SUBLANES)
            at = a_s[pl.ds(r0, SUBLANES), :]
            dt = gh_s[pl.ds(r0, SUBLANES), :]
            out = jnp.zeros((SUBLANES, p), F32)
            for j in range(SUBLANES - 1, -1, -1):
                gh = dt[j:j + 1, :] + carry
                out = jnp.where(rows == j, gh, out)
                carry = at[j:j + 1, :] * gh
            gh_s[pl.ds(r0, SUBLANES), :] = out
            return carry

        ghc[0:1, :] = lax.fori_loop(0, nblk, block, ghc[0:1, :])

        sp = _softplus(-lam_ref[...])
        dsp_dlam = -_sigmoid(-lam_ref[...])
        for h in range(LRU_HEADS):
            hs = slice(h * hd, (h + 1) * hd)
            gh, a, mult, r, ig, c = gh_s[:, hs], a_s[:, hs], mult_s[:, hs], r_s[:, hs], i_s[:, hs], c_s[:, hs]
            hprev = exth[SUBLANES - 1:SUBLANES - 1 + t, hs]
            dmult = gh * (ig * c)
            dlog_a = a * gh * hprev + jnp.where(t_idx == 0, 0.0, -dmult * a * a / mult)
            dlam_ref[:, hs] += jnp.sum(dlog_a * r, axis=0, keepdims=True) * (-LRU_C) * dsp_dlam[:, hs]
            dpa = dlog_a * (-LRU_C * sp[:, hs]) * r * (1.0 - r)
            dpx = gh * mult * c * ig * (1.0 - ig)
            dpab, dpxb, chb = dpa.astype(BF16), dpx.astype(BF16), c.astype(BF16)
            dwa_ref[h] += lax.dot_general(chb, dpab, TN_DIMS, preferred_element_type=F32)
            dwx_ref[h] += lax.dot_general(chb, dpxb, TN_DIMS, preferred_element_type=F32)
            dba_ref[:, hs] += jnp.sum(dpa, axis=0, keepdims=True)
            dbx_ref[:, hs] += jnp.sum(dpx, axis=0, keepdims=True)
            dc = (gh * mult * ig
                  + lax.dot_general(dpab, wa_ref[h], NT_DIMS, preferred_element_type=F32)
                  + lax.dot_general(dpxb, wx_ref[h], NT_DIMS, preferred_element_type=F32))
            extdc[0:t, hs] = dc
            dcb_ref[:, hs] += jnp.sum(dc, axis=0, keepdims=True)
            for kk in range(CONV_WIDTH):
                off = CONV_HALO - (CONV_WIDTH - 1) + kk
                dcw_ref[kk:kk + 1, hs] += jnp.sum(dc * extl[off:off + t, hs], axis=0, keepdims=True)
        du_lru = cw_ref[0:1, :] * extdc[CONV_WIDTH - 1:CONV_WIDTH - 1 + t, :]
        for kk in range(1, CONV_WIDTH):
            off = CONV_WIDTH - 1 - kk
            du_lru = du_lru + cw_ref[kk:kk + 1, :] * extdc[off:off + t, :]
        dproj_ref[:, p:2 * p] = du_lru.astype(BF16)
        extdc[t:t + CONV_HALO, :] = extdc[0:CONV_HALO, :]

        extp[0:POOL_HALO, :] = jnp.where(seq_start, 0.0, upp_ref[...])
        extp[POOL_HALO:POOL_HALO + t, :] = up_ref[...]
        for g, w in enumerate(POOL_WINDOWS):
            cs = slice(g * cg, (g + 1) * cg)
            cnt = jnp.minimum(t_idx + 1, w).astype(F32)
            mixed = (_window_sum(extp, POOL_HALO, t, cs, w, -1) / cnt - up_ref[:, cs]).astype(BF16)
            pre = jnp.dot(mixed, wp_ref[g], preferred_element_type=F32) + bp_ref[:, cs]
            dyp = dyp_ref[:, cs]
            dps_ref[:, cs] += jnp.sum(dyp * pre, axis=0, keepdims=True)
            dpre = dyp * ps_ref[:, cs]
            dpreb = dpre.astype(BF16)
            dbp_ref[:, cs] += jnp.sum(dpre, axis=0, keepdims=True)
            dwp_ref[g] += lax.dot_general(mixed, dpreb, TN_DIMS, preferred_element_type=F32)
            dmixed = lax.dot_general(dpreb, wp_ref[g], NT_DIMS, preferred_element_type=F32)
            extg[0:t, cs] = dmixed / cnt
            dproj_ref[:, cs] = (_window_sum(extg, 0, t, cs, w, 1) - dmixed).astype(BF16)
        extg[t:t + POOL_HALO, :] = extg[0:POOL_HALO, :]

    def col(j):
        return pl.BlockSpec((t, p), lambda i: (nt - 1 - i, j))

    def prev(rows, j):
        per = t // rows
        return pl.BlockSpec((rows, p), lambda i: (jnp.maximum((nt - 1 - i) * per - 1, 0), j))

    def whole(a):
        nd = a.ndim
        return pl.BlockSpec(a.shape, lambda i: (0,) * nd)

    consts = (wp, bp, ps, cw, cb, wa, ba, wx, bx, lam)
    grads = (wp, bp, ps, cw, cb, wa, ba, wx, bx, lam)
    tile = pltpu.VMEM((t, p), F32)
    return _call(
        body, grid=(nt,),
        in_specs=[col(0), col(1), col(0), col(1), col(2), prev(POOL_HALO, 0), prev(CONV_HALO, 1), col(0),
                  prev(SUBLANES, 0)] + [whole(a) for a in consts],
        out_specs=[pl.BlockSpec((t, 3 * p), lambda i: (nt - 1 - i, 0))] + [whole(a) for a in grads],
        out_shape=[jax.ShapeDtypeStruct((s, 3 * p), BF16)] + [jax.ShapeDtypeStruct(a.shape, F32) for a in grads],
        scratch_shapes=[pltpu.VMEM((t + POOL_HALO, p), F32), pltpu.VMEM((t + POOL_HALO, p), F32),
                        pltpu.VMEM((t + CONV_HALO, p), F32), pltpu.VMEM((t + CONV_HALO, p), F32),
                        pltpu.VMEM((t + SUBLANES, p), F32), pltpu.VMEM((SUBLANES, p), F32),
                        tile, tile, tile, tile, tile, tile],
        semantics=("arbitrary",), name=name,
        args=(dycat, dycat, proj, proj, proj, proj, proj, hsave, hsave, *consts), jobs=jobs)


def _pair_add(parts, got, core, name):
    n, r, c = got.shape
    tr = _tile(TILES["add"], r)

    def body(core_ref, a_ref, b_ref, o_ref):
        del core_ref
        o_ref[...] = (a_ref[...].astype(F32) + b_ref[...].astype(F32)).astype(o_ref.dtype)

    blk = pl.BlockSpec((None, tr, c), lambda k, i, core_ref: (k, i, 0))
    mine = pl.BlockSpec((None, tr, c), lambda k, i, core_ref: (2 * k + core_ref[0], i, 0))
    return _call(body, grid=(n, r // tr), in_specs=[mine, blk], out_specs=[blk],
                 out_shape=[jax.ShapeDtypeStruct(got.shape, got.dtype)], semantics=("parallel", "parallel"),
                 name=name, args=(parts, got), index=core)[0]


def _sum_parts(parts, name):
    n, r, c = parts.shape
    tr = _tile(TILES["adam"], r)

    def body(p_ref, o_ref):
        acc = p_ref[0].astype(F32)
        for d in range(1, n):
            acc = acc + p_ref[d].astype(F32)
        o_ref[...] = acc

    return _call(
        body, grid=(r // tr,), in_specs=[pl.BlockSpec((n, tr, c), lambda i: (0, i, 0))],
        out_specs=[pl.BlockSpec((tr, c), lambda i: (i, 0))], out_shape=[jax.ShapeDtypeStruct((r, c), F32)],
        semantics=("parallel",), name=name, args=(parts,))[0]


def _adamw(w, m, v, parts, name, jobs=(), own=None, chip=None):
    r, c = w.shape
    n = parts.shape[0]
    tr = _tile(TILES["adam"], r)

    def body(*refs):
        if own is not None:
            refs = refs[1:]
            own_ref, refs = refs[3], refs[:3] + refs[4:]
        w_ref, m_ref, v_ref, p_ref, g_ref, d_ref, nm_ref, nv_ref = refs
        g = p_ref[0].astype(F32)
        if own is not None:
            g = own_ref[...].astype(F32) + g
        for d in range(1, n):
            g = g + p_ref[d].astype(F32)
        nm = ADAM_B1 * m_ref[...] + (1.0 - ADAM_B1) * g
        nv = ADAM_B2 * v_ref[...] + (1.0 - ADAM_B2) * (g * g)
        m_hat = nm / (1.0 - ADAM_B1 ** ADAM_STEP)
        v_hat = nv / (1.0 - ADAM_B2 ** ADAM_STEP)
        g_ref[...] = g
        d_ref[...] = -ADAM_LR * (m_hat / (jnp.sqrt(v_hat) + ADAM_EPS) + ADAM_WD * w_ref[...])
        nm_ref[...] = nm
        nv_ref[...] = nv

    row = pl.BlockSpec((tr, c), lambda i, *_: (i, 0))
    in_specs, args = [row, row, row], [w, m, v]
    if own is not None:
        in_specs.append(pl.BlockSpec((None, tr, c), lambda i, chip_ref: (chip_ref[0], i, 0)))
        args.append(own)
    in_specs.append(pl.BlockSpec((n, tr, c), lambda i, *_: (0, i, 0)))
    args.append(parts)
    return _call(
        body, grid=(r // tr,), in_specs=in_specs, out_specs=[row] * 4, out_shape=[jax.ShapeDtypeStruct((r, c), F32)] * 4,
        semantics=("parallel",), name=name, args=args, jobs=jobs, index=chip if own is not None else None)


SMALL_ORDER = ("w_a", "w_x", "conv_w", "b_pool", "conv_b", "b_a", "b_x", "lru_lambda", "pool_scale",
               "ln1_g", "ln1_b", "ln2_g", "ln2_b", "ln3_g", "ln3_b")


def _pack_rows(a, p):
    flat = a.reshape(-1, p)
    pad = (-flat.shape[0]) % SUBLANES
    return jnp.pad(flat, ((0, pad), (0, 0))) if pad else flat


def kernel(x, mem, w_in, conv_w, conv_b, w_a, b_a, w_x, b_x, lru_lambda, w_pool, b_pool, pool_scale, w_out, ln1_g, ln1_b, w_q, w_k, w_v, w_o, ln2_g, ln2_b, w_ff1, w_ff2, ln3_g, ln3_b, loss_target, m_w_in, m_conv_w, m_conv_b, m_w_a, m_b_a, m_w_x, m_b_x, m_lru_lambda, m_w_pool, m_b_pool, m_pool_scale, m_w_out, m_ln1_g, m_ln1_b, m_w_q, m_w_k, m_w_v, m_w_o, m_ln2_g, m_ln2_b, m_w_ff1, m_w_ff2, m_ln3_g, m_ln3_b, v_w_in, v_conv_w, v_conv_b, v_w_a, v_b_a, v_w_x, v_b_x, v_lru_lambda, v_w_pool, v_b_pool, v_pool_scale, v_w_out, v_ln1_g, v_ln1_b, v_w_q, v_w_k, v_w_v, v_w_o, v_ln2_g, v_ln2_b, v_w_ff1, v_w_ff2, v_ln3_g, v_ln3_b):
    names = ("w_in", "conv_w", "conv_b", "w_a", "b_a", "w_x", "b_x", "lru_lambda", "w_pool", "b_pool", "pool_scale",
             "w_out", "ln1_g", "ln1_b", "w_q", "w_k", "w_v", "w_o", "ln2_g", "ln2_b", "w_ff1", "w_ff2", "ln3_g", "ln3_b")
    w_loc = dict(zip(names, (w_in, conv_w, conv_b, w_a, b_a, w_x, b_x, lru_lambda, w_pool, b_pool, pool_scale,
                             w_out, ln1_g, ln1_b, w_q, w_k, w_v, w_o, ln2_g, ln2_b, w_ff1, w_ff2, ln3_g, ln3_b)))
    m_loc = dict(zip(names, (m_w_in, m_conv_w, m_conv_b, m_w_a, m_b_a, m_w_x, m_b_x, m_lru_lambda, m_w_pool, m_b_pool,
                             m_pool_scale, m_w_out, m_ln1_g, m_ln1_b, m_w_q, m_w_k, m_w_v, m_w_o, m_ln2_g, m_ln2_b,
                             m_w_ff1, m_w_ff2, m_ln3_g, m_ln3_b)))
    v_loc = dict(zip(names, (v_w_in, v_conv_w, v_conv_b, v_w_a, v_b_a, v_w_x, v_b_x, v_lru_lambda, v_w_pool, v_b_pool,
                             v_pool_scale, v_w_out, v_ln1_g, v_ln1_b, v_w_q, v_w_k, v_w_v, v_w_o, v_ln2_g, v_ln2_b,
                             v_w_ff1, v_w_ff2, v_ln3_g, v_ln3_b)))
    s, d = x.shape[1], x.shape[2]
    p = conv_b.shape[1]
    cg = p // N_POOL_GROUPS
    hd = p // LRU_HEADS
    me = 4 * lax.axis_index("x") + 2 * lax.axis_index("y") + lax.axis_index("c")

    xs, mems, tgt = x[0], mem[0], loss_target[0]
    memb = mems.astype(BF16)

    gathers = {n: _Job("gather", w_loc[n][0].astype(BF16))
               for n in ("w_in", "w_out", "w_q", "w_k", "w_v", "w_o", "w_ff2", "w_pool")}
    ff1_shard = w_ff1[0].astype(BF16)
    ff1_rows = ff1_shard.shape[0] // FF1_PIECES

    def ff1_piece(i, earlier=None):
        return _Job("gather", ff1_shard[i * ff1_rows:(i + 1) * ff1_rows], window=(i * ff1_rows, ff1_shard.shape[0]),
                    into=None if earlier is None else earlier.out)
    tiny = jnp.concatenate([_pack_rows(conv_w[0], p // N_DEV),
                            _pack_rows(jnp.pad(b_pool[0], ((0, 0), (0, p // N_DEV - cg // N_DEV))), p // N_DEV)], axis=0)
    gathers["tiny"] = _Job("gather", tiny)

    def gathered(n):
        full = gathers[n].out
        if n == "w_in":
            return jnp.transpose(full, (1, 0, 2)).reshape(1, full.shape[1], -1)
        return full.reshape(1, -1, full.shape[-1])

    W = {"conv_b": conv_b, "b_a": b_a.reshape(1, p), "b_x": b_x.reshape(1, p), "lru_lambda": lru_lambda,
         "pool_scale": pool_scale, "w_a": w_a[0].astype(BF16), "w_x": w_x[0].astype(BF16)}
    for n in ("ln1_g", "ln1_b", "ln2_g", "ln2_b", "ln3_g", "ln3_b"):
        W[n] = w_loc[n]

    out_g, out_d, out_m, out_v = {}, {}, {}, {}
    pairs, quads, sums = {}, {}, {}
    core =lax.axis_index("c").astype(jnp.int32).reshape(1)
    chip = (2 * lax.axis_index("x") + lax.axis_index("y")).astype(jnp.int32).reshape(1)

    def pair(n, partial):
        pairs[n] = _Job("pair", partial.reshape(N_DEV, -1, partial.shape[-1]))
        return pairs[n]

    def quad(n, lo=0, hi=1, of=1):
        if lo == 0:
            sums[n] = _pair_add(pairs[n].src, pairs[n].out, core, "add_" + n)
        rows = sums[n].shape[1] // of
        quads[n] = _Job("quad", sums[n], window=None if (lo, hi) == (0, of) else (lo * rows, (hi - lo) * rows),
                        into=None if lo == 0 else quads[n].out)
        return quads[n]

    def update(n, parts, jobs=(), own=None):
        shp = w_loc[n].shape
        rows = parts.shape[1]
        w2, m2, v2 = (a.reshape(rows, -1) for a in (w_loc[n], m_loc[n], v_loc[n]))
        res = _adamw(w2, m2, v2, parts.reshape(parts.shape[0], rows, -1), "adamw_" + n, jobs=jobs, own=own, chip=chip)
        out_g[n], out_d[n], out_m[n], out_v[n] = (r.reshape(shp) for r in res)

    assert FF1_PIECES == 4
    xb = _to_bf16(xs, "cast_x", jobs=[gathers["w_in"], gathers["tiny"], gathers["w_pool"]])
    W["w_pool"] = jnp.transpose(gathers["w_pool"].out, (1, 0, 2, 3)).reshape(N_POOL_GROUPS, cg, cg)
    cwb = gathers["tiny"].out
    W["conv_w"] = jnp.transpose(cwb[:, :CONV_WIDTH, :], (1, 0, 2)).reshape(CONV_WIDTH, p)
    W["b_pool"] = jnp.transpose(cwb[:, SUBLANES:SUBLANES + N_POOL_GROUPS, :cg // N_DEV], (1, 0, 2)).reshape(1, p)
    mixer_consts = (W["w_pool"], W["b_pool"], W["pool_scale"], W["conv_w"], W["conv_b"], W["w_a"], W["b_a"],
                    W["w_x"], W["b_x"], W["lru_lambda"])

    w_in_full = gathered("w_in")
    piece = ff1_piece(FF1_PIECES - 1)
    (proj,) = _mm_nn(xb, w_in_full, [F32], "fwd_proj", jobs=[gathers["w_out"], piece])
    ycat, hsave = _mixer_fwd(proj, *mixer_consts, "fwd_mixer", jobs=[gathers["w_q"], gathers["w_k"]])
    (y1,) = _mm_nn(ycat, gathered("w_out"), [F32], "fwd_out", jobs=[gathers["w_v"]])
    x1, x1b, xhat1, rstd1 = _ln_fwd(xs, y1, W["ln1_g"], W["ln1_b"], "fwd_ln1", jobs=[gathers["w_o"]])
    piece = ff1_piece(0, piece)
    (q,) = _mm_nn(x1b, gathered("w_q"), [BF16], "fwd_q", jobs=[piece])
    (k,) = _mm_nn(memb, gathered("w_k"), [BF16], "fwd_k")
    (v,) = _mm_nn(memb, gathered("w_v"), [BF16], "fwd_v")
    o = _attn_fwd(q, k, v, "fwd_attn")
    piece = ff1_piece(1, piece)
    (y2,) = _mm_nn(o, gathered("w_o"), [F32], "fwd_o", jobs=[piece])
    piece = ff1_piece(2, piece)
    x2, x2b, xhat2, rstd2 = _ln_fwd(x1, y2, W["ln2_g"], W["ln2_b"], "fwd_ln2", jobs=[piece])
    w_ff1_full = piece.out

    def relu_sq(acc):
        r = jnp.maximum(acc, 0.0)
        return r, r * r

    rb, act = _mm_nn(x2b, w_ff1_full, [BF16, BF16], "fwd_ff1", epilogue=relu_sq, jobs=[gathers["w_ff2"]])
    (y3,) = _mm_nn(act, gathered("w_ff2"), [F32], "fwd_ff2")
    loss_rows, dz3, dz3b, dg3, db3 = _ln_loss(x2, y3, W["ln3_g"], W["ln3_b"], tgt, "ln3_loss")
    loss = lax.psum(loss_rows[0, 0], MESH_AXES)

    small = {"ln3_g": dg3, "ln3_b": db3}

    def add_residual(acc, e):
        return (acc + ALPHA * e,)

    dw_ff2 = _mm_tn(act, dz3b, 1, BF16, "bwd_dw_ff2")
    (dhid,) = _mm_nt(dz3b, gathered("w_ff2"), [BF16], "bwd_dact", extras=(rb,), jobs=[pair("w_ff2", dw_ff2)],
                     epilogue=lambda acc, r: (acc * (2.0 * r.astype(F32)),))
    dw_ff1 = _mm_tn(x2b, dhid, N_DEV, BF16, "bwd_dw_ff1", jobs=[quad("w_ff2", 0, 1, 2)])
    (dx2,) = _mm_nt(dhid, w_ff1_full, [F32], "bwd_dx2", epilogue=add_residual, extras=(dz3,), tk=TILES["tk"] // 2,
                    jobs=[pair("w_ff1", dw_ff1), quad("w_ff2", 1, 2, 2)])
    dz2, dz2b, small["ln2_g"], small["ln2_b"] = _ln_bwd(dx2, xhat2, rstd2, W["ln2_g"], "bwd_ln2")

    dw_o = _mm_tn(o, dz2b, 1, BF16, "bwd_dw_o")
    (do,) = _mm_nt(dz2b, gathered("w_o"), [BF16], "bwd_do", jobs=[pair("w_o", dw_o)])
    dq, dk, dv = _attn_bwd(q, k, v, do, "bwd_attn", jobs=[quad("w_o")])
    dw_q = _mm_tn(x1b, dq, 1, BF16, "bwd_dw_q")
    dw_k = _mm_tn(memb, dk.astype(BF16), 1, BF16, "bwd_dw_k")
    dw_v = _mm_tn(memb, dv.astype(BF16), 1, BF16, "bwd_dw_v")
    (dx1,) = _mm_nt(dq, gathered("w_q"), [F32], "bwd_dx1", epilogue=add_residual, extras=(dz2,),
                    jobs=[pair("w_q", dw_q), pair("w_k", dw_k), pair("w_v", dw_v), quad("w_ff1", 0, 2, 8)])
    dz1, dz1b, small["ln1_g"], small["ln1_b"] = _ln_bwd(dx1, xhat1, rstd1, W["ln1_g"], "bwd_ln1", jobs=[quad("w_q")])

    dw_out = _mm_tn(ycat, dz1b, 1, BF16, "bwd_dw_out")
    (dycat,) = _mm_nt(dz1b, gathered("w_out"), [F32], "bwd_dycat", jobs=[pair("w_out", dw_out), quad("w_ff1", 2, 3, 8)])
    (dproj, dwp, small["b_pool"], small["pool_scale"], small["conv_w"], small["conv_b"], small["w_a"], small["b_a"],
     small["w_x"], small["b_x"], small["lru_lambda"]) = _mixer_bwd(
        dycat, proj, hsave, *mixer_consts, "bwd_mixer", jobs=[quad("w_ff1", 3, 7, 8), quad("w_k"), quad("w_v")])
    dw_pool = jnp.transpose(dwp.astype(BF16).reshape(N_POOL_GROUPS, N_DEV, cg // N_DEV, cg), (1, 0, 2, 3))
    pack = jnp.concatenate([_pack_rows(small[n], p) for n in SMALL_ORDER], axis=0)
    small_gather = _Job("gather", pack)
    dw_in = _mm_tn(xb, dproj, 1, BF16, "bwd_dw_in", jobs=[quad("w_out"), pair("w_pool", dw_pool), small_gather])
    dw_in = jnp.transpose(dw_in.reshape(dw_in.shape[1], N_DEV, -1), (1, 0, 2))
    (gx_lo,) = _mm_nt(dproj, w_in_full, [F32], "bwd_dx_lo", epilogue=add_residual, extras=(dz1,), part=(0, 2),
                      jobs=[pair("w_in", dw_in), quad("w_pool"), quad("w_ff1", 7, 8, 8)])
    (gx_hi,) = _mm_nt(dproj, w_in_full, [F32], "bwd_dx_hi", epilogue=add_residual, extras=(dz1,), part=(1, 2),
                      jobs=[quad("w_in", 0, 1, 2)])
    grad_x = jnp.concatenate([gx_lo, gx_hi], axis=0)

    update("w_ff2", quads["w_ff2"].out, jobs=[quad("w_in", 1, 2, 2)], own=quads["w_ff2"].src)
    for n in ("w_ff1", "w_o", "w_q", "w_k", "w_v", "w_out", "w_pool", "w_in"):
        update(n, quads[n].out, own=quads[n].src)

    total = _sum_parts(small_gather.out, "sum_small")
    row = 0
    for n in SMALL_ORDER:
        size = small[n].size
        nrows = size // p
        g_full = total[row:row + nrows].reshape(small[n].shape)
        row += nrows + (-nrows) % SUBLANES
        if n == "conv_w":
            g_loc = lax.dynamic_slice_in_dim(g_full, me * (p // N_DEV), p // N_DEV, axis=1)
        elif n == "b_pool":
            g_loc = lax.dynamic_slice_in_dim(g_full.reshape(N_POOL_GROUPS, cg), me * (cg // N_DEV), cg // N_DEV, axis=1)
        else:
            g_loc = g_full
        rows = g_loc.shape[0] if n not in ("w_a", "w_x") else LRU_HEADS * hd
        update(n, g_loc.reshape(1, rows, -1))

    order = names
    return (loss, grad_x[None], *[out_g[n] for n in order], *[out_d[n] for n in order],
            *[out_m[n] for n in order], *[out_v[n] for n in order])
```

```python
import functools

import jax
import jax.numpy as jnp
from jax import lax
from jax.experimental import pallas as pl
from jax.experimental.pallas import tpu as pltpu

F32 = jnp.float32
BF16 = jnp.bfloat16

N_DEV = 8
MESH_AXES = ("x", "y", "c")
POOL_WINDOWS = (2, 4, 8, 16)
N_POOL_GROUPS = len(POOL_WINDOWS)
POOL_HALO = 16
CONV_WIDTH = 4
CONV_HALO = 8
FF1_PIECES = 4
LRU_HEADS = 8
LRU_C = 8.0
XATTN_HEADS = 4
LN_EPS = 1e-5
ALPHA = 2.0 ** 0.25
ADAM_LR = 0.001
ADAM_B1 = 0.9
ADAM_B2 = 0.999
ADAM_EPS = 1e-08
ADAM_WD = 0.01
ADAM_STEP = 10
SUBLANES = 8
VMEM_LIMIT = 56 * 1024 * 1024

NT_DIMS = (((1,), (1,)), ((), ()))
TN_DIMS = (((0,), (0,)), ((), ()))


def _params(*sem):
    return pltpu.CompilerParams(dimension_semantics=sem, vmem_limit_bytes=VMEM_LIMIT)


def _place():
    return lax.axis_index("x"), lax.axis_index("y"), lax.axis_index("c")


def _remote(src, dst, send_sem, recv_sem, to):
    return pltpu.make_async_remote_copy(src_ref=src, dst_ref=dst, send_sem=send_sem, recv_sem=recv_sem,
                                        device_id=to, device_id_type=pl.DeviceIdType.MESH)


class _Job:
    def __init__(self, kind, src, window=None, into=None):
        self.kind, self.src, self.out, self.window, self.into = kind, src, None, window, into

    def out_shape(self):
        s = self.src.shape
        if self.kind == "gather" and self.window is not None:
            s = (self.window[1],) + s[1:]
        shape = {"gather": (N_DEV,) + s, "pair": (4,) + s[1:], "quad": (3,) + s[1:]}[self.kind]
        return jax.ShapeDtypeStruct(shape, self.src.dtype)

    def scratch(self):
        n = {"gather": 7, "pair": 4, "quad": 3}[self.kind]
        sems = [pltpu.SemaphoreType.DMA((n,)), pltpu.SemaphoreType.DMA((n,))]
        if self.kind == "gather":
            sems += [pltpu.SemaphoreType.DMA((2,)), pltpu.VMEM(self.src.shape, self.src.dtype)]
        return sems

    def ops(self, src, out, *scratch):
        if self.kind == "gather":
            return _gather_ops(src, out, *scratch, first_row=None if self.window is None else self.window[0])
        if self.kind == "quad":
            return _quad_ops(src, out, *scratch, rows=self.window)
        return _pair_ops(src, out, *scratch)


def _gather_ops(x_ref, out_ref, send_sems, recv_sems, local_sems, bounce, first_row=None):
    x, y, c = _place()
    me, sibling = (x, y, c), (x, y, 1 - c)
    chips = [(1 - x, y), (x, 1 - y), (1 - x, 1 - y)]

    def slot(px, py, pc):
        block = out_ref.at[4 * px + 2 * py + pc]
        return block if first_row is None else block.at[pl.ds(first_row, x_ref.shape[0])]

    def copy(k, block, to, src=None):
        return _remote(slot(*block) if src is None else src, slot(*block), send_sems.at[k], recv_sems.at[k], to)

    mine_in = pltpu.make_async_copy(x_ref, bounce, local_sems.at[0])
    mine_out = pltpu.make_async_copy(bounce, slot(*me), local_sems.at[1])
    first = [copy(0, me, sibling, src=x_ref)] + [copy(1 + j, me, (*chip, c), src=x_ref) for j, chip in enumerate(chips)]
    passed = [copy(4 + j, (*chip, c), sibling) for j, chip in enumerate(chips)]

    def start():
        mine_in.start()
        for cp in first:
            cp.start()

    def mid():
        mine_in.wait()
        mine_out.start()
        for j, chip in enumerate(chips):
            copy(1 + j, (*chip, c), me).wait_recv()
            passed[j].start()

    def finish():
        copy(0, sibling, me).wait_recv()
        for j, chip in enumerate(chips):
            copy(4 + j, (*chip, 1 - c), me).wait_recv()
        for cp in first + passed:
            cp.wait_send()
        mine_out.wait()

    return start, mid, finish


def _pair_ops(p_ref, got_ref, send_sems, recv_sems):
    x, y, c = _place()
    give = [_remote(p_ref.at[2 * k + 1 - c], got_ref.at[k], send_sems.at[k], recv_sems.at[k], (x, y, 1 - c))
            for k in range(4)]

    def start():
        for cp in give:
            cp.start()

    def finish():
        for cp in give:
            cp.wait_recv()
        for cp in give:
            cp.wait_send()

    return start, None, finish


def _quad_ops(q_ref, out_ref, send_sems, recv_sems, rows=None):
    x, y, c = _place()

    def part(block):
        return block if rows is None else block.at[pl.ds(rows[0], rows[1])]

    copies = []
    for rel in range(1, 4):
        px = 1 - x if rel & 2 else x
        py = 1 - y if rel & 1 else y
        copies.append(_remote(part(q_ref.at[2 * px + py]), part(out_ref.at[rel - 1]), send_sems.at[rel - 1],
                              recv_sems.at[rel - 1], (px, py, c)))

    def start():
        for cp in copies:
            cp.start()

    def finish():
        for cp in copies:
            cp.wait_recv()
        for cp in copies:
            cp.wait_send()

    return start, None, finish


def _call(body, *, grid, in_specs, out_specs, out_shape, scratch_shapes=(), semantics, name, args, jobs=(), index=None):
    in_specs, out_specs, out_shape = list(in_specs), list(out_specs), list(out_shape)
    scratch_shapes, jobs = list(scratch_shapes), list(jobs)
    n_in, n_out, n_scr, n_job = len(in_specs), len(out_specs), len(scratch_shapes), len(jobs)
    n_idx = 0 if index is None else 1
    job_scratch = [j.scratch() for j in jobs]
    n_steps = functools.reduce(lambda a, b: a * b, grid, 1)
    early = n_steps - 1 - max(1, n_steps // 8) if n_steps >= 4 else None

    intos = [(k, j.into) for k, j in enumerate(jobs) if j.into is not None]

    def hosted(*refs):
        idx, refs = refs[:n_idx], refs[n_idx:]
        ins, jin = refs[:n_in], refs[n_in:n_in + n_job]
        o0 = n_in + n_job + len(intos)
        outs, jout = refs[o0:o0 + n_out], refs[o0 + n_out:o0 + n_out + n_job]
        s0 = o0 + n_out + n_job
        scr, jscr = refs[s0:s0 + n_scr], refs[s0 + n_scr:]
        ops, at = [], 0
        for k, j in enumerate(jobs):
            ops.append(j.ops(jin[k], jout[k], *jscr[at:at + len(job_scratch[k])]))
            at += len(job_scratch[k])
        step = functools.reduce(lambda acc, a: acc * grid[a] + pl.program_id(a), range(len(grid)), 0)
        mids = [mid for _, mid, _ in ops if mid is not None]

        @pl.when(step == 0)
        def _():
            for start, _, _ in ops:
                start()

        if mids and early is not None:
            @pl.when(step == early)
            def _():
                for mid in mids:
                    mid()

        body(*idx, *ins, *outs, *scr)

        @pl.when(step == n_steps - 1)
        def _():
            if early is None:
                for mid in mids:
                    mid()
            for _, _, finish in ops:
                finish()

    hbm = pl.BlockSpec(memory_space=pl.ANY)
    spec = pltpu.PrefetchScalarGridSpec(
        num_scalar_prefetch=n_idx, grid=grid, in_specs=in_specs + [hbm] * (n_job + len(intos)),
        out_specs=out_specs + [hbm] * n_job, scratch_shapes=scratch_shapes + [s for js in job_scratch for s in js])
    aliases = {n_idx + n_in + n_job + q: n_out + k for q, (k, _) in enumerate(intos)}
    res = pl.pallas_call(
        hosted if jobs else body, grid_spec=spec, out_shape=out_shape + [j.out_shape() for j in jobs],
        input_output_aliases=aliases,
        compiler_params=_params(*(["arbitrary"] * len(grid) if jobs else semantics)), name=name,
    )(*([] if index is None else [index]), *args, *[j.src for j in jobs], *[buf for _, buf in intos])
    for j, o in zip(jobs, res[n_out:]):
        j.out = o
    return res[:n_out]


def _to_bf16(a, name, jobs=()):
    r, c = a.shape
    tr = _tile(TILES["row"], r)

    def body(a_ref, o_ref):
        o_ref[...] = a_ref[...].astype(BF16)

    row = pl.BlockSpec((tr, c), lambda i: (i, 0))
    return _call(body, grid=(r // tr,), in_specs=[row], out_specs=[row], out_shape=[jax.ShapeDtypeStruct((r, c), BF16)],
                 semantics=("parallel",), name=name, args=(a,), jobs=jobs)[0]


TILES = dict(tm=1024, tn=1024, tk=2048, row=256, attn=512, mixer=256, adam=256, add=1024)


def _tile(pref, n):
    for t in range(min(pref, n), 0, -1):
        if n % t == 0 and (t % SUBLANES == 0 or t == n):
            return t
    return n


def _accumulate(acc, step, n_steps, product, write):
    if n_steps == 1:
        write(product())
        return

    @pl.when(step == 0)
    def _():
        acc[...] = product()

    @pl.when(jnp.logical_and(step > 0, step < n_steps - 1))
    def _():
        acc[...] += product()

    @pl.when(step == n_steps - 1)
    def _():
        write(acc[...] + product())


def _acc_scratch(n_steps, tm, tn):
    return [] if n_steps == 1 else [pltpu.VMEM((tm, tn), F32)]


def _mm_nn(a, b3, out_dtypes, name, *, tm=None, tn=None, tk=None, epilogue=None, extras=(), jobs=()):
    m, k = a.shape
    g, k2, ns = b3.shape
    assert k == k2
    n = g * ns
    tm, tn, tk = _tile(tm or TILES["tm"], m), _tile(tn or TILES["tn"], ns), _tile(tk or TILES["tk"], k)
    nb, nk = ns // tn, k // tk
    n_ex, n_out = len(extras), len(out_dtypes)

    def body(*refs):
        a_ref, b_ref = refs[:2]
        ex = refs[2:2 + n_ex]
        outs = refs[2 + n_ex:2 + n_ex + n_out]
        acc = refs[-1] if nk > 1 else None

        def write(r):
            res = epilogue(r, *[e[...] for e in ex]) if epilogue is not None else (r,)
            for o, v in zip(outs, res):
                o[...] = v.astype(o.dtype)

        _accumulate(acc, pl.program_id(2), nk,
                    lambda: jnp.dot(a_ref[...], b_ref[...], preferred_element_type=F32), write)

    tile_out = pl.BlockSpec((tm, tn), lambda i, j, kk: (i, j))
    return _call(
        body, grid=(m // tm, n // tn, nk),
        in_specs=[pl.BlockSpec((tm, tk), lambda i, j, kk: (i, kk)),
                  pl.BlockSpec((None, tk, tn), lambda i, j, kk: (j // nb, kk, j % nb))] + [tile_out] * n_ex,
        out_specs=[tile_out] * n_out,
        out_shape=[jax.ShapeDtypeStruct((m, n), d) for d in out_dtypes],
        scratch_shapes=_acc_scratch(nk, tm, tn),
        semantics=("parallel", "parallel", "arbitrary"), name=name, args=(a, b3, *extras), jobs=jobs)


def _mm_nt(a, b3, out_dtypes, name, *, tm=None, tn=None, tk=None, epilogue=None, extras=(), jobs=(), part=(0, 1)):
    m, n = a.shape
    g, k, ns = b3.shape
    assert n == g * ns
    tm, tn, tk = _tile(tm or TILES["tm"], m // part[1]), _tile(tn or TILES["tn"], k), _tile(tk or TILES["tk"], ns)
    nb, nc = ns // tk, n // tk
    n_ex, n_out = len(extras), len(out_dtypes)
    m_blocks = m // tm // part[1]
    first = part[0] * m_blocks

    def body(*refs):
        a_ref, b_ref = refs[:2]
        ex = refs[2:2 + n_ex]
        outs = refs[2 + n_ex:2 + n_ex + n_out]
        acc = refs[-1] if nc > 1 else None

        def write(r):
            res = epilogue(r, *[e[...] for e in ex]) if epilogue is not None else (r,)
            for o, v in zip(outs, res):
                o[...] = v.astype(o.dtype)

        _accumulate(acc, pl.program_id(2), nc,
                    lambda: lax.dot_general(a_ref[...], b_ref[...], NT_DIMS, preferred_element_type=F32), write)

    tile_out = pl.BlockSpec((tm, tn), lambda i, j, cc: (i, j))
    tile_ex = pl.BlockSpec((tm, tn), lambda i, j, cc: (first + i, j))
    return _call(
        body, grid=(m_blocks, k // tn, nc),
        in_specs=[pl.BlockSpec((tm, tk), lambda i, j, cc: (first + i, cc)),
                  pl.BlockSpec((None, tn, tk), lambda i, j, cc: (cc // nb, j, cc % nb))] + [tile_ex] * n_ex,
        out_specs=[tile_out] * n_out,
        out_shape=[jax.ShapeDtypeStruct((m_blocks * tm, k), d) for d in out_dtypes],
        scratch_shapes=_acc_scratch(nc, tm, tn),
        semantics=("parallel", "parallel", "arbitrary"), name=name, args=(a, b3, *extras), jobs=jobs)


def _mm_tn(a, b, g, out_dtype, name, *, tm=None, tn=None, tk=None, jobs=()):
    s, m = a.shape
    s2, n = b.shape
    assert s == s2 and n % g == 0
    ns = n // g
    tm, tn, tk = _tile(tm or TILES["tm"], m), _tile(tn or TILES["tn"], ns), _tile(tk or TILES["tk"], s)
    nb, nc = ns // tn, s // tk

    def body(a_ref, b_ref, o_ref, *scratch):
        def write(r):
            o_ref[...] = r.astype(o_ref.dtype)

        _accumulate(scratch[0] if nc > 1 else None, pl.program_id(2), nc,
                    lambda: lax.dot_general(a_ref[...], b_ref[...], TN_DIMS, preferred_element_type=F32), write)

    return _call(
        body, grid=(m // tm, n // tn, nc),
        in_specs=[pl.BlockSpec((tk, tm), lambda i, j, cc: (cc, i)),
                  pl.BlockSpec((tk, tn), lambda i, j, cc: (cc, j))],
        out_specs=[pl.BlockSpec((None, tm, tn), lambda i, j, cc: (j // nb, i, j % nb))],
        out_shape=[jax.ShapeDtypeStruct((g, m, ns), out_dtype)],
        scratch_shapes=_acc_scratch(nc, tm, tn),
        semantics=("parallel", "parallel", "arbitrary"), name=name, args=(a, b), jobs=jobs)[0]


def _ln_stats(z):
    mu = jnp.mean(z, axis=-1, keepdims=True)
    zc = z - mu
    var = jnp.mean(zc * zc, axis=-1, keepdims=True)
    rstd = lax.rsqrt(var + LN_EPS)
    return zc * rstd, rstd


def _ln_grad(dout, xhat, rstd, gain):
    dxhat = dout * gain
    m1 = jnp.mean(dxhat, axis=-1, keepdims=True)
    m2 = jnp.mean(dxhat * xhat, axis=-1, keepdims=True)
    return rstd * (dxhat - m1 - xhat * m2)


def _ln_fwd(xres, y, gain, bias, name, jobs=()):
    s, d = xres.shape
    tr = _tile(TILES["row"], s)

    def body(x_ref, y_ref, g_ref, b_ref, xn_ref, xnb_ref, xhat_ref, rstd_ref):
        xhat, rstd = _ln_stats(ALPHA * x_ref[...] + y_ref[...])
        out = xhat * g_ref[...] + b_ref[...]
        xn_ref[...] = out
        xnb_ref[...] = out.astype(BF16)
        xhat_ref[...] = xhat
        rstd_ref[...] = rstd

    row = pl.BlockSpec((tr, d), lambda i: (i, 0))
    vec = pl.BlockSpec((1, d), lambda i: (0, 0))
    return _call(
        body, grid=(s // tr,), in_specs=[row, row, vec, vec],
        out_specs=[row, row, row, pl.BlockSpec((tr, 1), lambda i: (i, 0))],
        out_shape=[jax.ShapeDtypeStruct((s, d), F32), jax.ShapeDtypeStruct((s, d), BF16),
                   jax.ShapeDtypeStruct((s, d), F32), jax.ShapeDtypeStruct((s, 1), F32)],
        semantics=("parallel",), name=name, args=(xres, y, gain, bias), jobs=jobs)


def _ln_bwd(dout, xhat, rstd, gain, name, jobs=()):
    s, d = dout.shape
    tr = _tile(TILES["row"], s)

    def body(d_ref, xhat_ref, rstd_ref, g_ref, dz_ref, dzb_ref, dg_ref, db_ref):
        @pl.when(pl.program_id(0) == 0)
        def _():
            dg_ref[...] = jnp.zeros_like(dg_ref)
            db_ref[...] = jnp.zeros_like(db_ref)

        dout_t, xhat_t = d_ref[...], xhat_ref[...]
        dz = _ln_grad(dout_t, xhat_t, rstd_ref[...], g_ref[...])
        dz_ref[...] = dz
        dzb_ref[...] = dz.astype(BF16)
        dg_ref[...] += jnp.sum(dout_t * xhat_t, axis=0, keepdims=True)
        db_ref[...] += jnp.sum(dout_t, axis=0, keepdims=True)

    row = pl.BlockSpec((tr, d), lambda i: (i, 0))
    vec = pl.BlockSpec((1, d), lambda i: (0, 0))
    return _call(
        body, grid=(s // tr,), in_specs=[row, row, pl.BlockSpec((tr, 1), lambda i: (i, 0)), vec],
        out_specs=[row, row, vec, vec],
        out_shape=[jax.ShapeDtypeStruct((s, d), F32), jax.ShapeDtypeStruct((s, d), BF16),
                   jax.ShapeDtypeStruct((1, d), F32), jax.ShapeDtypeStruct((1, d), F32)],
        semantics=("arbitrary",), name=name, args=(dout, xhat, rstd, gain), jobs=jobs)


def _ln_loss(xres, y, gain, bias, target, name, jobs=()):
    s, d = xres.shape
    tr = _tile(TILES["row"], s)

    def body(x_ref, y_ref, g_ref, b_ref, t_ref, loss_ref, dz_ref, dzb_ref, dg_ref, db_ref):
        @pl.when(pl.program_id(0) == 0)
        def _():
            loss_ref[...] = jnp.zeros_like(loss_ref)
            dg_ref[...] = jnp.zeros_like(dg_ref)
            db_ref[...] = jnp.zeros_like(db_ref)

        xhat, rstd = _ln_stats(ALPHA * x_ref[...] + y_ref[...])
        diff = xhat * g_ref[...] + b_ref[...] - t_ref[...]
        per_row = jnp.mean(diff * diff, axis=-1, keepdims=True)
        loss_ref[...] += 0.5 * jnp.sum(per_row, axis=0, keepdims=True)
        dout = diff * (1.0 / d)
        dz = _ln_grad(dout, xhat, rstd, g_ref[...])
        dz_ref[...] = dz
        dzb_ref[...] = dz.astype(BF16)
        dg_ref[...] += jnp.sum(dout * xhat, axis=0, keepdims=True)
        db_ref[...] += jnp.sum(dout, axis=0, keepdims=True)

    row = pl.BlockSpec((tr, d), lambda i: (i, 0))
    vec = pl.BlockSpec((1, d), lambda i: (0, 0))
    return _call(
        body, grid=(s // tr,), in_specs=[row, row, vec, vec, row],
        out_specs=[pl.BlockSpec((1, 128), lambda i: (0, 0)), row, row, vec, vec],
        out_shape=[jax.ShapeDtypeStruct((1, 128), F32), jax.ShapeDtypeStruct((s, d), F32),
                   jax.ShapeDtypeStruct((s, d), BF16), jax.ShapeDtypeStruct((1, d), F32),
                   jax.ShapeDtypeStruct((1, d), F32)],
        semantics=("arbitrary",), name=name, args=(xres, y, gain, bias, target), jobs=jobs)


def _softmax_rows(s):
    e = jnp.exp(s - jnp.max(s, axis=-1, keepdims=True))
    return e / jnp.sum(e, axis=-1, keepdims=True)


def _attn_fwd(q, k, v, name, jobs=()):
    s, d = q.shape
    m = k.shape[0]
    hd = d // XATTN_HEADS
    ts = _tile(TILES["attn"], s)
    scale = hd ** -0.5

    def body(q_ref, k_ref, v_ref, o_ref):
        for h in range(XATTN_HEADS):
            hs = slice(h * hd, (h + 1) * hd)
            sc = lax.dot_general(q_ref[:, hs], k_ref[:, hs], NT_DIMS, preferred_element_type=F32) * scale
            p = _softmax_rows(sc).astype(BF16)
            o_ref[:, hs] = jnp.dot(p, v_ref[:, hs], preferred_element_type=F32).astype(BF16)

    row = pl.BlockSpec((ts, d), lambda i: (i, 0))
    memb = pl.BlockSpec((m, d), lambda i: (0, 0))
    return _call(
        body, grid=(s // ts,), in_specs=[row, memb, memb], out_specs=[row],
        out_shape=[jax.ShapeDtypeStruct((s, d), BF16)],
        semantics=("parallel",), name=name, args=(q, k, v), jobs=jobs)[0]


def _attn_bwd(q, k, v, do, name, jobs=()):
    s, d = q.shape
    m = k.shape[0]
    hd = d // XATTN_HEADS
    ts = _tile(TILES["attn"], s)
    scale = hd ** -0.5

    def body(q_ref, k_ref, v_ref, do_ref, dq_ref, dk_ref, dv_ref):
        @pl.when(pl.program_id(0) == 0)
        def _():
            dk_ref[...] = jnp.zeros_like(dk_ref)
            dv_ref[...] = jnp.zeros_like(dv_ref)

        for h in range(XATTN_HEADS):
            hs = slice(h * hd, (h + 1) * hd)
            qh, kh, vh, doh = q_ref[:, hs], k_ref[:, hs], v_ref[:, hs], do_ref[:, hs]
            sc = lax.dot_general(qh, kh, NT_DIMS, preferred_element_type=F32) * scale
            p = _softmax_rows(sc)
            pb = p.astype(BF16)
            dp = lax.dot_general(doh, vh, NT_DIMS, preferred_element_type=F32)
            ds = (p * (dp - jnp.sum(dp * p, axis=-1, keepdims=True)) * scale).astype(BF16)
            dq_ref[:, hs] = jnp.dot(ds, kh, preferred_element_type=F32).astype(BF16)
            dk_ref[:, hs] += lax.dot_general(ds, qh, TN_DIMS, preferred_element_type=F32)
            dv_ref[:, hs] += lax.dot_general(pb, doh, TN_DIMS, preferred_element_type=F32)

    row = pl.BlockSpec((ts, d), lambda i: (i, 0))
    memb = pl.BlockSpec((m, d), lambda i: (0, 0))
    return _call(
        body, grid=(s // ts,), in_specs=[row, memb, memb, row], out_specs=[row, memb, memb],
        out_shape=[jax.ShapeDtypeStruct((s, d), BF16), jax.ShapeDtypeStruct((m, d), F32),
                   jax.ShapeDtypeStruct((m, d), F32)],
        semantics=("arbitrary",), name=name, args=(q, k, v, do), jobs=jobs)


def _sigmoid(x):
    return 1.0 / (1.0 + jnp.exp(-x))


def _log1p(x):
    u = 1.0 + x
    return jnp.where(u == 1.0, x, jnp.log(u) * (x / jnp.where(u == 1.0, 1.0, u - 1.0)))


def _softplus(x):
    return jnp.maximum(x, 0.0) + _log1p(jnp.exp(-jnp.abs(x)))


def _expm1(x):
    series = x * (1.0 + x * 0.5 * (1.0 + x * (1.0 / 3.0) * (1.0 + x * 0.25 * (1.0 + x * 0.2 * (1.0 + x * (1.0 / 6.0))))))
    return jnp.where(jnp.abs(x) < 0.1, series, jnp.exp(x) - 1.0)


GELU_K = 0.7978845608028654
GELU_C = 0.044715


def _gelu(x):
    return 0.5 * x * (1.0 + jnp.tanh(GELU_K * (x + GELU_C * (x * x * x))))


def _gelu_grad(x):
    th = jnp.tanh(GELU_K * (x + GELU_C * (x * x * x)))
    return 0.5 * (1.0 + th) + 0.5 * x * (1.0 - th * th) * GELU_K * (1.0 + 3.0 * GELU_C * x * x)


def _window_sum(ext_ref, first, rows, cols, w, step):
    acc = ext_ref[first:first + rows, cols]
    for kk in range(1, w):
        acc = acc + ext_ref[first + step * kk:first + step * kk + rows, cols]
    return acc


def _lru_gates(c_s, wa_ref, ba_ref, wx_ref, bx_ref, lam_ref, t_idx, hd, r_s, i_s, a_s, mult_s):
    sp = _softplus(-lam_ref[...])
    for h in range(LRU_HEADS):
        hs = slice(h * hd, (h + 1) * hd)
        chb = c_s[:, hs].astype(BF16)
        r = _sigmoid(jnp.dot(chb, wa_ref[h], preferred_element_type=F32) + ba_ref[:, hs])
        ig = _sigmoid(jnp.dot(chb, wx_ref[h], preferred_element_type=F32) + bx_ref[:, hs])
        log_a = -LRU_C * r * sp[:, hs]
        mult = jnp.sqrt(-_expm1(2.0 * log_a))
        r_s[:, hs] = r
        i_s[:, hs] = ig
        a_s[:, hs] = jnp.exp(log_a)
        mult_s[:, hs] = jnp.where(t_idx == 0, 1.0, mult)


def _conv(ext_ref, cw_ref, cb_ref, rows):
    acc = cb_ref[...] + cw_ref[0:1, :] * ext_ref[CONV_HALO - 3:CONV_HALO - 3 + rows, :]
    for kk in range(1, CONV_WIDTH):
        off = CONV_HALO - (CONV_WIDTH - 1) + kk
        acc = acc + cw_ref[kk:kk + 1, :] * ext_ref[off:off + rows, :]
    return acc


def _mixer_fwd(proj, wp, bp, ps, cw, cb, wa, ba, wx, bx, lam, name, jobs=()):
    s, p3 = proj.shape
    p = p3 // 3
    cg, hd = p // N_POOL_GROUPS, p // LRU_HEADS
    t = _tile(TILES["mixer"], s)

    def body(up_ref, ul_ref, ug_ref, wp_ref, bp_ref, ps_ref, cw_ref, cb_ref, wa_ref, ba_ref, wx_ref, bx_ref,
             lam_ref, ycat_ref, h_ref, extp, extl, hc, c_s, r_s, i_s, a_s, b_s):
        i = pl.program_id(0)

        @pl.when(i == 0)
        def _():
            extp[0:POOL_HALO, :] = jnp.zeros((POOL_HALO, p), F32)
            extl[0:CONV_HALO, :] = jnp.zeros((CONV_HALO, p), F32)
            hc[...] = jnp.zeros_like(hc)

        t_idx = i * t + lax.broadcasted_iota(jnp.int32, (t, 1), 0)

        extp[POOL_HALO:POOL_HALO + t, :] = up_ref[...]
        for g, w in enumerate(POOL_WINDOWS):
            cs = slice(g * cg, (g + 1) * cg)
            cnt = jnp.minimum(t_idx + 1, w).astype(F32)
            mixed = _window_sum(extp, POOL_HALO, t, cs, w, -1) / cnt - up_ref[:, cs]
            pre = jnp.dot(mixed.astype(BF16), wp_ref[g], preferred_element_type=F32) + bp_ref[:, cs]
            ycat_ref[:, cs] = (pre * ps_ref[:, cs]).astype(BF16)
        extp[0:POOL_HALO, :] = extp[t:t + POOL_HALO, :]

        extl[CONV_HALO:CONV_HALO + t, :] = ul_ref[...]
        c_s[...] = _conv(extl, cw_ref, cb_ref, t)
        extl[0:CONV_HALO, :] = extl[t:t + CONV_HALO, :]
        _lru_gates(c_s, wa_ref, ba_ref, wx_ref, bx_ref, lam_ref, t_idx, hd, r_s, i_s, a_s, b_s)
        b_s[...] = b_s[...] * (i_s[...] * c_s[...])

        rows = lax.broadcasted_iota(jnp.int32, (SUBLANES, p), 0)

        def block(bi, h):
            r0 = pl.multiple_of(bi * SUBLANES, SUBLANES)
            at = a_s[pl.ds(r0, SUBLANES), :]
            bt = b_s[pl.ds(r0, SUBLANES), :]
            out = jnp.zeros((SUBLANES, p), F32)
            for j in range(SUBLANES):
                h = at[j:j + 1, :] * h + bt[j:j + 1, :]
                out = jnp.where(rows == j, h, out)
            h_ref[pl.ds(r0, SUBLANES), :] = out
            return h

        hc[0:1, :] = lax.fori_loop(0, t // SUBLANES, block, hc[0:1, :])
        ycat_ref[:, p:2 * p] = (h_ref[...] * _gelu(ug_ref[...])).astype(BF16)

    def col(j):
        return pl.BlockSpec((t, p), lambda i: (i, j))

    def whole(a):
        nd = a.ndim
        return pl.BlockSpec(a.shape, lambda i: (0,) * nd)

    consts = (wp, bp, ps, cw, cb, wa, ba, wx, bx, lam)
    tile = pltpu.VMEM((t, p), F32)
    return _call(
        body, grid=(s // t,), in_specs=[col(0), col(1), col(2)] + [whole(a) for a in consts],
        out_specs=[pl.BlockSpec((t, 2 * p), lambda i: (i, 0)), pl.BlockSpec((t, p), lambda i: (i, 0))],
        out_shape=[jax.ShapeDtypeStruct((s, 2 * p), BF16), jax.ShapeDtypeStruct((s, p), F32)],
        scratch_shapes=[pltpu.VMEM((t + POOL_HALO, p), F32), pltpu.VMEM((t + CONV_HALO, p), F32),
                        pltpu.VMEM((SUBLANES, p), F32), tile, tile, tile, tile, tile],
        semantics=("arbitrary",), name=name, args=(proj, proj, proj, *consts), jobs=jobs)


def _mixer_bwd(dycat, proj, hsave, wp, bp, ps, cw, cb, wa, ba, wx, bx, lam, name, jobs=()):
    s, p3 = proj.shape
    p = p3 // 3
    cg, hd = p // N_POOL_GROUPS, p // LRU_HEADS
    t = _tile(TILES["mixer"], s)
    nt = s // t

    def body(dyp_ref, dyl_ref, up_ref, ul_ref, ug_ref, upp_ref, ulp_ref, h_ref, hp_ref,
             wp_ref, bp_ref, ps_ref, cw_ref, cb_ref, wa_ref, ba_ref, wx_ref, bx_ref, lam_ref,
             dproj_ref, dwp_ref, dbp_ref, dps_ref, dcw_ref, dcb_ref, dwa_ref, dba_ref, dwx_ref, dbx_ref, dlam_ref,
             extp, extg, extl, extdc, exth, ghc, c_s, r_s, i_s, a_s, mult_s, gh_s):
        i = pl.program_id(0)
        ib = nt - 1 - i

        @pl.when(i == 0)
        def _():
            for ref in (dwp_ref, dbp_ref, dps_ref, dcw_ref, dcb_ref, dwa_ref, dba_ref, dwx_ref, dbx_ref, dlam_ref):
                ref[...] = jnp.zeros_like(ref)
            extg[t:t + POOL_HALO, :] = jnp.zeros((POOL_HALO, p), F32)
            extdc[t:t + CONV_HALO, :] = jnp.zeros((CONV_HALO, p), F32)
            ghc[...] = jnp.zeros_like(ghc)

        t_idx = ib * t + lax.broadcasted_iota(jnp.int32, (t, 1), 0)
        seq_start = ib == 0

        extl[0:CONV_HALO, :] = jnp.where(seq_start, 0.0, ulp_ref[...])
        extl[CONV_HALO:CONV_HALO + t, :] = ul_ref[...]
        c_s[...] = _conv(extl, cw_ref, cb_ref, t)
        _lru_gates(c_s, wa_ref, ba_ref, wx_ref, bx_ref, lam_ref, t_idx, hd, r_s, i_s, a_s, mult_s)
        exth[0:SUBLANES, :] = jnp.where(seq_start, 0.0, hp_ref[...])
        exth[SUBLANES:SUBLANES + t, :] = h_ref[...]

        ug = ug_ref[...]
        dyl = dyl_ref[...]
        dproj_ref[:, 2 * p:3 * p] = (dyl * h_ref[...] * _gelu_grad(ug)).astype(BF16)
        gh_s[...] = dyl * _gelu(ug)

        rows = lax.broadcasted_iota(jnp.int32, (SUBLANES, p), 0)
        nblk = t // SUBLANES

        def block(bi, carry):
            r0 = pl.multiple_of((nblk - 1 - bi) * SUBLANES, SUBLANES)
            at = a_s[pl.ds(r0, SUBLANES), :]
            dt = gh_s[pl.ds(r0, SUBLANES), :]
            out = jnp.zeros((SUBLANES, p), F32)
            for j in range(SUBLANES - 1, -1, -1):
                gh = dt[j:j + 1, :] + carry
                out = jnp.where(rows == j, gh, out)
                carry = at[j:j + 1, :] * gh
            gh_s[pl.ds(r0, SUBLANES), :] = out
            return carry

        ghc[0:1, :] = lax.fori_loop(0, nblk, block, ghc[0:1, :])

        sp = _softplus(-lam_ref[...])
        dsp_dlam = -_sigmoid(-lam_ref[...])
        for h in range(LRU_HEADS):
            hs = slice(h * hd, (h + 1) * hd)
            gh, a, mult, r, ig, c = gh_s[:, hs], a_s[:, hs], mult_s[:, hs], r_s[:, hs], i_s[:, hs], c_s[:, hs]
            hprev = exth[SUBLANES - 1:SUBLANES - 1 + t, hs]
            dmult = gh * (ig * c)
            dlog_a = a * gh * hprev + jnp.where(t_idx == 0, 0.0, -dmult * a * a / mult)
            dlam_ref[:, hs] += jnp.sum(dlog_a * r, axis=0, keepdims=True) * (-LRU_C) * dsp_dlam[:, hs]
            dpa = dlog_a * (-LRU_C * sp[:, hs]) * r * (1.0 - r)
            dpx = gh * mult * c * ig * (1.0 - ig)
            dpab, dpxb, chb = dpa.astype(BF16), dpx.astype(BF16), c.astype(BF16)
            dwa_ref[h] += lax.dot_general(chb, dpab, TN_DIMS, preferred_element_type=F32)
            dwx_ref[h] += lax.dot_general(chb, dpxb, TN_DIMS, preferred_element_type=F32)
            dba_ref[:, hs] += jnp.sum(dpa, axis=0, keepdims=True)
            dbx_ref[:, hs] += jnp.sum(dpx, axis=0, keepdims=True)
            dc = (gh * mult * ig
                  + lax.dot_general(dpab, wa_ref[h], NT_DIMS, preferred_element_type=F32)
                  + lax.dot_general(dpxb, wx_ref[h], NT_DIMS, preferred_element_type=F32))
            extdc[0:t, hs] = dc
            dcb_ref[:, hs] += jnp.sum(dc, axis=0, keepdims=True)
            for kk in range(CONV_WIDTH):
                off = CONV_HALO - (CONV_WIDTH - 1) + kk
                dcw_ref[kk:kk + 1, hs] += jnp.sum(dc * extl[off:off + t, hs], axis=0, keepdims=True)
        du_lru = cw_ref[0:1, :] * extdc[CONV_WIDTH - 1:CONV_WIDTH - 1 + t, :]
        for kk in range(1, CONV_WIDTH):
            off = CONV_WIDTH - 1 - kk
            du_lru = du_lru + cw_ref[kk:kk + 1, :] * extdc[off:off + t, :]
        dproj_ref[:, p:2 * p] = du_lru.astype(BF16)
        extdc[t:t + CONV_HALO, :] = extdc[0:CONV_HALO, :]

        extp[0:POOL_HALO, :] = jnp.where(seq_start, 0.0, upp_ref[...])
        extp[POOL_HALO:POOL_HALO + t, :] = up_ref[...]
        for g, w in enumerate(POOL_WINDOWS):
            cs = slice(g * cg, (g + 1) * cg)
            cnt = jnp.minimum(t_idx + 1, w).astype(F32)
            mixed = (_window_sum(extp, POOL_HALO, t, cs, w, -1) / cnt - up_ref[:, cs]).astype(BF16)
            pre = jnp.dot(mixed, wp_ref[g], preferred_element_type=F32) + bp_ref[:, cs]
            dyp = dyp_ref[:, cs]
            dps_ref[:, cs] += jnp.sum(dyp * pre, axis=0, keepdims=True)
            dpre = dyp * ps_ref[:, cs]
            dpreb = dpre.astype(BF16)
            dbp_ref[:, cs] += jnp.sum(dpre, axis=0, keepdims=True)
            dwp_ref[g] += lax.dot_general(mixed, dpreb, TN_DIMS, preferred_element_type=F32)
            dmixed = lax.dot_general(dpreb, wp_ref[g], NT_DIMS, preferred_element_type=F32)
            extg[0:t, cs] = dmixed / cnt
            dproj_ref[:, cs] = (_window_sum(extg, 0, t, cs, w, 1) - dmixed).astype(BF16)
        extg[t:t + POOL_HALO, :] = extg[0:POOL_HALO, :]

    def col(j):
        return pl.BlockSpec((t, p), lambda i: (nt - 1 - i, j))

    def prev(rows, j):
        per = t // rows
        return pl.BlockSpec((rows, p), lambda i: (jnp.maximum((nt - 1 - i) * per - 1, 0), j))

    def whole(a):
        nd = a.ndim
        return pl.BlockSpec(a.shape, lambda i: (0,) * nd)

    consts = (wp, bp, ps, cw, cb, wa, ba, wx, bx, lam)
    grads = (wp, bp, ps, cw, cb, wa, ba, wx, bx, lam)
    tile = pltpu.VMEM((t, p), F32)
    return _call(
        body, grid=(nt,),
        in_specs=[col(0), col(1), col(0), col(1), col(2), prev(POOL_HALO, 0), prev(CONV_HALO, 1), col(0),
                  prev(SUBLANES, 0)] + [whole(a) for a in consts],
        out_specs=[pl.BlockSpec((t, 3 * p), lambda i: (nt - 1 - i, 0))] + [whole(a) for a in grads],
        out_shape=[jax.ShapeDtypeStruct((s, 3 * p), BF16)] + [jax.ShapeDtypeStruct(a.shape, F32) for a in grads],
        scratch_shapes=[pltpu.VMEM((t + POOL_HALO, p), F32), pltpu.VMEM((t + POOL_HALO, p), F32),
                        pltpu.VMEM((t + CONV_HALO, p), F32), pltpu.VMEM((t + CONV_HALO, p), F32),
                        pltpu.VMEM((t + SUBLANES, p), F32), pltpu.VMEM((SUBLANES, p), F32),
                        tile, tile, tile, tile, tile, tile],
        semantics=("arbitrary",), name=name,
        args=(dycat, dycat, proj, proj, proj, proj, proj, hsave, hsave, *consts), jobs=jobs)


def _pair_add(parts, got, core, name):
    n, r, c = got.shape
    tr = _tile(TILES["add"], r)

    def body(core_ref, a_ref, b_ref, o_ref):
        del core_ref
        o_ref[...] = (a_ref[...].astype(F32) + b_ref[...].astype(F32)).astype(o_ref.dtype)

    blk = pl.BlockSpec((None, tr, c), lambda k, i, core_ref: (k, i, 0))
    mine = pl.BlockSpec((None, tr, c), lambda k, i, core_ref: (2 * k + core_ref[0], i, 0))
    return _call(body, grid=(n, r // tr), in_specs=[mine, blk], out_specs=[blk],
                 out_shape=[jax.ShapeDtypeStruct(got.shape, got.dtype)], semantics=("parallel", "parallel"),
                 name=name, args=(parts, got), index=core)[0]


def _sum_parts(parts, name):
    n, r, c = parts.shape
    tr = _tile(TILES["adam"], r)

    def body(p_ref, o_ref):
        acc = p_ref[0].astype(F32)
        for d in range(1, n):
            acc = acc + p_ref[d].astype(F32)
        o_ref[...] = acc

    return _call(
        body, grid=(r // tr,), in_specs=[pl.BlockSpec((n, tr, c), lambda i: (0, i, 0))],
        out_specs=[pl.BlockSpec((tr, c), lambda i: (i, 0))], out_shape=[jax.ShapeDtypeStruct((r, c), F32)],
        semantics=("parallel",), name=name, args=(parts,))[0]


def _adamw(w, m, v, parts, name, jobs=(), own=None, chip=None):
    r, c = w.shape
    n = parts.shape[0]
    tr = _tile(TILES["adam"], r)

    def body(*refs):
        if own is not None:
            refs = refs[1:]
            own_ref, refs = refs[3], refs[:3] + refs[4:]
        w_ref, m_ref, v_ref, p_ref, g_ref, d_ref, nm_ref, nv_ref = refs
        g = p_ref[0].astype(F32)
        if own is not None:
            g = own_ref[...].astype(F32) + g
        for d in range(1, n):
            g = g + p_ref[d].astype(F32)
        nm = ADAM_B1 * m_ref[...] + (1.0 - ADAM_B1) * g
        nv = ADAM_B2 * v_ref[...] + (1.0 - ADAM_B2) * (g * g)
        m_hat = nm / (1.0 - ADAM_B1 ** ADAM_STEP)
        v_hat = nv / (1.0 - ADAM_B2 ** ADAM_STEP)
        g_ref[...] = g
        d_ref[...] = -ADAM_LR * (m_hat / (jnp.sqrt(v_hat) + ADAM_EPS) + ADAM_WD * w_ref[...])
        nm_ref[...] = nm
        nv_ref[...] = nv

    row = pl.BlockSpec((tr, c), lambda i, *_: (i, 0))
    in_specs, args = [row, row, row], [w, m, v]
    if own is not None:
        in_specs.append(pl.BlockSpec((None, tr, c), lambda i, chip_ref: (chip_ref[0], i, 0)))
        args.append(own)
    in_specs.append(pl.BlockSpec((n, tr, c), lambda i, *_: (0, i, 0)))
    args.append(parts)
    return _call(
        body, grid=(r // tr,), in_specs=in_specs, out_specs=[row] * 4, out_shape=[jax.ShapeDtypeStruct((r, c), F32)] * 4,
        semantics=("parallel",), name=name, args=args, jobs=jobs, index=chip if own is not None else None)


SMALL_ORDER = ("w_a", "w_x", "conv_w", "b_pool", "conv_b", "b_a", "b_x", "lru_lambda", "pool_scale",
               "ln1_g", "ln1_b", "ln2_g", "ln2_b", "ln3_g", "ln3_b")


def _pack_rows(a, p):
    flat = a.reshape(-1, p)
    pad = (-flat.shape[0]) % SUBLANES
    return jnp.pad(flat, ((0, pad), (0, 0))) if pad else flat


def kernel(x, mem, w_in, conv_w, conv_b, w_a, b_a, w_x, b_x, lru_lambda, w_pool, b_pool, pool_scale, w_out, ln1_g, ln1_b, w_q, w_k, w_v, w_o, ln2_g, ln2_b, w_ff1, w_ff2, ln3_g, ln3_b, loss_target, m_w_in, m_conv_w, m_conv_b, m_w_a, m_b_a, m_w_x, m_b_x, m_lru_lambda, m_w_pool, m_b_pool, m_pool_scale, m_w_out, m_ln1_g, m_ln1_b, m_w_q, m_w_k, m_w_v, m_w_o, m_ln2_g, m_ln2_b, m_w_ff1, m_w_ff2, m_ln3_g, m_ln3_b, v_w_in, v_conv_w, v_conv_b, v_w_a, v_b_a, v_w_x, v_b_x, v_lru_lambda, v_w_pool, v_b_pool, v_pool_scale, v_w_out, v_ln1_g, v_ln1_b, v_w_q, v_w_k, v_w_v, v_w_o, v_ln2_g, v_ln2_b, v_w_ff1, v_w_ff2, v_ln3_g, v_ln3_b):
    names = ("w_in", "conv_w", "conv_b", "w_a", "b_a", "w_x", "b_x", "lru_lambda", "w_pool", "b_pool", "pool_scale",
             "w_out", "ln1_g", "ln1_b", "w_q", "w_k", "w_v", "w_o", "ln2_g", "ln2_b", "w_ff1", "w_ff2", "ln3_g", "ln3_b")
    w_loc = dict(zip(names, (w_in, conv_w, conv_b, w_a, b_a, w_x, b_x, lru_lambda, w_pool, b_pool, pool_scale,
                             w_out, ln1_g, ln1_b, w_q, w_k, w_v, w_o, ln2_g, ln2_b, w_ff1, w_ff2, ln3_g, ln3_b)))
    m_loc = dict(zip(names, (m_w_in, m_conv_w, m_conv_b, m_w_a, m_b_a, m_w_x, m_b_x, m_lru_lambda, m_w_pool, m_b_pool,
                             m_pool_scale, m_w_out, m_ln1_g, m_ln1_b, m_w_q, m_w_k, m_w_v, m_w_o, m_ln2_g, m_ln2_b,
                             m_w_ff1, m_w_ff2, m_ln3_g, m_ln3_b)))
    v_loc = dict(zip(names, (v_w_in, v_conv_w, v_conv_b, v_w_a, v_b_a, v_w_x, v_b_x, v_lru_lambda, v_w_pool, v_b_pool,
                             v_pool_scale, v_w_out, v_ln1_g, v_ln1_b, v_w_q, v_w_k, v_w_v, v_w_o, v_ln2_g, v_ln2_b,
                             v_w_ff1, v_w_ff2, v_ln3_g, v_ln3_b)))
    s, d = x.shape[1], x.shape[2]
    p = conv_b.shape[1]
    cg = p // N_POOL_GROUPS
    hd = p // LRU_HEADS
    me = 4 * lax.axis_index("x") + 2 * lax.axis_index("y") + lax.axis_index("c")

    xs, mems, tgt = x[0], mem[0], loss_target[0]
    memb = mems.astype(BF16)

    gathers = {n: _Job("gather", w_loc[n][0].astype(BF16))
               for n in ("w_in", "w_out", "w_q", "w_k", "w_v", "w_o", "w_ff2", "w_pool")}
    ff1_shard = w_ff1[0].astype(BF16)
    ff1_rows = ff1_shard.shape[0] // FF1_PIECES

    def ff1_piece(i, earlier=None):
        return _Job("gather", ff1_shard[i * ff1_rows:(i + 1) * ff1_rows], window=(i * ff1_rows, ff1_shard.shape[0]),
                    into=None if earlier is None else earlier.out)
    tiny = jnp.concatenate([_pack_rows(conv_w[0], p // N_DEV),
                            _pack_rows(jnp.pad(b_pool[0], ((0, 0), (0, p // N_DEV - cg // N_DEV))), p // N_DEV)], axis=0)
    gathers["tiny"] = _Job("gather", tiny)

    def gathered(n):
        full = gathers[n].out
        if n == "w_in":
            return jnp.transpose(full, (1, 0, 2)).reshape(1, full.shape[1], -1)
        return full.reshape(1, -1, full.shape[-1])

    W = {"conv_b": conv_b, "b_a": b_a.reshape(1, p), "b_x": b_x.reshape(1, p), "lru_lambda": lru_lambda,
         "pool_scale": pool_scale, "w_a": w_a[0].astype(BF16), "w_x": w_x[0].astype(BF16)}
    for n in ("ln1_g", "ln1_b", "ln2_g", "ln2_b", "ln3_g", "ln3_b"):
        W[n] = w_loc[n]

    out_g, out_d, out_m, out_v = {}, {}, {}, {}
    pairs, quads, sums = {}, {}, {}
    core =lax.axis_index("c").astype(jnp.int32).reshape(1)
    chip = (2 * lax.axis_index("x") + lax.axis_index("y")).astype(jnp.int32).reshape(1)

    def pair(n, partial):
        pairs[n] = _Job("pair", partial.reshape(N_DEV, -1, partial.shape[-1]))
        return pairs[n]

    def quad(n, lo=0, hi=1, of=1):
        if lo == 0:
            sums[n] = _pair_add(pairs[n].src, pairs[n].out, core, "add_" + n)
        rows = sums[n].shape[1] // of
        quads[n] = _Job("quad", sums[n], window=None if (lo, hi) == (0, of) else (lo * rows, (hi - lo) * rows),
                        into=None if lo == 0 else quads[n].out)
        return quads[n]

    def update(n, parts, jobs=(), own=None):
        shp = w_loc[n].shape
        rows = parts.shape[1]
        w2, m2, v2 = (a.reshape(rows, -1) for a in (w_loc[n], m_loc[n], v_loc[n]))
        res = _adamw(w2, m2, v2, parts.reshape(parts.shape[0], rows, -1), "adamw_" + n, jobs=jobs, own=own, chip=chip)
        out_g[n], out_d[n], out_m[n], out_v[n] = (r.reshape(shp) for r in res)

    assert FF1_PIECES == 4
    xb = _to_bf16(xs, "cast_x", jobs=[gathers["w_in"], gathers["tiny"], gathers["w_pool"]])
    W["w_pool"] = jnp.transpose(gathers["w_pool"].out, (1, 0, 2, 3)).reshape(N_POOL_GROUPS, cg, cg)
    cwb = gathers["tiny"].out
    W["conv_w"] = jnp.transpose(cwb[:, :CONV_WIDTH, :], (1, 0, 2)).reshape(CONV_WIDTH, p)
    W["b_pool"] = jnp.transpose(cwb[:, SUBLANES:SUBLANES + N_POOL_GROUPS, :cg // N_DEV], (1, 0, 2)).reshape(1, p)
    mixer_consts = (W["w_pool"], W["b_pool"], W["pool_scale"], W["conv_w"], W["conv_b"], W["w_a"], W["b_a"],
                    W["w_x"], W["b_x"], W["lru_lambda"])

    w_in_full = gathered("w_in")
    piece = ff1_piece(FF1_PIECES - 1)
    (proj,) = _mm_nn(xb, w_in_full, [F32], "fwd_proj", jobs=[gathers["w_out"], piece])
    ycat, hsave = _mixer_fwd(proj, *mixer_consts, "fwd_mixer", jobs=[gathers["w_q"], gathers["w_k"]])
    (y1,) = _mm_nn(ycat, gathered("w_out"), [F32], "fwd_out", jobs=[gathers["w_v"]])
    x1, x1b, xhat1, rstd1 = _ln_fwd(xs, y1, W["ln1_g"], W["ln1_b"], "fwd_ln1", jobs=[gathers["w_o"]])
    piece = ff1_piece(0, piece)
    (q,) = _mm_nn(x1b, gathered("w_q"), [BF16], "fwd_q", jobs=[piece])
    (k,) = _mm_nn(memb, gathered("w_k"), [BF16], "fwd_k")
    (v,) = _mm_nn(memb, gathered("w_v"), [BF16], "fwd_v")
    o = _attn_fwd(q, k, v, "fwd_attn")
    piece = ff1_piece(1, piece)
    (y2,) = _mm_nn(o, gathered("w_o"), [F32], "fwd_o", jobs=[piece])
    piece = ff1_piece(2, piece)
    x2, x2b, xhat2, rstd2 = _ln_fwd(x1, y2, W["ln2_g"], W["ln2_b"], "fwd_ln2", jobs=[piece])
    w_ff1_full = piece.out

    def relu_sq(acc):
        r = jnp.maximum(acc, 0.0)
        return r, r * r

    rb, act = _mm_nn(x2b, w_ff1_full, [BF16, BF16], "fwd_ff1", epilogue=relu_sq, jobs=[gathers["w_ff2"]])
    (y3,) = _mm_nn(act, gathered("w_ff2"), [F32], "fwd_ff2")
    loss_rows, dz3, dz3b, dg3, db3 = _ln_loss(x2, y3, W["ln3_g"], W["ln3_b"], tgt, "ln3_loss")
    loss = lax.psum(loss_rows[0, 0], MESH_AXES)

    small = {"ln3_g": dg3, "ln3_b": db3}

    def add_residual(acc, e):
        return (acc + ALPHA * e,)

    dw_ff2 = _mm_tn(act, dz3b, 1, BF16, "bwd_dw_ff2")
    (dhid,) = _mm_nt(dz3b, gathered("w_ff2"), [BF16], "bwd_dact", extras=(rb,), jobs=[pair("w_ff2", dw_ff2)],
                     epilogue=lambda acc, r: (acc * (2.0 * r.astype(F32)),))
    dw_ff1 = _mm_tn(x2b, dhid, N_DEV, BF16, "bwd_dw_ff1", jobs=[quad("w_ff2", 0, 1, 2)])
    (dx2,) = _mm_nt(dhid, w_ff1_full, [F32], "bwd_dx2", epilogue=add_residual, extras=(dz3,), tk=TILES["tk"] // 2,
                    jobs=[pair("w_ff1", dw_ff1), quad("w_ff2", 1, 2, 2)])
    dz2, dz2b, small["ln2_g"], small["ln2_b"] = _ln_bwd(dx2, xhat2, rstd2, W["ln2_g"], "bwd_ln2")

    dw_o = _mm_tn(o, dz2b, 1, BF16, "bwd_dw_o")
    (do,) = _mm_nt(dz2b, gathered("w_o"), [BF16], "bwd_do", jobs=[pair("w_o", dw_o)])
    dq, dk, dv = _attn_bwd(q, k, v, do, "bwd_attn", jobs=[quad("w_o")])
    dw_q = _mm_tn(x1b, dq, 1, BF16, "bwd_dw_q")
    dw_k = _mm_tn(memb, dk.astype(BF16), 1, BF16, "bwd_dw_k")
    dw_v = _mm_tn(memb, dv.astype(BF16), 1, BF16, "bwd_dw_v")
    (dx1,) = _mm_nt(dq, gathered("w_q"), [F32], "bwd_dx1", epilogue=add_residual, extras=(dz2,),
                    jobs=[pair("w_q", dw_q), pair("w_k", dw_k), pair("w_v", dw_v), quad("w_ff1", 0, 2, 8)])
    dz1, dz1b, small["ln1_g"], small["ln1_b"] = _ln_bwd(dx1, xhat1, rstd1, W["ln1_g"], "bwd_ln1", jobs=[quad("w_q")])

    dw_out = _mm_tn(ycat, dz1b, 1, BF16, "bwd_dw_out")
    (dycat,) = _mm_nt(dz1b, gathered("w_out"), [F32], "bwd_dycat", jobs=[pair("w_out", dw_out), quad("w_ff1", 2, 3, 8)])
    (dproj, dwp, small["b_pool"], small["pool_scale"], small["conv_w"], small["conv_b"], small["w_a"], small["b_a"],
     small["w_x"], small["b_x"], small["lru_lambda"]) = _mixer_bwd(
        dycat, proj, hsave, *mixer_consts, "bwd_mixer", jobs=[quad("w_ff1", 3, 7, 8), quad("w_k"), quad("w_v")])
    dw_pool = jnp.transpose(dwp.astype(BF16).reshape(N_POOL_GROUPS, N_DEV, cg // N_DEV, cg), (1, 0, 2, 3))
    pack = jnp.concatenate([_pack_rows(small[n], p) for n in SMALL_ORDER], axis=0)
    small_gather = _Job("gather", pack)
    dw_in = _mm_tn(xb, dproj, 1, BF16, "bwd_dw_in", jobs=[quad("w_out"), pair("w_pool", dw_pool), small_gather])
    dw_in = jnp.transpose(dw_in.reshape(dw_in.shape[1], N_DEV, -1), (1, 0, 2))
    (gx_lo,) = _mm_nt(dproj, w_in_full, [F32], "bwd_dx_lo", epilogue=add_residual, extras=(dz1,), part=(0, 2),
                      jobs=[pair("w_in", dw_in), quad("w_pool"), quad("w_ff1", 7, 8, 8)])
    (gx_hi,) = _mm_nt(dproj, w_in_full, [F32], "bwd_dx_hi", epilogue=add_residual, extras=(dz1,), part=(1, 2),
                      jobs=[quad("w_in", 0, 1, 2)])
    grad_x = jnp.concatenate([gx_lo, gx_hi], axis=0)

    update("w_ff2", quads["w_ff2"].out, jobs=[quad("w_in", 1, 2, 2)], own=quads["w_ff2"].src)
    for n in ("w_ff1", "w_o", "w_q", "w_k", "w_v", "w_out", "w_pool", "w_in"):
        update(n, quads[n].out, own=quads[n].src)

    total = _sum_parts(small_gather.out, "sum_small")
    row = 0
    for n in SMALL_ORDER:
        size = small[n].size
        nrows = size // p
        g_full = total[row:row + nrows].reshape(small[n].shape)
        row += nrows + (-nrows) % SUBLANES
        if n == "conv_w":
            g_loc = lax.dynamic_slice_in_dim(g_full, me * (p // N_DEV), p // N_DEV, axis=1)
        elif n == "b_pool":
            g_loc = lax.dynamic_slice_in_dim(g_full.reshape(N_POOL_GROUPS, cg), me * (cg // N_DEV), cg // N_DEV, axis=1)
        else:
            g_loc = g_full
        rows = g_loc.shape[0] if n not in ("w_a", "w_x") else LRU_HEADS * hd
        update(n, g_loc.reshape(1, rows, -1))

    order = names
    return (loss, grad_x[None], *[out_g[n] for n in order], *[out_d[n] for n in order],
            *[out_m[n] for n in order], *[out_v[n] for n in order])
```

```python
import functools

import jax
import jax.numpy as jnp
from jax import lax
from jax.experimental import pallas as pl
from jax.experimental.pallas import tpu as pltpu

F32 = jnp.float32
BF16 = jnp.bfloat16

N_DEV = 8
MESH_AXES = ("x", "y", "c")
POOL_WINDOWS = (2, 4, 8, 16)
N_POOL_GROUPS = len(POOL_WINDOWS)
POOL_HALO = 16
CONV_WIDTH = 4
CONV_HALO = 8
FF1_PIECES = 4
LRU_HEADS = 8
LRU_C = 8.0
XATTN_HEADS = 4
LN_EPS = 1e-5
ALPHA = 2.0 ** 0.25
ADAM_LR = 0.001
ADAM_B1 = 0.9
ADAM_B2 = 0.999
ADAM_EPS = 1e-08
ADAM_WD = 0.01
ADAM_STEP = 10
SUBLANES = 8
VMEM_LIMIT = 56 * 1024 * 1024

NT_DIMS = (((1,), (1,)), ((), ()))
TN_DIMS = (((0,), (0,)), ((), ()))


def _params(*sem):
    return pltpu.CompilerParams(dimension_semantics=sem, vmem_limit_bytes=VMEM_LIMIT)


def _place():
    return lax.axis_index("x"), lax.axis_index("y"), lax.axis_index("c")


def _remote(src, dst, send_sem, recv_sem, to):
    return pltpu.make_async_remote_copy(src_ref=src, dst_ref=dst, send_sem=send_sem, recv_sem=recv_sem,
                                        device_id=to, device_id_type=pl.DeviceIdType.MESH)


class _Job:
    def __init__(self, kind, src, window=None, into=None):
        self.kind, self.src, self.out, self.window, self.into = kind, src, None, window, into

    def out_shape(self):
        s = self.src.shape
        if self.kind == "gather" and self.window is not None:
            s = (self.window[1],) + s[1:]
        shape = {"gather": (N_DEV,) + s, "pair": (4,) + s[1:], "quad": (3,) + s[1:]}[self.kind]
        return jax.ShapeDtypeStruct(shape, self.src.dtype)

    def scratch(self):
        n = {"gather": 7, "pair": 4, "quad": 3}[self.kind]
        sems = [pltpu.SemaphoreType.DMA((n,)), pltpu.SemaphoreType.DMA((n,))]
        if self.kind == "gather":
            sems += [pltpu.SemaphoreType.DMA((2,)), pltpu.VMEM(self.src.shape, self.src.dtype)]
        return sems

    def ops(self, src, out, *scratch):
        if self.kind == "gather":
            return _gather_ops(src, out, *scratch, first_row=None if self.window is None else self.window[0])
        if self.kind == "quad":
            return _quad_ops(src, out, *scratch, rows=self.window)
        return _pair_ops(src, out, *scratch)


def _gather_ops(x_ref, out_ref, send_sems, recv_sems, local_sems, bounce, first_row=None):
    x, y, c = _place()
    me, sibling = (x, y, c), (x, y, 1 - c)
    chips = [(1 - x, y), (x, 1 - y), (1 - x, 1 - y)]

    def slot(px, py, pc):
        block = out_ref.at[4 * px + 2 * py + pc]
        return block if first_row is None else block.at[pl.ds(first_row, x_ref.shape[0])]

    def copy(k, block, to, src=None):
        return _remote(slot(*block) if src is None else src, slot(*block), send_sems.at[k], recv_sems.at[k], to)

    mine_in = pltpu.make_async_copy(x_ref, bounce, local_sems.at[0])
    mine_out = pltpu.make_async_copy(bounce, slot(*me), local_sems.at[1])
    first = [copy(0, me, sibling, src=x_ref)] + [copy(1 + j, me, (*chip, c), src=x_ref) for j, chip in enumerate(chips)]
    passed = [copy(4 + j, (*chip, c), sibling) for j, chip in enumerate(chips)]

    def start():
        mine_in.start()
        for cp in first:
            cp.start()

    def mid():
        mine_in.wait()
        mine_out.start()
        for j, chip in enumerate(chips):
            copy(1 + j, (*chip, c), me).wait_recv()
            passed[j].start()

    def finish():
        copy(0, sibling, me).wait_recv()
        for j, chip in enumerate(chips):
            copy(4 + j, (*chip, 1 - c), me).wait_recv()
        for cp in first + passed:
            cp.wait_send()
        mine_out.wait()

    return start, mid, finish


def _pair_ops(p_ref, got_ref, send_sems, recv_sems):
    x, y, c = _place()
    give = [_remote(p_ref.at[2 * k + 1 - c], got_ref.at[k], send_sems.at[k], recv_sems.at[k], (x, y, 1 - c))
            for k in range(4)]

    def start():
        for cp in give:
            cp.start()

    def finish():
        for cp in give:
            cp.wait_recv()
        for cp in give:
            cp.wait_send()

    return start, None, finish


def _quad_ops(q_ref, out_ref, send_sems, recv_sems, rows=None):
    x, y, c = _place()

    def part(block):
        return block if rows is None else block.at[pl.ds(rows[0], rows[1])]

    copies = []
    for rel in range(1, 4):
        px = 1 - x if rel & 2 else x
        py = 1 - y if rel & 1 else y
        copies.append(_remote(part(q_ref.at[2 * px + py]), part(out_ref.at[rel - 1]), send_sems.at[rel - 1],
                              recv_sems.at[rel - 1], (px, py, c)))

    def start():
        for cp in copies:
            cp.start()

    def finish():
        for cp in copies:
            cp.wait_recv()
        for cp in copies:
            cp.wait_send()

    return start, None, finish


def _call(body, *, grid, in_specs, out_specs, out_shape, scratch_shapes=(), semantics, name, args, jobs=(), index=None):
    in_specs, out_specs, out_shape = list(in_specs), list(out_specs), list(out_shape)
    scratch_shapes, jobs = list(scratch_shapes), list(jobs)
    n_in, n_out, n_scr, n_job = len(in_specs), len(out_specs), len(scratch_shapes), len(jobs)
    n_idx = 0 if index is None else 1
    job_scratch = [j.scratch() for j in jobs]
    n_steps = functools.reduce(lambda a, b: a * b, grid, 1)
    early = n_steps - 1 - max(1, n_steps // 8) if n_steps >= 4 else None

    intos = [(k, j.into) for k, j in enumerate(jobs) if j.into is not None]

    def hosted(*refs):
        idx, refs = refs[:n_idx], refs[n_idx:]
        ins, jin = refs[:n_in], refs[n_in:n_in + n_job]
        o0 = n_in + n_job + len(intos)
        outs, jout = refs[o0:o0 + n_out], refs[o0 + n_out:o0 + n_out + n_job]
        s0 = o0 + n_out + n_job
        scr, jscr = refs[s0:s0 + n_scr], refs[s0 + n_scr:]
        ops, at = [], 0
        for k, j in enumerate(jobs):
            ops.append(j.ops(jin[k], jout[k], *jscr[at:at + len(job_scratch[k])]))
            at += len(job_scratch[k])
        step = functools.reduce(lambda acc, a: acc * grid[a] + pl.program_id(a), range(len(grid)), 0)
        mids = [mid for _, mid, _ in ops if mid is not None]

        @pl.when(step == 0)
        def _():
            for start, _, _ in ops:
                start()

        if mids and early is not None:
            @pl.when(step == early)
            def _():
                for mid in mids:
                    mid()

        body(*idx, *ins, *outs, *scr)

        @pl.when(step == n_steps - 1)
        def _():
            if early is None:
                for mid in mids:
                    mid()
            for _, _, finish in ops:
                finish()

    hbm = pl.BlockSpec(memory_space=pl.ANY)
    spec = pltpu.PrefetchScalarGridSpec(
        num_scalar_prefetch=n_idx, grid=grid, in_specs=in_specs + [hbm] * (n_job + len(intos)),
        out_specs=out_specs + [hbm] * n_job, scratch_shapes=scratch_shapes + [s for js in job_scratch for s in js])
    aliases = {n_idx + n_in + n_job + q: n_out + k for q, (k, _) in enumerate(intos)}
    res = pl.pallas_call(
        hosted if jobs else body, grid_spec=spec, out_shape=out_shape + [j.out_shape() for j in jobs],
        input_output_aliases=aliases,
        compiler_params=_params(*(["arbitrary"] * len(grid) if jobs else semantics)), name=name,
    )(*([] if index is None else [index]), *args, *[j.src for j in jobs], *[buf for _, buf in intos])
    for j, o in zip(jobs, res[n_out:]):
        j.out = o
    return res[:n_out]


def _to_bf16(a, name, jobs=()):
    r, c = a.shape
    tr = _tile(TILES["row"], r)

    def body(a_ref, o_ref):
        o_ref[...] = a_ref[...].astype(BF16)

    row = pl.BlockSpec((tr, c), lambda i: (i, 0))
    return _call(body, grid=(r // tr,), in_specs=[row], out_specs=[row], out_shape=[jax.ShapeDtypeStruct((r, c), BF16)],
                 semantics=("parallel",), name=name, args=(a,), jobs=jobs)[0]


TILES = dict(tm=1024, tn=1024, tk=2048, row=256, attn=512, mixer=256, adam=128, add=1024)


def _tile(pref, n):
    for t in range(min(pref, n), 0, -1):
        if n % t == 0 and (t % SUBLANES == 0 or t == n):
            return t
    return n


def _accumulate(acc, step, n_steps, product, write):
    if n_steps == 1:
        write(product())
        return

    @pl.when(step == 0)
    def _():
        acc[...] = product()

    @pl.when(jnp.logical_and(step > 0, step < n_steps - 1))
    def _():
        acc[...] += product()

    @pl.when(step == n_steps - 1)
    def _():
        write(acc[...] + product())


def _acc_scratch(n_steps, tm, tn):
    return [] if n_steps == 1 else [pltpu.VMEM((tm, tn), F32)]


def _mm_nn(a, b3, out_dtypes, name, *, tm=None, tn=None, tk=None, epilogue=None, extras=(), jobs=()):
    m, k = a.shape
    g, k2, ns = b3.shape
    assert k == k2
    n = g * ns
    tm, tn, tk = _tile(tm or TILES["tm"], m), _tile(tn or TILES["tn"], ns), _tile(tk or TILES["tk"], k)
    nb, nk = ns // tn, k // tk
    n_ex, n_out = len(extras), len(out_dtypes)

    def body(*refs):
        a_ref, b_ref = refs[:2]
        ex = refs[2:2 + n_ex]
        outs = refs[2 + n_ex:2 + n_ex + n_out]
        acc = refs[-1] if nk > 1 else None

        def write(r):
            res = epilogue(r, *[e[...] for e in ex]) if epilogue is not None else (r,)
            for o, v in zip(outs, res):
                o[...] = v.astype(o.dtype)

        _accumulate(acc, pl.program_id(2), nk,
                    lambda: jnp.dot(a_ref[...], b_ref[...], preferred_element_type=F32), write)

    tile_out = pl.BlockSpec((tm, tn), lambda i, j, kk: (i, j))
    return _call(
        body, grid=(m // tm, n // tn, nk),
        in_specs=[pl.BlockSpec((tm, tk), lambda i, j, kk: (i, kk)),
                  pl.BlockSpec((None, tk, tn), lambda i, j, kk: (j // nb, kk, j % nb))] + [tile_out] * n_ex,
        out_specs=[tile_out] * n_out,
        out_shape=[jax.ShapeDtypeStruct((m, n), d) for d in out_dtypes],
        scratch_shapes=_acc_scratch(nk, tm, tn),
        semantics=("parallel", "parallel", "arbitrary"), name=name, args=(a, b3, *extras), jobs=jobs)


def _mm_nt(a, b3, out_dtypes, name, *, tm=None, tn=None, tk=None, epilogue=None, extras=(), jobs=(), part=(0, 1)):
    m, n = a.shape
    g, k, ns = b3.shape
    assert n == g * ns
    tm, tn, tk = _tile(tm or TILES["tm"], m // part[1]), _tile(tn or TILES["tn"], k), _tile(tk or TILES["tk"], ns)
    nb, nc = ns // tk, n // tk
    n_ex, n_out = len(extras), len(out_dtypes)
    m_blocks = m // tm // part[1]
    first = part[0] * m_blocks

    def body(*refs):
        a_ref, b_ref = refs[:2]
        ex = refs[2:2 + n_ex]
        outs = refs[2 + n_ex:2 + n_ex + n_out]
        acc = refs[-1] if nc > 1 else None

        def write(r):
            res = epilogue(r, *[e[...] for e in ex]) if epilogue is not None else (r,)
            for o, v in zip(outs, res):
                o[...] = v.astype(o.dtype)

        _accumulate(acc, pl.program_id(2), nc,
                    lambda: lax.dot_general(a_ref[...], b_ref[...], NT_DIMS, preferred_element_type=F32), write)

    tile_out = pl.BlockSpec((tm, tn), lambda i, j, cc: (i, j))
    tile_ex = pl.BlockSpec((tm, tn), lambda i, j, cc: (first + i, j))
    return _call(
        body, grid=(m_blocks, k // tn, nc),
        in_specs=[pl.BlockSpec((tm, tk), lambda i, j, cc: (first + i, cc)),
                  pl.BlockSpec((None, tn, tk), lambda i, j, cc: (cc // nb, j, cc % nb))] + [tile_ex] * n_ex,
        out_specs=[tile_out] * n_out,
        out_shape=[jax.ShapeDtypeStruct((m_blocks * tm, k), d) for d in out_dtypes],
        scratch_shapes=_acc_scratch(nc, tm, tn),
        semantics=("parallel", "parallel", "arbitrary"), name=name, args=(a, b3, *extras), jobs=jobs)


def _mm_tn(a, b, g, out_dtype, name, *, tm=None, tn=None, tk=None, jobs=()):
    s, m = a.shape
    s2, n = b.shape
    assert s == s2 and n % g == 0
    ns = n // g
    tm, tn, tk = _tile(tm or TILES["tm"], m), _tile(tn or TILES["tn"], ns), _tile(tk or TILES["tk"], s)
    nb, nc = ns // tn, s // tk

    def body(a_ref, b_ref, o_ref, *scratch):
        def write(r):
            o_ref[...] = r.astype(o_ref.dtype)

        _accumulate(scratch[0] if nc > 1 else None, pl.program_id(2), nc,
                    lambda: lax.dot_general(a_ref[...], b_ref[...], TN_DIMS, preferred_element_type=F32), write)

    return _call(
        body, grid=(m // tm, n // tn, nc),
        in_specs=[pl.BlockSpec((tk, tm), lambda i, j, cc: (cc, i)),
                  pl.BlockSpec((tk, tn), lambda i, j, cc: (cc, j))],
        out_specs=[pl.BlockSpec((None, tm, tn), lambda i, j, cc: (j // nb, i, j % nb))],
        out_shape=[jax.ShapeDtypeStruct((g, m, ns), out_dtype)],
        scratch_shapes=_acc_scratch(nc, tm, tn),
        semantics=("parallel", "parallel", "arbitrary"), name=name, args=(a, b), jobs=jobs)[0]


def _ln_stats(z):
    mu = jnp.mean(z, axis=-1, keepdims=True)
    zc = z - mu
    var = jnp.mean(zc * zc, axis=-1, keepdims=True)
    rstd = lax.rsqrt(var + LN_EPS)
    return zc * rstd, rstd


def _ln_grad(dout, xhat, rstd, gain):
    dxhat = dout * gain
    m1 = jnp.mean(dxhat, axis=-1, keepdims=True)
    m2 = jnp.mean(dxhat * xhat, axis=-1, keepdims=True)
    return rstd * (dxhat - m1 - xhat * m2)


def _ln_fwd(xres, y, gain, bias, name, jobs=()):
    s, d = xres.shape
    tr = _tile(TILES["row"], s)

    def body(x_ref, y_ref, g_ref, b_ref, xn_ref, xnb_ref, xhat_ref, rstd_ref):
        xhat, rstd = _ln_stats(ALPHA * x_ref[...] + y_ref[...])
        out = xhat * g_ref[...] + b_ref[...]
        xn_ref[...] = out
        xnb_ref[...] = out.astype(BF16)
        xhat_ref[...] = xhat
        rstd_ref[...] = rstd

    row = pl.BlockSpec((tr, d), lambda i: (i, 0))
    vec = pl.BlockSpec((1, d), lambda i: (0, 0))
    return _call(
        body, grid=(s // tr,), in_specs=[row, row, vec, vec],
        out_specs=[row, row, row, pl.BlockSpec((tr, 1), lambda i: (i, 0))],
        out_shape=[jax.ShapeDtypeStruct((s, d), F32), jax.ShapeDtypeStruct((s, d), BF16),
                   jax.ShapeDtypeStruct((s, d), F32), jax.ShapeDtypeStruct((s, 1), F32)],
        semantics=("parallel",), name=name, args=(xres, y, gain, bias), jobs=jobs)


def _ln_bwd(dout, xhat, rstd, gain, name, jobs=()):
    s, d = dout.shape
    tr = _tile(TILES["row"], s)

    def body(d_ref, xhat_ref, rstd_ref, g_ref, dz_ref, dzb_ref, dg_ref, db_ref):
        @pl.when(pl.program_id(0) == 0)
        def _():
            dg_ref[...] = jnp.zeros_like(dg_ref)
            db_ref[...] = jnp.zeros_like(db_ref)

        dout_t, xhat_t = d_ref[...], xhat_ref[...]
        dz = _ln_grad(dout_t, xhat_t, rstd_ref[...], g_ref[...])
        dz_ref[...] = dz
        dzb_ref[...] = dz.astype(BF16)
        dg_ref[...] += jnp.sum(dout_t * xhat_t, axis=0, keepdims=True)
        db_ref[...] += jnp.sum(dout_t, axis=0, keepdims=True)

    row = pl.BlockSpec((tr, d), lambda i: (i, 0))
    vec = pl.BlockSpec((1, d), lambda i: (0, 0))
    return _call(
        body, grid=(s // tr,), in_specs=[row, row, pl.BlockSpec((tr, 1), lambda i: (i, 0)), vec],
        out_specs=[row, row, vec, vec],
        out_shape=[jax.ShapeDtypeStruct((s, d), F32), jax.ShapeDtypeStruct((s, d), BF16),
                   jax.ShapeDtypeStruct((1, d), F32), jax.ShapeDtypeStruct((1, d), F32)],
        semantics=("arbitrary",), name=name, args=(dout, xhat, rstd, gain), jobs=jobs)


def _ln_loss(xres, y, gain, bias, target, name, jobs=()):
    s, d = xres.shape
    tr = _tile(TILES["row"], s)

    def body(x_ref, y_ref, g_ref, b_ref, t_ref, loss_ref, dz_ref, dzb_ref, dg_ref, db_ref):
        @pl.when(pl.program_id(0) == 0)
        def _():
            loss_ref[...] = jnp.zeros_like(loss_ref)
            dg_ref[...] = jnp.zeros_like(dg_ref)
            db_ref[...] = jnp.zeros_like(db_ref)

        xhat, rstd = _ln_stats(ALPHA * x_ref[...] + y_ref[...])
        diff = xhat * g_ref[...] + b_ref[...] - t_ref[...]
        per_row = jnp.mean(diff * diff, axis=-1, keepdims=True)
        loss_ref[...] += 0.5 * jnp.sum(per_row, axis=0, keepdims=True)
        dout = diff * (1.0 / d)
        dz = _ln_grad(dout, xhat, rstd, g_ref[...])
        dz_ref[...] = dz
        dzb_ref[...] = dz.astype(BF16)
        dg_ref[...] += jnp.sum(dout * xhat, axis=0, keepdims=True)
        db_ref[...] += jnp.sum(dout, axis=0, keepdims=True)

    row = pl.BlockSpec((tr, d), lambda i: (i, 0))
    vec = pl.BlockSpec((1, d), lambda i: (0, 0))
    return _call(
        body, grid=(s // tr,), in_specs=[row, row, vec, vec, row],
        out_specs=[pl.BlockSpec((1, 128), lambda i: (0, 0)), row, row, vec, vec],
        out_shape=[jax.ShapeDtypeStruct((1, 128), F32), jax.ShapeDtypeStruct((s, d), F32),
                   jax.ShapeDtypeStruct((s, d), BF16), jax.ShapeDtypeStruct((1, d), F32),
                   jax.ShapeDtypeStruct((1, d), F32)],
        semantics=("arbitrary",), name=name, args=(xres, y, gain, bias, target), jobs=jobs)


def _softmax_rows(s):
    e = jnp.exp(s - jnp.max(s, axis=-1, keepdims=True))
    return e / jnp.sum(e, axis=-1, keepdims=True)


def _attn_fwd(q, k, v, name, jobs=()):
    s, d = q.shape
    m = k.shape[0]
    hd = d // XATTN_HEADS
    ts = _tile(TILES["attn"], s)
    scale = hd ** -0.5

    def body(q_ref, k_ref, v_ref, o_ref):
        for h in range(XATTN_HEADS):
            hs = slice(h * hd, (h + 1) * hd)
            sc = lax.dot_general(q_ref[:, hs], k_ref[:, hs], NT_DIMS, preferred_element_type=F32) * scale
            p = _softmax_rows(sc).astype(BF16)
            o_ref[:, hs] = jnp.dot(p, v_ref[:, hs], preferred_element_type=F32).astype(BF16)

    row = pl.BlockSpec((ts, d), lambda i: (i, 0))
    memb = pl.BlockSpec((m, d), lambda i: (0, 0))
    return _call(
        body, grid=(s // ts,), in_specs=[row, memb, memb], out_specs=[row],
        out_shape=[jax.ShapeDtypeStruct((s, d), BF16)],
        semantics=("parallel",), name=name, args=(q, k, v), jobs=jobs)[0]


def _attn_bwd(q, k, v, do, name, jobs=()):
    s, d = q.shape
    m = k.shape[0]
    hd = d // XATTN_HEADS
    ts = _tile(TILES["attn"], s)
    scale = hd ** -0.5

    def body(q_ref, k_ref, v_ref, do_ref, dq_ref, dk_ref, dv_ref):
        @pl.when(pl.program_id(0) == 0)
        def _():
            dk_ref[...] = jnp.zeros_like(dk_ref)
            dv_ref[...] = jnp.zeros_like(dv_ref)

        for h in range(XATTN_HEADS):
            hs = slice(h * hd, (h + 1) * hd)
            qh, kh, vh, doh = q_ref[:, hs], k_ref[:, hs], v_ref[:, hs], do_ref[:, hs]
            sc = lax.dot_general(qh, kh, NT_DIMS, preferred_element_type=F32) * scale
            p = _softmax_rows(sc)
            pb = p.astype(BF16)
            dp = lax.dot_general(doh, vh, NT_DIMS, preferred_element_type=F32)
            ds = (p * (dp - jnp.sum(dp * p, axis=-1, keepdims=True)) * scale).astype(BF16)
            dq_ref[:, hs] = jnp.dot(ds, kh, preferred_element_type=F32).astype(BF16)
            dk_ref[:, hs] += lax.dot_general(ds, qh, TN_DIMS, preferred_element_type=F32)
            dv_ref[:, hs] += lax.dot_general(pb, doh, TN_DIMS, preferred_element_type=F32)

    row = pl.BlockSpec((ts, d), lambda i: (i, 0))
    memb = pl.BlockSpec((m, d), lambda i: (0, 0))
    return _call(
        body, grid=(s // ts,), in_specs=[row, memb, memb, row], out_specs=[row, memb, memb],
        out_shape=[jax.ShapeDtypeStruct((s, d), BF16), jax.ShapeDtypeStruct((m, d), F32),
                   jax.ShapeDtypeStruct((m, d), F32)],
        semantics=("arbitrary",), name=name, args=(q, k, v, do), jobs=jobs)


def _sigmoid(x):
    return 1.0 / (1.0 + jnp.exp(-x))


def _log1p(x):
    u = 1.0 + x
    return jnp.where(u == 1.0, x, jnp.log(u) * (x / jnp.where(u == 1.0, 1.0, u - 1.0)))


def _softplus(x):
    return jnp.maximum(x, 0.0) + _log1p(jnp.exp(-jnp.abs(x)))


def _expm1(x):
    series = x * (1.0 + x * 0.5 * (1.0 + x * (1.0 / 3.0) * (1.0 + x * 0.25 * (1.0 + x * 0.2 * (1.0 + x * (1.0 / 6.0))))))
    return jnp.where(jnp.abs(x) < 0.1, series, jnp.exp(x) - 1.0)


GELU_K = 0.7978845608028654
GELU_C = 0.044715


def _gelu(x):
    return 0.5 * x * (1.0 + jnp.tanh(GELU_K * (x + GELU_C * (x * x * x))))


def _gelu_grad(x):
    th = jnp.tanh(GELU_K * (x + GELU_C * (x * x * x)))
    return 0.5 * (1.0 + th) + 0.5 * x * (1.0 - th * th) * GELU_K * (1.0 + 3.0 * GELU_C * x * x)


def _window_sum(ext_ref, first, rows, cols, w, step):
    acc = ext_ref[first:first + rows, cols]
    for kk in range(1, w):
        acc = acc + ext_ref[first + step * kk:first + step * kk + rows, cols]
    return acc


def _lru_gates(c_s, wa_ref, ba_ref, wx_ref, bx_ref, lam_ref, t_idx, hd, r_s, i_s, a_s, mult_s):
    sp = _softplus(-lam_ref[...])
    for h in range(LRU_HEADS):
        hs = slice(h * hd, (h + 1) * hd)
        chb = c_s[:, hs].astype(BF16)
        r = _sigmoid(jnp.dot(chb, wa_ref[h], preferred_element_type=F32) + ba_ref[:, hs])
        ig = _sigmoid(jnp.dot(chb, wx_ref[h], preferred_element_type=F32) + bx_ref[:, hs])
        log_a = -LRU_C * r * sp[:, hs]
        mult = jnp.sqrt(-_expm1(2.0 * log_a))
        r_s[:, hs] = r
        i_s[:, hs] = ig
        a_s[:, hs] = jnp.exp(log_a)
        mult_s[:, hs] = jnp.where(t_idx == 0, 1.0, mult)


def _conv(ext_ref, cw_ref, cb_ref, rows):
    acc = cb_ref[...] + cw_ref[0:1, :] * ext_ref[CONV_HALO - 3:CONV_HALO - 3 + rows, :]
    for kk in range(1, CONV_WIDTH):
        off = CONV_HALO - (CONV_WIDTH - 1) + kk
        acc = acc + cw_ref[kk:kk + 1, :] * ext_ref[off:off + rows, :]
    return acc


def _mixer_fwd(proj, wp, bp, ps, cw, cb, wa, ba, wx, bx, lam, name, jobs=()):
    s, p3 = proj.shape
    p = p3 // 3
    cg, hd = p // N_POOL_GROUPS, p // LRU_HEADS
    t = _tile(TILES["mixer"], s)

    def body(up_ref, ul_ref, ug_ref, wp_ref, bp_ref, ps_ref, cw_ref, cb_ref, wa_ref, ba_ref, wx_ref, bx_ref,
             lam_ref, ycat_ref, h_ref, extp, extl, hc, c_s, r_s, i_s, a_s, b_s):
        i = pl.program_id(0)

        @pl.when(i == 0)
        def _():
            extp[0:POOL_HALO, :] = jnp.zeros((POOL_HALO, p), F32)
            extl[0:CONV_HALO, :] = jnp.zeros((CONV_HALO, p), F32)
            hc[...] = jnp.zeros_like(hc)

        t_idx = i * t + lax.broadcasted_iota(jnp.int32, (t, 1), 0)

        extp[POOL_HALO:POOL_HALO + t, :] = up_ref[...]
        for g, w in enumerate(POOL_WINDOWS):
            cs = slice(g * cg, (g + 1) * cg)
            cnt = jnp.minimum(t_idx + 1, w).astype(F32)
            mixed = _window_sum(extp, POOL_HALO, t, cs, w, -1) / cnt - up_ref[:, cs]
            pre = jnp.dot(mixed.astype(BF16), wp_ref[g], preferred_element_type=F32) + bp_ref[:, cs]
            ycat_ref[:, cs] = (pre * ps_ref[:, cs]).astype(BF16)
        extp[0:POOL_HALO, :] = extp[t:t + POOL_HALO, :]

        extl[CONV_HALO:CONV_HALO + t, :] = ul_ref[...]
        c_s[...] = _conv(extl, cw_ref, cb_ref, t)
        extl[0:CONV_HALO, :] = extl[t:t + CONV_HALO, :]
        _lru_gates(c_s, wa_ref, ba_ref, wx_ref, bx_ref, lam_ref, t_idx, hd, r_s, i_s, a_s, b_s)
        b_s[...] = b_s[...] * (i_s[...] * c_s[...])

        rows = lax.broadcasted_iota(jnp.int32, (SUBLANES, p), 0)

        def block(bi, h):
            r0 = pl.multiple_of(bi * SUBLANES, SUBLANES)
            at = a_s[pl.ds(r0, SUBLANES), :]
            bt = b_s[pl.ds(r0, SUBLANES), :]
            out = jnp.zeros((SUBLANES, p), F32)
            for j in range(SUBLANES):
                h = at[j:j + 1, :] * h + bt[j:j + 1, :]
                out = jnp.where(rows == j, h, out)
            h_ref[pl.ds(r0, SUBLANES), :] = out
            return h

        hc[0:1, :] = lax.fori_loop(0, t // SUBLANES, block, hc[0:1, :])
        ycat_ref[:, p:2 * p] = (h_ref[...] * _gelu(ug_ref[...])).astype(BF16)

    def col(j):
        return pl.BlockSpec((t, p), lambda i: (i, j))

    def whole(a):
        nd = a.ndim
        return pl.BlockSpec(a.shape, lambda i: (0,) * nd)

    consts = (wp, bp, ps, cw, cb, wa, ba, wx, bx, lam)
    tile = pltpu.VMEM((t, p), F32)
    return _call(
        body, grid=(s // t,), in_specs=[col(0), col(1), col(2)] + [whole(a) for a in consts],
        out_specs=[pl.BlockSpec((t, 2 * p), lambda i: (i, 0)), pl.BlockSpec((t, p), lambda i: (i, 0))],
        out_shape=[jax.ShapeDtypeStruct((s, 2 * p), BF16), jax.ShapeDtypeStruct((s, p), F32)],
        scratch_shapes=[pltpu.VMEM((t + POOL_HALO, p), F32), pltpu.VMEM((t + CONV_HALO, p), F32),
                        pltpu.VMEM((SUBLANES, p), F32), tile, tile, tile, tile, tile],
        semantics=("arbitrary",), name=name, args=(proj, proj, proj, *consts), jobs=jobs)


def _mixer_bwd(dycat, proj, hsave, wp, bp, ps, cw, cb, wa, ba, wx, bx, lam, name, jobs=()):
    s, p3 = proj.shape
    p = p3 // 3
    cg, hd = p // N_POOL_GROUPS, p // LRU_HEADS
    t = _tile(TILES["mixer"], s)
    nt = s // t

    def body(dyp_ref, dyl_ref, up_ref, ul_ref, ug_ref, upp_ref, ulp_ref, h_ref, hp_ref,
             wp_ref, bp_ref, ps_ref, cw_ref, cb_ref, wa_ref, ba_ref, wx_ref, bx_ref, lam_ref,
             dproj_ref, dwp_ref, dbp_ref, dps_ref, dcw_ref, dcb_ref, dwa_ref, dba_ref, dwx_ref, dbx_ref, dlam_ref,
             extp, extg, extl, extdc, exth, ghc, c_s, r_s, i_s, a_s, mult_s, gh_s):
        i = pl.program_id(0)
        ib = nt - 1 - i

        @pl.when(i == 0)
        def _():
            for ref in (dwp_ref, dbp_ref, dps_ref, dcw_ref, dcb_ref, dwa_ref, dba_ref, dwx_ref, dbx_ref, dlam_ref):
                ref[...] = jnp.zeros_like(ref)
            extg[t:t + POOL_HALO, :] = jnp.zeros((POOL_HALO, p), F32)
            extdc[t:t + CONV_HALO, :] = jnp.zeros((CONV_HALO, p), F32)
            ghc[...] = jnp.zeros_like(ghc)

        t_idx = ib * t + lax.broadcasted_iota(jnp.int32, (t, 1), 0)
        seq_start = ib == 0

        extl[0:CONV_HALO, :] = jnp.where(seq_start, 0.0, ulp_ref[...])
        extl[CONV_HALO:CONV_HALO + t, :] = ul_ref[...]
        c_s[...] = _conv(extl, cw_ref, cb_ref, t)
        _lru_gates(c_s, wa_ref, ba_ref, wx_ref, bx_ref, lam_ref, t_idx, hd, r_s, i_s, a_s, mult_s)
        exth[0:SUBLANES, :] = jnp.where(seq_start, 0.0, hp_ref[...])
        exth[SUBLANES:SUBLANES + t, :] = h_ref[...]

        ug = ug_ref[...]
        dyl = dyl_ref[...]
        dproj_ref[:, 2 * p:3 * p] = (dyl * h_ref[...] * _gelu_grad(ug)).astype(BF16)
        gh_s[...] = dyl * _gelu(ug)

        rows = lax.broadcasted_iota(jnp.int32, (SUBLANES, p), 0)
        nblk = t // SUBLANES

        def block(bi, carry):
            r0 = pl.multiple_of((nblk - 1 - bi) * SUBLANES, SUBLANES)
            at = a_s[pl.ds(r0, SUBLANES), :]
            dt = gh_s[pl.ds(r0, SUBLANES), :]
            out = jnp.zeros((SUBLANES, p), F32)
            for j in range(SUBLANES - 1, -1, -1):
                gh = dt[j:j + 1, :] + carry
                out = jnp.where(rows == j, gh, out)
                carry = at[j:j + 1, :] * gh
            gh_s[pl.ds(r0, SUBLANES), :] = out
            return carry

        ghc[0:1, :] = lax.fori_loop(0, nblk, block, ghc[0:1, :])

        sp = _softplus(-lam_ref[...])
        dsp_dlam = -_sigmoid(-lam_ref[...])
        for h in range(LRU_HEADS):
            hs = slice(h * hd, (h + 1) * hd)
            gh, a, mult, r, ig, c = gh_s[:, hs], a_s[:, hs], mult_s[:, hs], r_s[:, hs], i_s[:, hs], c_s[:, hs]
            hprev = exth[SUBLANES - 1:SUBLANES - 1 + t, hs]
            dmult = gh * (ig * c)
            dlog_a = a * gh * hprev + jnp.where(t_idx == 0, 0.0, -dmult * a * a / mult)
            dlam_ref[:, hs] += jnp.sum(dlog_a * r, axis=0, keepdims=True) * (-LRU_C) * dsp_dlam[:, hs]
            dpa = dlog_a * (-LRU_C * sp[:, hs]) * r * (1.0 - r)
            dpx = gh * mult * c * ig * (1.0 - ig)
            dpab, dpxb, chb = dpa.astype(BF16), dpx.astype(BF16), c.astype(BF16)
            dwa_ref[h] += lax.dot_general(chb, dpab, TN_DIMS, preferred_element_type=F32)
            dwx_ref[h] += lax.dot_general(chb, dpxb, TN_DIMS, preferred_element_type=F32)
            dba_ref[:, hs] += jnp.sum(dpa, axis=0, keepdims=True)
            dbx_ref[:, hs] += jnp.sum(dpx, axis=0, keepdims=True)
            dc = (gh * mult * ig
                  + lax.dot_general(dpab, wa_ref[h], NT_DIMS, preferred_element_type=F32)
                  + lax.dot_general(dpxb, wx_ref[h], NT_DIMS, preferred_element_type=F32))
            extdc[0:t, hs] = dc
            dcb_ref[:, hs] += jnp.sum(dc, axis=0, keepdims=True)
            for kk in range(CONV_WIDTH):
                off = CONV_HALO - (CONV_WIDTH - 1) + kk
                dcw_ref[kk:kk + 1, hs] += jnp.sum(dc * extl[off:off + t, hs], axis=0, keepdims=True)
        du_lru = cw_ref[0:1, :] * extdc[CONV_WIDTH - 1:CONV_WIDTH - 1 + t, :]
        for kk in range(1, CONV_WIDTH):
            off = CONV_WIDTH - 1 - kk
            du_lru = du_lru + cw_ref[kk:kk + 1, :] * extdc[off:off + t, :]
        dproj_ref[:, p:2 * p] = du_lru.astype(BF16)
        extdc[t:t + CONV_HALO, :] = extdc[0:CONV_HALO, :]

        extp[0:POOL_HALO, :] = jnp.where(seq_start, 0.0, upp_ref[...])
        extp[POOL_HALO:POOL_HALO + t, :] = up_ref[...]
        for g, w in enumerate(POOL_WINDOWS):
            cs = slice(g * cg, (g + 1) * cg)
            cnt = jnp.minimum(t_idx + 1, w).astype(F32)
            mixed = (_window_sum(extp, POOL_HALO, t, cs, w, -1) / cnt - up_ref[:, cs]).astype(BF16)
            pre = jnp.dot(mixed, wp_ref[g], preferred_element_type=F32) + bp_ref[:, cs]
            dyp = dyp_ref[:, cs]
            dps_ref[:, cs] += jnp.sum(dyp * pre, axis=0, keepdims=True)
            dpre = dyp * ps_ref[:, cs]
            dpreb = dpre.astype(BF16)
            dbp_ref[:, cs] += jnp.sum(dpre, axis=0, keepdims=True)
            dwp_ref[g] += lax.dot_general(mixed, dpreb, TN_DIMS, preferred_element_type=F32)
            dmixed = lax.dot_general(dpreb, wp_ref[g], NT_DIMS, preferred_element_type=F32)
            extg[0:t, cs] = dmixed / cnt
            dproj_ref[:, cs] = (_window_sum(extg, 0, t, cs, w, 1) - dmixed).astype(BF16)
        extg[t:t + POOL_HALO, :] = extg[0:POOL_HALO, :]

    def col(j):
        return pl.BlockSpec((t, p), lambda i: (nt - 1 - i, j))

    def prev(rows, j):
        per = t // rows
        return pl.BlockSpec((rows, p), lambda i: (jnp.maximum((nt - 1 - i) * per - 1, 0), j))

    def whole(a):
        nd = a.ndim
        return pl.BlockSpec(a.shape, lambda i: (0,) * nd)

    consts = (wp, bp, ps, cw, cb, wa, ba, wx, bx, lam)
    grads = (wp, bp, ps, cw, cb, wa, ba, wx, bx, lam)
    tile = pltpu.VMEM((t, p), F32)
    return _call(
        body, grid=(nt,),
        in_specs=[col(0), col(1), col(0), col(1), col(2), prev(POOL_HALO, 0), prev(CONV_HALO, 1), col(0),
                  prev(SUBLANES, 0)] + [whole(a) for a in consts],
        out_specs=[pl.BlockSpec((t, 3 * p), lambda i: (nt - 1 - i, 0))] + [whole(a) for a in grads],
        out_shape=[jax.ShapeDtypeStruct((s, 3 * p), BF16)] + [jax.ShapeDtypeStruct(a.shape, F32) for a in grads],
        scratch_shapes=[pltpu.VMEM((t + POOL_HALO, p), F32), pltpu.VMEM((t + POOL_HALO, p), F32),
                        pltpu.VMEM((t + CONV_HALO, p), F32), pltpu.VMEM((t + CONV_HALO, p), F32),
                        pltpu.VMEM((t + SUBLANES, p), F32), pltpu.VMEM((SUBLANES, p), F32),
                        tile, tile, tile, tile, tile, tile],
        semantics=("arbitrary",), name=name,
        args=(dycat, dycat, proj, proj, proj, proj, proj, hsave, hsave, *consts), jobs=jobs)


def _pair_add(parts, got, core, name):
    n, r, c = got.shape
    tr = _tile(TILES["add"], r)

    def body(core_ref, a_ref, b_ref, o_ref):
        del core_ref
        o_ref[...] = (a_ref[...].astype(F32) + b_ref[...].astype(F32)).astype(o_ref.dtype)

    blk = pl.BlockSpec((None, tr, c), lambda k, i, core_ref: (k, i, 0))
    mine = pl.BlockSpec((None, tr, c), lambda k, i, core_ref: (2 * k + core_ref[0], i, 0))
    return _call(body, grid=(n, r // tr), in_specs=[mine, blk], out_specs=[blk],
                 out_shape=[jax.ShapeDtypeStruct(got.shape, got.dtype)], semantics=("parallel", "parallel"),
                 name=name, args=(parts, got), index=core)[0]


def _sum_parts(parts, name):
    n, r, c = parts.shape
    tr = _tile(TILES["adam"], r)

    def body(p_ref, o_ref):
        acc = p_ref[0].astype(F32)
        for d in range(1, n):
            acc = acc + p_ref[d].astype(F32)
        o_ref[...] = acc

    return _call(
        body, grid=(r // tr,), in_specs=[pl.BlockSpec((n, tr, c), lambda i: (0, i, 0))],
        out_specs=[pl.BlockSpec((tr, c), lambda i: (i, 0))], out_shape=[jax.ShapeDtypeStruct((r, c), F32)],
        semantics=("parallel",), name=name, args=(parts,))[0]


def _adamw(w, m, v, parts, name, jobs=(), own=None, chip=None):
    r, c = w.shape
    n = parts.shape[0]
    tr = _tile(TILES["adam"], r)

    def body(*refs):
        if own is not None:
            refs = refs[1:]
            own_ref, refs = refs[3], refs[:3] + refs[4:]
        w_ref, m_ref, v_ref, p_ref, g_ref, d_ref, nm_ref, nv_ref = refs
        g = p_ref[0].astype(F32)
        if own is not None:
            g = own_ref[...].astype(F32) + g
        for d in range(1, n):
            g = g + p_ref[d].astype(F32)
        nm = ADAM_B1 * m_ref[...] + (1.0 - ADAM_B1) * g
        nv = ADAM_B2 * v_ref[...] + (1.0 - ADAM_B2) * (g * g)
        m_hat = nm / (1.0 - ADAM_B1 ** ADAM_STEP)
        v_hat = nv / (1.0 - ADAM_B2 ** ADAM_STEP)
        g_ref[...] = g
        d_ref[...] = -ADAM_LR * (m_hat / (jnp.sqrt(v_hat) + ADAM_EPS) + ADAM_WD * w_ref[...])
        nm_ref[...] = nm
        nv_ref[...] = nv

    row = pl.BlockSpec((tr, c), lambda i, *_: (i, 0))
    in_specs, args = [row, row, row], [w, m, v]
    if own is not None:
        in_specs.append(pl.BlockSpec((None, tr, c), lambda i, chip_ref: (chip_ref[0], i, 0)))
        args.append(own)
    in_specs.append(pl.BlockSpec((n, tr, c), lambda i, *_: (0, i, 0)))
    args.append(parts)
    return _call(
        body, grid=(r // tr,), in_specs=in_specs, out_specs=[row] * 4, out_shape=[jax.ShapeDtypeStruct((r, c), F32)] * 4,
        semantics=("parallel",), name=name, args=args, jobs=jobs, index=chip if own is not None else None)


SMALL_ORDER = ("w_a", "w_x", "conv_w", "b_pool", "conv_b", "b_a", "b_x", "lru_lambda", "pool_scale",
               "ln1_g", "ln1_b", "ln2_g", "ln2_b", "ln3_g", "ln3_b")


def _pack_rows(a, p):
    flat = a.reshape(-1, p)
    pad = (-flat.shape[0]) % SUBLANES
    return jnp.pad(flat, ((0, pad), (0, 0))) if pad else flat


def kernel(x, mem, w_in, conv_w, conv_b, w_a, b_a, w_x, b_x, lru_lambda, w_pool, b_pool, pool_scale, w_out, ln1_g, ln1_b, w_q, w_k, w_v, w_o, ln2_g, ln2_b, w_ff1, w_ff2, ln3_g, ln3_b, loss_target, m_w_in, m_conv_w, m_conv_b, m_w_a, m_b_a, m_w_x, m_b_x, m_lru_lambda, m_w_pool, m_b_pool, m_pool_scale, m_w_out, m_ln1_g, m_ln1_b, m_w_q, m_w_k, m_w_v, m_w_o, m_ln2_g, m_ln2_b, m_w_ff1, m_w_ff2, m_ln3_g, m_ln3_b, v_w_in, v_conv_w, v_conv_b, v_w_a, v_b_a, v_w_x, v_b_x, v_lru_lambda, v_w_pool, v_b_pool, v_pool_scale, v_w_out, v_ln1_g, v_ln1_b, v_w_q, v_w_k, v_w_v, v_w_o, v_ln2_g, v_ln2_b, v_w_ff1, v_w_ff2, v_ln3_g, v_ln3_b):
    names = ("w_in", "conv_w", "conv_b", "w_a", "b_a", "w_x", "b_x", "lru_lambda", "w_pool", "b_pool", "pool_scale",
             "w_out", "ln1_g", "ln1_b", "w_q", "w_k", "w_v", "w_o", "ln2_g", "ln2_b", "w_ff1", "w_ff2", "ln3_g", "ln3_b")
    w_loc = dict(zip(names, (w_in, conv_w, conv_b, w_a, b_a, w_x, b_x, lru_lambda, w_pool, b_pool, pool_scale,
                             w_out, ln1_g, ln1_b, w_q, w_k, w_v, w_o, ln2_g, ln2_b, w_ff1, w_ff2, ln3_g, ln3_b)))
    m_loc = dict(zip(names, (m_w_in, m_conv_w, m_conv_b, m_w_a, m_b_a, m_w_x, m_b_x, m_lru_lambda, m_w_pool, m_b_pool,
                             m_pool_scale, m_w_out, m_ln1_g, m_ln1_b, m_w_q, m_w_k, m_w_v, m_w_o, m_ln2_g, m_ln2_b,
                             m_w_ff1, m_w_ff2, m_ln3_g, m_ln3_b)))
    v_loc = dict(zip(names, (v_w_in, v_conv_w, v_conv_b, v_w_a, v_b_a, v_w_x, v_b_x, v_lru_lambda, v_w_pool, v_b_pool,
                             v_pool_scale, v_w_out, v_ln1_g, v_ln1_b, v_w_q, v_w_k, v_w_v, v_w_o, v_ln2_g, v_ln2_b,
                             v_w_ff1, v_w_ff2, v_ln3_g, v_ln3_b)))
    s, d = x.shape[1], x.shape[2]
    p = conv_b.shape[1]
    cg = p // N_POOL_GROUPS
    hd = p // LRU_HEADS
    me = 4 * lax.axis_index("x") + 2 * lax.axis_index("y") + lax.axis_index("c")

    xs, mems, tgt = x[0], mem[0], loss_target[0]
    memb = mems.astype(BF16)

    gathers = {n: _Job("gather", w_loc[n][0].astype(BF16))
               for n in ("w_in", "w_out", "w_q", "w_k", "w_v", "w_o", "w_ff2", "w_pool")}
    ff1_shard = w_ff1[0].astype(BF16)
    ff1_rows = ff1_shard.shape[0] // FF1_PIECES

    def ff1_piece(i, earlier=None):
        return _Job("gather", ff1_shard[i * ff1_rows:(i + 1) * ff1_rows], window=(i * ff1_rows, ff1_shard.shape[0]),
                    into=None if earlier is None else earlier.out)
    tiny = jnp.concatenate([_pack_rows(conv_w[0], p // N_DEV),
                            _pack_rows(jnp.pad(b_pool[0], ((0, 0), (0, p // N_DEV - cg // N_DEV))), p // N_DEV)], axis=0)
    gathers["tiny"] = _Job("gather", tiny)

    def gathered(n):
        full = gathers[n].out
        if n == "w_in":
            return jnp.transpose(full, (1, 0, 2)).reshape(1, full.shape[1], -1)
        return full.reshape(1, -1, full.shape[-1])

    W = {"conv_b": conv_b, "b_a": b_a.reshape(1, p), "b_x": b_x.reshape(1, p), "lru_lambda": lru_lambda,
         "pool_scale": pool_scale, "w_a": w_a[0].astype(BF16), "w_x": w_x[0].astype(BF16)}
    for n in ("ln1_g", "ln1_b", "ln2_g", "ln2_b", "ln3_g", "ln3_b"):
        W[n] = w_loc[n]

    out_g, out_d, out_m, out_v = {}, {}, {}, {}
    pairs, quads, sums = {}, {}, {}
    core =lax.axis_index("c").astype(jnp.int32).reshape(1)
    chip = (2 * lax.axis_index("x") + lax.axis_index("y")).astype(jnp.int32).reshape(1)

    def pair(n, partial):
        pairs[n] = _Job("pair", partial.reshape(N_DEV, -1, partial.shape[-1]))
        return pairs[n]

    def quad(n, lo=0, hi=1, of=1):
        if lo == 0:
            sums[n] = _pair_add(pairs[n].src, pairs[n].out, core, "add_" + n)
        rows = sums[n].shape[1] // of
        quads[n] = _Job("quad", sums[n], window=None if (lo, hi) == (0, of) else (lo * rows, (hi - lo) * rows),
                        into=None if lo == 0 else quads[n].out)
        return quads[n]

    def update(n, parts, jobs=(), own=None):
        shp = w_loc[n].shape
        rows = parts.shape[1]
        w2, m2, v2 = (a.reshape(rows, -1) for a in (w_loc[n], m_loc[n], v_loc[n]))
        res = _adamw(w2, m2, v2, parts.reshape(parts.shape[0], rows, -1), "adamw_" + n, jobs=jobs, own=own, chip=chip)
        out_g[n], out_d[n], out_m[n], out_v[n] = (r.reshape(shp) for r in res)

    assert FF1_PIECES == 4
    xb = _to_bf16(xs, "cast_x", jobs=[gathers["w_in"], gathers["tiny"], gathers["w_pool"]])
    W["w_pool"] = jnp.transpose(gathers["w_pool"].out, (1, 0, 2, 3)).reshape(N_POOL_GROUPS, cg, cg)
    cwb = gathers["tiny"].out
    W["conv_w"] = jnp.transpose(cwb[:, :CONV_WIDTH, :], (1, 0, 2)).reshape(CONV_WIDTH, p)
    W["b_pool"] = jnp.transpose(cwb[:, SUBLANES:SUBLANES + N_POOL_GROUPS, :cg // N_DEV], (1, 0, 2)).reshape(1, p)
    mixer_consts = (W["w_pool"], W["b_pool"], W["pool_scale"], W["conv_w"], W["conv_b"], W["w_a"], W["b_a"],
                    W["w_x"], W["b_x"], W["lru_lambda"])

    w_in_full = gathered("w_in")
    piece = ff1_piece(FF1_PIECES - 1)
    (proj,) = _mm_nn(xb, w_in_full, [F32], "fwd_proj", jobs=[gathers["w_out"], piece])
    ycat, hsave = _mixer_fwd(proj, *mixer_consts, "fwd_mixer", jobs=[gathers["w_q"], gathers["w_k"]])
    (y1,) = _mm_nn(ycat, gathered("w_out"), [F32], "fwd_out", jobs=[gathers["w_v"]])
    x1, x1b, xhat1, rstd1 = _ln_fwd(xs, y1, W["ln1_g"], W["ln1_b"], "fwd_ln1", jobs=[gathers["w_o"]])
    piece = ff1_piece(0, piece)
    (q,) = _mm_nn(x1b, gathered("w_q"), [BF16], "fwd_q", jobs=[piece])
    (k,) = _mm_nn(memb, gathered("w_k"), [BF16], "fwd_k")
    (v,) = _mm_nn(memb, gathered("w_v"), [BF16], "fwd_v")
    o = _attn_fwd(q, k, v, "fwd_attn")
    piece = ff1_piece(1, piece)
    (y2,) = _mm_nn(o, gathered("w_o"), [F32], "fwd_o", jobs=[piece])
    piece = ff1_piece(2, piece)
    x2, x2b, xhat2, rstd2 = _ln_fwd(x1, y2, W["ln2_g"], W["ln2_b"], "fwd_ln2", jobs=[piece])
    w_ff1_full = piece.out

    def relu_sq(acc):
        r = jnp.maximum(acc, 0.0)
        return r, r * r

    rb, act = _mm_nn(x2b, w_ff1_full, [BF16, BF16], "fwd_ff1", epilogue=relu_sq, jobs=[gathers["w_ff2"]])
    (y3,) = _mm_nn(act, gathered("w_ff2"), [F32], "fwd_ff2")
    loss_rows, dz3, dz3b, dg3, db3 = _ln_loss(x2, y3, W["ln3_g"], W["ln3_b"], tgt, "ln3_loss")
    loss = lax.psum(loss_rows[0, 0], MESH_AXES)

    small = {"ln3_g": dg3, "ln3_b": db3}

    def add_residual(acc, e):
        return (acc + ALPHA * e,)

    dw_ff2 = _mm_tn(act, dz3b, 1, BF16, "bwd_dw_ff2")
    (dhid,) = _mm_nt(dz3b, gathered("w_ff2"), [BF16], "bwd_dact", extras=(rb,), jobs=[pair("w_ff2", dw_ff2)],
                     epilogue=lambda acc, r: (acc * (2.0 * r.astype(F32)),))
    dw_ff1 = _mm_tn(x2b, dhid, N_DEV, BF16, "bwd_dw_ff1", jobs=[quad("w_ff2", 0, 1, 2)])
    (dx2,) = _mm_nt(dhid, w_ff1_full, [F32], "bwd_dx2", epilogue=add_residual, extras=(dz3,), tk=TILES["tk"] // 2,
                    jobs=[pair("w_ff1", dw_ff1), quad("w_ff2", 1, 2, 2)])
    dz2, dz2b, small["ln2_g"], small["ln2_b"] = _ln_bwd(dx2, xhat2, rstd2, W["ln2_g"], "bwd_ln2")

    dw_o = _mm_tn(o, dz2b, 1, BF16, "bwd_dw_o")
    (do,) = _mm_nt(dz2b, gathered("w_o"), [BF16], "bwd_do", jobs=[pair("w_o", dw_o)])
    dq, dk, dv = _attn_bwd(q, k, v, do, "bwd_attn", jobs=[quad("w_o", 0, 1, 2)])
    dw_q = _mm_tn(x1b, dq, 1, BF16, "bwd_dw_q", jobs=[quad("w_o", 1, 2, 2)])
    dw_k = _mm_tn(memb, dk.astype(BF16), 1, BF16, "bwd_dw_k")
    dw_v = _mm_tn(memb, dv.astype(BF16), 1, BF16, "bwd_dw_v")
    (dx1,) = _mm_nt(dq, gathered("w_q"), [F32], "bwd_dx1", epilogue=add_residual, extras=(dz2,),
                    jobs=[pair("w_q", dw_q), pair("w_k", dw_k), pair("w_v", dw_v), quad("w_ff1", 0, 2, 8)])
    dz1, dz1b, small["ln1_g"], small["ln1_b"] = _ln_bwd(dx1, xhat1, rstd1, W["ln1_g"], "bwd_ln1",
                                                        jobs=[quad("w_q", 0, 1, 2)])

    dw_out = _mm_tn(ycat, dz1b, 1, BF16, "bwd_dw_out", jobs=[quad("w_q", 1, 2, 2)])
    (dycat,) = _mm_nt(dz1b, gathered("w_out"), [F32], "bwd_dycat", jobs=[pair("w_out", dw_out), quad("w_ff1", 2, 3, 8)])
    (dproj, dwp, small["b_pool"], small["pool_scale"], small["conv_w"], small["conv_b"], small["w_a"], small["b_a"],
     small["w_x"], small["b_x"], small["lru_lambda"]) = _mixer_bwd(
        dycat, proj, hsave, *mixer_consts, "bwd_mixer", jobs=[quad("w_ff1", 3, 7, 8), quad("w_k"), quad("w_v")])
    dw_pool = jnp.transpose(dwp.astype(BF16).reshape(N_POOL_GROUPS, N_DEV, cg // N_DEV, cg), (1, 0, 2, 3))
    pack = jnp.concatenate([_pack_rows(small[n], p) for n in SMALL_ORDER], axis=0)
    small_gather = _Job("gather", pack)
    dw_in = _mm_tn(xb, dproj, 1, BF16, "bwd_dw_in", jobs=[quad("w_out"), pair("w_pool", dw_pool), small_gather])
    dw_in = jnp.transpose(dw_in.reshape(dw_in.shape[1], N_DEV, -1), (1, 0, 2))
    (gx_lo,) = _mm_nt(dproj, w_in_full, [F32], "bwd_dx_lo", epilogue=add_residual, extras=(dz1,), part=(0, 2),
                      jobs=[pair("w_in", dw_in), quad("w_pool"), quad("w_ff1", 7, 8, 8)])
    (gx_hi,) = _mm_nt(dproj, w_in_full, [F32], "bwd_dx_hi", epilogue=add_residual, extras=(dz1,), part=(1, 2),
                      jobs=[quad("w_in", 0, 1, 2)])
    grad_x = jnp.concatenate([gx_lo, gx_hi], axis=0)

    update("w_ff2", quads["w_ff2"].out, jobs=[quad("w_in", 1, 2, 2)], own=quads["w_ff2"].src)
    for n in ("w_ff1", "w_o", "w_q", "w_k", "w_v", "w_out", "w_pool", "w_in"):
        update(n, quads[n].out, own=quads[n].src)

    total = _sum_parts(small_gather.out, "sum_small")
    row = 0
    for n in SMALL_ORDER:
        size = small[n].size
        nrows = size // p
        g_full = total[row:row + nrows].reshape(small[n].shape)
        row += nrows + (-nrows) % SUBLANES
        if n == "conv_w":
            g_loc = lax.dynamic_slice_in_dim(g_full, me * (p // N_DEV), p // N_DEV, axis=1)
        elif n == "b_pool":
            g_loc = lax.dynamic_slice_in_dim(g_full.reshape(N_POOL_GROUPS, cg), me * (cg // N_DEV), cg // N_DEV, axis=1)
        else:
            g_loc = g_full
        rows = g_loc.shape[0] if n not in ("w_a", "w_x") else LRU_HEADS * hd
        update(n, g_loc.reshape(1, rows, -1))

    order = names
    return (loss, grad_x[None], *[out_g[n] for n in order], *[out_d[n] for n in order],
            *[out_m[n] for n in order], *[out_v[n] for n in order])
```
